```python
import math
import jax, jax.numpy as jnp
from jax import lax
import numpy as np


D_MODEL = 1024
BATCH = 8
SEQ = 4096
DEPTH = 1

HEAD_DIM = 64
MIX_WIDTH = D_MODEL
RWKV_WIDTH = MIX_WIDTH // 2
RWKV_HEADS = RWKV_WIDTH // HEAD_DIM
DIFF_WIDTH = MIX_WIDTH - RWKV_WIDTH
DIFF_QK = HEAD_DIM
DIFF_VD = 2 * HEAD_DIM
DIFF_HEADS = DIFF_WIDTH // DIFF_VD
DECAY_LORA = 32
AAA_LORA = 32
GATE_LORA = 96
N_RWKV_COLS = 3 * RWKV_WIDTH + DECAY_LORA + AAA_LORA + GATE_LORA
N_DIFF_COLS = 3 * DIFF_WIDTH
N_IN = N_RWKV_COLS + N_DIFF_COLS
NUM_BUCKETS = 32
MAX_DISTANCE = 128
Q_BLOCK = 128
N_EXPERTS = 32
TOP_K = 4
D_EXPERT = D_MODEL
SWIGLU_LIMIT = 7.0
SWIGLU_ALPHA = 1.702
ROW_BLOCK = 128
NORM_EPS = 1e-5
LNX_EPS = 64e-5

kernel_name = 'hymba_rwkv7_diffattn_moe_block'


def rms_norm(t, g, eps=NORM_EPS):
    t32 = t.astype(jnp.float32)
    y = t32 * lax.rsqrt(jnp.mean(t32 * t32, axis=-1, keepdims=True) + eps)
    return (y * g.astype(jnp.float32)).astype(t.dtype)


def t5_causal_bucket(dist):
    n = jnp.maximum(dist, 0)
    max_exact = NUM_BUCKETS // 2
    nf = jnp.maximum(n, 1).astype(jnp.float32)
    large = max_exact + (jnp.log(nf / max_exact) / math.log(MAX_DISTANCE / max_exact)
                         * (NUM_BUCKETS - max_exact)).astype(jnp.int32)
    large = jnp.minimum(large, NUM_BUCKETS - 1)
    return jnp.where(n < max_exact, n, large)


def rwkv7_time_mix(p, mu, w0, w2, a0, a2, g2, k_k, k_a, r_k, lnx_g, lnx_b):
    B, S, _ = p.shape
    H, N, C = RWKV_HEADS, HEAD_DIM, RWKV_WIDTH
    prev = jnp.pad(p, ((0, 0), (1, 0), (0, 0)))[:, :-1]
    p = (p + (prev - p) * mu).astype(jnp.float32)
    r, k, v, xw, xa, xg = jnp.split(
        p, [C, 2 * C, 3 * C, 3 * C + DECAY_LORA, 3 * C + DECAY_LORA + AAA_LORA], axis=-1)
    w = -jax.nn.softplus(-(w0 + jnp.tanh(xw) @ w2)) - 0.5
    decay = jnp.exp(-jnp.exp(w))
    a = jax.nn.sigmoid(a0 + xa @ a2)
    g = jax.nn.sigmoid(xg) @ g2
    heads = lambda t: t.reshape(B, S, H, N)
    kk = heads(k * k_k)
    kk = kk * lax.rsqrt(jnp.maximum(jnp.sum(kk * kk, -1, keepdims=True), 1e-24))
    k = k * (1.0 + (a - 1.0) * k_a)
    r_h, k_h, v_h, a_h, d_h = heads(r), heads(k), heads(v), heads(a), heads(decay)
    tm = lambda t: jnp.moveaxis(t, 1, 0)
    xs = (tm(r_h), tm(d_h), tm(k_h), tm(v_h), tm(-kk), tm(kk * a_h))

    def step(state, inp):
        r_t, w_t, k_t, v_t, a_t, b_t = inp
        sa = jnp.einsum('bhij,bhj->bhi', state, a_t)
        state = (state * w_t[:, :, None, :] + sa[..., None] * b_t[:, :, None, :]
                 + v_t[..., None] * k_t[:, :, None, :])
        return state, jnp.einsum('bhij,bhj->bhi', state, r_t)

    state0 = jnp.zeros((B, H, N, N), jnp.float32)
    _, y = lax.scan(step, state0, xs)
    y = jnp.moveaxis(y, 0, 1)
    mean = jnp.mean(y, -1, keepdims=True)
    var = jnp.mean(jnp.square(y - mean), -1, keepdims=True)
    y = ((y - mean) * lax.rsqrt(var + LNX_EPS)).reshape(B, S, C) * lnx_g + lnx_b
    bonus = jnp.sum(r_h * k_h * r_k, -1, keepdims=True) * v_h
    return (y + bonus.reshape(B, S, C)) * g


def diff_attention(p, qn_g, kn_g, lq1, lk1, lq2, lk2, subln_g, rel_bias, lam_init):
    B, S, _ = p.shape
    H = DIFF_HEADS
    q, k, v = jnp.split(p, [DIFF_WIDTH, 2 * DIFF_WIDTH], axis=-1)
    q = q.reshape(B, S, H, 2, DIFF_QK).transpose(0, 2, 3, 1, 4)
    k = k.reshape(B, S, H, 2, DIFF_QK).transpose(0, 2, 3, 1, 4)
    v = v.reshape(B, S, H, DIFF_VD).transpose(0, 2, 1, 3)
    q = rms_norm(q, qn_g) * (DIFF_QK ** -0.5)
    k = rms_norm(k, kn_g)
    lam = (jnp.exp(jnp.sum(lq1 * lk1).astype(jnp.float32))
           - jnp.exp(jnp.sum(lq2 * lk2).astype(jnp.float32)) + lam_init)
    k_pos = jnp.arange(S)

    def block(i):
        q_blk = lax.dynamic_slice_in_dim(q, i * Q_BLOCK, Q_BLOCK, axis=3)
        q_pos = i * Q_BLOCK + jnp.arange(Q_BLOCK)
        dist = q_pos[:, None] - k_pos[None, :]
        bias = rel_bias[t5_causal_bucket(dist)].transpose(2, 0, 1)
        s = jnp.einsum('bhcqd,bhckd->bhcqk', q_blk, k).astype(jnp.float32)
        s = s + bias[None, :, None].astype(jnp.float32)
        s = jnp.where(dist >= 0, s, -jnp.inf)
        pr = jax.nn.softmax(s, axis=-1)
        a_diff = pr[:, :, 0] - lam * pr[:, :, 1]
        return jnp.einsum('bhqk,bhkd->bhqd', a_diff.astype(v.dtype), v)

    o = lax.map(block, jnp.arange(S // Q_BLOCK))
    o = o.transpose(1, 2, 0, 3, 4).reshape(B, H, S, DIFF_VD)
    o = rms_norm(o, subln_g) * (1.0 - lam_init)
    return o.transpose(0, 2, 1, 3).reshape(B, S, H * DIFF_VD)


def clamped_swiglu(hid):
    x_glu, x_lin = jnp.split(hid, 2, axis=-1)
    x_glu = jnp.minimum(x_glu, SWIGLU_LIMIT)
    x_lin = jnp.clip(x_lin, -SWIGLU_LIMIT, SWIGLU_LIMIT)
    return x_glu * jax.nn.sigmoid(SWIGLU_ALPHA * x_glu) * (x_lin + 1.0)


def moe_ffn(h, router_w, router_b, w1, b1, w2, b2):
    T = h.shape[0]
    logits = (h @ router_w + router_b).astype(jnp.float32)
    top_vals, top_idx = lax.top_k(logits, TOP_K)
    gates = jax.nn.softmax(top_vals, axis=-1)
    flat_e = top_idx.reshape(-1)
    flat_tok = jnp.repeat(jnp.arange(T, dtype=jnp.int32), TOP_K)
    flat_g = gates.reshape(-1)
    order = jnp.argsort(flat_e, stable=True)
    e_sorted, tok_sorted, g_sorted = flat_e[order], flat_tok[order], flat_g[order]
    counts = jnp.bincount(flat_e, length=N_EXPERTS)
    starts = jnp.cumsum(counts) - counts
    padded = (counts + ROW_BLOCK - 1) // ROW_BLOCK * ROW_BLOCK
    pad_ends = jnp.cumsum(padded)
    pad_starts = pad_ends - padded
    dest = pad_starts[e_sorted] + (jnp.arange(T * TOP_K) - starts[e_sorted])
    n_rows = T * TOP_K + N_EXPERTS * ROW_BLOCK
    n_blocks = n_rows // ROW_BLOCK
    row_tok = jnp.zeros((n_rows,), jnp.int32).at[dest].set(tok_sorted)
    row_gate = jnp.zeros((n_rows,), h.dtype).at[dest].set(g_sorted.astype(h.dtype))
    blk_expert = jnp.minimum(
        jnp.searchsorted(pad_ends, jnp.arange(n_blocks) * ROW_BLOCK, side='right'), N_EXPERTS - 1)

    def run_block(args):
        tok, e = args
        xb = h[tok]
        hid = clamped_swiglu(xb @ w1[e] + b1[e])
        return hid @ w2[e] + b2[e]

    ys = lax.map(run_block, (row_tok.reshape(n_blocks, ROW_BLOCK), blk_expert))
    ys = ys.reshape(n_rows, -1) * row_gate[:, None]
    return jax.ops.segment_sum(ys, row_tok, num_segments=T)


def setup_inputs(seed: int = 0) -> dict:
    key = jax.random.key(seed)
    ks = jax.random.split(key, 32)
    L = DEPTH
    nrm = lambda i, shape, s: jax.random.normal(ks[i], shape, jnp.float32) * s
    gain = lambda i, shape: 1.0 + nrm(i, shape, 0.02)
    return {
        'x': nrm(0, (BATCH, SEQ, D_MODEL), 1.0),
        'norm1_g': gain(1, (L, D_MODEL)),
        'w_in': nrm(2, (L, D_MODEL, N_IN), D_MODEL ** -0.5),
        'mu_shift': jax.random.uniform(ks[3], (L, N_RWKV_COLS), jnp.float32),
        'w0': jax.random.uniform(ks[4], (L, RWKV_WIDTH), jnp.float32, -4.0, 0.0),
        'w2': nrm(5, (L, DECAY_LORA, RWKV_WIDTH), 0.1),
        'a0': nrm(6, (L, RWKV_WIDTH), 0.1),
        'a2': nrm(7, (L, AAA_LORA, RWKV_WIDTH), AAA_LORA ** -0.5),
        'g2': nrm(8, (L, GATE_LORA, RWKV_WIDTH), GATE_LORA ** -0.5),
        'k_k': 0.85 + nrm(9, (L, RWKV_WIDTH), 0.05),
        'k_a': 1.0 + nrm(10, (L, RWKV_WIDTH), 0.05),
        'r_k': nrm(11, (L, RWKV_HEADS, HEAD_DIM), 0.1),
        'lnx_g': gain(12, (L, RWKV_WIDTH)),
        'lnx_b': nrm(13, (L, RWKV_WIDTH), 0.01),
        'qn_g': gain(14, (L, DIFF_QK)),
        'kn_g': gain(15, (L, DIFF_QK)),
        'lam_q1': nrm(16, (L, DIFF_QK), 0.1),
        'lam_k1': nrm(17, (L, DIFF_QK), 0.1),
        'lam_q2': nrm(18, (L, DIFF_QK), 0.1),
        'lam_k2': nrm(19, (L, DIFF_QK), 0.1),
        'subln_g': gain(20, (L, DIFF_VD)),
        'rel_bias': nrm(21, (NUM_BUCKETS, DIFF_HEADS), 0.1),
        'w_out': nrm(22, (L, MIX_WIDTH, D_MODEL), MIX_WIDTH ** -0.5),
        'norm2_g': gain(23, (L, D_MODEL)),
        'router_w': nrm(24, (L, D_MODEL, N_EXPERTS), D_MODEL ** -0.5),
        'router_b': nrm(25, (L, N_EXPERTS), 0.01),
        'exp_w1': nrm(26, (L, N_EXPERTS, D_MODEL, 2 * D_EXPERT), D_MODEL ** -0.5),
        'exp_b1': nrm(27, (L, N_EXPERTS, 2 * D_EXPERT), 0.01),
        'exp_w2': nrm(28, (L, N_EXPERTS, D_EXPERT, D_MODEL), D_EXPERT ** -0.5),
        'exp_b2': nrm(29, (L, N_EXPERTS, D_MODEL), 0.01),
    }


def reference(x, norm1_g, w_in, mu_shift, w0, w2, a0, a2, g2, k_k, k_a, r_k, lnx_g, lnx_b,
              qn_g, kn_g, lam_q1, lam_k1, lam_q2, lam_k2, subln_g, rel_bias, w_out,
              norm2_g, router_w, router_b, exp_w1, exp_b1, exp_w2, exp_b2):
    B, S, D = x.shape
    for layer in range(DEPTH):
        h = rms_norm(x, norm1_g[layer])
        p = h @ w_in[layer]
        p_rwkv, p_diff = p[..., :N_RWKV_COLS], p[..., N_RWKV_COLS:]
        y_a = rwkv7_time_mix(p_rwkv, mu_shift[layer], w0[layer], w2[layer], a0[layer],
                             a2[layer], g2[layer], k_k[layer], k_a[layer], r_k[layer],
                             lnx_g[layer], lnx_b[layer]).astype(x.dtype)
        lam_init = 0.8 - 0.6 * math.exp(-0.3 * layer)
        y_b = diff_attention(p_diff, qn_g[layer], kn_g[layer], lam_q1[layer], lam_k1[layer],
                             lam_q2[layer], lam_k2[layer], subln_g[layer], rel_bias,
                             lam_init).astype(x.dtype)
        x = x + jnp.concatenate([y_a, y_b], axis=-1) @ w_out[layer]
        h2 = rms_norm(x, norm2_g[layer]).reshape(B * S, D)
        x = x + moe_ffn(h2, router_w[layer], router_b[layer], exp_w1[layer], exp_b1[layer],
                        exp_w2[layer], exp_b2[layer]).reshape(B, S, D).astype(x.dtype)
    return x
```

```python
import functools
import math

import numpy as np
import jax
import jax.numpy as jnp
from jax import lax
from jax.experimental import pallas as pl
from jax.experimental.pallas import tpu as pltpu

F32 = jnp.float32
BF16 = jnp.bfloat16
HP = lax.Precision.HIGHEST

HEAD_DIM = 64
DECAY_LORA = 32
AAA_LORA = 32
GATE_LORA = 96
NUM_BUCKETS = 32
MAX_DISTANCE = 128
N_EXPERTS = 32
TOP_K = 4
SWIGLU_LIMIT = 7.0
SWIGLU_ALPHA = 1.702
NORM_EPS = 1e-5
LNX_EPS = 64e-5

LANES = 128
VMEM_LIMIT_BYTES = 56 * 1024 * 1024

NEG_BIG = -1e30

PROJ_ROWS = 256
RWKV_CHUNK = 64
RWKV_ROWS = 256
ATT_TILE = 256
MOE_ROWS = 256
ROUTE_ROWS = 512
MOVE_ROWS = 256

NN = (((1,), (0,)), ((), ()))
NT = (((1,), (1,)), ((), ()))
TN = (((0,), (0,)), ((), ()))


def _dot(a, b, dims=NN, prec=None):
    return lax.dot_general(a, b, dims, precision=prec, preferred_element_type=F32)


def _cparams(sem):
    return pltpu.CompilerParams(dimension_semantics=sem, vmem_limit_bytes=VMEM_LIMIT_BYTES)


def _sigmoid(x):
    return 1.0 / (1.0 + jnp.exp(-x))


def _lane_lo(shape):
    lane = lax.broadcasted_iota(jnp.int32, shape, len(shape) - 1)
    return (lane % LANES) < HEAD_DIM


def _head_rms_normalize(x, gain, scale):
    outs = []
    for t in range(x.shape[1] // LANES):
        xt = x[:, t * LANES:(t + 1) * LANES]
        lo = _lane_lo(xt.shape)
        x2 = xt * xt
        s0 = jnp.sum(jnp.where(lo, x2, 0.0), axis=-1, keepdims=True)
        s1 = jnp.sum(jnp.where(lo, 0.0, x2), axis=-1, keepdims=True)
        inv = lax.rsqrt(jnp.where(lo, s0, s1) * (1.0 / HEAD_DIM) + NORM_EPS)
        outs.append(xt * inv)
    return jnp.concatenate(outs, axis=-1) * gain * scale


def _inproj_kernel(x_ref, g_ref, w_ref, qg_ref, kg_ref, pr_ref, q_ref, k_ref, v_ref, *, n_rwkv, width):
    x = x_ref[...]
    ms = jnp.mean(x * x, axis=-1, keepdims=True)
    h = (x * lax.rsqrt(ms + NORM_EPS) * g_ref[...]).astype(BF16)
    step = 384
    for c0 in range(0, n_rwkv, step):
        pr_ref[:, c0:c0 + step] = _dot(h, w_ref[:, c0:c0 + step])
    q = _dot(h, w_ref[:, n_rwkv:n_rwkv + width])
    q_ref[...] = _head_rms_normalize(q, qg_ref[...], HEAD_DIM ** -0.5).astype(BF16)
    k = _dot(h, w_ref[:, n_rwkv + width:n_rwkv + 2 * width])
    k_ref[...] = _head_rms_normalize(k, kg_ref[...], 1.0).astype(BF16)
    v_ref[...] = _dot(h, w_ref[:, n_rwkv + 2 * width:n_rwkv + 3 * width]).astype(BF16)


def _inproj(x2d, g, w_all, qg, kg, n_rwkv, width):
    T, D = x2d.shape
    n_all = w_all.shape[1]
    tm = PROJ_ROWS
    row = lambda i: (i, 0)
    fix = lambda i: (0, 0)
    return pl.pallas_call(
        functools.partial(_inproj_kernel, n_rwkv=n_rwkv, width=width),
        grid=(T // tm,),
        in_specs=[
            pl.BlockSpec((tm, D), row),
            pl.BlockSpec((1, D), fix),
            pl.BlockSpec((D, n_all), fix),
            pl.BlockSpec((1, width), fix),
            pl.BlockSpec((1, width), fix),
        ],
        out_specs=[
            pl.BlockSpec((tm, n_rwkv), row),
            pl.BlockSpec((tm, width), row),
            pl.BlockSpec((tm, width), row),
            pl.BlockSpec((tm, width), row),
        ],
        out_shape=[
            jax.ShapeDtypeStruct((T, n_rwkv), F32),
            jax.ShapeDtypeStruct((T, width), BF16),
            jax.ShapeDtypeStruct((T, width), BF16),
            jax.ShapeDtypeStruct((T, width), BF16),
        ],
        compiler_params=_cparams(("arbitrary",)),
        name="inproj",
    )(x2d, g, w_all, qg, kg)


def _group_sum(x, ones_bd):
    outs = []
    for t in range(x.shape[1] // LANES):
        outs.append(_dot(x[:, t * LANES:(t + 1) * LANES], ones_bd, prec=HP))
    return jnp.concatenate(outs, axis=-1)


def _rwkv_kernel(pr_ref, mu_ref, w0_ref, w2_ref, a0_ref, a2_ref, g2_ref, kk_ref, ka_ref, rk_ref,
                 lg_ref, lb_ref, o_ref, state_sc, prev_sc, *, width):
    C = RWKV_CHUNK
    rows = pr_ref.shape[0]
    n_pairs = width // LANES

    @pl.when(pl.program_id(1) == 0)
    def _():
        state_sc[...] = jnp.zeros_like(state_sc)
        prev_sc[...] = jnp.zeros_like(prev_sc)

    x = pr_ref[...]
    rid = lax.broadcasted_iota(jnp.int32, x.shape, 0)
    prev = jnp.where(rid == 0, prev_sc[...], pltpu.roll(x, 1, axis=0))
    prev_sc[...] = x[rows - 1:rows, :]
    xs = x + (prev - x) * mu_ref[...]

    r = xs[:, 0:width]
    k = xs[:, width:2 * width]
    v = xs[:, 2 * width:3 * width]
    o = 3 * width
    xw = xs[:, o:o + LANES]
    xa = xs[:, o + LANES:o + 2 * LANES]
    xg = xs[:, o + 2 * LANES:o + 3 * LANES]

    z = -(w0_ref[...] + _dot(jnp.tanh(xw), w2_ref[...], prec=HP))
    softplus = jnp.maximum(z, 0.0) + jnp.log(1.0 + jnp.exp(-jnp.abs(z)))
    log_decay = -jnp.exp(-softplus - 0.5)
    a_lr = _sigmoid(a0_ref[...] + _dot(xa, a2_ref[...], prec=HP))
    gate = _dot(_sigmoid(xg), g2_ref[...], prec=HP)

    r2 = lax.broadcasted_iota(jnp.int32, (LANES, LANES), 0)
    c2 = lax.broadcasted_iota(jnp.int32, (LANES, LANES), 1)
    same_head = (r2 // HEAD_DIM) == (c2 // HEAD_DIM)
    ones_bd = jnp.where(same_head, 1.0, 0.0)

    kk = k * kk_ref[...]
    kk = kk * lax.rsqrt(jnp.maximum(_group_sum(kk * kk, ones_bd), 1e-24))
    k = k * (1.0 + (a_lr - 1.0) * ka_ref[...])
    a_vec = -kk
    b_vec = kk * a_lr

    rr = lax.broadcasted_iota(jnp.int32, (rows, rows), 0)
    cc = lax.broadcasted_iota(jnp.int32, (rows, rows), 1)
    tri = jnp.where((rr // C == cc // C) & (cc <= rr), 1.0, 0.0)
    cum = _dot(tri, log_decay, prec=HP)

    t_i = lax.broadcasted_iota(jnp.int32, (C, 2 * LANES), 0)
    s_i = lax.broadcasted_iota(jnp.int32, (C, 2 * LANES), 1) % C
    strict2 = t_i > s_i
    incl2 = t_i >= s_i
    strict1 = strict2[:, :LANES]
    lo_c = _lane_lo((C, LANES))
    col_lo = lo_c

    def split_heads(m):
        return jnp.concatenate([jnp.where(lo_c, m, 0.0), jnp.where(lo_c, 0.0, m)], axis=0)

    y_pairs = []
    for p in range(n_pairs):
        ls = slice(p * LANES, (p + 1) * LANES)
        s_bd = state_sc[p]
        y_chunks = []
        for c in range(rows // C):
            rs = slice(c * C, (c + 1) * C)
            ld = log_decay[rs, ls]
            cm = cum[rs, ls]
            cl = cm[C - 1:C, :]
            e_in = jnp.exp(cm)
            e_ex = jnp.exp(cm - ld)
            e_neg = jnp.exp(-cm)
            e_hat = jnp.exp(cl - cm)
            a_t = a_vec[rs, ls] * e_ex
            r_t = r[rs, ls] * e_in
            b_t = b_vec[rs, ls] * e_neg
            k_t = k[rs, ls] * e_neg
            b_h = b_vec[rs, ls] * e_hat
            k_h = k[rs, ls] * e_hat
            vv = v[rs, ls]

            lhs = jnp.concatenate([a_t, r_t], axis=0)
            rhs = jnp.concatenate([split_heads(b_t), split_heads(k_t)], axis=0)
            gram = _dot(lhs, rhs, NT, prec=HP)
            l_cat = jnp.where(strict1, gram[0:C, 0:LANES], 0.0)
            ak_cat = jnp.where(strict1, gram[0:C, LANES:2 * LANES], 0.0)
            rbk = jnp.where(incl2, gram[C:2 * C, :], 0.0)

            v_m = split_heads(vv)
            u = _dot(a_t, s_bd, NT, prec=HP) + _dot(ak_cat, v_m, prec=HP)
            l_pow = l_cat
            n_fac = int(math.log2(C))
            for it in range(n_fac):
                u = u + _dot(l_pow, split_heads(u), prec=HP)
                if it + 1 < n_fac:
                    bd = jnp.concatenate([jnp.where(col_lo, l_pow, 0.0), jnp.where(col_lo, 0.0, l_pow)], axis=0)
                    l_pow = _dot(l_pow, bd, prec=HP)
            y = _dot(r_t, s_bd, NT, prec=HP) + _dot(rbk, jnp.concatenate([split_heads(u), v_m], axis=0), prec=HP)
            upd = _dot(jnp.concatenate([u, vv], axis=0), jnp.concatenate([b_h, k_h], axis=0), TN, prec=HP)
            s_bd = s_bd * jnp.exp(cl) + jnp.where(same_head, upd, 0.0)
            y_chunks.append(y)
        state_sc[p] = s_bd
        y_pairs.append(jnp.concatenate(y_chunks, axis=0))
    y = jnp.concatenate(y_pairs, axis=-1)

    inv_n = 1.0 / HEAD_DIM
    mean = _group_sum(y, ones_bd) * inv_n
    yc = y - mean
    var = _group_sum(yc * yc, ones_bd) * inv_n
    yn = yc * lax.rsqrt(var + LNX_EPS) * lg_ref[...] + lb_ref[...]
    bonus = _group_sum(r * k * rk_ref[...], ones_bd) * v
    o_ref[...] = ((yn + bonus) * gate).astype(o_ref.dtype)


def _rwkv(pr, mu, w0, w2p, a0, a2p, g2p, k_k, k_a, r_k, lnx_g, lnx_b, width):
    B, S, n_rwkv = pr.shape
    rows = RWKV_ROWS
    vec = lambda n: pl.BlockSpec((1, n), lambda b, i: (0, 0))
    mat = pl.BlockSpec((LANES, width), lambda b, i: (0, 0))
    return pl.pallas_call(
        functools.partial(_rwkv_kernel, width=width),
        grid=(B, S // rows),
        in_specs=[
            pl.BlockSpec((None, rows, n_rwkv), lambda b, i: (b, i, 0)),
            vec(n_rwkv), vec(width), mat, vec(width), mat, mat,
            vec(width), vec(width), vec(width), vec(width), vec(width),
        ],
        out_specs=pl.BlockSpec((None, rows, width), lambda b, i: (b, i, 0)),
        out_shape=jax.ShapeDtypeStruct((B, S, width), BF16),
        scratch_shapes=[
            pltpu.VMEM((width // LANES, LANES, LANES), F32),
            pltpu.VMEM((1, n_rwkv), F32),
        ],
        compiler_params=_cparams(("arbitrary", "arbitrary")),
        name="rwkv7",
    )(pr, mu, w0, w2p, a0, a2p, g2p, k_k, k_a, r_k, lnx_g, lnx_b)


def _t5_bucket_np(dist):
    n = np.maximum(dist, 0)
    max_exact = NUM_BUCKETS // 2
    nf = np.maximum(n, 1).astype(np.float32)
    large = max_exact + (np.log(nf / max_exact) / math.log(MAX_DISTANCE / max_exact)
                         * (NUM_BUCKETS - max_exact)).astype(np.int32)
    large = np.minimum(large, NUM_BUCKETS - 1)
    return np.where(n < max_exact, n, large).astype(np.int32)


def _near_bucket_tiles(tile):
    kpos = np.arange(tile)[:, None]
    qpos = np.arange(tile)[None, :]
    out = []
    for delta in (0, 1):
        dist = delta * tile + qpos - kpos
        out.append(np.where(dist >= 0, _t5_bucket_np(dist), -1))
    return np.stack(out).astype(np.int32)


def _attn_kernel(relb_ref, bucket_ref, lq1_ref, lk1_ref, lq2_ref, lk2_ref, sg_ref,
                 q_ref, k_ref, vt_ref, o_ref, bias_sc, q2_sc, m_sc, l_sc, acc_sc, *, lam_init):
    tq = q_ref.shape[0]
    tk = tq
    h = pl.program_id(1)
    qi = pl.program_id(2)

    @pl.when(qi == 0)
    def _():
        for d in range(2):
            bk = bucket_ref[d]
            tile = jnp.full(bk.shape, NEG_BIG, F32)
            for j in range(NUM_BUCKETS):
                tile = jnp.where(bk == j, relb_ref[h, j], tile)
            bias_sc[d, :, 0:tq] = tile
            bias_sc[d, :, tq:2 * tq] = tile

    q = q_ref[...]
    lo = _lane_lo(q.shape)
    zero = jnp.zeros_like(q)
    q2_sc[0:tq, :] = jnp.where(lo, q, zero)
    q2_sc[tq:2 * tq, :] = jnp.where(lo, zero, q)
    m_sc[...] = jnp.full(m_sc.shape, NEG_BIG, F32)
    l_sc[...] = jnp.zeros_like(l_sc)
    acc_sc[...] = jnp.zeros_like(acc_sc)
    far_bias = relb_ref[h, NUM_BUCKETS - 1]

    def step(j, near):
        k0 = pl.multiple_of(j * tk, tk)
        kblk = k_ref[pl.ds(k0, tk), :]
        st = _dot(kblk, q2_sc[...], NT)
        if near is None:
            st = st + far_bias
        else:
            st = st + bias_sc[near]
        m_old = m_sc[...]
        m_new = jnp.maximum(m_old, jnp.max(st, axis=0, keepdims=True))
        alpha = jnp.exp(m_old - m_new)
        p = jnp.exp(st - m_new)
        l_sc[...] = alpha * l_sc[...] + jnp.sum(p, axis=0, keepdims=True)
        vblk = vt_ref[:, pl.ds(k0, tk)]
        acc_sc[...] = alpha * acc_sc[...] + _dot(vblk, p.astype(BF16))
        m_sc[...] = m_new

    def far_body(j, carry):
        step(j, None)
        return carry

    lax.fori_loop(0, jnp.maximum(qi - 1, 0), far_body, 0)

    @pl.when(qi >= 1)
    def _():
        step(qi - 1, 1)

    step(qi, 0)

    l = l_sc[...]
    acc = acc_sc[...]
    o1 = acc[:, 0:tq] / l[:, 0:tq]
    o2 = acc[:, tq:2 * tq] / l[:, tq:2 * tq]
    lam = (jnp.exp(jnp.sum(lq1_ref[...] * lk1_ref[...], axis=-1, keepdims=True))
           - jnp.exp(jnp.sum(lq2_ref[...] * lk2_ref[...], axis=-1, keepdims=True)) + lam_init)
    ot = o1 - lam * o2
    ms = jnp.mean(ot * ot, axis=0, keepdims=True)
    y = ot * lax.rsqrt(ms + NORM_EPS) * sg_ref[...] * (1.0 - lam_init)
    o_ref[...] = y.T.astype(o_ref.dtype)


def _diff_attn(q, k, vt, relb_t, lq1, lk1, lq2, lk2, sg_col, lam_init):
    B, S, width = q.shape
    H = width // LANES
    t = ATT_TILE
    buckets = jnp.asarray(_near_bucket_tiles(t))
    vec = pl.BlockSpec((1, HEAD_DIM), lambda b, h, i: (0, 0))
    return pl.pallas_call(
        functools.partial(_attn_kernel, lam_init=lam_init),
        grid=(B, H, S // t),
        in_specs=[
            pl.BlockSpec(memory_space=pltpu.SMEM),
            pl.BlockSpec((2, t, t), lambda b, h, i: (0, 0, 0)),
            vec, vec, vec, vec,
            pl.BlockSpec((LANES, 1), lambda b, h, i: (0, 0)),
            pl.BlockSpec((None, t, LANES), lambda b, h, i: (b, i, h)),
            pl.BlockSpec((None, S, LANES), lambda b, h, i: (b, 0, h)),
            pl.BlockSpec((None, None, LANES, S), lambda b, h, i: (b, h, 0, 0)),
        ],
        out_specs=pl.BlockSpec((None, t, LANES), lambda b, h, i: (b, i, h)),
        out_shape=jax.ShapeDtypeStruct((B, S, width), BF16),
        scratch_shapes=[
            pltpu.VMEM((2, t, 2 * t), F32),
            pltpu.VMEM((2 * t, LANES), BF16),
            pltpu.VMEM((1, 2 * t), F32),
            pltpu.VMEM((1, 2 * t), F32),
            pltpu.VMEM((LANES, 2 * t), F32),
        ],
        compiler_params=_cparams(("arbitrary", "arbitrary", "arbitrary")),
        name="diff_attn",
    )(relb_t, buckets, lq1, lk1, lq2, lk2, sg_col, q, k, vt)


def _outproj_kernel(x_ref, ya_ref, yb_ref, wa_ref, wb_ref, g_ref, rw_ref, rb_ref,
                    x1_ref, h2_ref, gd_ref):
    x1 = x_ref[...] + _dot(ya_ref[...], wa_ref[...]) + _dot(yb_ref[...], wb_ref[...])
    x1_ref[...] = x1
    ms = jnp.mean(x1 * x1, axis=-1, keepdims=True)
    h2 = x1 * lax.rsqrt(ms + NORM_EPS) * g_ref[...]
    h2_ref[...] = h2
    logits = _dot(h2, rw_ref[...], prec=HP) + rb_ref[...]
    lane = lax.broadcasted_iota(jnp.int32, logits.shape, 1).astype(F32)
    work = logits
    picks = []
    for _ in range(TOP_K):
        m = jnp.max(work, axis=-1, keepdims=True)
        idx = jnp.min(jnp.where(work == m, lane, float(LANES)), axis=-1, keepdims=True)
        hit = lane == idx
        picks.append((m, hit))
        work = jnp.where(hit, NEG_BIG, work)
    m0 = picks[0][0]
    es = [jnp.exp(m - m0) for m, _ in picks]
    denom = es[0] + es[1] + es[2] + es[3]
    gd = jnp.zeros_like(logits)
    for e, (_, hit) in zip(es, picks):
        gd = jnp.where(hit, e / denom, gd)
    gd_ref[...] = gd


def _outproj(x2d, ya, yb, wa, wb, g2, rw, rb):
    T, D = x2d.shape
    half = ya.shape[1]
    tm = PROJ_ROWS
    row = lambda i: (i, 0)
    fix = lambda i: (0, 0)
    return pl.pallas_call(
        _outproj_kernel,
        grid=(T // tm,),
        in_specs=[
            pl.BlockSpec((tm, D), row),
            pl.BlockSpec((tm, half), row),
            pl.BlockSpec((tm, half), row),
            pl.BlockSpec((half, D), fix),
            pl.BlockSpec((half, D), fix),
            pl.BlockSpec((1, D), fix),
            pl.BlockSpec((D, LANES), fix),
            pl.BlockSpec((1, LANES), fix),
        ],
        out_specs=[
            pl.BlockSpec((tm, D), row),
            pl.BlockSpec((tm, D), row),
            pl.BlockSpec((tm, LANES), row),
        ],
        out_shape=[
            jax.ShapeDtypeStruct((T, D), F32),
            jax.ShapeDtypeStruct((T, D), F32),
            jax.ShapeDtypeStruct((T, LANES), F32),
        ],
        compiler_params=_cparams(("arbitrary",)),
        name="outproj_router",
    )(x2d, ya, yb, wa, wb, g2, rw, rb)


def _route_kernel(gd_ref, dest_ref, gate_ref, cnt_ref, carry_sc, pstart_sc, *, rb):
    ph = pl.program_id(0)
    i = pl.program_id(1)
    tb = gd_ref.shape[0]
    gd = gd_ref[...]
    sel = gd > 0.0
    self = jnp.where(sel, 1.0, 0.0)
    colsum = jnp.sum(self, axis=0, keepdims=True)

    @pl.when((ph == 0) & (i == 0))
    def _():
        carry_sc[...] = jnp.zeros_like(carry_sc)

    @pl.when(ph == 0)
    def _():
        carry_sc[...] = carry_sc[...] + colsum

    @pl.when((ph == 1) & (i == 0))
    def _():
        cnt = carry_sc[...]
        padded = jnp.ceil(cnt * (1.0 / rb)) * rb
        r2 = lax.broadcasted_iota(jnp.int32, (LANES, LANES), 0)
        c2 = lax.broadcasted_iota(jnp.int32, (LANES, LANES), 1)
        upper = jnp.where(r2 < c2, 1.0, 0.0)
        pstart = _dot(jnp.broadcast_to(padded, (8, LANES)), upper, prec=HP)[0:1, :]
        pstart_sc[...] = pstart
        cnt_ref[0:1, :] = cnt
        cnt_ref[1:2, :] = pstart
        cnt_ref[2:8, :] = jnp.zeros((6, LANES), F32)
        carry_sc[...] = jnp.zeros_like(carry_sc)

    @pl.when(ph == 1)
    def _():
        rr = lax.broadcasted_iota(jnp.int32, (tb, tb), 0)
        cc = lax.broadcasted_iota(jnp.int32, (tb, tb), 1)
        lower = jnp.where(cc < rr, 1.0, 0.0).astype(BF16)
        sel_b = self.astype(BF16)
        rank = carry_sc[...] + _dot(lower, sel_b)
        dest = pstart_sc[...] + rank
        r2 = lax.broadcasted_iota(jnp.int32, (LANES, LANES), 0)
        c2 = lax.broadcasted_iota(jnp.int32, (LANES, LANES), 1)
        upper_incl = jnp.where(r2 <= c2, 1.0, 0.0).astype(BF16)
        slot = _dot(sel_b, upper_incl)
        lane = lax.broadcasted_iota(jnp.int32, (tb, LANES), 1)
        d_out = jnp.zeros((tb, LANES), F32)
        g_out = jnp.zeros((tb, LANES), F32)
        for kk in range(TOP_K):
            mk = sel & (slot == float(kk + 1))
            d_k = jnp.sum(jnp.where(mk, dest, 0.0), axis=-1, keepdims=True)
            g_k = jnp.sum(jnp.where(mk, gd, 0.0), axis=-1, keepdims=True)
            d_out = jnp.where(lane == kk, d_k, d_out)
            g_out = jnp.where(lane == kk, g_k, g_out)
        dest_ref[...] = d_out.astype(jnp.int32)
        gate_ref[...] = g_out
        carry_sc[...] = carry_sc[...] + colsum


def _route(gd, rb):
    T = gd.shape[0]
    tb = ROUTE_ROWS
    return pl.pallas_call(
        functools.partial(_route_kernel, rb=rb),
        grid=(2, T // tb),
        in_specs=[pl.BlockSpec((tb, LANES), lambda ph, i: (i, 0))],
        out_specs=[
            pl.BlockSpec((tb, LANES), lambda ph, i: (i * ph, 0)),
            pl.BlockSpec((tb, LANES), lambda ph, i: (i * ph, 0)),
            pl.BlockSpec((8, LANES), lambda ph, i: (0, 0)),
        ],
        out_shape=[
            jax.ShapeDtypeStruct((T, LANES), jnp.int32),
            jax.ShapeDtypeStruct((T, LANES), F32),
            jax.ShapeDtypeStruct((8, LANES), F32),
        ],
        scratch_shapes=[pltpu.VMEM((1, LANES), F32), pltpu.VMEM((1, LANES), F32)],
        compiler_params=_cparams(("arbitrary", "arbitrary")),
        name="route_positions",
    )(gd)


def _row_copy(src_ref, src_row, dst_ref, dst_row, sem):
    return pltpu.make_async_copy(src_ref.at[pl.ds(src_row, 1), :], dst_ref.at[pl.ds(dst_row, 1), :], sem)


def _scatter_kernel(dest_ref, h_ref, xs_in_ref, xs_ref, sem):
    del xs_in_ref
    tb = h_ref.shape[0]

    def issue(i, carry):
        for kk in range(TOP_K):
            _row_copy(h_ref, i, xs_ref, dest_ref[i * TOP_K + kk], sem).start()
        return carry

    def drain(i, carry):
        for kk in range(TOP_K):
            _row_copy(h_ref, 0, xs_ref, 0, sem).wait()
        return carry

    lax.fori_loop(0, tb, issue, 0)
    lax.fori_loop(0, tb, drain, 0)


def _scatter_rows(dest_flat, h2, xs_init):
    T, D = h2.shape
    tb = MOVE_ROWS
    return pl.pallas_call(
        _scatter_kernel,
        grid=(T // tb,),
        in_specs=[
            pl.BlockSpec((tb * TOP_K,), lambda i: (i,), memory_space=pltpu.SMEM),
            pl.BlockSpec((tb, D), lambda i: (i, 0)),
            pl.BlockSpec(memory_space=pl.ANY),
        ],
        out_specs=pl.BlockSpec(memory_space=pl.ANY),
        out_shape=jax.ShapeDtypeStruct(xs_init.shape, xs_init.dtype),
        scratch_shapes=[pltpu.SemaphoreType.DMA],
        input_output_aliases={2: 0},
        compiler_params=_cparams(("arbitrary",)),
        name="scatter_rows",
    )(dest_flat, h2, xs_init)


def _moe_kernel(be_ref, na_ref, xs_ref, w1_ref, b1_ref, w2_ref, b2_ref, ys_ref):
    del be_ref
    j = pl.program_id(0)

    @pl.when(j < na_ref[0])
    def _():
        xb = xs_ref[...].astype(BF16)
        hid = _dot(xb, w1_ref[...]) + b1_ref[...]
        half = hid.shape[1] // 2
        x_glu = jnp.minimum(hid[:, :half], SWIGLU_LIMIT)
        x_lin = jnp.clip(hid[:, half:], -SWIGLU_LIMIT, SWIGLU_LIMIT)
        act = x_glu * _sigmoid(SWIGLU_ALPHA * x_glu) * (x_lin + 1.0)
        ys_ref[...] = _dot(act.astype(BF16), w2_ref[...]) + b2_ref[...]

    @pl.when(j >= na_ref[0])
    def _():
        ys_ref[...] = jnp.zeros_like(ys_ref)


def _moe(blk_expert, n_active, xs, w1, b1, w2, b2):
    R, D = xs.shape
    E, _, D2 = w1.shape
    rb = MOE_ROWS
    nb = R // rb
    blk = lambda j, be, na: (jnp.minimum(j, na[0] - 1), 0)
    return pl.pallas_call(
        _moe_kernel,
        grid_spec=pltpu.PrefetchScalarGridSpec(
            num_scalar_prefetch=2,
            grid=(nb,),
            in_specs=[
                pl.BlockSpec((rb, D), blk),
                pl.BlockSpec((None, D, D2), lambda j, be, na: (be[j], 0, 0)),
                pl.BlockSpec((None, 1, D2), lambda j, be, na: (be[j], 0, 0)),
                pl.BlockSpec((None, D2 // 2, D), lambda j, be, na: (be[j], 0, 0)),
                pl.BlockSpec((None, 1, D), lambda j, be, na: (be[j], 0, 0)),
            ],
            out_specs=pl.BlockSpec((rb, D), lambda j, be, na: (j, 0)),
        ),
        out_shape=jax.ShapeDtypeStruct((R, D), F32),
        compiler_params=_cparams(("arbitrary",)),
        name="moe_experts",
    )(blk_expert, n_active, xs, w1, b1, w2, b2)


def _combine_kernel(dest_ref, gate_ref, x1_ref, ys_ref, o_ref, buf, sem):
    tb = x1_ref.shape[0]

    def issue(i, carry):
        for kk in range(TOP_K):
            _row_copy(ys_ref, dest_ref[i * TOP_K + kk], buf.at[kk], i, sem).start()
        return carry

    def drain(i, carry):
        for kk in range(TOP_K):
            _row_copy(ys_ref, 0, buf.at[kk], 0, sem).wait()
        return carry

    lax.fori_loop(0, tb, issue, 0)
    lax.fori_loop(0, tb, drain, 0)
    g = gate_ref[...]
    acc = x1_ref[...]
    for kk in range(TOP_K):
        acc = acc + g[:, kk:kk + 1] * buf[kk]
    o_ref[...] = acc


def _combine(dest_flat, gate4, x1, ys):
    T, D = x1.shape
    tb = MOVE_ROWS
    return pl.pallas_call(
        _combine_kernel,
        grid=(T // tb,),
        in_specs=[
            pl.BlockSpec((tb * TOP_K,), lambda i: (i,), memory_space=pltpu.SMEM),
            pl.BlockSpec((tb, TOP_K), lambda i: (i, 0)),
            pl.BlockSpec((tb, D), lambda i: (i, 0)),
            pl.BlockSpec(memory_space=pl.ANY),
        ],
        out_specs=pl.BlockSpec((tb, D), lambda i: (i, 0)),
        out_shape=jax.ShapeDtypeStruct((T, D), F32),
        scratch_shapes=[pltpu.VMEM((TOP_K, tb, D), F32), pltpu.SemaphoreType.DMA],
        compiler_params=_cparams(("arbitrary",)),
        name="combine_rows",
    )(dest_flat, gate4, x1, ys)


def _pad_cols(a, n):
    return jnp.pad(a, ((0, 0), (0, n - a.shape[1])))


def _pad_rows(a, n):
    return jnp.pad(a, ((0, n - a.shape[0]), (0, 0)))


def _split_rwkv_cols(a, width):
    o = 3 * width
    return jnp.concatenate([
        a[:, :o],
        _pad_cols(a[:, o:o + DECAY_LORA], LANES),
        _pad_cols(a[:, o + DECAY_LORA:o + DECAY_LORA + AAA_LORA], LANES),
        _pad_cols(a[:, o + DECAY_LORA + AAA_LORA:o + DECAY_LORA + AAA_LORA + GATE_LORA], LANES),
    ], axis=1)


def kernel(x, norm1_g, w_in, mu_shift, w0, w2, a0, a2, g2, k_k, k_a, r_k, lnx_g, lnx_b, qn_g, kn_g,
           lam_q1, lam_k1, lam_q2, lam_k2, subln_g, rel_bias, w_out, norm2_g, router_w, router_b,
           exp_w1, exp_b1, exp_w2, exp_b2):
    B, S, D = x.shape
    T = B * S
    depth = norm1_g.shape[0]
    width = w0.shape[1]
    n_rwkv_cols = 3 * width + DECAY_LORA + AAA_LORA + GATE_LORA
    n_rwkv_pad = 3 * width + 3 * LANES
    n_groups = width // HEAD_DIM
    H = width // LANES
    relb_t = rel_bias.T
    row = lambda a: a.reshape(1, -1)

    for layer in range(depth):
        lam_init = 0.8 - 0.6 * math.exp(-0.3 * layer)
        w_l = w_in[layer]
        w_all = jnp.concatenate([_split_rwkv_cols(w_l[:, :n_rwkv_cols], width), w_l[:, n_rwkv_cols:]],
                                axis=1).astype(BF16)
        mu = _split_rwkv_cols(row(mu_shift[layer]), width)
        qg = row(jnp.tile(qn_g[layer], n_groups))
        kg = row(jnp.tile(kn_g[layer], n_groups))

        pr, q, k, v = _inproj(x.reshape(T, D), row(norm1_g[layer]), w_all, qg, kg, n_rwkv_pad, width)

        y_a = _rwkv(pr.reshape(B, S, n_rwkv_pad), mu, row(w0[layer]), _pad_rows(w2[layer], LANES),
                    row(a0[layer]), _pad_rows(a2[layer], LANES), _pad_rows(g2[layer], LANES),
                    row(k_k[layer]), row(k_a[layer]), row(r_k[layer]), row(lnx_g[layer]),
                    row(lnx_b[layer]), width)

        vt = v.reshape(B, S, H, LANES).transpose(0, 2, 3, 1)
        y_b = _diff_attn(q.reshape(B, S, width), k.reshape(B, S, width), vt, relb_t,
                         row(lam_q1[layer]), row(lam_k1[layer]), row(lam_q2[layer]), row(lam_k2[layer]),
                         subln_g[layer].reshape(-1, 1), lam_init)

        wo = w_out[layer].astype(BF16)
        rw = _pad_cols(router_w[layer], LANES)
        rb = jnp.concatenate([router_b[layer], jnp.full((LANES - N_EXPERTS,), NEG_BIG, F32)]).reshape(1, -1)
        x1, h2, gd = _outproj(x.reshape(T, D), y_a.reshape(T, width), y_b.reshape(T, width),
                              wo[:width], wo[width:], row(norm2_g[layer]), rw, rb)

        dest, gate, cnt = _route(gd, MOE_ROWS)
        dest_flat = dest[:, :TOP_K].reshape(-1)
        gate4 = gate[:, :TOP_K]
        counts = cnt[0, :N_EXPERTS]
        pad_end = cnt[1, :N_EXPERTS] + jnp.ceil(counts / MOE_ROWS) * MOE_ROWS
        n_blocks = (T * TOP_K) // MOE_ROWS + N_EXPERTS
        blk_start = (jnp.arange(n_blocks) * MOE_ROWS).astype(F32)
        blk_expert = jnp.minimum(jnp.sum(pad_end[None, :] <= blk_start[:, None], axis=1), N_EXPERTS - 1)
        n_active = (pad_end[N_EXPERTS - 1] / MOE_ROWS).astype(jnp.int32).reshape(1)

        xs = _scatter_rows(dest_flat, h2, jnp.zeros((n_blocks * MOE_ROWS, D), F32))
        ys = _moe(blk_expert.astype(jnp.int32), n_active, xs,
                  exp_w1[layer].astype(BF16), exp_b1[layer][:, None, :],
                  exp_w2[layer].astype(BF16), exp_b2[layer][:, None, :])
        x = _combine(dest_flat, gate4, x1, ys).reshape(B, S, D)
    return x
```

```python
import functools
import math

import numpy as np
import jax
import jax.numpy as jnp
from jax import lax
from jax.experimental import pallas as pl
from jax.experimental.pallas import tpu as pltpu

F32 = jnp.float32
BF16 = jnp.bfloat16
HP = lax.Precision.HIGHEST

HEAD_DIM = 64
DECAY_LORA = 32
AAA_LORA = 32
GATE_LORA = 96
NUM_BUCKETS = 32
MAX_DISTANCE = 128
N_EXPERTS = 32
TOP_K = 4
SWIGLU_LIMIT = 7.0
SWIGLU_ALPHA = 1.702
NORM_EPS = 1e-5
LNX_EPS = 64e-5

LANES = 128
VMEM_LIMIT_BYTES = 56 * 1024 * 1024

NEG_BIG = -1e30
LOG2E = math.log2(math.e)
ACC_L = LANES
ACC_ROWS = LANES + 8

PROJ_ROWS = 256
RWKV_CHUNK = 64
RWKV_ROWS = 256
ATT_TILE = 256
MOE_ROWS = 256
ROUTE_ROWS = 512
MOVE_ROWS = 256

NN = (((1,), (0,)), ((), ()))
NT = (((1,), (1,)), ((), ()))
TN = (((0,), (0,)), ((), ()))


def _dot(a, b, dims=NN, prec=None):
    return lax.dot_general(a, b, dims, precision=prec, preferred_element_type=F32)


def _cparams(sem):
    return pltpu.CompilerParams(dimension_semantics=sem, vmem_limit_bytes=VMEM_LIMIT_BYTES)


def _sigmoid(x):
    return 1.0 / (1.0 + jnp.exp(-x))


def _lane_lo(shape):
    lane = lax.broadcasted_iota(jnp.int32, shape, len(shape) - 1)
    return (lane % LANES) < HEAD_DIM


def _head_rms_normalize(x, gain, scale):
    outs = []
    for t in range(x.shape[1] // LANES):
        xt = x[:, t * LANES:(t + 1) * LANES]
        lo = _lane_lo(xt.shape)
        x2 = xt * xt
        s0 = jnp.sum(jnp.where(lo, x2, 0.0), axis=-1, keepdims=True)
        s1 = jnp.sum(jnp.where(lo, 0.0, x2), axis=-1, keepdims=True)
        inv = lax.rsqrt(jnp.where(lo, s0, s1) * (1.0 / HEAD_DIM) + NORM_EPS)
        outs.append(xt * inv)
    return jnp.concatenate(outs, axis=-1) * gain * scale


def _inproj_kernel(x_ref, g_ref, w_ref, qg_ref, kg_ref, pr_ref, q_ref, k_ref, v_ref, *, n_rwkv, width):
    x = x_ref[...]
    ms = jnp.mean(x * x, axis=-1, keepdims=True)
    h = (x * lax.rsqrt(ms + NORM_EPS) * g_ref[...]).astype(BF16)
    step = 384
    for c0 in range(0, n_rwkv, step):
        pr_ref[:, c0:c0 + step] = _dot(h, w_ref[:, c0:c0 + step])
    q = _dot(h, w_ref[:, n_rwkv:n_rwkv + width])
    q_ref[...] = _head_rms_normalize(q, qg_ref[...], HEAD_DIM ** -0.5 * LOG2E).astype(BF16)
    k = _dot(h, w_ref[:, n_rwkv + width:n_rwkv + 2 * width])
    k_ref[...] = _head_rms_normalize(k, kg_ref[...], 1.0).astype(BF16)
    v_ref[...] = _dot(h, w_ref[:, n_rwkv + 2 * width:n_rwkv + 3 * width]).astype(BF16)


def _inproj(x2d, g, w_all, qg, kg, n_rwkv, width):
    T, D = x2d.shape
    n_all = w_all.shape[1]
    tm = PROJ_ROWS
    row = lambda i: (i, 0)
    fix = lambda i: (0, 0)
    return pl.pallas_call(
        functools.partial(_inproj_kernel, n_rwkv=n_rwkv, width=width),
        grid=(T // tm,),
        in_specs=[
            pl.BlockSpec((tm, D), row),
            pl.BlockSpec((1, D), fix),
            pl.BlockSpec((D, n_all), fix),
            pl.BlockSpec((1, width), fix),
            pl.BlockSpec((1, width), fix),
        ],
        out_specs=[
            pl.BlockSpec((tm, n_rwkv), row),
            pl.BlockSpec((tm, width), row),
            pl.BlockSpec((tm, width), row),
            pl.BlockSpec((tm, width), row),
        ],
        out_shape=[
            jax.ShapeDtypeStruct((T, n_rwkv), F32),
            jax.ShapeDtypeStruct((T, width), BF16),
            jax.ShapeDtypeStruct((T, width), BF16),
            jax.ShapeDtypeStruct((T, width), BF16),
        ],
        compiler_params=_cparams(("arbitrary",)),
        name="inproj",
    )(x2d, g, w_all, qg, kg)


def _split2(x):
    hi = x.astype(BF16)
    return hi, (x - hi.astype(F32)).astype(BF16)


def _mm(a, b, dims=NN, mode="bf16"):
    if mode == "bf16":
        return _dot(a.astype(BF16), b.astype(BF16), dims)
    ah, al = _split2(a)
    bh, bl = _split2(b)
    lhs = jnp.concatenate([ah, ah, al], axis=dims[0][0][0])
    rhs = jnp.concatenate([bh, bl, bh], axis=dims[0][1][0])
    return _dot(lhs, rhs, dims)


def _group_sum(x, ones2):
    hi, lo = _split2(x)
    return _dot(jnp.concatenate([hi, lo], axis=1), ones2)


P_GRAM = "bf16"
P_INV = "x3"
P_SUBST = "bf16"
P_STATE = "bf16"
INV_BLOCK = 16
QUAD = 256


def _rwkv_kernel(pr_ref, mu_ref, w0_ref, wl_ref, a0_ref, kk_ref, ka_ref, rk_ref, lg_ref, lb_ref,
                 tri3_ref, ones2_ref, o_ref, state_sc, prev_sc, *, width):
    C = RWKV_CHUNK
    rows = pr_ref.shape[0]
    n_quads = width // QUAD
    n_chunks = rows // C
    heads_q = QUAD // HEAD_DIM

    @pl.when(pl.program_id(1) == 0)
    def _():
        state_sc[...] = jnp.zeros_like(state_sc)
        prev_sc[...] = jnp.zeros_like(prev_sc)

    x = pr_ref[...]
    rid = lax.broadcasted_iota(jnp.int32, x.shape, 0)
    prev = jnp.where(rid == 0, prev_sc[...], pltpu.roll(x, 1, axis=0))
    prev_sc[...] = x[rows - 1:rows, :]
    xs = x + (prev - x) * mu_ref[...]

    r = xs[:, 0:width]
    k = xs[:, width:2 * width]
    v = xs[:, 2 * width:3 * width]
    o = 3 * width
    lora_in = jnp.concatenate([jnp.tanh(xs[:, o:o + LANES]), xs[:, o + LANES:o + 2 * LANES],
                               _sigmoid(xs[:, o + 2 * LANES:o + 3 * LANES])], axis=1).astype(BF16)
    lora = _dot(lora_in, wl_ref[...])
    z = -(w0_ref[...] + lora[:, 0:width])
    softplus = jnp.maximum(z, 0.0) + jnp.log(1.0 + jnp.exp(-jnp.abs(z)))
    log_decay = -jnp.exp(-softplus - 0.5)
    a_lr = _sigmoid(a0_ref[...] + lora[:, width:2 * width])
    gate = lora[:, 2 * width:3 * width]

    ones2 = ones2_ref[...]
    kk = k * kk_ref[...]
    kk = kk * lax.rsqrt(jnp.maximum(_group_sum(kk * kk, ones2), 1e-24))
    k = k * (1.0 + (a_lr - 1.0) * ka_ref[...])
    a_vec = -kk
    b_vec = kk * a_lr

    p1 = log_decay.astype(BF16)
    rem = log_decay - p1.astype(F32)
    p2 = rem.astype(BF16)
    p3 = (rem - p2.astype(F32)).astype(BF16)
    cum = _dot(tri3_ref[...], jnp.concatenate([p1, p2, p3], axis=0))

    t_i = lax.broadcasted_iota(jnp.int32, (C, 2 * QUAD), 0)
    s_i = lax.broadcasted_iota(jnp.int32, (C, 2 * QUAD), 1) % C
    strict2 = t_i > s_i
    incl2 = t_i >= s_i
    lane_head = lax.broadcasted_iota(jnp.int32, (C, QUAD), 1) // HEAD_DIM
    row_blk = lax.broadcasted_iota(jnp.int32, (C, QUAD), 0) // INV_BLOCK
    same_blk = row_blk == ((lax.broadcasted_iota(jnp.int32, (C, QUAD), 1) % C) // INV_BLOCK)
    r2 = lax.broadcasted_iota(jnp.int32, (QUAD, QUAD), 0)
    c2 = lax.broadcasted_iota(jnp.int32, (QUAD, QUAD), 1)
    same_head = (r2 // HEAD_DIM) == (c2 // HEAD_DIM)

    def by_head(m):
        return jnp.concatenate([jnp.where(lane_head == h, m, 0.0) for h in range(heads_q)], axis=0)

    class Work:
        pass

    works = []
    for c in range(n_chunks):
        for q in range(n_quads):
            w = Work()
            ls = slice(q * QUAD, (q + 1) * QUAD)
            rs = slice(c * C, (c + 1) * C)
            ld = log_decay[rs, ls]
            cm = cum[rs, ls]
            cl = cm[C - 1:C, :]
            e_in = jnp.exp(cm)
            e_neg = jnp.exp(-cm)
            e_hat = jnp.exp(cl - cm)
            w.q, w.c = q, c
            w.dec = jnp.exp(cl)
            w.vv = v[rs, ls]
            w.ar_t = jnp.concatenate([a_vec[rs, ls] * jnp.exp(cm - ld), r[rs, ls] * e_in], axis=0)
            w.bk_t = jnp.concatenate([by_head(b_vec[rs, ls] * e_neg), by_head(k[rs, ls] * e_neg)], axis=0)
            w.bk_h = jnp.concatenate([b_vec[rs, ls] * e_hat, k[rs, ls] * e_hat], axis=0)
            works.append(w)

    for w in works:
        gram = _mm(w.ar_t, w.bk_t, NT, mode=P_GRAM)
        low = jnp.where(strict2, gram[0:C, :], 0.0)
        l_cat = low[:, 0:QUAD]
        rbk = jnp.where(incl2, gram[C:2 * C, :], 0.0)
        w.rb_cat = rbk[:, 0:QUAD]
        w.kv_lhs = jnp.concatenate([low[:, QUAD:], rbk[:, QUAD:]], axis=0)
        w.l_off = jnp.where(same_blk, 0.0, l_cat)
        w.d_inv = jnp.where(same_blk, l_cat, 0.0)

    for w in works:
        w.l_pow = _mm(w.d_inv, by_head(w.d_inv), mode=P_INV)
    n_sq = int(math.log2(INV_BLOCK))
    for it in range(1, n_sq):
        for w in works:
            if it + 1 < n_sq:
                both = _mm(jnp.concatenate([w.d_inv, w.l_pow], axis=0), by_head(w.l_pow), mode=P_INV)
                w.d_inv = w.d_inv + w.l_pow + both[0:C]
                w.l_pow = both[C:2 * C]
            else:
                w.d_inv = w.d_inv + w.l_pow + _mm(w.d_inv, by_head(w.l_pow), mode=P_INV)
    for w in works:
        w.m1 = w.l_off + _mm(w.d_inv, by_head(w.l_off), mode=P_SUBST)
        w.m1_bd = by_head(w.m1)
    for w in works:
        w.m2 = _mm(w.m1, w.m1_bd, mode=P_SUBST)
    for w in works:
        w.nn = w.m1 + w.m2 + _mm(w.m2, w.m1_bd, mode=P_SUBST)
    for w in works:
        w.t_inv = w.nn + w.d_inv + _mm(w.nn, by_head(w.d_inv), mode=P_SUBST)
    for w in works:
        w.kv = _mm(w.kv_lhs, by_head(w.vv), mode=P_STATE)

    states = [state_sc[q] for q in range(n_quads)]
    y_rows = []
    for c in range(n_chunks):
        ws = works[c * n_quads:(c + 1) * n_quads]
        for w in ws:
            w.base = _mm(w.ar_t, states[w.q], NT, mode=P_STATE) + w.kv
        for w in ws:
            u0 = w.base[0:C]
            w.u = u0 + _mm(w.t_inv, by_head(u0), mode=P_STATE)
        for w in ws:
            upd = _mm(jnp.concatenate([w.u, w.vv], axis=0), w.bk_h, TN, mode=P_STATE)
            states[w.q] = states[w.q] * w.dec + jnp.where(same_head, upd, 0.0)
        y_rows.append(jnp.concatenate(
            [w.base[C:2 * C] + _mm(w.rb_cat, by_head(w.u), mode=P_STATE) for w in ws], axis=-1))
    for q in range(n_quads):
        state_sc[q] = states[q]
    y = jnp.concatenate(y_rows, axis=0)

    inv_n = 1.0 / HEAD_DIM
    sums = _group_sum(jnp.concatenate([y, r * k * rk_ref[...]], axis=0), ones2)
    mean = sums[0:rows] * inv_n
    yc = y - mean
    var = _group_sum(yc * yc, ones2) * inv_n
    yn = yc * lax.rsqrt(var + LNX_EPS) * lg_ref[...] + lb_ref[...]
    o_ref[...] = ((yn + sums[rows:2 * rows] * v) * gate).astype(o_ref.dtype)


def _rwkv(pr, mu, w0, w_lora, a0, k_k, k_a, r_k, lnx_g, lnx_b, width):
    B, S, n_rwkv = pr.shape
    rows = RWKV_ROWS
    C = RWKV_CHUNK
    rr = np.arange(rows)[:, None]
    cc = np.arange(rows)[None, :]
    tri = ((rr // C == cc // C) & (cc <= rr)).astype(np.float32)
    tri3 = jnp.asarray(np.concatenate([tri, tri, tri], axis=1), BF16)
    gg = np.arange(width)
    ones = (gg[:, None] // HEAD_DIM == gg[None, :] // HEAD_DIM).astype(np.float32)
    ones2 = jnp.asarray(np.concatenate([ones, ones], axis=0), BF16)
    fix = lambda shape: pl.BlockSpec(shape, lambda b, i: (0, 0))
    vec = lambda n: fix((1, n))
    return pl.pallas_call(
        functools.partial(_rwkv_kernel, width=width),
        grid=(B, S // rows),
        in_specs=[
            pl.BlockSpec((None, rows, n_rwkv), lambda b, i: (b, i, 0)),
            vec(n_rwkv), vec(width), fix(w_lora.shape), vec(width),
            vec(width), vec(width), vec(width), vec(width), vec(width),
            fix(tri3.shape), fix(ones2.shape),
        ],
        out_specs=pl.BlockSpec((None, rows, width), lambda b, i: (b, i, 0)),
        out_shape=jax.ShapeDtypeStruct((B, S, width), BF16),
        scratch_shapes=[
            pltpu.VMEM((width // QUAD, QUAD, QUAD), F32),
            pltpu.VMEM((1, n_rwkv), F32),
        ],
        compiler_params=_cparams(("arbitrary", "arbitrary")),
        name="rwkv7",
    )(pr, mu, w0, w_lora, a0, k_k, k_a, r_k, lnx_g, lnx_b, tri3, ones2)


def _t5_bucket_np(dist):
    n = np.maximum(dist, 0)
    max_exact = NUM_BUCKETS // 2
    nf = np.maximum(n, 1).astype(np.float32)
    large = max_exact + (np.log(nf / max_exact) / math.log(MAX_DISTANCE / max_exact)
                         * (NUM_BUCKETS - max_exact)).astype(np.int32)
    large = np.minimum(large, NUM_BUCKETS - 1)
    return np.where(n < max_exact, n, large).astype(np.int32)


def _near_bucket_tiles(tile):
    kpos = np.arange(tile)[:, None]
    qpos = np.arange(tile)[None, :]
    out = []
    for delta in (0, 1):
        dist = delta * tile + qpos - kpos
        out.append(np.where(dist >= 0, _t5_bucket_np(dist), -1))
    return np.stack(out).astype(np.int32)


def _attn_kernel(relb_ref, bucket_ref, lq1_ref, lk1_ref, lq2_ref, lk2_ref, sg_ref,
                 q_ref, k_ref, vt_ref, o_ref, bias_sc, q2_sc, m_sc, acc_sc, st_sc, *, lam_init):
    tq = q_ref.shape[0]
    n_heads = vt_ref.shape[0]
    heads = range(n_heads)
    qi = pl.program_id(1)

    @pl.when(qi == 0)
    def _():
        for h in heads:
            for d in range(2):
                bk = bucket_ref[d]
                tile = jnp.full(bk.shape, NEG_BIG, F32)
                for j in range(NUM_BUCKETS):
                    tile = jnp.where(bk == j, relb_ref[h, j] * LOG2E, tile)
                bias_sc[h, d, :, 0:tq] = tile
                bias_sc[h, d, :, tq:2 * tq] = tile

    lo = _lane_lo((tq, LANES))
    for h in heads:
        q = q_ref[:, h * LANES:(h + 1) * LANES]
        zero = jnp.zeros_like(q)
        q2_sc[h, 0:tq, :] = jnp.where(lo, q, zero)
        q2_sc[h, tq:2 * tq, :] = jnp.where(lo, zero, q)
    m_sc[...] = jnp.full(m_sc.shape, NEG_BIG, F32)
    acc_sc[...] = jnp.zeros_like(acc_sc)

    def step(k0, tk, near):
        for h in heads:
            st_sc[h, 0:tk, :] = _dot(k_ref[pl.ds(k0, tk), h * LANES:(h + 1) * LANES], q2_sc[h], NT)
        pvs, alphas = [], []
        for h in heads:
            m_old = m_sc[h]
            if near is None:
                st = st_sc[h, 0:tk, :]
                far = relb_ref[h, NUM_BUCKETS - 1] * LOG2E
                m_new = jnp.maximum(m_old, jnp.max(st, axis=0, keepdims=True) + far)
                p = jnp.exp2(st - (m_new - far))
            else:
                st = st_sc[h, 0:tk, :] + bias_sc[h, near]
                m_new = jnp.maximum(m_old, jnp.max(st, axis=0, keepdims=True))
                p = jnp.exp2(st - m_new)
            m_sc[h] = m_new
            pvs.append(_dot(vt_ref[h, :, pl.ds(k0, tk)], p.astype(BF16)))
            alphas.append(jnp.exp2(m_old - m_new))
        for h in heads:
            acc_sc[h] = alphas[h] * acc_sc[h] + pvs[h]

    n_far = jnp.maximum(qi - 1, 0)
    far_tk = 2 * tq

    def far_body(i, carry):
        step(pl.multiple_of(i * far_tk, far_tk), far_tk, None)
        return carry

    lax.fori_loop(0, n_far // 2, far_body, 0)

    @pl.when(n_far % 2 == 1)
    def _():
        step(pl.multiple_of((n_far - 1) * tq, tq), tq, None)

    @pl.when(qi >= 1)
    def _():
        step(pl.multiple_of((qi - 1) * tq, tq), tq, 1)

    step(pl.multiple_of(qi * tq, tq), tq, 0)

    lam = (jnp.exp(jnp.sum(lq1_ref[...] * lk1_ref[...], axis=-1, keepdims=True))
           - jnp.exp(jnp.sum(lq2_ref[...] * lk2_ref[...], axis=-1, keepdims=True)) + lam_init)
    for h in heads:
        acc = acc_sc[h]
        l = acc[ACC_L:ACC_L + 1, :]
        o1 = acc[0:LANES, 0:tq] / l[:, 0:tq]
        o2 = acc[0:LANES, tq:2 * tq] / l[:, tq:2 * tq]
        ot = o1 - lam * o2
        ms = jnp.mean(ot * ot, axis=0, keepdims=True)
        y = ot * lax.rsqrt(ms + NORM_EPS) * sg_ref[...] * (1.0 - lam_init)
        o_ref[:, h * LANES:(h + 1) * LANES] = y.T.astype(o_ref.dtype)


def _diff_attn(q, k, vt, relb_t, lq1, lk1, lq2, lk2, sg_col, lam_init):
    B, S, width = q.shape
    H = width // LANES
    t = ATT_TILE
    buckets = jnp.asarray(_near_bucket_tiles(t))
    vec = pl.BlockSpec((1, HEAD_DIM), lambda b, i: (0, 0))
    return pl.pallas_call(
        functools.partial(_attn_kernel, lam_init=lam_init),
        grid=(B, S // t),
        in_specs=[
            pl.BlockSpec(memory_space=pltpu.SMEM),
            pl.BlockSpec((2, t, t), lambda b, i: (0, 0, 0)),
            vec, vec, vec, vec,
            pl.BlockSpec((LANES, 1), lambda b, i: (0, 0)),
            pl.BlockSpec((None, t, width), lambda b, i: (b, i, 0)),
            pl.BlockSpec((None, S, width), lambda b, i: (b, 0, 0)),
            pl.BlockSpec((None, H, ACC_ROWS, S), lambda b, i: (b, 0, 0, 0)),
        ],
        out_specs=pl.BlockSpec((None, t, width), lambda b, i: (b, i, 0)),
        out_shape=jax.ShapeDtypeStruct((B, S, width), BF16),
        scratch_shapes=[
            pltpu.VMEM((H, 2, t, 2 * t), F32),
            pltpu.VMEM((H, 2 * t, LANES), BF16),
            pltpu.VMEM((H, 1, 2 * t), F32),
            pltpu.VMEM((H, ACC_ROWS, 2 * t), F32),
            pltpu.VMEM((H, 2 * t, 2 * t), F32),
        ],
        compiler_params=_cparams(("arbitrary", "arbitrary")),
        name="diff_attn",
    )(relb_t, buckets, lq1, lk1, lq2, lk2, sg_col, q, k, vt)


def _outproj_kernel(x_ref, ya_ref, yb_ref, wa_ref, wb_ref, g_ref, rw_ref, rb_ref,
                    x1_ref, h2_ref, gd_ref):
    x1 = x_ref[...] + _dot(ya_ref[...], wa_ref[...]) + _dot(yb_ref[...], wb_ref[...])
    x1_ref[...] = x1
    ms = jnp.mean(x1 * x1, axis=-1, keepdims=True)
    h2 = x1 * lax.rsqrt(ms + NORM_EPS) * g_ref[...]
    h2_ref[...] = h2
    logits = _mm(h2, rw_ref[...], mode="x3") + rb_ref[...]
    lane = lax.broadcasted_iota(jnp.int32, logits.shape, 1).astype(F32)
    work = logits
    picks = []
    for _ in range(TOP_K):
        m = jnp.max(work, axis=-1, keepdims=True)
        idx = jnp.min(jnp.where(work == m, lane, float(LANES)), axis=-1, keepdims=True)
        hit = lane == idx
        picks.append((m, hit))
        work = jnp.where(hit, NEG_BIG, work)
    m0 = picks[0][0]
    es = [jnp.exp(m - m0) for m, _ in picks]
    denom = es[0] + es[1] + es[2] + es[3]
    gd = jnp.zeros_like(logits)
    for e, (_, hit) in zip(es, picks):
        gd = jnp.where(hit, e / denom, gd)
    gd_ref[...] = gd


def _outproj(x2d, ya, yb, wa, wb, g2, rw, rb):
    T, D = x2d.shape
    half = ya.shape[1]
    tm = PROJ_ROWS
    row = lambda i: (i, 0)
    fix = lambda i: (0, 0)
    return pl.pallas_call(
        _outproj_kernel,
        grid=(T // tm,),
        in_specs=[
            pl.BlockSpec((tm, D), row),
            pl.BlockSpec((tm, half), row),
            pl.BlockSpec((tm, half), row),
            pl.BlockSpec((half, D), fix),
            pl.BlockSpec((half, D), fix),
            pl.BlockSpec((1, D), fix),
            pl.BlockSpec((D, LANES), fix),
            pl.BlockSpec((1, LANES), fix),
        ],
        out_specs=[
            pl.BlockSpec((tm, D), row),
            pl.BlockSpec((tm, D), row),
            pl.BlockSpec((tm, LANES), row),
        ],
        out_shape=[
            jax.ShapeDtypeStruct((T, D), F32),
            jax.ShapeDtypeStruct((T, D), F32),
            jax.ShapeDtypeStruct((T, LANES), F32),
        ],
        compiler_params=_cparams(("arbitrary",)),
        name="outproj_router",
    )(x2d, ya, yb, wa, wb, g2, rw, rb)


def _route_kernel(gd_ref, dest_ref, gate_ref, cnt_ref, carry_sc, pstart_sc, *, rb):
    ph = pl.program_id(0)
    i = pl.program_id(1)
    tb = gd_ref.shape[0]
    gd = gd_ref[...]
    sel = gd > 0.0
    self = jnp.where(sel, 1.0, 0.0)
    colsum = jnp.sum(self, axis=0, keepdims=True)

    @pl.when((ph == 0) & (i == 0))
    def _():
        carry_sc[...] = jnp.zeros_like(carry_sc)

    @pl.when(ph == 0)
    def _():
        carry_sc[...] = carry_sc[...] + colsum

    @pl.when((ph == 1) & (i == 0))
    def _():
        cnt = carry_sc[...]
        padded = jnp.ceil(cnt * (1.0 / rb)) * rb
        r2 = lax.broadcasted_iota(jnp.int32, (LANES, LANES), 0)
        c2 = lax.broadcasted_iota(jnp.int32, (LANES, LANES), 1)
        upper = jnp.where(r2 < c2, 1.0, 0.0)
        pstart = _dot(jnp.broadcast_to(padded, (8, LANES)), upper, prec=HP)[0:1, :]
        pstart_sc[...] = pstart
        cnt_ref[0:1, :] = cnt
        cnt_ref[1:2, :] = pstart
        cnt_ref[2:8, :] = jnp.zeros((6, LANES), F32)
        carry_sc[...] = jnp.zeros_like(carry_sc)

    @pl.when(ph == 1)
    def _():
        rr = lax.broadcasted_iota(jnp.int32, (tb, tb), 0)
        cc = lax.broadcasted_iota(jnp.int32, (tb, tb), 1)
        lower = jnp.where(cc < rr, 1.0, 0.0).astype(BF16)
        sel_b = self.astype(BF16)
        rank = carry_sc[...] + _dot(lower, sel_b)
        dest = pstart_sc[...] + rank
        r2 = lax.broadcasted_iota(jnp.int32, (LANES, LANES), 0)
        c2 = lax.broadcasted_iota(jnp.int32, (LANES, LANES), 1)
        upper_incl = jnp.where(r2 <= c2, 1.0, 0.0).astype(BF16)
        slot = _dot(sel_b, upper_incl)
        lane = lax.broadcasted_iota(jnp.int32, (tb, LANES), 1)
        d_out = jnp.zeros((tb, LANES), F32)
        g_out = jnp.zeros((tb, LANES), F32)
        for kk in range(TOP_K):
            mk = sel & (slot == float(kk + 1))
            d_k = jnp.sum(jnp.where(mk, dest, 0.0), axis=-1, keepdims=True)
            g_k = jnp.sum(jnp.where(mk, gd, 0.0), axis=-1, keepdims=True)
            d_out = jnp.where(lane == kk, d_k, d_out)
            g_out = jnp.where(lane == kk, g_k, g_out)
        dest_ref[...] = d_out.astype(jnp.int32)
        gate_ref[...] = g_out
        carry_sc[...] = carry_sc[...] + colsum


def _route(gd, rb):
    T = gd.shape[0]
    tb = ROUTE_ROWS
    return pl.pallas_call(
        functools.partial(_route_kernel, rb=rb),
        grid=(2, T // tb),
        in_specs=[pl.BlockSpec((tb, LANES), lambda ph, i: (i, 0))],
        out_specs=[
            pl.BlockSpec((tb, LANES), lambda ph, i: (i * ph, 0)),
            pl.BlockSpec((tb, LANES), lambda ph, i: (i * ph, 0)),
            pl.BlockSpec((8, LANES), lambda ph, i: (0, 0)),
        ],
        out_shape=[
            jax.ShapeDtypeStruct((T, LANES), jnp.int32),
            jax.ShapeDtypeStruct((T, LANES), F32),
            jax.ShapeDtypeStruct((8, LANES), F32),
        ],
        scratch_shapes=[pltpu.VMEM((1, LANES), F32), pltpu.VMEM((1, LANES), F32)],
        compiler_params=_cparams(("arbitrary", "arbitrary")),
        name="route_positions",
    )(gd)


def _row_copy(src_ref, src_row, dst_ref, dst_row, sem):
    return pltpu.make_async_copy(src_ref.at[pl.ds(src_row, 1), :], dst_ref.at[pl.ds(dst_row, 1), :], sem)


def _scatter_kernel(dest_ref, h_ref, xs_in_ref, xs_ref, sem):
    del xs_in_ref
    tb = h_ref.shape[0]

    def issue(i, carry):
        for kk in range(TOP_K):
            _row_copy(h_ref, i, xs_ref, dest_ref[i * TOP_K + kk], sem).start(priority=kk % 2)
        return carry

    def drain(i, carry):
        for kk in range(TOP_K):
            _row_copy(h_ref, 0, xs_ref, 0, sem).wait()
        return carry

    lax.fori_loop(0, tb, issue, 0)
    lax.fori_loop(0, tb, drain, 0)


def _scatter_rows(dest_flat, h2, xs_init):
    T, D = h2.shape
    tb = MOVE_ROWS
    return pl.pallas_call(
        _scatter_kernel,
        grid=(T // tb,),
        in_specs=[
            pl.BlockSpec((tb * TOP_K,), lambda i: (i,), memory_space=pltpu.SMEM),
            pl.BlockSpec((tb, D), lambda i: (i, 0)),
            pl.BlockSpec(memory_space=pl.ANY),
        ],
        out_specs=pl.BlockSpec(memory_space=pl.ANY),
        out_shape=jax.ShapeDtypeStruct(xs_init.shape, xs_init.dtype),
        scratch_shapes=[pltpu.SemaphoreType.DMA],
        input_output_aliases={2: 0},
        compiler_params=_cparams(("arbitrary",)),
        name="scatter_rows",
    )(dest_flat, h2, xs_init)


def _moe_kernel(be_ref, na_ref, xs_ref, w1_ref, b1_ref, w2_ref, b2_ref, ys_ref):
    del be_ref
    j = pl.program_id(0)

    @pl.when(j < na_ref[0])
    def _():
        xb = xs_ref[...].astype(BF16)
        hid = _dot(xb, w1_ref[...]) + b1_ref[...]
        half = hid.shape[1] // 2
        x_glu = jnp.minimum(hid[:, :half], SWIGLU_LIMIT)
        x_lin = jnp.clip(hid[:, half:], -SWIGLU_LIMIT, SWIGLU_LIMIT)
        act = x_glu * _sigmoid(SWIGLU_ALPHA * x_glu) * (x_lin + 1.0)
        ys_ref[...] = _dot(act.astype(BF16), w2_ref[...]) + b2_ref[...]

    @pl.when(j >= na_ref[0])
    def _():
        ys_ref[...] = jnp.zeros_like(ys_ref)


def _moe(blk_expert, n_active, xs, w1, b1, w2, b2):
    R, D = xs.shape
    E, _, D2 = w1.shape
    rb = MOE_ROWS
    nb = R // rb
    blk = lambda j, be, na: (jnp.minimum(j, na[0] - 1), 0)
    return pl.pallas_call(
        _moe_kernel,
        grid_spec=pltpu.PrefetchScalarGridSpec(
            num_scalar_prefetch=2,
            grid=(nb,),
            in_specs=[
                pl.BlockSpec((rb, D), blk),
                pl.BlockSpec((None, D, D2), lambda j, be, na: (be[j], 0, 0)),
                pl.BlockSpec((None, 1, D2), lambda j, be, na: (be[j], 0, 0)),
                pl.BlockSpec((None, D2 // 2, D), lambda j, be, na: (be[j], 0, 0)),
                pl.BlockSpec((None, 1, D), lambda j, be, na: (be[j], 0, 0)),
            ],
            out_specs=pl.BlockSpec((rb, D), lambda j, be, na: (j, 0)),
        ),
        out_shape=jax.ShapeDtypeStruct((R, D), F32),
        compiler_params=_cparams(("arbitrary",)),
        name="moe_experts",
    )(blk_expert, n_active, xs, w1, b1, w2, b2)


def _combine_kernel(dest_ref, gate_ref, x1_ref, ys_ref, o_ref, buf, sem):
    tb = x1_ref.shape[0]

    def issue(i, carry):
        for kk in range(TOP_K):
            _row_copy(ys_ref, dest_ref[i * TOP_K + kk], buf.at[kk], i, sem).start(priority=kk % 2)
        return carry

    def drain(i, carry):
        for kk in range(TOP_K):
            _row_copy(ys_ref, 0, buf.at[kk], 0, sem).wait()
        return carry

    lax.fori_loop(0, tb, issue, 0)
    lax.fori_loop(0, tb, drain, 0)
    g = gate_ref[...]
    acc = x1_ref[...]
    for kk in range(TOP_K):
        acc = acc + g[:, kk:kk + 1] * buf[kk]
    o_ref[...] = acc


def _combine(dest_flat, gate4, x1, ys):
    T, D = x1.shape
    tb = MOVE_ROWS
    return pl.pallas_call(
        _combine_kernel,
        grid=(T // tb,),
        in_specs=[
            pl.BlockSpec((tb * TOP_K,), lambda i: (i,), memory_space=pltpu.SMEM),
            pl.BlockSpec((tb, TOP_K), lambda i: (i, 0)),
            pl.BlockSpec((tb, D), lambda i: (i, 0)),
            pl.BlockSpec(memory_space=pl.ANY),
        ],
        out_specs=pl.BlockSpec((tb, D), lambda i: (i, 0)),
        out_shape=jax.ShapeDtypeStruct((T, D), F32),
        scratch_shapes=[pltpu.VMEM((TOP_K, tb, D), F32), pltpu.SemaphoreType.DMA],
        compiler_params=_cparams(("arbitrary",)),
        name="combine_rows",
    )(dest_flat, gate4, x1, ys)


def _pad_cols(a, n):
    return jnp.pad(a, ((0, 0), (0, n - a.shape[1])))


def _pad_rows(a, n):
    return jnp.pad(a, ((0, n - a.shape[0]), (0, 0)))


def _lora_weight(w2, a2, g2, width):
    z = jnp.zeros((LANES, width), F32)
    return jnp.concatenate([
        jnp.concatenate([_pad_rows(w2, LANES), z, z], axis=1),
        jnp.concatenate([z, _pad_rows(a2, LANES), z], axis=1),
        jnp.concatenate([z, z, _pad_rows(g2, LANES)], axis=1),
    ], axis=0).astype(BF16)


def _split_rwkv_cols(a, width):
    o = 3 * width
    return jnp.concatenate([
        a[:, :o],
        _pad_cols(a[:, o:o + DECAY_LORA], LANES),
        _pad_cols(a[:, o + DECAY_LORA:o + DECAY_LORA + AAA_LORA], LANES),
        _pad_cols(a[:, o + DECAY_LORA + AAA_LORA:o + DECAY_LORA + AAA_LORA + GATE_LORA], LANES),
    ], axis=1)


def kernel(x, norm1_g, w_in, mu_shift, w0, w2, a0, a2, g2, k_k, k_a, r_k, lnx_g, lnx_b, qn_g, kn_g,
           lam_q1, lam_k1, lam_q2, lam_k2, subln_g, rel_bias, w_out, norm2_g, router_w, router_b,
           exp_w1, exp_b1, exp_w2, exp_b2):
    B, S, D = x.shape
    T = B * S
    depth = norm1_g.shape[0]
    width = w0.shape[1]
    n_rwkv_cols = 3 * width + DECAY_LORA + AAA_LORA + GATE_LORA
    n_rwkv_pad = 3 * width + 3 * LANES
    n_groups = width // HEAD_DIM
    H = width // LANES
    relb_t = rel_bias.T
    row = lambda a: a.reshape(1, -1)

    for layer in range(depth):
        lam_init = 0.8 - 0.6 * math.exp(-0.3 * layer)
        w_l = w_in[layer]
        w_all = jnp.concatenate([_split_rwkv_cols(w_l[:, :n_rwkv_cols], width), w_l[:, n_rwkv_cols:]],
                                axis=1).astype(BF16)
        mu = _split_rwkv_cols(row(mu_shift[layer]), width)
        qg = row(jnp.tile(qn_g[layer], n_groups))
        kg = row(jnp.tile(kn_g[layer], n_groups))

        pr, q, k, v = _inproj(x.reshape(T, D), row(norm1_g[layer]), w_all, qg, kg, n_rwkv_pad, width)

        y_a = _rwkv(pr.reshape(B, S, n_rwkv_pad), mu, row(w0[layer]),
                    _lora_weight(w2[layer], a2[layer], g2[layer], width), row(a0[layer]),
                    row(k_k[layer]), row(k_a[layer]), row(r_k[layer]), row(lnx_g[layer]),
                    row(lnx_b[layer]), width)

        vt = jnp.concatenate([v.reshape(B, S, H, LANES).transpose(0, 2, 3, 1),
                              jnp.ones((B, H, ACC_ROWS - LANES, S), BF16)], axis=2)
        y_b = _diff_attn(q.reshape(B, S, width), k.reshape(B, S, width), vt, relb_t,
                         row(lam_q1[layer]), row(lam_k1[layer]), row(lam_q2[layer]), row(lam_k2[layer]),
                         subln_g[layer].reshape(-1, 1), lam_init)

        wo = w_out[layer].astype(BF16)
        rw = _pad_cols(router_w[layer], LANES)
        rb = jnp.concatenate([router_b[layer], jnp.full((LANES - N_EXPERTS,), NEG_BIG, F32)]).reshape(1, -1)
        x1, h2, gd = _outproj(x.reshape(T, D), y_a.reshape(T, width), y_b.reshape(T, width),
                              wo[:width], wo[width:], row(norm2_g[layer]), rw, rb)

        dest, gate, cnt = _route(gd, MOE_ROWS)
        dest_flat = dest[:, :TOP_K].reshape(-1)
        gate4 = gate[:, :TOP_K]
        counts = cnt[0, :N_EXPERTS]
        pad_end = cnt[1, :N_EXPERTS] + jnp.ceil(counts / MOE_ROWS) * MOE_ROWS
        n_blocks = (T * TOP_K) // MOE_ROWS + N_EXPERTS
        blk_start = (jnp.arange(n_blocks) * MOE_ROWS).astype(F32)
        blk_expert = jnp.minimum(jnp.sum(pad_end[None, :] <= blk_start[:, None], axis=1), N_EXPERTS - 1)
        n_active = (pad_end[N_EXPERTS - 1] / MOE_ROWS).astype(jnp.int32).reshape(1)

        xs = _scatter_rows(dest_flat, h2, jnp.zeros((n_blocks * MOE_ROWS, D), F32))
        ys = _moe(blk_expert.astype(jnp.int32), n_active, xs,
                  exp_w1[layer].astype(BF16), exp_b1[layer][:, None, :],
                  exp_w2[layer].astype(BF16), exp_b2[layer][:, None, :])
        x = _combine(dest_flat, gate4, x1, ys).reshape(B, S, D)
    return x
```

```python
import functools
import math

import numpy as np
import jax
import jax.numpy as jnp
from jax import lax
from jax.experimental import pallas as pl
from jax.experimental.pallas import tpu as pltpu

F32 = jnp.float32
BF16 = jnp.bfloat16
HP = lax.Precision.HIGHEST

HEAD_DIM = 64
DECAY_LORA = 32
AAA_LORA = 32
GATE_LORA = 96
NUM_BUCKETS = 32
MAX_DISTANCE = 128
N_EXPERTS = 32
TOP_K = 4
SWIGLU_LIMIT = 7.0
SWIGLU_ALPHA = 1.702
NORM_EPS = 1e-5
LNX_EPS = 64e-5

LANES = 128
VMEM_LIMIT_BYTES = 56 * 1024 * 1024

NEG_BIG = -1e30
LOG2E = math.log2(math.e)
ACC_L = LANES
ACC_ROWS = LANES + 8

PROJ_ROWS = 256
RWKV_CHUNK = 64
RWKV_ROWS = 256
ATT_TILE = 256
MOE_ROWS = 256
MOVE_ROWS = 256
RUN_CHUNK = 8
LOCAL_ROWS = MOVE_ROWS * 4 + 32 * RUN_CHUNK

NN = (((1,), (0,)), ((), ()))
NT = (((1,), (1,)), ((), ()))
TN = (((0,), (0,)), ((), ()))


def _dot(a, b, dims=NN, prec=None):
    return lax.dot_general(a, b, dims, precision=prec, preferred_element_type=F32)


def _cparams(sem):
    return pltpu.CompilerParams(dimension_semantics=sem, vmem_limit_bytes=VMEM_LIMIT_BYTES)


def _sigmoid(x):
    return 1.0 / (1.0 + jnp.exp(-x))


def _lane_lo(shape):
    lane = lax.broadcasted_iota(jnp.int32, shape, len(shape) - 1)
    return (lane % LANES) < HEAD_DIM


def _head_rms_normalize(x, gain, scale):
    outs = []
    for t in range(x.shape[1] // LANES):
        xt = x[:, t * LANES:(t + 1) * LANES]
        lo = _lane_lo(xt.shape)
        x2 = xt * xt
        s0 = jnp.sum(jnp.where(lo, x2, 0.0), axis=-1, keepdims=True)
        s1 = jnp.sum(jnp.where(lo, 0.0, x2), axis=-1, keepdims=True)
        inv = lax.rsqrt(jnp.where(lo, s0, s1) * (1.0 / HEAD_DIM) + NORM_EPS)
        outs.append(xt * inv)
    return jnp.concatenate(outs, axis=-1) * gain * scale


def _inproj_kernel(x_ref, g_ref, w_ref, qg_ref, kg_ref, pr_ref, q_ref, k_ref, v_ref, *, n_rwkv, width):
    x = x_ref[...]
    ms = jnp.mean(x * x, axis=-1, keepdims=True)
    h = (x * lax.rsqrt(ms + NORM_EPS) * g_ref[...]).astype(BF16)
    step = 384
    for c0 in range(0, n_rwkv, step):
        pr_ref[:, c0:c0 + step] = _dot(h, w_ref[:, c0:c0 + step])
    q = _dot(h, w_ref[:, n_rwkv:n_rwkv + width])
    q_ref[...] = _head_rms_normalize(q, qg_ref[...], HEAD_DIM ** -0.5 * LOG2E).astype(BF16)
    k = _dot(h, w_ref[:, n_rwkv + width:n_rwkv + 2 * width])
    k_ref[...] = _head_rms_normalize(k, kg_ref[...], 1.0).astype(BF16)
    v_ref[...] = _dot(h, w_ref[:, n_rwkv + 2 * width:n_rwkv + 3 * width]).astype(BF16)


def _inproj(x2d, g, w_all, qg, kg, n_rwkv, width):
    T, D = x2d.shape
    n_all = w_all.shape[1]
    tm = PROJ_ROWS
    row = lambda i: (i, 0)
    fix = lambda i: (0, 0)
    return pl.pallas_call(
        functools.partial(_inproj_kernel, n_rwkv=n_rwkv, width=width),
        grid=(T // tm,),
        in_specs=[
            pl.BlockSpec((tm, D), row),
            pl.BlockSpec((1, D), fix),
            pl.BlockSpec((D, n_all), fix),
            pl.BlockSpec((1, width), fix),
            pl.BlockSpec((1, width), fix),
        ],
        out_specs=[
            pl.BlockSpec((tm, n_rwkv), row),
            pl.BlockSpec((tm, width), row),
            pl.BlockSpec((tm, width), row),
            pl.BlockSpec((tm, width), row),
        ],
        out_shape=[
            jax.ShapeDtypeStruct((T, n_rwkv), F32),
            jax.ShapeDtypeStruct((T, width), BF16),
            jax.ShapeDtypeStruct((T, width), BF16),
            jax.ShapeDtypeStruct((T, width), BF16),
        ],
        compiler_params=_cparams(("arbitrary",)),
        name="inproj",
    )(x2d, g, w_all, qg, kg)


def _split2(x):
    hi = x.astype(BF16)
    return hi, (x - hi.astype(F32)).astype(BF16)


def _mm(a, b, dims=NN, mode="bf16"):
    if mode == "bf16":
        return _dot(a.astype(BF16), b.astype(BF16), dims)
    ah, al = _split2(a)
    bh, bl = _split2(b)
    lhs = jnp.concatenate([ah, ah, al], axis=dims[0][0][0])
    rhs = jnp.concatenate([bh, bl, bh], axis=dims[0][1][0])
    return _dot(lhs, rhs, dims)


def _group_sum(x, ones2):
    hi, lo = _split2(x)
    return _dot(jnp.concatenate([hi, lo], axis=1), ones2)


P_GRAM = "bf16"
P_INV = "x3"
P_SUBST = "bf16"
P_STATE = "bf16"
INV_BLOCK = 16
QUAD = 256


def _rwkv_kernel(pr_ref, mu_ref, w0_ref, wl_ref, a0_ref, kk_ref, ka_ref, rk_ref, lg_ref, lb_ref,
                 tri3_ref, ones2_ref, o_ref, state_sc, prev_sc, *, width):
    C = RWKV_CHUNK
    rows = pr_ref.shape[0]
    n_quads = width // QUAD
    n_chunks = rows // C
    heads_q = QUAD // HEAD_DIM

    @pl.when(pl.program_id(1) == 0)
    def _():
        state_sc[...] = jnp.zeros_like(state_sc)
        prev_sc[...] = jnp.zeros_like(prev_sc)

    x = pr_ref[...]
    rid = lax.broadcasted_iota(jnp.int32, x.shape, 0)
    prev = jnp.where(rid == 0, prev_sc[...], pltpu.roll(x, 1, axis=0))
    prev_sc[...] = x[rows - 1:rows, :]
    xs = x + (prev - x) * mu_ref[...]

    r = xs[:, 0:width]
    k = xs[:, width:2 * width]
    v = xs[:, 2 * width:3 * width]
    o = 3 * width
    lora_in = jnp.concatenate([jnp.tanh(xs[:, o:o + LANES]), xs[:, o + LANES:o + 2 * LANES],
                               _sigmoid(xs[:, o + 2 * LANES:o + 3 * LANES])], axis=1).astype(BF16)
    lora = _dot(lora_in, wl_ref[...])
    z = -(w0_ref[...] + lora[:, 0:width])
    softplus = jnp.maximum(z, 0.0) + jnp.log(1.0 + jnp.exp(-jnp.abs(z)))
    log_decay = -jnp.exp(-softplus - 0.5)
    a_lr = _sigmoid(a0_ref[...] + lora[:, width:2 * width])
    gate = lora[:, 2 * width:3 * width]

    ones2 = ones2_ref[...]
    kk = k * kk_ref[...]
    kk = kk * lax.rsqrt(jnp.maximum(_group_sum(kk * kk, ones2), 1e-24))
    k = k * (1.0 + (a_lr - 1.0) * ka_ref[...])
    a_vec = -kk
    b_vec = kk * a_lr

    p1 = log_decay.astype(BF16)
    rem = log_decay - p1.astype(F32)
    p2 = rem.astype(BF16)
    p3 = (rem - p2.astype(F32)).astype(BF16)
    cum = _dot(tri3_ref[...], jnp.concatenate([p1, p2, p3], axis=0))

    t_i = lax.broadcasted_iota(jnp.int32, (C, 2 * QUAD), 0)
    s_i = lax.broadcasted_iota(jnp.int32, (C, 2 * QUAD), 1) % C
    strict2 = t_i > s_i
    incl2 = t_i >= s_i
    lane_head = lax.broadcasted_iota(jnp.int32, (C, QUAD), 1) // HEAD_DIM
    row_blk = lax.broadcasted_iota(jnp.int32, (C, QUAD), 0) // INV_BLOCK
    same_blk = row_blk == ((lax.broadcasted_iota(jnp.int32, (C, QUAD), 1) % C) // INV_BLOCK)
    r2 = lax.broadcasted_iota(jnp.int32, (QUAD, QUAD), 0)
    c2 = lax.broadcasted_iota(jnp.int32, (QUAD, QUAD), 1)
    same_head = (r2 // HEAD_DIM) == (c2 // HEAD_DIM)

    def by_head(m):
        return jnp.concatenate([jnp.where(lane_head == h, m, 0.0) for h in range(heads_q)], axis=0)

    class Work:
        pass

    works = []
    for c in range(n_chunks):
        for q in range(n_quads):
            w = Work()
            ls = slice(q * QUAD, (q + 1) * QUAD)
            rs = slice(c * C, (c + 1) * C)
            ld = log_decay[rs, ls]
            cm = cum[rs, ls]
            cl = cm[C - 1:C, :]
            e_in = jnp.exp(cm)
            e_neg = jnp.exp(-cm)
            e_hat = jnp.exp(cl - cm)
            w.q, w.c = q, c
            w.dec = jnp.exp(cl)
            w.vv = v[rs, ls]
            w.ar_t = jnp.concatenate([a_vec[rs, ls] * jnp.exp(cm - ld), r[rs, ls] * e_in], axis=0)
            w.bk_t = jnp.concatenate([by_head(b_vec[rs, ls] * e_neg), by_head(k[rs, ls] * e_neg)], axis=0)
            w.bk_h = jnp.concatenate([b_vec[rs, ls] * e_hat, k[rs, ls] * e_hat], axis=0)
            works.append(w)

    for w in works:
        gram = _mm(w.ar_t, w.bk_t, NT, mode=P_GRAM)
        low = jnp.where(strict2, gram[0:C, :], 0.0)
        l_cat = low[:, 0:QUAD]
        rbk = jnp.where(incl2, gram[C:2 * C, :], 0.0)
        w.rb_cat = rbk[:, 0:QUAD]
        w.kv_lhs = jnp.concatenate([low[:, QUAD:], rbk[:, QUAD:]], axis=0)
        w.l_off = jnp.where(same_blk, 0.0, l_cat)
        w.d_inv = jnp.where(same_blk, l_cat, 0.0)

    for w in works:
        w.l_pow = _mm(w.d_inv, by_head(w.d_inv), mode=P_INV)
    n_sq = int(math.log2(INV_BLOCK))
    for it in range(1, n_sq):
        for w in works:
            if it + 1 < n_sq:
                both = _mm(jnp.concatenate([w.d_inv, w.l_pow], axis=0), by_head(w.l_pow), mode=P_INV)
                w.d_inv = w.d_inv + w.l_pow + both[0:C]
                w.l_pow = both[C:2 * C]
            else:
                w.d_inv = w.d_inv + w.l_pow + _mm(w.d_inv, by_head(w.l_pow), mode=P_INV)
    for w in works:
        w.m1 = w.l_off + _mm(w.d_inv, by_head(w.l_off), mode=P_SUBST)
        w.m1_bd = by_head(w.m1)
    for w in works:
        w.m2 = _mm(w.m1, w.m1_bd, mode=P_SUBST)
    for w in works:
        w.nn = w.m1 + w.m2 + _mm(w.m2, w.m1_bd, mode=P_SUBST)
    for w in works:
        w.t_inv = w.nn + w.d_inv + _mm(w.nn, by_head(w.d_inv), mode=P_SUBST)
    for w in works:
        w.kv = _mm(w.kv_lhs, by_head(w.vv), mode=P_STATE)

    states = [state_sc[q] for q in range(n_quads)]
    y_rows = []
    for c in range(n_chunks):
        ws = works[c * n_quads:(c + 1) * n_quads]
        for w in ws:
            w.base = _mm(w.ar_t, states[w.q], NT, mode=P_STATE) + w.kv
        for w in ws:
            u0 = w.base[0:C]
            w.u = u0 + _mm(w.t_inv, by_head(u0), mode=P_STATE)
        for w in ws:
            upd = _mm(jnp.concatenate([w.u, w.vv], axis=0), w.bk_h, TN, mode=P_STATE)
            states[w.q] = states[w.q] * w.dec + jnp.where(same_head, upd, 0.0)
        y_rows.append(jnp.concatenate(
            [w.base[C:2 * C] + _mm(w.rb_cat, by_head(w.u), mode=P_STATE) for w in ws], axis=-1))
    for q in range(n_quads):
        state_sc[q] = states[q]
    y = jnp.concatenate(y_rows, axis=0)

    inv_n = 1.0 / HEAD_DIM
    sums = _group_sum(jnp.concatenate([y, r * k * rk_ref[...]], axis=0), ones2)
    mean = sums[0:rows] * inv_n
    yc = y - mean
    var = _group_sum(yc * yc, ones2) * inv_n
    yn = yc * lax.rsqrt(var + LNX_EPS) * lg_ref[...] + lb_ref[...]
    o_ref[...] = ((yn + sums[rows:2 * rows] * v) * gate).astype(o_ref.dtype)


def _rwkv(pr, mu, w0, w_lora, a0, k_k, k_a, r_k, lnx_g, lnx_b, width):
    B, S, n_rwkv = pr.shape
    rows = RWKV_ROWS
    C = RWKV_CHUNK
    rr = np.arange(rows)[:, None]
    cc = np.arange(rows)[None, :]
    tri = ((rr // C == cc // C) & (cc <= rr)).astype(np.float32)
    tri3 = jnp.asarray(np.concatenate([tri, tri, tri], axis=1), BF16)
    gg = np.arange(width)
    ones = (gg[:, None] // HEAD_DIM == gg[None, :] // HEAD_DIM).astype(np.float32)
    ones2 = jnp.asarray(np.concatenate([ones, ones], axis=0), BF16)
    fix = lambda shape: pl.BlockSpec(shape, lambda b, i: (0, 0))
    vec = lambda n: fix((1, n))
    return pl.pallas_call(
        functools.partial(_rwkv_kernel, width=width),
        grid=(B, S // rows),
        in_specs=[
            pl.BlockSpec((None, rows, n_rwkv), lambda b, i: (b, i, 0)),
            vec(n_rwkv), vec(width), fix(w_lora.shape), vec(width),
            vec(width), vec(width), vec(width), vec(width), vec(width),
            fix(tri3.shape), fix(ones2.shape),
        ],
        out_specs=pl.BlockSpec((None, rows, width), lambda b, i: (b, i, 0)),
        out_shape=jax.ShapeDtypeStruct((B, S, width), BF16),
        scratch_shapes=[
            pltpu.VMEM((width // QUAD, QUAD, QUAD), F32),
            pltpu.VMEM((1, n_rwkv), F32),
        ],
        compiler_params=_cparams(("arbitrary", "arbitrary")),
        name="rwkv7",
    )(pr, mu, w0, w_lora, a0, k_k, k_a, r_k, lnx_g, lnx_b, tri3, ones2)


def _t5_bucket_np(dist):
    n = np.maximum(dist, 0)
    max_exact = NUM_BUCKETS // 2
    nf = np.maximum(n, 1).astype(np.float32)
    large = max_exact + (np.log(nf / max_exact) / math.log(MAX_DISTANCE / max_exact)
                         * (NUM_BUCKETS - max_exact)).astype(np.int32)
    large = np.minimum(large, NUM_BUCKETS - 1)
    return np.where(n < max_exact, n, large).astype(np.int32)


def _near_bucket_tiles(tile):
    kpos = np.arange(tile)[:, None]
    qpos = np.arange(tile)[None, :]
    out = []
    for delta in (0, 1):
        dist = delta * tile + qpos - kpos
        out.append(np.where(dist >= 0, _t5_bucket_np(dist), -1))
    return np.stack(out).astype(np.int32)


def _attn_kernel(relb_ref, bucket_ref, lq1_ref, lk1_ref, lq2_ref, lk2_ref, sg_ref,
                 q_ref, k_ref, vt_ref, o_ref, bias_sc, q2_sc, m_sc, acc_sc, st_sc, *, lam_init):
    tq = q_ref.shape[0]
    n_heads = vt_ref.shape[0]
    heads = range(n_heads)
    qi = pl.program_id(1)

    @pl.when(qi == 0)
    def _():
        for h in heads:
            for d in range(2):
                bk = bucket_ref[d]
                tile = jnp.full(bk.shape, NEG_BIG, F32)
                for j in range(NUM_BUCKETS):
                    tile = jnp.where(bk == j, relb_ref[h, j] * LOG2E, tile)
                bias_sc[h, d, :, 0:tq] = tile
                bias_sc[h, d, :, tq:2 * tq] = tile

    lo = _lane_lo((tq, LANES))
    for h in heads:
        q = q_ref[:, h * LANES:(h + 1) * LANES]
        zero = jnp.zeros_like(q)
        q2_sc[h, 0:tq, :] = jnp.where(lo, q, zero)
        q2_sc[h, tq:2 * tq, :] = jnp.where(lo, zero, q)
    m_sc[...] = jnp.full(m_sc.shape, NEG_BIG, F32)
    acc_sc[...] = jnp.zeros_like(acc_sc)

    def step(k0, tk, near):
        for h in heads:
            st_sc[h, 0:tk, :] = _dot(k_ref[pl.ds(k0, tk), h * LANES:(h + 1) * LANES], q2_sc[h], NT)
        pvs, alphas = [], []
        for h in heads:
            m_old = m_sc[h]
            if near is None:
                st = st_sc[h, 0:tk, :]
                far = relb_ref[h, NUM_BUCKETS - 1] * LOG2E
                m_new = jnp.maximum(m_old, jnp.max(st, axis=0, keepdims=True) + far)
                p = jnp.exp2(st - (m_new - far))
            else:
                st = st_sc[h, 0:tk, :] + bias_sc[h, near]
                m_new = jnp.maximum(m_old, jnp.max(st, axis=0, keepdims=True))
                p = jnp.exp2(st - m_new)
            m_sc[h] = m_new
            pvs.append(_dot(vt_ref[h, :, pl.ds(k0, tk)], p.astype(BF16)))
            alphas.append(jnp.exp2(m_old - m_new))
        for h in heads:
            acc_sc[h] = alphas[h] * acc_sc[h] + pvs[h]

    n_far = jnp.maximum(qi - 1, 0)
    far_tk = 2 * tq

    def far_body(i, carry):
        step(pl.multiple_of(i * far_tk, far_tk), far_tk, None)
        return carry

    lax.fori_loop(0, n_far // 2, far_body, 0)

    @pl.when(n_far % 2 == 1)
    def _():
        step(pl.multiple_of((n_far - 1) * tq, tq), tq, None)

    @pl.when(qi >= 1)
    def _():
        step(pl.multiple_of((qi - 1) * tq, tq), tq, 1)

    step(pl.multiple_of(qi * tq, tq), tq, 0)

    lam = (jnp.exp(jnp.sum(lq1_ref[...] * lk1_ref[...], axis=-1, keepdims=True))
           - jnp.exp(jnp.sum(lq2_ref[...] * lk2_ref[...], axis=-1, keepdims=True)) + lam_init)
    for h in heads:
        acc = acc_sc[h]
        l = acc[ACC_L:ACC_L + 1, :]
        o1 = acc[0:LANES, 0:tq] / l[:, 0:tq]
        o2 = acc[0:LANES, tq:2 * tq] / l[:, tq:2 * tq]
        ot = o1 - lam * o2
        ms = jnp.mean(ot * ot, axis=0, keepdims=True)
        y = ot * lax.rsqrt(ms + NORM_EPS) * sg_ref[...] * (1.0 - lam_init)
        o_ref[:, h * LANES:(h + 1) * LANES] = y.T.astype(o_ref.dtype)


def _diff_attn(q, k, vt, relb_t, lq1, lk1, lq2, lk2, sg_col, lam_init):
    B, S, width = q.shape
    H = width // LANES
    t = ATT_TILE
    buckets = jnp.asarray(_near_bucket_tiles(t))
    vec = pl.BlockSpec((1, HEAD_DIM), lambda b, i: (0, 0))
    return pl.pallas_call(
        functools.partial(_attn_kernel, lam_init=lam_init),
        grid=(B, S // t),
        in_specs=[
            pl.BlockSpec(memory_space=pltpu.SMEM),
            pl.BlockSpec((2, t, t), lambda b, i: (0, 0, 0)),
            vec, vec, vec, vec,
            pl.BlockSpec((LANES, 1), lambda b, i: (0, 0)),
            pl.BlockSpec((None, t, width), lambda b, i: (b, i, 0)),
            pl.BlockSpec((None, S, width), lambda b, i: (b, 0, 0)),
            pl.BlockSpec((None, H, ACC_ROWS, S), lambda b, i: (b, 0, 0, 0)),
        ],
        out_specs=pl.BlockSpec((None, t, width), lambda b, i: (b, i, 0)),
        out_shape=jax.ShapeDtypeStruct((B, S, width), BF16),
        scratch_shapes=[
            pltpu.VMEM((H, 2, t, 2 * t), F32),
            pltpu.VMEM((H, 2 * t, LANES), BF16),
            pltpu.VMEM((H, 1, 2 * t), F32),
            pltpu.VMEM((H, ACC_ROWS, 2 * t), F32),
            pltpu.VMEM((H, 2 * t, 2 * t), F32),
        ],
        compiler_params=_cparams(("arbitrary", "arbitrary")),
        name="diff_attn",
    )(relb_t, buckets, lq1, lk1, lq2, lk2, sg_col, q, k, vt)


def _outproj_kernel(x_ref, ya_ref, yb_ref, wa_ref, wb_ref, g_ref, rw_ref, rb_ref,
                    x1_ref, h2_ref, gd_ref):
    x1 = x_ref[...] + _dot(ya_ref[...], wa_ref[...]) + _dot(yb_ref[...], wb_ref[...])
    x1_ref[...] = x1
    ms = jnp.mean(x1 * x1, axis=-1, keepdims=True)
    h2 = x1 * lax.rsqrt(ms + NORM_EPS) * g_ref[...]
    h2_ref[...] = h2.astype(h2_ref.dtype)
    logits = _mm(h2, rw_ref[...], mode="x3") + rb_ref[...]
    lane = lax.broadcasted_iota(jnp.int32, logits.shape, 1).astype(F32)
    work = logits
    picks = []
    for _ in range(TOP_K):
        m = jnp.max(work, axis=-1, keepdims=True)
        idx = jnp.min(jnp.where(work == m, lane, float(LANES)), axis=-1, keepdims=True)
        hit = lane == idx
        picks.append((m, hit))
        work = jnp.where(hit, NEG_BIG, work)
    m0 = picks[0][0]
    es = [jnp.exp(m - m0) for m, _ in picks]
    denom = es[0] + es[1] + es[2] + es[3]
    gd = jnp.zeros_like(logits)
    for e, (_, hit) in zip(es, picks):
        gd = jnp.where(hit, e / denom, gd)
    gd_ref[...] = gd


def _outproj(x2d, ya, yb, wa, wb, g2, rw, rb):
    T, D = x2d.shape
    half = ya.shape[1]
    tm = PROJ_ROWS
    row = lambda i: (i, 0)
    fix = lambda i: (0, 0)
    return pl.pallas_call(
        _outproj_kernel,
        grid=(T // tm,),
        in_specs=[
            pl.BlockSpec((tm, D), row),
            pl.BlockSpec((tm, half), row),
            pl.BlockSpec((tm, half), row),
            pl.BlockSpec((half, D), fix),
            pl.BlockSpec((half, D), fix),
            pl.BlockSpec((1, D), fix),
            pl.BlockSpec((D, LANES), fix),
            pl.BlockSpec((1, LANES), fix),
        ],
        out_specs=[
            pl.BlockSpec((tm, D), row),
            pl.BlockSpec((tm, D), row),
            pl.BlockSpec((tm, LANES), row),
        ],
        out_shape=[
            jax.ShapeDtypeStruct((T, D), F32),
            jax.ShapeDtypeStruct((T, D), BF16),
            jax.ShapeDtypeStruct((T, LANES), F32),
        ],
        compiler_params=_cparams(("arbitrary",)),
        name="outproj_router",
    )(x2d, ya, yb, wa, wb, g2, rw, rb)


def _padded_rows(cnt, rb):
    return jnp.ceil(cnt * (1.0 / rb)) * rb


def _route_kernel(gd_ref, lp_ref, gate_ref, tab_ref, cnt_ref, carry_sc, pstart_sc, *, rb):
    ph = pl.program_id(0)
    i = pl.program_id(1)
    tb = gd_ref.shape[0]
    gd = gd_ref[...]
    sel = gd > 0.0
    self = jnp.where(sel, 1.0, 0.0)
    colsum = jnp.sum(self, axis=0, keepdims=True)
    run_pad = jnp.ceil(colsum * (1.0 / RUN_CHUNK)) * RUN_CHUNK
    r2 = lax.broadcasted_iota(jnp.int32, (LANES, LANES), 0)
    c2 = lax.broadcasted_iota(jnp.int32, (LANES, LANES), 1)

    @pl.when((ph == 0) & (i == 0))
    def _():
        carry_sc[...] = jnp.zeros_like(carry_sc)

    @pl.when(ph == 0)
    def _():
        carry_sc[...] = carry_sc[...] + run_pad

    @pl.when((ph == 1) & (i == 0))
    def _():
        cnt = carry_sc[...]
        upper = jnp.where(r2 < c2, 1.0, 0.0)
        pstart = _dot(jnp.broadcast_to(_padded_rows(cnt, rb), (8, LANES)), upper, prec=HP)[0:1, :]
        pstart_sc[...] = pstart
        cnt_ref[0:1, :] = cnt
        cnt_ref[1:2, :] = pstart
        cnt_ref[2:8, :] = jnp.zeros((6, LANES), F32)
        carry_sc[...] = jnp.zeros_like(carry_sc)

    @pl.when(ph == 1)
    def _():
        rr = lax.broadcasted_iota(jnp.int32, (tb, tb), 0)
        cc = lax.broadcasted_iota(jnp.int32, (tb, tb), 1)
        lower = jnp.where(cc < rr, 1.0, 0.0).astype(BF16)
        sel_b = self.astype(BF16)
        prefix = _dot(lower, sel_b)
        upper_b = jnp.where(r2 < c2, 1.0, 0.0).astype(BF16)
        off = _dot(jnp.broadcast_to(run_pad, (8, LANES)).astype(BF16), upper_b)[0:1, :]
        lp = off + prefix
        upper_incl = jnp.where(r2 <= c2, 1.0, 0.0).astype(BF16)
        slot = _dot(sel_b, upper_incl)
        lane = lax.broadcasted_iota(jnp.int32, (tb, LANES), 1)
        p_out = jnp.zeros((tb, LANES), F32)
        g_out = jnp.zeros((tb, LANES), F32)
        for kk in range(TOP_K):
            mk = sel & (slot == float(kk + 1))
            p_k = jnp.sum(jnp.where(mk, lp, 0.0), axis=-1, keepdims=True)
            g_k = jnp.sum(jnp.where(mk, gd, 0.0), axis=-1, keepdims=True)
            p_out = jnp.where(lane == kk, p_k, p_out)
            g_out = jnp.where(lane == kk, g_k, g_out)
        lp_ref[...] = p_out
        gate_ref[...] = g_out
        tab_ref[0:1, :] = (run_pad * (1.0 / RUN_CHUNK)).astype(jnp.int32)
        tab_ref[1:2, :] = off.astype(jnp.int32)
        tab_ref[2:3, :] = (pstart_sc[...] + carry_sc[...]).astype(jnp.int32)
        tab_ref[3:8, :] = jnp.zeros((5, LANES), jnp.int32)
        carry_sc[...] = carry_sc[...] + run_pad


def _route(gd, rb):
    T = gd.shape[0]
    tb = MOVE_ROWS
    blk = lambda ph, i: (i * ph, 0)
    return pl.pallas_call(
        functools.partial(_route_kernel, rb=rb),
        grid=(2, T // tb),
        in_specs=[pl.BlockSpec((tb, LANES), lambda ph, i: (i, 0))],
        out_specs=[
            pl.BlockSpec((tb, LANES), blk),
            pl.BlockSpec((tb, LANES), blk),
            pl.BlockSpec((8, LANES), blk),
            pl.BlockSpec((8, LANES), lambda ph, i: (0, 0)),
        ],
        out_shape=[
            jax.ShapeDtypeStruct((T, LANES), F32),
            jax.ShapeDtypeStruct((T, LANES), F32),
            jax.ShapeDtypeStruct((T // tb * 8, LANES), jnp.int32),
            jax.ShapeDtypeStruct((8, LANES), F32),
        ],
        scratch_shapes=[pltpu.VMEM((1, LANES), F32), pltpu.VMEM((1, LANES), F32)],
        compiler_params=_cparams(("arbitrary", "arbitrary")),
        name="route_positions",
    )(gd)


def _chunk_copy(src_ref, src_row, dst_ref, dst_row, sem):
    return pltpu.make_async_copy(src_ref.at[pl.ds(src_row, RUN_CHUNK), :],
                                 dst_ref.at[pl.ds(dst_row, RUN_CHUNK), :], sem)


def _move_runs(b, nc_ref, off_ref, d0_ref, copy_chunk, wait_chunk):
    total = 0
    for e in range(N_EXPERTS):
        n_chunks = nc_ref[b * N_EXPERTS + e]
        s0 = off_ref[b * N_EXPERTS + e]
        d0 = d0_ref[b * N_EXPERTS + e]

        def issue(c, carry, s0=s0, d0=d0):
            copy_chunk(pl.multiple_of(s0 + c * RUN_CHUNK, RUN_CHUNK),
                       pl.multiple_of(d0 + c * RUN_CHUNK, RUN_CHUNK)).start()
            return carry

        lax.fori_loop(0, n_chunks, issue, 0)
        total = total + n_chunks

    def drain(c, carry):
        wait_chunk().wait()
        return carry

    lax.fori_loop(0, total, drain, 0)


def _local_onehot(lp, lb):
    j = lax.broadcasted_iota(jnp.int32, (lp.shape[0], lb), 1).astype(F32)
    hit = lp[:, 0:1] == j
    for kk in range(1, TOP_K):
        hit = hit | (lp[:, kk:kk + 1] == j)
    return hit


def _scatter_kernel(nc_ref, off_ref, d0_ref, t0_ref, tc_ref, lp_ref, h_ref, xs_ref, buf, zero_buf, sem):
    b = pl.program_id(0)
    pt = jnp.where(_local_onehot(lp_ref[...], buf.shape[0]), 1.0, 0.0).astype(BF16)
    buf[...] = _dot(pt, h_ref[...], TN)
    _move_runs(b, nc_ref, off_ref, d0_ref,
               lambda s, d: _chunk_copy(buf, s, xs_ref, d, sem),
               lambda: _chunk_copy(buf, 0, xs_ref, 0, sem))

    @pl.when(b == pl.num_programs(0) - 1)
    def _():
        zero_buf[...] = jnp.zeros_like(zero_buf)
        total = 0
        for e in range(N_EXPERTS + 1):
            t0 = t0_ref[e]

            def issue(c, carry, t0=t0):
                _chunk_copy(zero_buf, 0, xs_ref, pl.multiple_of(t0 + c * RUN_CHUNK, RUN_CHUNK), sem).start()
                return carry

            lax.fori_loop(0, tc_ref[e], issue, 0)
            total = total + tc_ref[e]

        def drain(c, carry):
            _chunk_copy(zero_buf, 0, xs_ref, 0, sem).wait()
            return carry

        lax.fori_loop(0, total, drain, 0)


def _scatter_rows(tabs, tails, lp, h2, n_rows):
    T, D = h2.shape
    tb = MOVE_ROWS
    return pl.pallas_call(
        _scatter_kernel,
        grid_spec=pltpu.PrefetchScalarGridSpec(
            num_scalar_prefetch=5,
            grid=(T // tb,),
            in_specs=[
                pl.BlockSpec((tb, LANES), lambda i, *_: (i, 0)),
                pl.BlockSpec((tb, D), lambda i, *_: (i, 0)),
            ],
            out_specs=pl.BlockSpec(memory_space=pl.ANY),
            scratch_shapes=[pltpu.VMEM((LOCAL_ROWS, D), F32), pltpu.VMEM((RUN_CHUNK, D), F32),
                            pltpu.SemaphoreType.DMA],
        ),
        out_shape=jax.ShapeDtypeStruct((n_rows, D), F32),
        compiler_params=_cparams(("arbitrary",)),
        name="scatter_rows",
    )(*tabs, *tails, lp, h2)


def _moe_kernel(be_ref, na_ref, xs_ref, w1_ref, b1_ref, w2_ref, b2_ref, ys_ref):
    del be_ref
    j = pl.program_id(0)

    @pl.when(j < na_ref[0])
    def _():
        xb = xs_ref[...].astype(BF16)
        hid = _dot(xb, w1_ref[...]) + b1_ref[...]
        half = hid.shape[1] // 2
        x_glu = jnp.minimum(hid[:, :half], SWIGLU_LIMIT)
        x_lin = jnp.clip(hid[:, half:], -SWIGLU_LIMIT, SWIGLU_LIMIT)
        act = x_glu * _sigmoid(SWIGLU_ALPHA * x_glu) * (x_lin + 1.0)
        ys_ref[...] = _dot(act.astype(BF16), w2_ref[...]) + b2_ref[...]

    @pl.when(j >= na_ref[0])
    def _():
        ys_ref[...] = jnp.zeros_like(ys_ref)


def _moe(blk_expert, n_active, xs, w1, b1, w2, b2):
    R, D = xs.shape
    E, _, D2 = w1.shape
    rb = MOE_ROWS
    nb = R // rb
    blk = lambda j, be, na: (jnp.minimum(j, na[0] - 1), 0)
    return pl.pallas_call(
        _moe_kernel,
        grid_spec=pltpu.PrefetchScalarGridSpec(
            num_scalar_prefetch=2,
            grid=(nb,),
            in_specs=[
                pl.BlockSpec((rb, D), blk),
                pl.BlockSpec((None, D, D2), lambda j, be, na: (be[j], 0, 0)),
                pl.BlockSpec((None, 1, D2), lambda j, be, na: (be[j], 0, 0)),
                pl.BlockSpec((None, D2 // 2, D), lambda j, be, na: (be[j], 0, 0)),
                pl.BlockSpec((None, 1, D), lambda j, be, na: (be[j], 0, 0)),
            ],
            out_specs=pl.BlockSpec((rb, D), lambda j, be, na: (j, 0)),
        ),
        out_shape=jax.ShapeDtypeStruct((R, D), F32),
        compiler_params=_cparams(("arbitrary",)),
        name="moe_experts",
    )(blk_expert, n_active, xs, w1, b1, w2, b2)


def _combine_kernel(nc_ref, off_ref, d0_ref, lp_ref, gate_ref, x1_ref, ys_ref, o_ref, buf, sem):
    b = pl.program_id(0)

    @pl.when(b == 0)
    def _():
        buf[...] = jnp.zeros_like(buf)

    _move_runs(b, nc_ref, off_ref, d0_ref,
               lambda s, d: _chunk_copy(ys_ref, d, buf, s, sem),
               lambda: _chunk_copy(ys_ref, 0, buf, 0, sem))
    lp = lp_ref[...]
    g = gate_ref[...]
    lb, d = buf.shape
    j = lax.broadcasted_iota(jnp.int32, (lp.shape[0], lb), 1).astype(F32)
    gt = jnp.zeros(j.shape, F32)
    for kk in range(TOP_K):
        gt = jnp.where(lp[:, kk:kk + 1] == j, g[:, kk:kk + 1], gt)
    g_hi, g_lo = _split2(gt)
    ones = jnp.ones((lp.shape[0], LANES), BF16)
    g_row = _dot(g_hi, ones, TN) + _dot(g_lo, ones, TN)
    rows = (buf[...] * jnp.concatenate([g_row] * (d // LANES), axis=1)).astype(BF16)
    p01 = jnp.where(gt != 0.0, 1.0, 0.0).astype(BF16)
    o_ref[...] = x1_ref[...] + _dot(p01, rows)


def _combine(tabs, lp, gate, x1, ys):
    T, D = x1.shape
    tb = MOVE_ROWS
    return pl.pallas_call(
        _combine_kernel,
        grid_spec=pltpu.PrefetchScalarGridSpec(
            num_scalar_prefetch=3,
            grid=(T // tb,),
            in_specs=[
                pl.BlockSpec((tb, LANES), lambda i, *_: (i, 0)),
                pl.BlockSpec((tb, LANES), lambda i, *_: (i, 0)),
                pl.BlockSpec((tb, D), lambda i, *_: (i, 0)),
                pl.BlockSpec(memory_space=pl.ANY),
            ],
            out_specs=pl.BlockSpec((tb, D), lambda i, *_: (i, 0)),
            scratch_shapes=[pltpu.VMEM((LOCAL_ROWS, D), F32), pltpu.SemaphoreType.DMA],
        ),
        out_shape=jax.ShapeDtypeStruct((T, D), F32),
        compiler_params=_cparams(("arbitrary",)),
        name="combine_rows",
    )(*tabs, lp, gate, x1, ys)


def _pad_cols(a, n):
    return jnp.pad(a, ((0, 0), (0, n - a.shape[1])))


def _pad_rows(a, n):
    return jnp.pad(a, ((0, n - a.shape[0]), (0, 0)))


def _lora_weight(w2, a2, g2, width):
    z = jnp.zeros((LANES, width), F32)
    return jnp.concatenate([
        jnp.concatenate([_pad_rows(w2, LANES), z, z], axis=1),
        jnp.concatenate([z, _pad_rows(a2, LANES), z], axis=1),
        jnp.concatenate([z, z, _pad_rows(g2, LANES)], axis=1),
    ], axis=0).astype(BF16)


def _split_rwkv_cols(a, width):
    o = 3 * width
    return jnp.concatenate([
        a[:, :o],
        _pad_cols(a[:, o:o + DECAY_LORA], LANES),
        _pad_cols(a[:, o + DECAY_LORA:o + DECAY_LORA + AAA_LORA], LANES),
        _pad_cols(a[:, o + DECAY_LORA + AAA_LORA:o + DECAY_LORA + AAA_LORA + GATE_LORA], LANES),
    ], axis=1)


def kernel(x, norm1_g, w_in, mu_shift, w0, w2, a0, a2, g2, k_k, k_a, r_k, lnx_g, lnx_b, qn_g, kn_g,
           lam_q1, lam_k1, lam_q2, lam_k2, subln_g, rel_bias, w_out, norm2_g, router_w, router_b,
           exp_w1, exp_b1, exp_w2, exp_b2):
    B, S, D = x.shape
    T = B * S
    depth = norm1_g.shape[0]
    width = w0.shape[1]
    n_rwkv_cols = 3 * width + DECAY_LORA + AAA_LORA + GATE_LORA
    n_rwkv_pad = 3 * width + 3 * LANES
    n_groups = width // HEAD_DIM
    H = width // LANES
    relb_t = rel_bias.T
    row = lambda a: a.reshape(1, -1)

    for layer in range(depth):
        lam_init = 0.8 - 0.6 * math.exp(-0.3 * layer)
        w_l = w_in[layer]
        w_all = jnp.concatenate([_split_rwkv_cols(w_l[:, :n_rwkv_cols], width), w_l[:, n_rwkv_cols:]],
                                axis=1).astype(BF16)
        mu = _split_rwkv_cols(row(mu_shift[layer]), width)
        qg = row(jnp.tile(qn_g[layer], n_groups))
        kg = row(jnp.tile(kn_g[layer], n_groups))

        pr, q, k, v = _inproj(x.reshape(T, D), row(norm1_g[layer]), w_all, qg, kg, n_rwkv_pad, width)

        y_a = _rwkv(pr.reshape(B, S, n_rwkv_pad), mu, row(w0[layer]),
                    _lora_weight(w2[layer], a2[layer], g2[layer], width), row(a0[layer]),
                    row(k_k[layer]), row(k_a[layer]), row(r_k[layer]), row(lnx_g[layer]),
                    row(lnx_b[layer]), width)

        vt = jnp.concatenate([v.reshape(B, S, H, LANES).transpose(0, 2, 3, 1),
                              jnp.ones((B, H, ACC_ROWS - LANES, S), BF16)], axis=2)
        y_b = _diff_attn(q.reshape(B, S, width), k.reshape(B, S, width), vt, relb_t,
                         row(lam_q1[layer]), row(lam_k1[layer]), row(lam_q2[layer]), row(lam_k2[layer]),
                         subln_g[layer].reshape(-1, 1), lam_init)

        wo = w_out[layer].astype(BF16)
        rw = _pad_cols(router_w[layer], LANES)
        rb = jnp.concatenate([router_b[layer], jnp.full((LANES - N_EXPERTS,), NEG_BIG, F32)]).reshape(1, -1)
        x1, h2, gd = _outproj(x.reshape(T, D), y_a.reshape(T, width), y_b.reshape(T, width),
                              wo[:width], wo[width:], row(norm2_g[layer]), rw, rb)

        lp, gate, tab, cnt = _route(gd, MOE_ROWS)
        tab = tab.reshape(-1, 8, LANES)[:, :, :N_EXPERTS]
        tabs = tuple(tab[:, r, :].reshape(-1) for r in range(3))
        run_end = cnt[1, :N_EXPERTS] + cnt[0, :N_EXPERTS]
        pad_end = cnt[1, :N_EXPERTS] + _padded_rows(cnt[0, :N_EXPERTS], MOE_ROWS)
        run_pad_max = (T // MOVE_ROWS) * N_EXPERTS * (RUN_CHUNK - 1)
        n_blocks = -(-(T * TOP_K + run_pad_max + N_EXPERTS * (MOE_ROWS - 1)) // MOE_ROWS)
        gap_start = jnp.concatenate([run_end, pad_end[-1:]])
        gap_end = jnp.concatenate([pad_end, jnp.full((1,), n_blocks * MOE_ROWS, F32)])
        tails = (gap_start.astype(jnp.int32), ((gap_end - gap_start) / RUN_CHUNK).astype(jnp.int32))
        blk_start = (jnp.arange(n_blocks) * MOE_ROWS).astype(F32)
        blk_expert = jnp.minimum(jnp.sum(pad_end[None, :] <= blk_start[:, None], axis=1), N_EXPERTS - 1)
        n_active = (pad_end[N_EXPERTS - 1] / MOE_ROWS).astype(jnp.int32).reshape(1)

        xs = _scatter_rows(tabs, tails, lp, h2, n_blocks * MOE_ROWS)
        ys = _moe(blk_expert.astype(jnp.int32), n_active, xs,
                  exp_w1[layer].astype(BF16), exp_b1[layer][:, None, :],
                  exp_w2[layer].astype(BF16), exp_b2[layer][:, None, :])
        x = _combine(tabs, lp, gate, x1, ys).reshape(B, S, D)
    return x
```

```python
import functools
import math

import numpy as np
import jax
import jax.numpy as jnp
from jax import lax
from jax.experimental import pallas as pl
from jax.experimental.pallas import tpu as pltpu

F32 = jnp.float32
BF16 = jnp.bfloat16
HP = lax.Precision.HIGHEST

HEAD_DIM = 64
DECAY_LORA = 32
AAA_LORA = 32
GATE_LORA = 96
NUM_BUCKETS = 32
MAX_DISTANCE = 128
N_EXPERTS = 32
TOP_K = 4
SWIGLU_LIMIT = 7.0
SWIGLU_ALPHA = 1.702
NORM_EPS = 1e-5
LNX_EPS = 64e-5

LANES = 128
VMEM_LIMIT_BYTES = 56 * 1024 * 1024

NEG_BIG = -1e30
LOG2E = math.log2(math.e)
ACC_L = LANES
ACC_ROWS = LANES + 8

PROJ_ROWS = 256
RWKV_CHUNK = 64
RWKV_ROWS = 256
RWKV_SEQS = 2
ATT_TILE = 256
MOE_ROWS = 256
MOVE_ROWS = 256
RUN_CHUNK = 8
LOCAL_ROWS = MOVE_ROWS * 4 + 32 * RUN_CHUNK

NN = (((1,), (0,)), ((), ()))
NT = (((1,), (1,)), ((), ()))
TN = (((0,), (0,)), ((), ()))


def _dot(a, b, dims=NN, prec=None):
    return lax.dot_general(a, b, dims, precision=prec, preferred_element_type=F32)


def _cparams(sem):
    return pltpu.CompilerParams(dimension_semantics=sem, vmem_limit_bytes=VMEM_LIMIT_BYTES)


def _sigmoid(x):
    return 1.0 / (1.0 + jnp.exp(-x))


def _lane_lo(shape):
    lane = lax.broadcasted_iota(jnp.int32, shape, len(shape) - 1)
    return (lane % LANES) < HEAD_DIM


def _head_rms_normalize(x, gain, scale):
    outs = []
    for t in range(x.shape[1] // LANES):
        xt = x[:, t * LANES:(t + 1) * LANES]
        lo = _lane_lo(xt.shape)
        x2 = xt * xt
        s0 = jnp.sum(jnp.where(lo, x2, 0.0), axis=-1, keepdims=True)
        s1 = jnp.sum(jnp.where(lo, 0.0, x2), axis=-1, keepdims=True)
        inv = lax.rsqrt(jnp.where(lo, s0, s1) * (1.0 / HEAD_DIM) + NORM_EPS)
        outs.append(xt * inv)
    return jnp.concatenate(outs, axis=-1) * gain * scale


def _inproj_kernel(x_ref, g_ref, w_ref, qg_ref, kg_ref, pr_ref, q_ref, k_ref, v_ref, *, n_rwkv, width):
    x = x_ref[...]
    ms = jnp.mean(x * x, axis=-1, keepdims=True)
    h = (x * lax.rsqrt(ms + NORM_EPS) * g_ref[...]).astype(BF16)
    step = 384
    for c0 in range(0, n_rwkv, step):
        pr_ref[:, c0:c0 + step] = _dot(h, w_ref[:, c0:c0 + step])
    q = _dot(h, w_ref[:, n_rwkv:n_rwkv + width])
    q_ref[...] = _head_rms_normalize(q, qg_ref[...], HEAD_DIM ** -0.5 * LOG2E).astype(BF16)
    k = _dot(h, w_ref[:, n_rwkv + width:n_rwkv + 2 * width])
    k_ref[...] = _head_rms_normalize(k, kg_ref[...], 1.0).astype(BF16)
    v_ref[...] = _dot(h, w_ref[:, n_rwkv + 2 * width:n_rwkv + 3 * width]).astype(BF16)


def _inproj(x2d, g, w_all, qg, kg, n_rwkv, width):
    T, D = x2d.shape
    n_all = w_all.shape[1]
    tm = PROJ_ROWS
    row = lambda i: (i, 0)
    fix = lambda i: (0, 0)
    return pl.pallas_call(
        functools.partial(_inproj_kernel, n_rwkv=n_rwkv, width=width),
        grid=(T // tm,),
        in_specs=[
            pl.BlockSpec((tm, D), row),
            pl.BlockSpec((1, D), fix),
            pl.BlockSpec((D, n_all), fix),
            pl.BlockSpec((1, width), fix),
            pl.BlockSpec((1, width), fix),
        ],
        out_specs=[
            pl.BlockSpec((tm, n_rwkv), row),
            pl.BlockSpec((tm, width), row),
            pl.BlockSpec((tm, width), row),
            pl.BlockSpec((tm, width), row),
        ],
        out_shape=[
            jax.ShapeDtypeStruct((T, n_rwkv), F32),
            jax.ShapeDtypeStruct((T, width), BF16),
            jax.ShapeDtypeStruct((T, width), BF16),
            jax.ShapeDtypeStruct((T, width), BF16),
        ],
        compiler_params=_cparams(("arbitrary",)),
        name="inproj",
    )(x2d, g, w_all, qg, kg)


def _split2(x):
    hi = x.astype(BF16)
    return hi, (x - hi.astype(F32)).astype(BF16)


def _mm(a, b, dims=NN, mode="bf16"):
    if mode == "bf16":
        return _dot(a.astype(BF16), b.astype(BF16), dims)
    ah, al = _split2(a)
    bh, bl = _split2(b)
    lhs = jnp.concatenate([ah, ah, al], axis=dims[0][0][0])
    rhs = jnp.concatenate([bh, bl, bh], axis=dims[0][1][0])
    return _dot(lhs, rhs, dims)


def _group_sum(x, ones2):
    hi, lo = _split2(x)
    return _dot(jnp.concatenate([hi, lo], axis=1), ones2)


P_INV = "x3"
P_SUBST = "bf16"
P_STATE = "bf16"
INV_BLOCK = 16
QUAD = 256


def _rwkv_kernel(pr_ref, mu_ref, w0_ref, wl_ref, a0_ref, kk_ref, ka_ref, rk_ref, lg_ref, lb_ref,
                 tri3_ref, ones2_ref, o_ref, state_sc, prev_sc, *, width):
    C = RWKV_CHUNK
    n_seq, seq_rows, _ = pr_ref.shape
    rows = n_seq * seq_rows
    n_quads = width // QUAD
    n_chunks = seq_rows // C
    heads_q = QUAD // HEAD_DIM

    @pl.when(pl.program_id(1) == 0)
    def _():
        state_sc[...] = jnp.zeros_like(state_sc)
        prev_sc[...] = jnp.zeros_like(prev_sc)

    x = jnp.concatenate([pr_ref[s] for s in range(n_seq)], axis=0)
    rid = lax.broadcasted_iota(jnp.int32, x.shape, 0)
    prev = pltpu.roll(x, 1, axis=0)
    for s in range(n_seq):
        prev = jnp.where(rid == s * seq_rows, prev_sc[s:s + 1, :], prev)
        prev_sc[s:s + 1, :] = x[(s + 1) * seq_rows - 1:(s + 1) * seq_rows, :]
    xs = x + (prev - x) * mu_ref[...]

    r = xs[:, 0:width]
    k = xs[:, width:2 * width]
    v = xs[:, 2 * width:3 * width]
    o = 3 * width
    lora_in = jnp.concatenate([jnp.tanh(xs[:, o:o + LANES]), xs[:, o + LANES:o + 2 * LANES],
                               _sigmoid(xs[:, o + 2 * LANES:o + 3 * LANES])], axis=1).astype(BF16)
    lora = _dot(lora_in, wl_ref[...])
    z = -(w0_ref[...] + lora[:, 0:width])
    softplus = jnp.maximum(z, 0.0) + jnp.log(1.0 + jnp.exp(-jnp.abs(z)))
    log_decay = -jnp.exp(-softplus - 0.5)
    a_lr = _sigmoid(a0_ref[...] + lora[:, width:2 * width])
    gate = lora[:, 2 * width:3 * width]

    ones2 = ones2_ref[...]
    kk = k * kk_ref[...]
    kk = kk * lax.rsqrt(jnp.maximum(_group_sum(kk * kk, ones2), 1e-24))
    k = k * (1.0 + (a_lr - 1.0) * ka_ref[...])
    a_vec = -kk
    b_vec = kk * a_lr

    p1 = log_decay.astype(BF16)
    rem = log_decay - p1.astype(F32)
    p2 = rem.astype(BF16)
    p3 = (rem - p2.astype(F32)).astype(BF16)
    cum = _dot(tri3_ref[...], jnp.concatenate([p1, p2, p3], axis=0))

    t_i = lax.broadcasted_iota(jnp.int32, (C, 2 * QUAD), 0)
    s_i = lax.broadcasted_iota(jnp.int32, (C, 2 * QUAD), 1) % C
    strict2 = t_i > s_i
    incl2 = t_i >= s_i
    lane_head = lax.broadcasted_iota(jnp.int32, (C, QUAD), 1) // HEAD_DIM
    row_blk = lax.broadcasted_iota(jnp.int32, (C, QUAD), 0) // INV_BLOCK
    same_blk = row_blk == ((lax.broadcasted_iota(jnp.int32, (C, QUAD), 1) % C) // INV_BLOCK)
    r2 = lax.broadcasted_iota(jnp.int32, (QUAD, QUAD), 0)
    c2 = lax.broadcasted_iota(jnp.int32, (QUAD, QUAD), 1)
    same_head = (r2 // HEAD_DIM) == (c2 // HEAD_DIM)

    def by_head(m):
        zero = jnp.zeros_like(m)
        return jnp.concatenate([jnp.where(lane_head == h, m, zero) for h in range(heads_q)], axis=0)

    def bd_rhs(m, mode):
        if mode == "bf16":
            return by_head(m.astype(BF16))
        hi, lo = _split2(m)
        hi_bd = by_head(hi)
        return jnp.concatenate([hi_bd, by_head(lo), hi_bd], axis=0)

    def mm_bd(a, rhs, mode):
        if mode == "bf16":
            return _dot(a.astype(BF16), rhs)
        ah, al = _split2(a)
        return _dot(jnp.concatenate([ah, ah, al], axis=1), rhs)

    class Work:
        pass

    n_par = n_seq * n_quads
    works = []
    for c in range(n_chunks):
        for sq in range(n_par):
            w = Work()
            s, q = divmod(sq, n_quads)
            ls = slice(q * QUAD, (q + 1) * QUAD)
            rs = slice(s * seq_rows + c * C, s * seq_rows + (c + 1) * C)
            ld = log_decay[rs, ls]
            cm = cum[rs, ls]
            cl = cm[C - 1:C, :]
            e_in = jnp.exp(cm)
            e_neg = jnp.exp(-cm)
            e_hat = jnp.exp(cl - cm)
            w.q, w.c = sq, c
            w.dec = jnp.exp(cl)
            w.vv = v[rs, ls]
            w.ar_t = jnp.concatenate([a_vec[rs, ls] * jnp.exp(cm - ld), r[rs, ls] * e_in], axis=0)
            w.bk_t = jnp.concatenate([bd_rhs(b_vec[rs, ls] * e_neg, "bf16"), bd_rhs(k[rs, ls] * e_neg, "bf16")], axis=0)
            w.bk_h = jnp.concatenate([b_vec[rs, ls] * e_hat, k[rs, ls] * e_hat], axis=0)
            works.append(w)

    for w in works:
        gram = _dot(w.ar_t.astype(BF16), w.bk_t, NT)
        low = jnp.where(strict2, gram[0:C, :], 0.0)
        l_cat = low[:, 0:QUAD]
        rbk = jnp.where(incl2, gram[C:2 * C, :], 0.0)
        w.rb_cat = rbk[:, 0:QUAD]
        w.kv_lhs = jnp.concatenate([low[:, QUAD:], rbk[:, QUAD:]], axis=0)
        w.l_off = jnp.where(same_blk, 0.0, l_cat)
        w.d_inv = jnp.where(same_blk, l_cat, 0.0)

    for w in works:
        w.l_pow = mm_bd(w.d_inv, bd_rhs(w.d_inv, P_INV), P_INV)
    n_sq = int(math.log2(INV_BLOCK))
    for it in range(1, n_sq):
        for w in works:
            if it + 1 < n_sq:
                both = mm_bd(jnp.concatenate([w.d_inv, w.l_pow], axis=0), bd_rhs(w.l_pow, P_INV), P_INV)
                w.d_inv = w.d_inv + w.l_pow + both[0:C]
                w.l_pow = both[C:2 * C]
            else:
                w.d_inv = w.d_inv + w.l_pow + mm_bd(w.d_inv, bd_rhs(w.l_pow, P_INV), P_INV)
    for w in works:
        w.m1 = w.l_off + mm_bd(w.d_inv, bd_rhs(w.l_off, P_SUBST), P_SUBST)
        w.m1_bd = bd_rhs(w.m1, P_SUBST)
    for w in works:
        w.m2 = mm_bd(w.m1, w.m1_bd, P_SUBST)
    for w in works:
        w.nn = w.m1 + w.m2 + mm_bd(w.m2, w.m1_bd, P_SUBST)
    for w in works:
        w.t_inv = w.nn + w.d_inv + mm_bd(w.nn, bd_rhs(w.d_inv, P_SUBST), P_SUBST)
    for w in works:
        w.kv = mm_bd(w.kv_lhs, bd_rhs(w.vv, P_STATE), P_STATE)

    states = [state_sc[sq] for sq in range(n_par)]
    y_parts = [[] for _ in range(n_par)]
    for c in range(n_chunks):
        ws = works[c * n_par:(c + 1) * n_par]
        for w in ws:
            w.base = _mm(w.ar_t, states[w.q], NT, mode=P_STATE) + w.kv
        for w in ws:
            u0 = w.base[0:C]
            w.u = u0 + mm_bd(w.t_inv, bd_rhs(u0, P_STATE), P_STATE)
        for w in ws:
            upd = _mm(jnp.concatenate([w.u, w.vv], axis=0), w.bk_h, TN, mode=P_STATE)
            states[w.q] = states[w.q] * w.dec + jnp.where(same_head, upd, 0.0)
        for w in ws:
            y_parts[w.q].append(w.base[C:2 * C] + mm_bd(w.rb_cat, bd_rhs(w.u, P_STATE), P_STATE))
    for sq in range(n_par):
        state_sc[sq] = states[sq]
    y = jnp.concatenate([jnp.concatenate([jnp.concatenate(y_parts[s * n_quads + q], axis=0)
                                          for q in range(n_quads)], axis=-1)
                         for s in range(n_seq)], axis=0)

    inv_n = 1.0 / HEAD_DIM
    sums = _group_sum(jnp.concatenate([y, r * k * rk_ref[...]], axis=0), ones2)
    mean = sums[0:rows] * inv_n
    yc = y - mean
    var = _group_sum(yc * yc, ones2) * inv_n
    yn = yc * lax.rsqrt(var + LNX_EPS) * lg_ref[...] + lb_ref[...]
    out = ((yn + sums[rows:2 * rows] * v) * gate).astype(o_ref.dtype)
    for s in range(n_seq):
        o_ref[s] = out[s * seq_rows:(s + 1) * seq_rows]


def _rwkv(pr, mu, w0, w_lora, a0, k_k, k_a, r_k, lnx_g, lnx_b, width):
    B, S, n_rwkv = pr.shape
    rows = RWKV_ROWS
    n_seq = RWKV_SEQS
    C = RWKV_CHUNK
    rr = np.arange(n_seq * rows)[:, None]
    cc = np.arange(n_seq * rows)[None, :]
    tri = ((rr // C == cc // C) & (cc <= rr)).astype(np.float32)
    tri3 = jnp.asarray(np.concatenate([tri, tri, tri], axis=1), BF16)
    gg = np.arange(width)
    ones = (gg[:, None] // HEAD_DIM == gg[None, :] // HEAD_DIM).astype(np.float32)
    ones2 = jnp.asarray(np.concatenate([ones, ones], axis=0), BF16)
    fix = lambda shape: pl.BlockSpec(shape, lambda b, i: (0, 0))
    vec = lambda n: fix((1, n))
    return pl.pallas_call(
        functools.partial(_rwkv_kernel, width=width),
        grid=(B // n_seq, S // rows),
        in_specs=[
            pl.BlockSpec((n_seq, rows, n_rwkv), lambda b, i: (b, i, 0)),
            vec(n_rwkv), vec(width), fix(w_lora.shape), vec(width),
            vec(width), vec(width), vec(width), vec(width), vec(width),
            fix(tri3.shape), fix(ones2.shape),
        ],
        out_specs=pl.BlockSpec((n_seq, rows, width), lambda b, i: (b, i, 0)),
        out_shape=jax.ShapeDtypeStruct((B, S, width), BF16),
        scratch_shapes=[
            pltpu.VMEM((n_seq * (width // QUAD), QUAD, QUAD), F32),
            pltpu.VMEM((n_seq, n_rwkv), F32),
        ],
        compiler_params=_cparams(("arbitrary", "arbitrary")),
        name="rwkv7",
    )(pr, mu, w0, w_lora, a0, k_k, k_a, r_k, lnx_g, lnx_b, tri3, ones2)


def _t5_bucket_np(dist):
    n = np.maximum(dist, 0)
    max_exact = NUM_BUCKETS // 2
    nf = np.maximum(n, 1).astype(np.float32)
    large = max_exact + (np.log(nf / max_exact) / math.log(MAX_DISTANCE / max_exact)
                         * (NUM_BUCKETS - max_exact)).astype(np.int32)
    large = np.minimum(large, NUM_BUCKETS - 1)
    return np.where(n < max_exact, n, large).astype(np.int32)


def _near_bucket_tiles(tile):
    kpos = np.arange(tile)[:, None]
    qpos = np.arange(tile)[None, :]
    out = []
    for delta in (0, 1):
        dist = delta * tile + qpos - kpos
        out.append(np.where(dist >= 0, _t5_bucket_np(dist), -1))
    return np.stack(out).astype(np.int32)


def _attn_kernel(relb_ref, bucket_ref, lq1_ref, lk1_ref, lq2_ref, lk2_ref, sg_ref,
                 q_ref, k_ref, vt_ref, o_ref, bias_sc, q2_sc, m_sc, acc_sc, st_sc, *, lam_init):
    tq = q_ref.shape[0]
    n_heads = vt_ref.shape[0]
    heads = range(n_heads)
    qi = pl.program_id(1)

    @pl.when(qi == 0)
    def _():
        for h in heads:
            for d in range(2):
                bk = bucket_ref[d]
                tile = jnp.full(bk.shape, NEG_BIG, F32)
                for j in range(NUM_BUCKETS):
                    tile = jnp.where(bk == j, relb_ref[h, j] * LOG2E, tile)
                bias_sc[h, d, :, 0:tq] = tile
                bias_sc[h, d, :, tq:2 * tq] = tile

    lo = _lane_lo((tq, LANES))
    for h in heads:
        q = q_ref[:, h * LANES:(h + 1) * LANES]
        zero = jnp.zeros_like(q)
        q2_sc[h, 0:tq, :] = jnp.where(lo, q, zero)
        q2_sc[h, tq:2 * tq, :] = jnp.where(lo, zero, q)
    m_sc[...] = jnp.full(m_sc.shape, NEG_BIG, F32)
    acc_sc[...] = jnp.zeros_like(acc_sc)

    def step(k0, tk, near):
        for h in heads:
            st_sc[h, 0:tk, :] = _dot(k_ref[pl.ds(k0, tk), h * LANES:(h + 1) * LANES], q2_sc[h], NT)
        pvs, alphas = [], []
        for h in heads:
            m_old = m_sc[h]
            if near is None:
                st = st_sc[h, 0:tk, :]
                far = relb_ref[h, NUM_BUCKETS - 1] * LOG2E
                m_new = jnp.maximum(m_old, jnp.max(st, axis=0, keepdims=True) + far)
                p = jnp.exp2(st - (m_new - far))
            else:
                st = st_sc[h, 0:tk, :] + bias_sc[h, near]
                m_new = jnp.maximum(m_old, jnp.max(st, axis=0, keepdims=True))
                p = jnp.exp2(st - m_new)
            m_sc[h] = m_new
            pvs.append(_dot(vt_ref[h, :, pl.ds(k0, tk)], p.astype(BF16)))
            alphas.append(jnp.exp2(m_old - m_new))
        for h in heads:
            acc_sc[h] = alphas[h] * acc_sc[h] + pvs[h]

    n_far = jnp.maximum(qi - 1, 0)
    far_tk = 2 * tq

    def far_body(i, carry):
        step(pl.multiple_of(i * far_tk, far_tk), far_tk, None)
        return carry

    lax.fori_loop(0, n_far // 2, far_body, 0)

    @pl.when(n_far % 2 == 1)
    def _():
        step(pl.multiple_of((n_far - 1) * tq, tq), tq, None)

    @pl.when(qi >= 1)
    def _():
        step(pl.multiple_of((qi - 1) * tq, tq), tq, 1)

    step(pl.multiple_of(qi * tq, tq), tq, 0)

    lam = (jnp.exp(jnp.sum(lq1_ref[...] * lk1_ref[...], axis=-1, keepdims=True))
           - jnp.exp(jnp.sum(lq2_ref[...] * lk2_ref[...], axis=-1, keepdims=True)) + lam_init)
    for h in heads:
        acc = acc_sc[h]
        l = acc[ACC_L:ACC_L + 1, :]
        o1 = acc[0:LANES, 0:tq] / l[:, 0:tq]
        o2 = acc[0:LANES, tq:2 * tq] / l[:, tq:2 * tq]
        ot = o1 - lam * o2
        ms = jnp.mean(ot * ot, axis=0, keepdims=True)
        y = ot * lax.rsqrt(ms + NORM_EPS) * sg_ref[...] * (1.0 - lam_init)
        o_ref[:, h * LANES:(h + 1) * LANES] = y.T.astype(o_ref.dtype)


def _diff_attn(q, k, vt, relb_t, lq1, lk1, lq2, lk2, sg_col, lam_init):
    B, S, width = q.shape
    H = width // LANES
    t = ATT_TILE
    buckets = jnp.asarray(_near_bucket_tiles(t))
    vec = pl.BlockSpec((1, HEAD_DIM), lambda b, i: (0, 0))
    return pl.pallas_call(
        functools.partial(_attn_kernel, lam_init=lam_init),
        grid=(B, S // t),
        in_specs=[
            pl.BlockSpec(memory_space=pltpu.SMEM),
            pl.BlockSpec((2, t, t), lambda b, i: (0, 0, 0)),
            vec, vec, vec, vec,
            pl.BlockSpec((LANES, 1), lambda b, i: (0, 0)),
            pl.BlockSpec((None, t, width), lambda b, i: (b, i, 0)),
            pl.BlockSpec((None, S, width), lambda b, i: (b, 0, 0)),
            pl.BlockSpec((None, H, ACC_ROWS, S), lambda b, i: (b, 0, 0, 0)),
        ],
        out_specs=pl.BlockSpec((None, t, width), lambda b, i: (b, i, 0)),
        out_shape=jax.ShapeDtypeStruct((B, S, width), BF16),
        scratch_shapes=[
            pltpu.VMEM((H, 2, t, 2 * t), F32),
            pltpu.VMEM((H, 2 * t, LANES), BF16),
            pltpu.VMEM((H, 1, 2 * t), F32),
            pltpu.VMEM((H, ACC_ROWS, 2 * t), F32),
            pltpu.VMEM((H, 2 * t, 2 * t), F32),
        ],
        compiler_params=_cparams(("arbitrary", "arbitrary")),
        name="diff_attn",
    )(relb_t, buckets, lq1, lk1, lq2, lk2, sg_col, q, k, vt)


def _outproj_kernel(x_ref, ya_ref, yb_ref, wa_ref, wb_ref, g_ref, rw_ref, rb_ref,
                    x1_ref, h2_ref, gd_ref):
    x1 = x_ref[...] + _dot(ya_ref[...], wa_ref[...]) + _dot(yb_ref[...], wb_ref[...])
    x1_ref[...] = x1
    ms = jnp.mean(x1 * x1, axis=-1, keepdims=True)
    h2 = x1 * lax.rsqrt(ms + NORM_EPS) * g_ref[...]
    h2_ref[...] = h2.astype(h2_ref.dtype)
    logits = _mm(h2, rw_ref[...], mode="x3") + rb_ref[...]
    lane = lax.broadcasted_iota(jnp.int32, logits.shape, 1).astype(F32)
    work = logits
    picks = []
    for _ in range(TOP_K):
        m = jnp.max(work, axis=-1, keepdims=True)
        idx = jnp.min(jnp.where(work == m, lane, float(LANES)), axis=-1, keepdims=True)
        hit = lane == idx
        picks.append((m, hit))
        work = jnp.where(hit, NEG_BIG, work)
    m0 = picks[0][0]
    es = [jnp.exp(m - m0) for m, _ in picks]
    denom = es[0] + es[1] + es[2] + es[3]
    gd = jnp.zeros_like(logits)
    for e, (_, hit) in zip(es, picks):
        gd = jnp.where(hit, e / denom, gd)
    gd_ref[...] = gd


def _outproj(x2d, ya, yb, wa, wb, g2, rw, rb):
    T, D = x2d.shape
    half = ya.shape[1]
    tm = PROJ_ROWS
    row = lambda i: (i, 0)
    fix = lambda i: (0, 0)
    return pl.pallas_call(
        _outproj_kernel,
        grid=(T // tm,),
        in_specs=[
            pl.BlockSpec((tm, D), row),
            pl.BlockSpec((tm, half), row),
            pl.BlockSpec((tm, half), row),
            pl.BlockSpec((half, D), fix),
            pl.BlockSpec((half, D), fix),
            pl.BlockSpec((1, D), fix),
            pl.BlockSpec((D, LANES), fix),
            pl.BlockSpec((1, LANES), fix),
        ],
        out_specs=[
            pl.BlockSpec((tm, D), row),
            pl.BlockSpec((tm, D), row),
            pl.BlockSpec((tm, LANES), row),
        ],
        out_shape=[
            jax.ShapeDtypeStruct((T, D), F32),
            jax.ShapeDtypeStruct((T, D), BF16),
            jax.ShapeDtypeStruct((T, LANES), F32),
        ],
        compiler_params=_cparams(("arbitrary",)),
        name="outproj_router",
    )(x2d, ya, yb, wa, wb, g2, rw, rb)


def _padded_rows(cnt, rb):
    return jnp.ceil(cnt * (1.0 / rb)) * rb


def _route_kernel(gd_ref, lp_ref, gate_ref, tab_ref, cnt_ref, carry_sc, pstart_sc, *, rb):
    ph = pl.program_id(0)
    i = pl.program_id(1)
    tb = gd_ref.shape[0]
    gd = gd_ref[...]
    sel = gd > 0.0
    self = jnp.where(sel, 1.0, 0.0)
    colsum = jnp.sum(self, axis=0, keepdims=True)
    run_pad = jnp.ceil(colsum * (1.0 / RUN_CHUNK)) * RUN_CHUNK
    r2 = lax.broadcasted_iota(jnp.int32, (LANES, LANES), 0)
    c2 = lax.broadcasted_iota(jnp.int32, (LANES, LANES), 1)

    @pl.when((ph == 0) & (i == 0))
    def _():
        carry_sc[...] = jnp.zeros_like(carry_sc)

    @pl.when(ph == 0)
    def _():
        carry_sc[...] = carry_sc[...] + run_pad

    @pl.when((ph == 1) & (i == 0))
    def _():
        cnt = carry_sc[...]
        upper = jnp.where(r2 < c2, 1.0, 0.0)
        pstart = _dot(jnp.broadcast_to(_padded_rows(cnt, rb), (8, LANES)), upper, prec=HP)[0:1, :]
        pstart_sc[...] = pstart
        cnt_ref[0:1, :] = cnt
        cnt_ref[1:2, :] = pstart
        cnt_ref[2:8, :] = jnp.zeros((6, LANES), F32)
        carry_sc[...] = jnp.zeros_like(carry_sc)

    @pl.when(ph == 1)
    def _():
        rr = lax.broadcasted_iota(jnp.int32, (tb, tb), 0)
        cc = lax.broadcasted_iota(jnp.int32, (tb, tb), 1)
        lower = jnp.where(cc < rr, 1.0, 0.0).astype(BF16)
        sel_b = self.astype(BF16)
        prefix = _dot(lower, sel_b)
        upper_b = jnp.where(r2 < c2, 1.0, 0.0).astype(BF16)
        off = _dot(jnp.broadcast_to(run_pad, (8, LANES)).astype(BF16), upper_b)[0:1, :]
        lp = off + prefix
        upper_incl = jnp.where(r2 <= c2, 1.0, 0.0).astype(BF16)
        slot = _dot(sel_b, upper_incl)
        lane = lax.broadcasted_iota(jnp.int32, (tb, LANES), 1)
        p_out = jnp.zeros((tb, LANES), F32)
        g_out = jnp.zeros((tb, LANES), F32)
        for kk in range(TOP_K):
            mk = sel & (slot == float(kk + 1))
            p_k = jnp.sum(jnp.where(mk, lp, 0.0), axis=-1, keepdims=True)
            g_k = jnp.sum(jnp.where(mk, gd, 0.0), axis=-1, keepdims=True)
            p_out = jnp.where(lane == kk, p_k, p_out)
            g_out = jnp.where(lane == kk, g_k, g_out)
        lp_ref[...] = p_out
        gate_ref[...] = g_out
        tab_ref[0:1, :] = (run_pad * (1.0 / RUN_CHUNK)).astype(jnp.int32)
        tab_ref[1:2, :] = off.astype(jnp.int32)
        tab_ref[2:3, :] = (pstart_sc[...] + carry_sc[...]).astype(jnp.int32)
        tab_ref[3:8, :] = jnp.zeros((5, LANES), jnp.int32)
        carry_sc[...] = carry_sc[...] + run_pad


def _route(gd, rb):
    T = gd.shape[0]
    tb = MOVE_ROWS
    blk = lambda ph, i: (i * ph, 0)
    return pl.pallas_call(
        functools.partial(_route_kernel, rb=rb),
        grid=(2, T // tb),
        in_specs=[pl.BlockSpec((tb, LANES), lambda ph, i: (i, 0))],
        out_specs=[
            pl.BlockSpec((tb, LANES), blk),
            pl.BlockSpec((tb, LANES), blk),
            pl.BlockSpec((8, LANES), blk),
            pl.BlockSpec((8, LANES), lambda ph, i: (0, 0)),
        ],
        out_shape=[
            jax.ShapeDtypeStruct((T, LANES), F32),
            jax.ShapeDtypeStruct((T, LANES), F32),
            jax.ShapeDtypeStruct((T // tb * 8, LANES), jnp.int32),
            jax.ShapeDtypeStruct((8, LANES), F32),
        ],
        scratch_shapes=[pltpu.VMEM((1, LANES), F32), pltpu.VMEM((1, LANES), F32)],
        compiler_params=_cparams(("arbitrary", "arbitrary")),
        name="route_positions",
    )(gd)


def _chunk_copy(src_ref, src_row, dst_ref, dst_row, sem):
    return pltpu.make_async_copy(src_ref.at[pl.ds(src_row, RUN_CHUNK), :],
                                 dst_ref.at[pl.ds(dst_row, RUN_CHUNK), :], sem)


def _start_runs(b, nc_ref, off_ref, d0_ref, copy_chunk):
    for e in range(N_EXPERTS):
        s0 = off_ref[b * N_EXPERTS + e]
        d0 = d0_ref[b * N_EXPERTS + e]

        def issue(c, carry, s0=s0, d0=d0):
            copy_chunk(pl.multiple_of(s0 + c * RUN_CHUNK, RUN_CHUNK),
                       pl.multiple_of(d0 + c * RUN_CHUNK, RUN_CHUNK)).start()
            return carry

        lax.fori_loop(0, nc_ref[b * N_EXPERTS + e], issue, 0)


def _wait_runs(b, nc_ref, wait_chunk):
    total = 0
    for e in range(N_EXPERTS):
        total = total + nc_ref[b * N_EXPERTS + e]

    def drain(c, carry):
        wait_chunk().wait()
        return carry

    lax.fori_loop(0, total, drain, 0)


def _local_onehot(lp, lb):
    j = lax.broadcasted_iota(jnp.int32, (lp.shape[0], lb), 1).astype(F32)
    hit = lp[:, 0:1] == j
    for kk in range(1, TOP_K):
        hit = hit | (lp[:, kk:kk + 1] == j)
    return hit


def _scatter_kernel(nc_ref, off_ref, d0_ref, t0_ref, tc_ref, lp_ref, h_ref, xs_ref, buf, zero_buf, sems):
    b = pl.program_id(0)
    slot = b % 2
    pt = jnp.where(_local_onehot(lp_ref[...], buf.shape[1]), 1.0, 0.0).astype(BF16)
    buf[slot] = _dot(pt, h_ref[...], TN)
    _start_runs(b, nc_ref, off_ref, d0_ref,
                lambda s, d: _chunk_copy(buf.at[slot], s, xs_ref, d, sems.at[slot]))

    @pl.when(b >= 1)
    def _():
        _wait_runs(b - 1, nc_ref, lambda: _chunk_copy(buf.at[1 - slot], 0, xs_ref, 0, sems.at[1 - slot]))

    @pl.when(b == pl.num_programs(0) - 1)
    def _():
        _wait_runs(b, nc_ref, lambda: _chunk_copy(buf.at[slot], 0, xs_ref, 0, sems.at[slot]))
        zero_buf[...] = jnp.zeros_like(zero_buf)
        total = 0
        for e in range(N_EXPERTS + 1):
            t0 = t0_ref[e]

            def issue(c, carry, t0=t0):
                _chunk_copy(zero_buf, 0, xs_ref, pl.multiple_of(t0 + c * RUN_CHUNK, RUN_CHUNK), sems.at[0]).start()
                return carry

            lax.fori_loop(0, tc_ref[e], issue, 0)
            total = total + tc_ref[e]

        def drain(c, carry):
            _chunk_copy(zero_buf, 0, xs_ref, 0, sems.at[0]).wait()
            return carry

        lax.fori_loop(0, total, drain, 0)


def _scatter_rows(tabs, tails, lp, h2, n_rows):
    T, D = h2.shape
    tb = MOVE_ROWS
    return pl.pallas_call(
        _scatter_kernel,
        grid_spec=pltpu.PrefetchScalarGridSpec(
            num_scalar_prefetch=5,
            grid=(T // tb,),
            in_specs=[
                pl.BlockSpec((tb, LANES), lambda i, *_: (i, 0)),
                pl.BlockSpec((tb, D), lambda i, *_: (i, 0)),
            ],
            out_specs=pl.BlockSpec(memory_space=pl.ANY),
            scratch_shapes=[pltpu.VMEM((2, LOCAL_ROWS, D), F32), pltpu.VMEM((RUN_CHUNK, D), F32),
                            pltpu.SemaphoreType.DMA((2,))],
        ),
        out_shape=jax.ShapeDtypeStruct((n_rows, D), F32),
        compiler_params=_cparams(("arbitrary",)),
        name="scatter_rows",
    )(*tabs, *tails, lp, h2)


def _moe_kernel(be_ref, na_ref, xs_ref, w1_ref, b1_ref, w2_ref, b2_ref, ys_ref, w1b, w2b):
    j = pl.program_id(0)

    @pl.when((j == 0) | (be_ref[j] != be_ref[jnp.maximum(j - 1, 0)]))
    def _():
        w1b[...] = w1_ref[...].astype(BF16)
        w2b[...] = w2_ref[...].astype(BF16)

    @pl.when(j < na_ref[0])
    def _():
        xb = xs_ref[...].astype(BF16)
        hid = _dot(xb, w1b[...]) + b1_ref[...]
        half = hid.shape[1] // 2
        x_glu = jnp.minimum(hid[:, :half], SWIGLU_LIMIT)
        x_lin = jnp.clip(hid[:, half:], -SWIGLU_LIMIT, SWIGLU_LIMIT)
        act = x_glu * _sigmoid(SWIGLU_ALPHA * x_glu) * (x_lin + 1.0)
        ys_ref[...] = _dot(act.astype(BF16), w2b[...]) + b2_ref[...]

    @pl.when(j >= na_ref[0])
    def _():
        ys_ref[...] = jnp.zeros_like(ys_ref)


def _moe(blk_expert, n_active, xs, w1, b1, w2, b2):
    R, D = xs.shape
    E, _, D2 = w1.shape
    rb = MOE_ROWS
    nb = R // rb
    blk = lambda j, be, na: (jnp.minimum(j, na[0] - 1), 0)
    return pl.pallas_call(
        _moe_kernel,
        grid_spec=pltpu.PrefetchScalarGridSpec(
            num_scalar_prefetch=2,
            grid=(nb,),
            in_specs=[
                pl.BlockSpec((rb, D), blk),
                pl.BlockSpec((None, D, D2), lambda j, be, na: (be[j], 0, 0)),
                pl.BlockSpec((None, 1, D2), lambda j, be, na: (be[j], 0, 0)),
                pl.BlockSpec((None, D2 // 2, D), lambda j, be, na: (be[j], 0, 0)),
                pl.BlockSpec((None, 1, D), lambda j, be, na: (be[j], 0, 0)),
            ],
            out_specs=pl.BlockSpec((rb, D), lambda j, be, na: (j, 0)),
            scratch_shapes=[pltpu.VMEM((D, D2), BF16), pltpu.VMEM((D2 // 2, D), BF16)],
        ),
        out_shape=jax.ShapeDtypeStruct((R, D), F32),
        compiler_params=_cparams(("arbitrary",)),
        name="moe_experts",
    )(blk_expert, n_active, xs, w1, b1, w2, b2)


def _combine_kernel(nc_ref, off_ref, d0_ref, lp_ref, gate_ref, x1_ref, ys_ref, o_ref, buf, sems):
    b = pl.program_id(0)
    slot = b % 2

    def fetch(blk, sl):
        _start_runs(blk, nc_ref, off_ref, d0_ref,
                    lambda s, d: _chunk_copy(ys_ref, d, buf.at[sl], s, sems.at[sl]))

    @pl.when(b == 0)
    def _():
        buf[...] = jnp.zeros_like(buf)
        fetch(b, slot)

    @pl.when(b + 1 < pl.num_programs(0))
    def _():
        fetch(b + 1, 1 - slot)

    _wait_runs(b, nc_ref, lambda: _chunk_copy(ys_ref, 0, buf.at[slot], 0, sems.at[slot]))
    lp = lp_ref[...]
    g = gate_ref[...]
    _, lb, d = buf.shape
    j = lax.broadcasted_iota(jnp.int32, (lp.shape[0], lb), 1).astype(F32)
    gt = jnp.zeros(j.shape, F32)
    for kk in range(TOP_K):
        gt = jnp.where(lp[:, kk:kk + 1] == j, g[:, kk:kk + 1], gt)
    g_hi, g_lo = _split2(gt)
    ones = jnp.ones((lp.shape[0], LANES), BF16)
    g_row = _dot(g_hi, ones, TN) + _dot(g_lo, ones, TN)
    rows = (buf[slot] * jnp.concatenate([g_row] * (d // LANES), axis=1)).astype(BF16)
    p01 = jnp.where(gt != 0.0, 1.0, 0.0).astype(BF16)
    o_ref[...] = x1_ref[...] + _dot(p01, rows)


def _combine(tabs, lp, gate, x1, ys):
    T, D = x1.shape
    tb = MOVE_ROWS
    return pl.pallas_call(
        _combine_kernel,
        grid_spec=pltpu.PrefetchScalarGridSpec(
            num_scalar_prefetch=3,
            grid=(T // tb,),
            in_specs=[
                pl.BlockSpec((tb, LANES), lambda i, *_: (i, 0)),
                pl.BlockSpec((tb, LANES), lambda i, *_: (i, 0)),
                pl.BlockSpec((tb, D), lambda i, *_: (i, 0)),
                pl.BlockSpec(memory_space=pl.ANY),
            ],
            out_specs=pl.BlockSpec((tb, D), lambda i, *_: (i, 0)),
            scratch_shapes=[pltpu.VMEM((2, LOCAL_ROWS, D), F32), pltpu.SemaphoreType.DMA((2,))],
        ),
        out_shape=jax.ShapeDtypeStruct((T, D), F32),
        compiler_params=_cparams(("arbitrary",)),
        name="combine_rows",
    )(*tabs, lp, gate, x1, ys)


def _pad_cols(a, n):
    return jnp.pad(a, ((0, 0), (0, n - a.shape[1])))


def _pad_rows(a, n):
    return jnp.pad(a, ((0, n - a.shape[0]), (0, 0)))


def _lora_weight(w2, a2, g2, width):
    z = jnp.zeros((LANES, width), F32)
    return jnp.concatenate([
        jnp.concatenate([_pad_rows(w2, LANES), z, z], axis=1),
        jnp.concatenate([z, _pad_rows(a2, LANES), z], axis=1),
        jnp.concatenate([z, z, _pad_rows(g2, LANES)], axis=1),
    ], axis=0).astype(BF16)


def _split_rwkv_cols(a, width):
    o = 3 * width
    return jnp.concatenate([
        a[:, :o],
        _pad_cols(a[:, o:o + DECAY_LORA], LANES),
        _pad_cols(a[:, o + DECAY_LORA:o + DECAY_LORA + AAA_LORA], LANES),
        _pad_cols(a[:, o + DECAY_LORA + AAA_LORA:o + DECAY_LORA + AAA_LORA + GATE_LORA], LANES),
    ], axis=1)


def kernel(x, norm1_g, w_in, mu_shift, w0, w2, a0, a2, g2, k_k, k_a, r_k, lnx_g, lnx_b, qn_g, kn_g,
           lam_q1, lam_k1, lam_q2, lam_k2, subln_g, rel_bias, w_out, norm2_g, router_w, router_b,
           exp_w1, exp_b1, exp_w2, exp_b2):
    B, S, D = x.shape
    T = B * S
    depth = norm1_g.shape[0]
    width = w0.shape[1]
    n_rwkv_cols = 3 * width + DECAY_LORA + AAA_LORA + GATE_LORA
    n_rwkv_pad = 3 * width + 3 * LANES
    n_groups = width // HEAD_DIM
    H = width // LANES
    relb_t = rel_bias.T
    row = lambda a: a.reshape(1, -1)

    for layer in range(depth):
        lam_init = 0.8 - 0.6 * math.exp(-0.3 * layer)
        w_l = w_in[layer]
        w_all = jnp.concatenate([_split_rwkv_cols(w_l[:, :n_rwkv_cols], width), w_l[:, n_rwkv_cols:]],
                                axis=1).astype(BF16)
        mu = _split_rwkv_cols(row(mu_shift[layer]), width)
        qg = row(jnp.tile(qn_g[layer], n_groups))
        kg = row(jnp.tile(kn_g[layer], n_groups))

        pr, q, k, v = _inproj(x.reshape(T, D), row(norm1_g[layer]), w_all, qg, kg, n_rwkv_pad, width)

        y_a = _rwkv(pr.reshape(B, S, n_rwkv_pad), mu, row(w0[layer]),
                    _lora_weight(w2[layer], a2[layer], g2[layer], width), row(a0[layer]),
                    row(k_k[layer]), row(k_a[layer]), row(r_k[layer]), row(lnx_g[layer]),
                    row(lnx_b[layer]), width)

        vt = jnp.concatenate([v.reshape(B, S, H, LANES).transpose(0, 2, 3, 1),
                              jnp.ones((B, H, ACC_ROWS - LANES, S), BF16)], axis=2)
        y_b = _diff_attn(q.reshape(B, S, width), k.reshape(B, S, width), vt, relb_t,
                         row(lam_q1[layer]), row(lam_k1[layer]), row(lam_q2[layer]), row(lam_k2[layer]),
                         subln_g[layer].reshape(-1, 1), lam_init)

        wo = w_out[layer].astype(BF16)
        rw = _pad_cols(router_w[layer], LANES)
        rb = jnp.concatenate([router_b[layer], jnp.full((LANES - N_EXPERTS,), NEG_BIG, F32)]).reshape(1, -1)
        x1, h2, gd = _outproj(x.reshape(T, D), y_a.reshape(T, width), y_b.reshape(T, width),
                              wo[:width], wo[width:], row(norm2_g[layer]), rw, rb)

        lp, gate, tab, cnt = _route(gd, MOE_ROWS)
        tab = tab.reshape(-1, 8, LANES)[:, :, :N_EXPERTS]
        tabs = tuple(tab[:, r, :].reshape(-1) for r in range(3))
        run_end = cnt[1, :N_EXPERTS] + cnt[0, :N_EXPERTS]
        pad_end = cnt[1, :N_EXPERTS] + _padded_rows(cnt[0, :N_EXPERTS], MOE_ROWS)
        run_pad_max = (T // MOVE_ROWS) * N_EXPERTS * (RUN_CHUNK - 1)
        n_blocks = -(-(T * TOP_K + run_pad_max + N_EXPERTS * (MOE_ROWS - 1)) // MOE_ROWS)
        gap_start = jnp.concatenate([run_end, pad_end[-1:]])
        gap_end = jnp.concatenate([pad_end, jnp.full((1,), n_blocks * MOE_ROWS, F32)])
        tails = (gap_start.astype(jnp.int32), ((gap_end - gap_start) / RUN_CHUNK).astype(jnp.int32))
        blk_start = (jnp.arange(n_blocks) * MOE_ROWS).astype(F32)
        blk_expert = jnp.minimum(jnp.sum(pad_end[None, :] <= blk_start[:, None], axis=1), N_EXPERTS - 1)
        n_active = (pad_end[N_EXPERTS - 1] / MOE_ROWS).astype(jnp.int32).reshape(1)

        xs = _scatter_rows(tabs, tails, lp, h2, n_blocks * MOE_ROWS)
        ys = _moe(blk_expert.astype(jnp.int32), n_active, xs,
                  exp_w1[layer], exp_b1[layer][:, None, :], exp_w2[layer], exp_b2[layer][:, None, :])
        x = _combine(tabs, lp, gate, x1, ys).reshape(B, S, D)
    return x
```

```python
import functools
import math

import numpy as np
import jax
import jax.numpy as jnp
from jax import lax
from jax.experimental import pallas as pl
from jax.experimental.pallas import tpu as pltpu

F32 = jnp.float32
BF16 = jnp.bfloat16
HP = lax.Precision.HIGHEST

HEAD_DIM = 64
DECAY_LORA = 32
AAA_LORA = 32
GATE_LORA = 96
NUM_BUCKETS = 32
MAX_DISTANCE = 128
N_EXPERTS = 32
TOP_K = 4
SWIGLU_LIMIT = 7.0
SWIGLU_ALPHA = 1.702
NORM_EPS = 1e-5
LNX_EPS = 64e-5

LANES = 128
VMEM_LIMIT_BYTES = 56 * 1024 * 1024

NEG_BIG = -1e30
LOG2E = math.log2(math.e)
ACC_L = LANES
ACC_ROWS = LANES + 8

PROJ_ROWS = 512
RWKV_CHUNK = 64
RWKV_ROWS = 256
RWKV_SEQS = 2
ATT_TILE = 256
MOE_ROWS = 512
MOVE_ROWS = 256
RUN_CHUNK = 8
LOCAL_ROWS = MOVE_ROWS * 4 + 32 * RUN_CHUNK

NN = (((1,), (0,)), ((), ()))
NT = (((1,), (1,)), ((), ()))
TN = (((0,), (0,)), ((), ()))


def _dot(a, b, dims=NN, prec=None):
    return lax.dot_general(a, b, dims, precision=prec, preferred_element_type=F32)


def _cparams(sem):
    return pltpu.CompilerParams(dimension_semantics=sem, vmem_limit_bytes=VMEM_LIMIT_BYTES)


def _sigmoid(x):
    return 1.0 / (1.0 + jnp.exp(-x))


def _lane_lo(shape):
    lane = lax.broadcasted_iota(jnp.int32, shape, len(shape) - 1)
    return (lane % LANES) < HEAD_DIM


def _head_rms_normalize(x, gain, scale):
    outs = []
    for t in range(x.shape[1] // LANES):
        xt = x[:, t * LANES:(t + 1) * LANES]
        lo = _lane_lo(xt.shape)
        x2 = xt * xt
        s0 = jnp.sum(jnp.where(lo, x2, 0.0), axis=-1, keepdims=True)
        s1 = jnp.sum(jnp.where(lo, 0.0, x2), axis=-1, keepdims=True)
        inv = lax.rsqrt(jnp.where(lo, s0, s1) * (1.0 / HEAD_DIM) + NORM_EPS)
        outs.append(xt * inv)
    return jnp.concatenate(outs, axis=-1) * gain * scale


def _inproj_kernel(x_ref, g_ref, w_ref, qg_ref, kg_ref, pr_ref, q_ref, k_ref, vt_ref, *, n_rwkv, width):
    x = x_ref[...]
    ms = jnp.mean(x * x, axis=-1, keepdims=True)
    h = (x * lax.rsqrt(ms + NORM_EPS) * g_ref[...]).astype(BF16)
    step = 384
    for c0 in range(0, n_rwkv, step):
        pr_ref[:, c0:c0 + step] = _dot(h, w_ref[:, c0:c0 + step])
    q = _dot(h, w_ref[:, n_rwkv:n_rwkv + width])
    q_ref[...] = _head_rms_normalize(q, qg_ref[...], HEAD_DIM ** -0.5 * LOG2E).astype(BF16)
    k = _dot(h, w_ref[:, n_rwkv + width:n_rwkv + 2 * width])
    k_ref[...] = _head_rms_normalize(k, kg_ref[...], 1.0).astype(BF16)
    v = _dot(h, w_ref[:, n_rwkv + 2 * width:n_rwkv + 3 * width])
    for hd in range(width // LANES):
        vt_ref[hd, 0:LANES, :] = v[:, hd * LANES:(hd + 1) * LANES].T.astype(BF16)
        vt_ref[hd, LANES:ACC_ROWS, :] = jnp.ones((ACC_ROWS - LANES, v.shape[0]), BF16)


def _inproj(x2d, g, w_all, qg, kg, n_rwkv, width, seq_len):
    T, D = x2d.shape
    n_all = w_all.shape[1]
    tm = PROJ_ROWS
    H = width // LANES
    per_seq = seq_len // tm
    row = lambda i: (i, 0)
    fix = lambda i: (0, 0)
    return pl.pallas_call(
        functools.partial(_inproj_kernel, n_rwkv=n_rwkv, width=width),
        grid=(T // tm,),
        in_specs=[
            pl.BlockSpec((tm, D), row),
            pl.BlockSpec((1, D), fix),
            pl.BlockSpec((D, n_all), fix),
            pl.BlockSpec((1, width), fix),
            pl.BlockSpec((1, width), fix),
        ],
        out_specs=[
            pl.BlockSpec((tm, n_rwkv), row),
            pl.BlockSpec((tm, width), row),
            pl.BlockSpec((tm, width), row),
            pl.BlockSpec((None, H, ACC_ROWS, tm), lambda i: (i // per_seq, 0, 0, i % per_seq)),
        ],
        out_shape=[
            jax.ShapeDtypeStruct((T, n_rwkv), F32),
            jax.ShapeDtypeStruct((T, width), BF16),
            jax.ShapeDtypeStruct((T, width), BF16),
            jax.ShapeDtypeStruct((T // seq_len, H, ACC_ROWS, seq_len), BF16),
        ],
        compiler_params=_cparams(("arbitrary",)),
        name="inproj",
    )(x2d, g, w_all, qg, kg)


def _split2(x):
    hi = x.astype(BF16)
    return hi, (x - hi.astype(F32)).astype(BF16)


def _mm(a, b, dims=NN, mode="bf16"):
    if mode == "bf16":
        return _dot(a.astype(BF16), b.astype(BF16), dims)
    ah, al = _split2(a)
    bh, bl = _split2(b)
    lhs = jnp.concatenate([ah, ah, al], axis=dims[0][0][0])
    rhs = jnp.concatenate([bh, bl, bh], axis=dims[0][1][0])
    return _dot(lhs, rhs, dims)


def _group_sum(x, ones2):
    hi, lo = _split2(x)
    return _dot(jnp.concatenate([hi, lo], axis=1), ones2)


P_INV = "x3"
P_SUBST = "bf16"
P_STATE = "bf16"
INV_BLOCK = 16
QUAD = 256


def _rwkv_kernel(pr_ref, mu_ref, w0_ref, wl_ref, a0_ref, kk_ref, ka_ref, rk_ref, lg_ref, lb_ref,
                 tri3_ref, ones2_ref, o_ref, state_sc, prev_sc, *, width):
    C = RWKV_CHUNK
    n_seq, seq_rows, _ = pr_ref.shape
    rows = n_seq * seq_rows
    n_quads = width // QUAD
    n_chunks = seq_rows // C
    heads_q = QUAD // HEAD_DIM

    @pl.when(pl.program_id(1) == 0)
    def _():
        state_sc[...] = jnp.zeros_like(state_sc)
        prev_sc[...] = jnp.zeros_like(prev_sc)

    x = jnp.concatenate([pr_ref[s] for s in range(n_seq)], axis=0)
    rid = lax.broadcasted_iota(jnp.int32, x.shape, 0)
    prev = pltpu.roll(x, 1, axis=0)
    for s in range(n_seq):
        prev = jnp.where(rid == s * seq_rows, prev_sc[s:s + 1, :], prev)
        prev_sc[s:s + 1, :] = x[(s + 1) * seq_rows - 1:(s + 1) * seq_rows, :]
    xs = x + (prev - x) * mu_ref[...]

    r = xs[:, 0:width]
    k = xs[:, width:2 * width]
    v = xs[:, 2 * width:3 * width]
    o = 3 * width
    lora_in = jnp.concatenate([jnp.tanh(xs[:, o:o + LANES]), xs[:, o + LANES:o + 2 * LANES],
                               _sigmoid(xs[:, o + 2 * LANES:o + 3 * LANES])], axis=1).astype(BF16)
    lora = _dot(lora_in, wl_ref[...])
    z = -(w0_ref[...] + lora[:, 0:width])
    softplus = jnp.maximum(z, 0.0) + jnp.log(1.0 + jnp.exp(-jnp.abs(z)))
    log_decay = -jnp.exp(-softplus - 0.5)
    a_lr = _sigmoid(a0_ref[...] + lora[:, width:2 * width])
    gate = lora[:, 2 * width:3 * width]

    ones2 = ones2_ref[...]
    kk = k * kk_ref[...]
    kk = kk * lax.rsqrt(jnp.maximum(_group_sum(kk * kk, ones2), 1e-24))
    k = k * (1.0 + (a_lr - 1.0) * ka_ref[...])
    a_vec = -kk
    b_vec = kk * a_lr

    p1 = log_decay.astype(BF16)
    rem = log_decay - p1.astype(F32)
    p2 = rem.astype(BF16)
    p3 = (rem - p2.astype(F32)).astype(BF16)
    cum = _dot(tri3_ref[...], jnp.concatenate([p1, p2, p3], axis=0))

    t_i = lax.broadcasted_iota(jnp.int32, (C, 2 * QUAD), 0)
    s_i = lax.broadcasted_iota(jnp.int32, (C, 2 * QUAD), 1) % C
    strict2 = t_i > s_i
    incl2 = t_i >= s_i
    lane_head = lax.broadcasted_iota(jnp.int32, (C, QUAD), 1) // HEAD_DIM
    row_blk = lax.broadcasted_iota(jnp.int32, (C, QUAD), 0) // INV_BLOCK
    same_blk = row_blk == ((lax.broadcasted_iota(jnp.int32, (C, QUAD), 1) % C) // INV_BLOCK)
    r2 = lax.broadcasted_iota(jnp.int32, (QUAD, QUAD), 0)
    c2 = lax.broadcasted_iota(jnp.int32, (QUAD, QUAD), 1)
    same_head = (r2 // HEAD_DIM) == (c2 // HEAD_DIM)

    def by_head(m):
        zero = jnp.zeros_like(m)
        return jnp.concatenate([jnp.where(lane_head == h, m, zero) for h in range(heads_q)], axis=0)

    def bd_rhs(m, mode):
        if mode == "bf16":
            return by_head(m.astype(BF16))
        hi, lo = _split2(m)
        hi_bd = by_head(hi)
        return jnp.concatenate([hi_bd, by_head(lo), hi_bd], axis=0)

    def mm_bd(a, rhs, mode):
        if mode == "bf16":
            return _dot(a.astype(BF16), rhs)
        ah, al = _split2(a)
        return _dot(jnp.concatenate([ah, ah, al], axis=1), rhs)

    class Work:
        pass

    n_par = n_seq * n_quads
    works = []
    for c in range(n_chunks):
        for sq in range(n_par):
            w = Work()
            s, q = divmod(sq, n_quads)
            ls = slice(q * QUAD, (q + 1) * QUAD)
            rs = slice(s * seq_rows + c * C, s * seq_rows + (c + 1) * C)
            ld = log_decay[rs, ls]
            cm = cum[rs, ls]
            cl = cm[C - 1:C, :]
            e_in = jnp.exp(cm)
            e_neg = jnp.exp(-cm)
            e_hat = jnp.exp(cl - cm)
            w.q, w.c = sq, c
            w.dec = jnp.exp(cl)
            w.vv = v[rs, ls]
            w.ar_t = jnp.concatenate([a_vec[rs, ls] * jnp.exp(cm - ld), r[rs, ls] * e_in], axis=0)
            w.bk_t = jnp.concatenate([bd_rhs(b_vec[rs, ls] * e_neg, "bf16"), bd_rhs(k[rs, ls] * e_neg, "bf16")], axis=0)
            w.bk_h = jnp.concatenate([b_vec[rs, ls] * e_hat, k[rs, ls] * e_hat], axis=0)
            works.append(w)

    for w in works:
        gram = _dot(w.ar_t.astype(BF16), w.bk_t, NT)
        low = jnp.where(strict2, gram[0:C, :], 0.0)
        l_cat = low[:, 0:QUAD]
        rbk = jnp.where(incl2, gram[C:2 * C, :], 0.0)
        w.rb_cat = rbk[:, 0:QUAD]
        w.kv_lhs = jnp.concatenate([low[:, QUAD:], rbk[:, QUAD:]], axis=0)
        w.l_off = jnp.where(same_blk, 0.0, l_cat)
        w.d_inv = jnp.where(same_blk, l_cat, 0.0)

    for w in works:
        w.l_pow = mm_bd(w.d_inv, bd_rhs(w.d_inv, P_INV), P_INV)
    n_sq = int(math.log2(INV_BLOCK))
    for it in range(1, n_sq):
        for w in works:
            if it + 1 < n_sq:
                both = mm_bd(jnp.concatenate([w.d_inv, w.l_pow], axis=0), bd_rhs(w.l_pow, P_INV), P_INV)
                w.d_inv = w.d_inv + w.l_pow + both[0:C]
                w.l_pow = both[C:2 * C]
            else:
                w.d_inv = w.d_inv + w.l_pow + mm_bd(w.d_inv, bd_rhs(w.l_pow, P_INV), P_INV)
    for w in works:
        w.m1 = w.l_off + mm_bd(w.d_inv, bd_rhs(w.l_off, P_SUBST), P_SUBST)
        w.m1_bd = bd_rhs(w.m1, P_SUBST)
    for w in works:
        w.m2 = mm_bd(w.m1, w.m1_bd, P_SUBST)
    for w in works:
        w.nn = w.m1 + w.m2 + mm_bd(w.m2, w.m1_bd, P_SUBST)
    for w in works:
        w.t_inv = w.nn + w.d_inv + mm_bd(w.nn, bd_rhs(w.d_inv, P_SUBST), P_SUBST)
    for w in works:
        w.kv = mm_bd(w.kv_lhs, bd_rhs(w.vv, P_STATE), P_STATE)

    states = [state_sc[sq] for sq in range(n_par)]
    y_parts = [[] for _ in range(n_par)]
    for c in range(n_chunks):
        ws = works[c * n_par:(c + 1) * n_par]
        for w in ws:
            w.base = _mm(w.ar_t, states[w.q], NT, mode=P_STATE) + w.kv
        for w in ws:
            u0 = w.base[0:C]
            w.u = u0 + mm_bd(w.t_inv, bd_rhs(u0, P_STATE), P_STATE)
        for w in ws:
            upd = _mm(jnp.concatenate([w.u, w.vv], axis=0), w.bk_h, TN, mode=P_STATE)
            states[w.q] = states[w.q] * w.dec + jnp.where(same_head, upd, 0.0)
        for w in ws:
            y_parts[w.q].append(w.base[C:2 * C] + mm_bd(w.rb_cat, bd_rhs(w.u, P_STATE), P_STATE))
    for sq in range(n_par):
        state_sc[sq] = states[sq]
    y = jnp.concatenate([jnp.concatenate([jnp.concatenate(y_parts[s * n_quads + q], axis=0)
                                          for q in range(n_quads)], axis=-1)
                         for s in range(n_seq)], axis=0)

    inv_n = 1.0 / HEAD_DIM
    sums = _group_sum(jnp.concatenate([y, r * k * rk_ref[...]], axis=0), ones2)
    mean = sums[0:rows] * inv_n
    yc = y - mean
    var = _group_sum(yc * yc, ones2) * inv_n
    yn = yc * lax.rsqrt(var + LNX_EPS) * lg_ref[...] + lb_ref[...]
    out = ((yn + sums[rows:2 * rows] * v) * gate).astype(o_ref.dtype)
    for s in range(n_seq):
        o_ref[s] = out[s * seq_rows:(s + 1) * seq_rows]


def _rwkv(pr, mu, w0, w_lora, a0, k_k, k_a, r_k, lnx_g, lnx_b, width):
    B, S, n_rwkv = pr.shape
    rows = RWKV_ROWS
    n_seq = RWKV_SEQS
    C = RWKV_CHUNK
    rr = np.arange(n_seq * rows)[:, None]
    cc = np.arange(n_seq * rows)[None, :]
    tri = ((rr // C == cc // C) & (cc <= rr)).astype(np.float32)
    tri3 = jnp.asarray(np.concatenate([tri, tri, tri], axis=1), BF16)
    gg = np.arange(width)
    ones = (gg[:, None] // HEAD_DIM == gg[None, :] // HEAD_DIM).astype(np.float32)
    ones2 = jnp.asarray(np.concatenate([ones, ones], axis=0), BF16)
    fix = lambda shape: pl.BlockSpec(shape, lambda b, i: (0, 0))
    vec = lambda n: fix((1, n))
    return pl.pallas_call(
        functools.partial(_rwkv_kernel, width=width),
        grid=(B // n_seq, S // rows),
        in_specs=[
            pl.BlockSpec((n_seq, rows, n_rwkv), lambda b, i: (b, i, 0)),
            vec(n_rwkv), vec(width), fix(w_lora.shape), vec(width),
            vec(width), vec(width), vec(width), vec(width), vec(width),
            fix(tri3.shape), fix(ones2.shape),
        ],
        out_specs=pl.BlockSpec((n_seq, rows, width), lambda b, i: (b, i, 0)),
        out_shape=jax.ShapeDtypeStruct((B, S, width), BF16),
        scratch_shapes=[
            pltpu.VMEM((n_seq * (width // QUAD), QUAD, QUAD), F32),
            pltpu.VMEM((n_seq, n_rwkv), F32),
        ],
        compiler_params=_cparams(("arbitrary", "arbitrary")),
        name="rwkv7",
    )(pr, mu, w0, w_lora, a0, k_k, k_a, r_k, lnx_g, lnx_b, tri3, ones2)


def _t5_bucket_np(dist):
    n = np.maximum(dist, 0)
    max_exact = NUM_BUCKETS // 2
    nf = np.maximum(n, 1).astype(np.float32)
    large = max_exact + (np.log(nf / max_exact) / math.log(MAX_DISTANCE / max_exact)
                         * (NUM_BUCKETS - max_exact)).astype(np.int32)
    large = np.minimum(large, NUM_BUCKETS - 1)
    return np.where(n < max_exact, n, large).astype(np.int32)


def _near_bucket_tiles(tile):
    kpos = np.arange(tile)[:, None]
    qpos = np.arange(tile)[None, :]
    out = []
    for delta in (0, 1):
        dist = delta * tile + qpos - kpos
        out.append(np.where(dist >= 0, _t5_bucket_np(dist), -1))
    return np.stack(out).astype(np.int32)


def _attn_kernel(relb_ref, bucket_ref, lq1_ref, lk1_ref, lq2_ref, lk2_ref, sg_ref,
                 q_ref, k_ref, vt_ref, o_ref, bias_sc, q2_sc, m_sc, acc_sc, st_sc, *, lam_init):
    tq = q_ref.shape[0]
    n_heads = vt_ref.shape[0]
    heads = range(n_heads)
    qi = pl.program_id(1)

    @pl.when(qi == 0)
    def _():
        for h in heads:
            for d in range(2):
                bk = bucket_ref[d]
                tile = jnp.full(bk.shape, NEG_BIG, F32)
                for j in range(NUM_BUCKETS):
                    tile = jnp.where(bk == j, relb_ref[h, j] * LOG2E, tile)
                bias_sc[h, d, :, 0:tq] = tile
                bias_sc[h, d, :, tq:2 * tq] = tile

    lo = _lane_lo((tq, LANES))
    for h in heads:
        q = q_ref[:, h * LANES:(h + 1) * LANES]
        zero = jnp.zeros_like(q)
        q2_sc[h, 0:tq, :] = jnp.where(lo, q, zero)
        q2_sc[h, tq:2 * tq, :] = jnp.where(lo, zero, q)
    m_sc[...] = jnp.full(m_sc.shape, NEG_BIG, F32)
    acc_sc[...] = jnp.zeros_like(acc_sc)

    def step(k0, tk, near):
        for h in heads:
            st_sc[h, 0:tk, :] = _dot(k_ref[pl.ds(k0, tk), h * LANES:(h + 1) * LANES], q2_sc[h], NT)
        pvs, alphas = [], []
        for h in heads:
            m_old = m_sc[h]
            if near is None:
                st = st_sc[h, 0:tk, :]
                far = relb_ref[h, NUM_BUCKETS - 1] * LOG2E
                m_new = jnp.maximum(m_old, jnp.max(st, axis=0, keepdims=True) + far)
                p = jnp.exp2(st - (m_new - far))
            else:
                st = st_sc[h, 0:tk, :] + bias_sc[h, near]
                m_new = jnp.maximum(m_old, jnp.max(st, axis=0, keepdims=True))
                p = jnp.exp2(st - m_new)
            m_sc[h] = m_new
            pvs.append(_dot(vt_ref[h, :, pl.ds(k0, tk)], p.astype(BF16)))
            alphas.append(jnp.exp2(m_old - m_new))
        for h in heads:
            acc_sc[h] = alphas[h] * acc_sc[h] + pvs[h]

    n_far = jnp.maximum(qi - 1, 0)
    far_tk = 2 * tq

    def far_body(i, carry):
        step(pl.multiple_of(i * far_tk, far_tk), far_tk, None)
        return carry

    lax.fori_loop(0, n_far // 2, far_body, 0)

    @pl.when(n_far % 2 == 1)
    def _():
        step(pl.multiple_of((n_far - 1) * tq, tq), tq, None)

    @pl.when(qi >= 1)
    def _():
        step(pl.multiple_of((qi - 1) * tq, tq), tq, 1)

    step(pl.multiple_of(qi * tq, tq), tq, 0)

    lam = (jnp.exp(jnp.sum(lq1_ref[...] * lk1_ref[...], axis=-1, keepdims=True))
           - jnp.exp(jnp.sum(lq2_ref[...] * lk2_ref[...], axis=-1, keepdims=True)) + lam_init)
    for h in heads:
        acc = acc_sc[h]
        l = acc[ACC_L:ACC_L + 1, :]
        o1 = acc[0:LANES, 0:tq] / l[:, 0:tq]
        o2 = acc[0:LANES, tq:2 * tq] / l[:, tq:2 * tq]
        ot = o1 - lam * o2
        ms = jnp.mean(ot * ot, axis=0, keepdims=True)
        y = ot * lax.rsqrt(ms + NORM_EPS) * sg_ref[...] * (1.0 - lam_init)
        o_ref[:, h * LANES:(h + 1) * LANES] = y.T.astype(o_ref.dtype)


def _diff_attn(q, k, vt, relb_t, lq1, lk1, lq2, lk2, sg_col, lam_init):
    B, S, width = q.shape
    H = width // LANES
    t = ATT_TILE
    buckets = jnp.asarray(_near_bucket_tiles(t))
    vec = pl.BlockSpec((1, HEAD_DIM), lambda b, i: (0, 0))
    return pl.pallas_call(
        functools.partial(_attn_kernel, lam_init=lam_init),
        grid=(B, S // t),
        in_specs=[
            pl.BlockSpec(memory_space=pltpu.SMEM),
            pl.BlockSpec((2, t, t), lambda b, i: (0, 0, 0)),
            vec, vec, vec, vec,
            pl.BlockSpec((LANES, 1), lambda b, i: (0, 0)),
            pl.BlockSpec((None, t, width), lambda b, i: (b, i, 0)),
            pl.BlockSpec((None, S, width), lambda b, i: (b, 0, 0)),
            pl.BlockSpec((None, H, ACC_ROWS, S), lambda b, i: (b, 0, 0, 0)),
        ],
        out_specs=pl.BlockSpec((None, t, width), lambda b, i: (b, i, 0)),
        out_shape=jax.ShapeDtypeStruct((B, S, width), BF16),
        scratch_shapes=[
            pltpu.VMEM((H, 2, t, 2 * t), F32),
            pltpu.VMEM((H, 2 * t, LANES), BF16),
            pltpu.VMEM((H, 1, 2 * t), F32),
            pltpu.VMEM((H, ACC_ROWS, 2 * t), F32),
            pltpu.VMEM((H, 2 * t, 2 * t), F32),
        ],
        compiler_params=_cparams(("arbitrary", "arbitrary")),
        name="diff_attn",
    )(relb_t, buckets, lq1, lk1, lq2, lk2, sg_col, q, k, vt)


def _run_rows(count):
    return jnp.ceil(count * (1.0 / RUN_CHUNK)) * RUN_CHUNK


def _outproj_kernel(x_ref, ya_ref, yb_ref, wa_ref, wb_ref, g_ref, rw_ref, rb_ref,
                    x1_ref, h2_ref, gd_ref, cnt_ref):
    x1 = x_ref[...] + _dot(ya_ref[...], wa_ref[...]) + _dot(yb_ref[...], wb_ref[...])
    x1_ref[...] = x1
    ms = jnp.mean(x1 * x1, axis=-1, keepdims=True)
    h2 = x1 * lax.rsqrt(ms + NORM_EPS) * g_ref[...]
    h2_ref[...] = h2.astype(h2_ref.dtype)
    logits = _mm(h2, rw_ref[...], mode="x3") + rb_ref[...]
    lane = lax.broadcasted_iota(jnp.int32, logits.shape, 1).astype(F32)
    work = logits
    picks = []
    for _ in range(TOP_K):
        m = jnp.max(work, axis=-1, keepdims=True)
        idx = jnp.min(jnp.where(work == m, lane, float(LANES)), axis=-1, keepdims=True)
        hit = lane == idx
        picks.append((m, hit))
        work = jnp.where(hit, NEG_BIG, work)
    m0 = picks[0][0]
    es = [jnp.exp(m - m0) for m, _ in picks]
    denom = es[0] + es[1] + es[2] + es[3]
    gd = jnp.zeros_like(logits)
    for e, (_, hit) in zip(es, picks):
        gd = jnp.where(hit, e / denom, gd)
    gd_ref[...] = gd

    @pl.when(pl.program_id(0) == 0)
    def _():
        cnt_ref[...] = jnp.zeros_like(cnt_ref)

    cnt_ref[0:1, :] = cnt_ref[0:1, :] + _run_rows(jnp.sum(jnp.where(gd > 0.0, 1.0, 0.0), axis=0, keepdims=True))


def _outproj(x2d, ya, yb, wa, wb, g2, rw, rb):
    T, D = x2d.shape
    half = ya.shape[1]
    tm = MOVE_ROWS
    row = lambda i: (i, 0)
    fix = lambda i: (0, 0)
    return pl.pallas_call(
        _outproj_kernel,
        grid=(T // tm,),
        in_specs=[
            pl.BlockSpec((tm, D), row),
            pl.BlockSpec((tm, half), row),
            pl.BlockSpec((tm, half), row),
            pl.BlockSpec((half, D), fix),
            pl.BlockSpec((half, D), fix),
            pl.BlockSpec((1, D), fix),
            pl.BlockSpec((D, LANES), fix),
            pl.BlockSpec((1, LANES), fix),
        ],
        out_specs=[
            pl.BlockSpec((tm, D), row),
            pl.BlockSpec((tm, D), row),
            pl.BlockSpec((tm, LANES), row),
            pl.BlockSpec((8, LANES), fix),
        ],
        out_shape=[
            jax.ShapeDtypeStruct((T, D), F32),
            jax.ShapeDtypeStruct((T, D), BF16),
            jax.ShapeDtypeStruct((T, LANES), F32),
            jax.ShapeDtypeStruct((8, LANES), F32),
        ],
        compiler_params=_cparams(("arbitrary",)),
        name="outproj_router",
    )(x2d, ya, yb, wa, wb, g2, rw, rb)


def _padded_rows(cnt, rb):
    return jnp.ceil(cnt * (1.0 / rb)) * rb


def _route_kernel(gd_ref, tot_ref, lp_ref, gate_ref, tab_ref, cnt_ref, carry_sc, pstart_sc, *, rb):
    i = pl.program_id(0)
    tb = gd_ref.shape[0]
    gd = gd_ref[...]
    sel = gd > 0.0
    self = jnp.where(sel, 1.0, 0.0)
    run_pad = _run_rows(jnp.sum(self, axis=0, keepdims=True))
    r2 = lax.broadcasted_iota(jnp.int32, (LANES, LANES), 0)
    c2 = lax.broadcasted_iota(jnp.int32, (LANES, LANES), 1)

    @pl.when(i == 0)
    def _():
        cnt = tot_ref[0:1, :]
        upper = jnp.where(r2 < c2, 1.0, 0.0)
        pstart = _dot(jnp.broadcast_to(_padded_rows(cnt, rb), (8, LANES)), upper, prec=HP)[0:1, :]
        pstart_sc[...] = pstart
        cnt_ref[0:1, :] = cnt
        cnt_ref[1:2, :] = pstart
        cnt_ref[2:8, :] = jnp.zeros((6, LANES), F32)
        carry_sc[...] = jnp.zeros_like(carry_sc)

    rr = lax.broadcasted_iota(jnp.int32, (tb, tb), 0)
    cc = lax.broadcasted_iota(jnp.int32, (tb, tb), 1)
    lower = jnp.where(cc < rr, 1.0, 0.0).astype(BF16)
    sel_b = self.astype(BF16)
    prefix = _dot(lower, sel_b)
    upper_b = jnp.where(r2 < c2, 1.0, 0.0).astype(BF16)
    off = _dot(jnp.broadcast_to(run_pad, (8, LANES)).astype(BF16), upper_b)[0:1, :]
    lp = off + prefix
    upper_incl = jnp.where(r2 <= c2, 1.0, 0.0).astype(BF16)
    slot = _dot(sel_b, upper_incl)
    lane = lax.broadcasted_iota(jnp.int32, (tb, LANES), 1)
    p_out = jnp.zeros((tb, LANES), F32)
    g_out = jnp.zeros((tb, LANES), F32)
    for kk in range(TOP_K):
        mk = sel & (slot == float(kk + 1))
        p_k = jnp.sum(jnp.where(mk, lp, 0.0), axis=-1, keepdims=True)
        g_k = jnp.sum(jnp.where(mk, gd, 0.0), axis=-1, keepdims=True)
        p_out = jnp.where(lane == kk, p_k, p_out)
        g_out = jnp.where(lane == kk, g_k, g_out)
    lp_ref[...] = p_out
    gate_ref[...] = g_out
    tab_ref[0:1, :] = (run_pad * (1.0 / RUN_CHUNK)).astype(jnp.int32)
    tab_ref[1:2, :] = off.astype(jnp.int32)
    tab_ref[2:3, :] = (pstart_sc[...] + carry_sc[...]).astype(jnp.int32)
    tab_ref[3:8, :] = jnp.zeros((5, LANES), jnp.int32)
    carry_sc[...] = carry_sc[...] + run_pad


def _route(gd, totals, rb):
    T = gd.shape[0]
    tb = MOVE_ROWS
    blk = lambda i: (i, 0)
    fix = lambda i: (0, 0)
    return pl.pallas_call(
        functools.partial(_route_kernel, rb=rb),
        grid=(T // tb,),
        in_specs=[pl.BlockSpec((tb, LANES), blk), pl.BlockSpec((8, LANES), fix)],
        out_specs=[
            pl.BlockSpec((tb, LANES), blk),
            pl.BlockSpec((tb, LANES), blk),
            pl.BlockSpec((8, LANES), blk),
            pl.BlockSpec((8, LANES), fix),
        ],
        out_shape=[
            jax.ShapeDtypeStruct((T, LANES), F32),
            jax.ShapeDtypeStruct((T, LANES), F32),
            jax.ShapeDtypeStruct((T // tb * 8, LANES), jnp.int32),
            jax.ShapeDtypeStruct((8, LANES), F32),
        ],
        scratch_shapes=[pltpu.VMEM((1, LANES), F32), pltpu.VMEM((1, LANES), F32)],
        compiler_params=_cparams(("arbitrary",)),
        name="route_positions",
    )(gd, totals)


def _chunk_copy(src_ref, src_row, dst_ref, dst_row, sem):
    return pltpu.make_async_copy(src_ref.at[pl.ds(src_row, RUN_CHUNK), :],
                                 dst_ref.at[pl.ds(dst_row, RUN_CHUNK), :], sem)


def _start_runs(b, nc_ref, off_ref, d0_ref, copy_chunk):
    for e in range(N_EXPERTS):
        s0 = off_ref[b * N_EXPERTS + e]
        d0 = d0_ref[b * N_EXPERTS + e]

        def issue(c, carry, s0=s0, d0=d0):
            copy_chunk(pl.multiple_of(s0 + c * RUN_CHUNK, RUN_CHUNK),
                       pl.multiple_of(d0 + c * RUN_CHUNK, RUN_CHUNK)).start()
            return carry

        lax.fori_loop(0, nc_ref[b * N_EXPERTS + e], issue, 0)


def _wait_runs(b, nc_ref, wait_chunk):
    total = 0
    for e in range(N_EXPERTS):
        total = total + nc_ref[b * N_EXPERTS + e]

    def drain(c, carry):
        wait_chunk().wait()
        return carry

    lax.fori_loop(0, total, drain, 0)


def _local_onehot(lp, lb):
    j = lax.broadcasted_iota(jnp.int32, (lp.shape[0], lb), 1).astype(F32)
    hit = lp[:, 0:1] == j
    for kk in range(1, TOP_K):
        hit = hit | (lp[:, kk:kk + 1] == j)
    return hit


def _scatter_kernel(nc_ref, off_ref, d0_ref, t0_ref, tc_ref, lp_ref, h_ref, xs_ref, buf, zero_buf, sems):
    b = pl.program_id(0)
    slot = b % 2
    pt = jnp.where(_local_onehot(lp_ref[...], buf.shape[1]), 1.0, 0.0).astype(BF16)
    buf[slot] = _dot(pt, h_ref[...], TN)
    _start_runs(b, nc_ref, off_ref, d0_ref,
                lambda s, d: _chunk_copy(buf.at[slot], s, xs_ref, d, sems.at[slot]))

    @pl.when(b >= 1)
    def _():
        _wait_runs(b - 1, nc_ref, lambda: _chunk_copy(buf.at[1 - slot], 0, xs_ref, 0, sems.at[1 - slot]))

    @pl.when(b == pl.num_programs(0) - 1)
    def _():
        _wait_runs(b, nc_ref, lambda: _chunk_copy(buf.at[slot], 0, xs_ref, 0, sems.at[slot]))
        zero_buf[...] = jnp.zeros_like(zero_buf)
        total = 0
        for e in range(N_EXPERTS + 1):
            t0 = t0_ref[e]

            def issue(c, carry, t0=t0):
                _chunk_copy(zero_buf, 0, xs_ref, pl.multiple_of(t0 + c * RUN_CHUNK, RUN_CHUNK), sems.at[0]).start()
                return carry

            lax.fori_loop(0, tc_ref[e], issue, 0)
            total = total + tc_ref[e]

        def drain(c, carry):
            _chunk_copy(zero_buf, 0, xs_ref, 0, sems.at[0]).wait()
            return carry

        lax.fori_loop(0, total, drain, 0)


def _scatter_rows(tabs, tails, lp, h2, n_rows):
    T, D = h2.shape
    tb = MOVE_ROWS
    return pl.pallas_call(
        _scatter_kernel,
        grid_spec=pltpu.PrefetchScalarGridSpec(
            num_scalar_prefetch=5,
            grid=(T // tb,),
            in_specs=[
                pl.BlockSpec((tb, LANES), lambda i, *_: (i, 0)),
                pl.BlockSpec((tb, D), lambda i, *_: (i, 0)),
            ],
            out_specs=pl.BlockSpec(memory_space=pl.ANY),
            scratch_shapes=[pltpu.VMEM((2, LOCAL_ROWS, D), F32), pltpu.VMEM((RUN_CHUNK, D), F32),
                            pltpu.SemaphoreType.DMA((2,))],
        ),
        out_shape=jax.ShapeDtypeStruct((n_rows, D), F32),
        compiler_params=_cparams(("arbitrary",)),
        name="scatter_rows",
    )(*tabs, *tails, lp, h2)


def _moe_kernel(be_ref, na_ref, xs_ref, w1_ref, b1_ref, w2_ref, b2_ref, ys_ref, w1b, w2b):
    j = pl.program_id(0)

    @pl.when((j == 0) | (be_ref[j] != be_ref[jnp.maximum(j - 1, 0)]))
    def _():
        w1b[...] = w1_ref[...].astype(BF16)
        w2b[...] = w2_ref[...].astype(BF16)

    @pl.when(j < na_ref[0])
    def _():
        xb = xs_ref[...].astype(BF16)
        hid = _dot(xb, w1b[...]) + b1_ref[...]
        half = hid.shape[1] // 2
        x_glu = jnp.minimum(hid[:, :half], SWIGLU_LIMIT)
        x_lin = jnp.clip(hid[:, half:], -SWIGLU_LIMIT, SWIGLU_LIMIT)
        act = x_glu * _sigmoid(SWIGLU_ALPHA * x_glu) * (x_lin + 1.0)
        ys_ref[...] = _dot(act.astype(BF16), w2b[...]) + b2_ref[...]

    @pl.when(j >= na_ref[0])
    def _():
        ys_ref[...] = jnp.zeros_like(ys_ref)


def _moe(blk_expert, n_active, xs, w1, b1, w2, b2):
    R, D = xs.shape
    E, _, D2 = w1.shape
    rb = MOE_ROWS
    nb = R // rb
    blk = lambda j, be, na: (jnp.minimum(j, na[0] - 1), 0)
    return pl.pallas_call(
        _moe_kernel,
        grid_spec=pltpu.PrefetchScalarGridSpec(
            num_scalar_prefetch=2,
            grid=(nb,),
            in_specs=[
                pl.BlockSpec((rb, D), blk),
                pl.BlockSpec((None, D, D2), lambda j, be, na: (be[j], 0, 0)),
                pl.BlockSpec((None, 1, D2), lambda j, be, na: (be[j], 0, 0)),
                pl.BlockSpec((None, D2 // 2, D), lambda j, be, na: (be[j], 0, 0)),
                pl.BlockSpec((None, 1, D), lambda j, be, na: (be[j], 0, 0)),
            ],
            out_specs=pl.BlockSpec((rb, D), lambda j, be, na: (j, 0)),
            scratch_shapes=[pltpu.VMEM((D, D2), BF16), pltpu.VMEM((D2 // 2, D), BF16)],
        ),
        out_shape=jax.ShapeDtypeStruct((R, D), F32),
        compiler_params=_cparams(("arbitrary",)),
        name="moe_experts",
    )(blk_expert, n_active, xs, w1, b1, w2, b2)


def _combine_kernel(nc_ref, off_ref, d0_ref, lp_ref, gate_ref, x1_ref, ys_ref, o_ref, buf, sems):
    b = pl.program_id(0)
    slot = b % 2

    def fetch(blk, sl):
        _start_runs(blk, nc_ref, off_ref, d0_ref,
                    lambda s, d: _chunk_copy(ys_ref, d, buf.at[sl], s, sems.at[sl]))

    @pl.when(b == 0)
    def _():
        buf[...] = jnp.zeros_like(buf)
        fetch(b, slot)

    @pl.when(b + 1 < pl.num_programs(0))
    def _():
        fetch(b + 1, 1 - slot)

    _wait_runs(b, nc_ref, lambda: _chunk_copy(ys_ref, 0, buf.at[slot], 0, sems.at[slot]))
    lp = lp_ref[...]
    g = gate_ref[...]
    _, lb, d = buf.shape
    j = lax.broadcasted_iota(jnp.int32, (lp.shape[0], lb), 1).astype(F32)
    gt = jnp.zeros(j.shape, F32)
    for kk in range(TOP_K):
        gt = jnp.where(lp[:, kk:kk + 1] == j, g[:, kk:kk + 1], gt)
    g_hi, g_lo = _split2(gt)
    ones = jnp.ones((lp.shape[0], LANES), BF16)
    g_row = _dot(g_hi, ones, TN) + _dot(g_lo, ones, TN)
    rows = (buf[slot] * jnp.concatenate([g_row] * (d // LANES), axis=1)).astype(BF16)
    p01 = jnp.where(gt != 0.0, 1.0, 0.0).astype(BF16)
    o_ref[...] = x1_ref[...] + _dot(p01, rows)


def _combine(tabs, lp, gate, x1, ys):
    T, D = x1.shape
    tb = MOVE_ROWS
    return pl.pallas_call(
        _combine_kernel,
        grid_spec=pltpu.PrefetchScalarGridSpec(
            num_scalar_prefetch=3,
            grid=(T // tb,),
            in_specs=[
                pl.BlockSpec((tb, LANES), lambda i, *_: (i, 0)),
                pl.BlockSpec((tb, LANES), lambda i, *_: (i, 0)),
                pl.BlockSpec((tb, D), lambda i, *_: (i, 0)),
                pl.BlockSpec(memory_space=pl.ANY),
            ],
            out_specs=pl.BlockSpec((tb, D), lambda i, *_: (i, 0)),
            scratch_shapes=[pltpu.VMEM((2, LOCAL_ROWS, D), F32), pltpu.SemaphoreType.DMA((2,))],
        ),
        out_shape=jax.ShapeDtypeStruct((T, D), F32),
        compiler_params=_cparams(("arbitrary",)),
        name="combine_rows",
    )(*tabs, lp, gate, x1, ys)


def _pad_cols(a, n):
    return jnp.pad(a, ((0, 0), (0, n - a.shape[1])))


def _pad_rows(a, n):
    return jnp.pad(a, ((0, n - a.shape[0]), (0, 0)))


def _lora_weight(w2, a2, g2, width):
    z = jnp.zeros((LANES, width), F32)
    return jnp.concatenate([
        jnp.concatenate([_pad_rows(w2, LANES), z, z], axis=1),
        jnp.concatenate([z, _pad_rows(a2, LANES), z], axis=1),
        jnp.concatenate([z, z, _pad_rows(g2, LANES)], axis=1),
    ], axis=0).astype(BF16)


def _split_rwkv_cols(a, width):
    o = 3 * width
    return jnp.concatenate([
        a[:, :o],
        _pad_cols(a[:, o:o + DECAY_LORA], LANES),
        _pad_cols(a[:, o + DECAY_LORA:o + DECAY_LORA + AAA_LORA], LANES),
        _pad_cols(a[:, o + DECAY_LORA + AAA_LORA:o + DECAY_LORA + AAA_LORA + GATE_LORA], LANES),
    ], axis=1)


def kernel(x, norm1_g, w_in, mu_shift, w0, w2, a0, a2, g2, k_k, k_a, r_k, lnx_g, lnx_b, qn_g, kn_g,
           lam_q1, lam_k1, lam_q2, lam_k2, subln_g, rel_bias, w_out, norm2_g, router_w, router_b,
           exp_w1, exp_b1, exp_w2, exp_b2):
    B, S, D = x.shape
    T = B * S
    depth = norm1_g.shape[0]
    width = w0.shape[1]
    n_rwkv_cols = 3 * width + DECAY_LORA + AAA_LORA + GATE_LORA
    n_rwkv_pad = 3 * width + 3 * LANES
    n_groups = width // HEAD_DIM
    H = width // LANES
    relb_t = rel_bias.T
    row = lambda a: a.reshape(1, -1)

    for layer in range(depth):
        lam_init = 0.8 - 0.6 * math.exp(-0.3 * layer)
        w_l = w_in[layer]
        w_all = jnp.concatenate([_split_rwkv_cols(w_l[:, :n_rwkv_cols], width), w_l[:, n_rwkv_cols:]],
                                axis=1).astype(BF16)
        mu = _split_rwkv_cols(row(mu_shift[layer]), width)
        qg = row(jnp.tile(qn_g[layer], n_groups))
        kg = row(jnp.tile(kn_g[layer], n_groups))

        pr, q, k, vt = _inproj(x.reshape(T, D), row(norm1_g[layer]), w_all, qg, kg, n_rwkv_pad, width, S)

        y_a = _rwkv(pr.reshape(B, S, n_rwkv_pad), mu, row(w0[layer]),
                    _lora_weight(w2[layer], a2[layer], g2[layer], width), row(a0[layer]),
                    row(k_k[layer]), row(k_a[layer]), row(r_k[layer]), row(lnx_g[layer]),
                    row(lnx_b[layer]), width)

        y_b = _diff_attn(q.reshape(B, S, width), k.reshape(B, S, width), vt, relb_t,
                         row(lam_q1[layer]), row(lam_k1[layer]), row(lam_q2[layer]), row(lam_k2[layer]),
                         subln_g[layer].reshape(-1, 1), lam_init)

        wo = w_out[layer].astype(BF16)
        rw = _pad_cols(router_w[layer], LANES)
        rb = jnp.concatenate([router_b[layer], jnp.full((LANES - N_EXPERTS,), NEG_BIG, F32)]).reshape(1, -1)
        x1, h2, gd, totals = _outproj(x.reshape(T, D), y_a.reshape(T, width), y_b.reshape(T, width),
                                      wo[:width], wo[width:], row(norm2_g[layer]), rw, rb)

        lp, gate, tab, cnt = _route(gd, totals, MOE_ROWS)
        tab = tab.reshape(-1, 8, LANES)[:, :, :N_EXPERTS]
        tabs = tuple(tab[:, r, :].reshape(-1) for r in range(3))
        run_end = cnt[1, :N_EXPERTS] + cnt[0, :N_EXPERTS]
        pad_end = cnt[1, :N_EXPERTS] + _padded_rows(cnt[0, :N_EXPERTS], MOE_ROWS)
        run_pad_max = (T // MOVE_ROWS) * N_EXPERTS * (RUN_CHUNK - 1)
        n_blocks = -(-(T * TOP_K + run_pad_max + N_EXPERTS * (MOE_ROWS - 1)) // MOE_ROWS)
        gap_start = jnp.concatenate([run_end, pad_end[-1:]])
        gap_end = jnp.concatenate([pad_end, jnp.full((1,), n_blocks * MOE_ROWS, F32)])
        tails = (gap_start.astype(jnp.int32), ((gap_end - gap_start) / RUN_CHUNK).astype(jnp.int32))
        blk_start = (jnp.arange(n_blocks) * MOE_ROWS).astype(F32)
        blk_expert = jnp.minimum(jnp.sum(pad_end[None, :] <= blk_start[:, None], axis=1), N_EXPERTS - 1)
        n_active = (pad_end[N_EXPERTS - 1] / MOE_ROWS).astype(jnp.int32).reshape(1)

        xs = _scatter_rows(tabs, tails, lp, h2, n_blocks * MOE_ROWS)
        ys = _moe(blk_expert.astype(jnp.int32), n_active, xs,
                  exp_w1[layer], exp_b1[layer][:, None, :], exp_w2[layer], exp_b2[layer][:, None, :])
        x = _combine(tabs, lp, gate, x1, ys).reshape(B, S, D)
    return x
```

```python
import functools
import math

import numpy as np
import jax
import jax.numpy as jnp
from jax import lax
from jax.experimental import pallas as pl
from jax.experimental.pallas import tpu as pltpu

F32 = jnp.float32
BF16 = jnp.bfloat16
HP = lax.Precision.HIGHEST

HEAD_DIM = 64
DECAY_LORA = 32
AAA_LORA = 32
GATE_LORA = 96
NUM_BUCKETS = 32
MAX_DISTANCE = 128
N_EXPERTS = 32
TOP_K = 4
SWIGLU_LIMIT = 7.0
SWIGLU_ALPHA = 1.702
NORM_EPS = 1e-5
LNX_EPS = 64e-5

LANES = 128
VMEM_LIMIT_BYTES = 56 * 1024 * 1024

NEG_BIG = -1e30
LOG2E = math.log2(math.e)
ACC_L = LANES
ACC_ROWS = LANES + 8

PROJ_ROWS = 512
RWKV_CHUNK = 64
RWKV_ROWS = 256
RWKV_SEQS = 2
ATT_TILE = 256
MOE_ROWS = 512
MOVE_ROWS = 256
RUN_CHUNK = 8
LOCAL_ROWS = MOVE_ROWS * 4 + 32 * RUN_CHUNK
LOCAL_CHUNKS = LOCAL_ROWS // RUN_CHUNK
CHUNK_LANES = 256

NN = (((1,), (0,)), ((), ()))
NT = (((1,), (1,)), ((), ()))
TN = (((0,), (0,)), ((), ()))


def _dot(a, b, dims=NN, prec=None):
    return lax.dot_general(a, b, dims, precision=prec, preferred_element_type=F32)


def _cparams(sem):
    return pltpu.CompilerParams(dimension_semantics=sem, vmem_limit_bytes=VMEM_LIMIT_BYTES)


def _sigmoid(x):
    return 1.0 / (1.0 + jnp.exp(-x))


def _lane_lo(shape):
    lane = lax.broadcasted_iota(jnp.int32, shape, len(shape) - 1)
    return (lane % LANES) < HEAD_DIM


def _head_rms_normalize(x, gain, scale):
    outs = []
    for t in range(x.shape[1] // LANES):
        xt = x[:, t * LANES:(t + 1) * LANES]
        lo = _lane_lo(xt.shape)
        x2 = xt * xt
        s0 = jnp.sum(jnp.where(lo, x2, 0.0), axis=-1, keepdims=True)
        s1 = jnp.sum(jnp.where(lo, 0.0, x2), axis=-1, keepdims=True)
        inv = lax.rsqrt(jnp.where(lo, s0, s1) * (1.0 / HEAD_DIM) + NORM_EPS)
        outs.append(xt * inv)
    return jnp.concatenate(outs, axis=-1) * gain * scale


def _inproj_kernel(x_ref, g_ref, w_ref, qg_ref, kg_ref, pr_ref, q_ref, k_ref, vt_ref, *, n_rwkv, width):
    x = x_ref[...]
    ms = jnp.mean(x * x, axis=-1, keepdims=True)
    h = (x * lax.rsqrt(ms + NORM_EPS) * g_ref[...]).astype(BF16)
    step = 384
    for c0 in range(0, n_rwkv, step):
        pr_ref[:, c0:c0 + step] = _dot(h, w_ref[:, c0:c0 + step])
    q = _dot(h, w_ref[:, n_rwkv:n_rwkv + width])
    q_ref[...] = _head_rms_normalize(q, qg_ref[...], HEAD_DIM ** -0.5 * LOG2E).astype(BF16)
    k = _dot(h, w_ref[:, n_rwkv + width:n_rwkv + 2 * width])
    k_ref[...] = _head_rms_normalize(k, kg_ref[...], 1.0).astype(BF16)
    v = _dot(h, w_ref[:, n_rwkv + 2 * width:n_rwkv + 3 * width])
    for hd in range(width // LANES):
        vt_ref[hd, 0:LANES, :] = v[:, hd * LANES:(hd + 1) * LANES].T.astype(BF16)
        vt_ref[hd, LANES:ACC_ROWS, :] = jnp.ones((ACC_ROWS - LANES, v.shape[0]), BF16)


def _inproj(x2d, g, w_all, qg, kg, n_rwkv, width, seq_len):
    T, D = x2d.shape
    n_all = w_all.shape[1]
    tm = PROJ_ROWS
    H = width // LANES
    per_seq = seq_len // tm
    row = lambda i: (i, 0)
    fix = lambda i: (0, 0)
    return pl.pallas_call(
        functools.partial(_inproj_kernel, n_rwkv=n_rwkv, width=width),
        grid=(T // tm,),
        in_specs=[
            pl.BlockSpec((tm, D), row),
            pl.BlockSpec((1, D), fix),
            pl.BlockSpec((D, n_all), fix),
            pl.BlockSpec((1, width), fix),
            pl.BlockSpec((1, width), fix),
        ],
        out_specs=[
            pl.BlockSpec((tm, n_rwkv), row),
            pl.BlockSpec((tm, width), row),
            pl.BlockSpec((tm, width), row),
            pl.BlockSpec((None, H, ACC_ROWS, tm), lambda i: (i // per_seq, 0, 0, i % per_seq)),
        ],
        out_shape=[
            jax.ShapeDtypeStruct((T, n_rwkv), F32),
            jax.ShapeDtypeStruct((T, width), BF16),
            jax.ShapeDtypeStruct((T, width), BF16),
            jax.ShapeDtypeStruct((T // seq_len, H, ACC_ROWS, seq_len), BF16),
        ],
        compiler_params=_cparams(("arbitrary",)),
        name="inproj",
    )(x2d, g, w_all, qg, kg)


def _split2(x):
    hi = x.astype(BF16)
    return hi, (x - hi.astype(F32)).astype(BF16)


def _mm(a, b, dims=NN, mode="bf16"):
    if mode == "bf16":
        return _dot(a.astype(BF16), b.astype(BF16), dims)
    ah, al = _split2(a)
    bh, bl = _split2(b)
    lhs = jnp.concatenate([ah, ah, al], axis=dims[0][0][0])
    rhs = jnp.concatenate([bh, bl, bh], axis=dims[0][1][0])
    return _dot(lhs, rhs, dims)


def _group_sum(x, ones2):
    hi, lo = _split2(x)
    return _dot(jnp.concatenate([hi, lo], axis=1), ones2)


P_INV = "x3"
P_SUBST = "bf16"
P_STATE = "bf16"
INV_BLOCK = 16
QUAD = 256


def _rwkv_kernel(pr_ref, mu_ref, w0_ref, wl_ref, a0_ref, kk_ref, ka_ref, rk_ref, lg_ref, lb_ref,
                 tri3_ref, ones2_ref, o_ref, state_sc, prev_sc, *, width):
    C = RWKV_CHUNK
    n_seq, seq_rows, _ = pr_ref.shape
    rows = n_seq * seq_rows
    n_quads = width // QUAD
    n_chunks = seq_rows // C
    heads_q = QUAD // HEAD_DIM

    @pl.when(pl.program_id(1) == 0)
    def _():
        state_sc[...] = jnp.zeros_like(state_sc)
        prev_sc[...] = jnp.zeros_like(prev_sc)

    x = jnp.concatenate([pr_ref[s] for s in range(n_seq)], axis=0)
    rid = lax.broadcasted_iota(jnp.int32, x.shape, 0)
    prev = pltpu.roll(x, 1, axis=0)
    for s in range(n_seq):
        prev = jnp.where(rid == s * seq_rows, prev_sc[s:s + 1, :], prev)
        prev_sc[s:s + 1, :] = x[(s + 1) * seq_rows - 1:(s + 1) * seq_rows, :]
    xs = x + (prev - x) * mu_ref[...]

    r = xs[:, 0:width]
    k = xs[:, width:2 * width]
    v = xs[:, 2 * width:3 * width]
    o = 3 * width
    lora_in = jnp.concatenate([jnp.tanh(xs[:, o:o + LANES]), xs[:, o + LANES:o + 2 * LANES],
                               _sigmoid(xs[:, o + 2 * LANES:o + 3 * LANES])], axis=1).astype(BF16)
    lora = _dot(lora_in, wl_ref[...])
    z = -(w0_ref[...] + lora[:, 0:width])
    softplus = jnp.maximum(z, 0.0) + jnp.log(1.0 + jnp.exp(-jnp.abs(z)))
    log_decay = -jnp.exp(-softplus - 0.5)
    a_lr = _sigmoid(a0_ref[...] + lora[:, width:2 * width])
    gate = lora[:, 2 * width:3 * width]

    ones2 = ones2_ref[...]
    kk = k * kk_ref[...]
    kk = kk * lax.rsqrt(jnp.maximum(_group_sum(kk * kk, ones2), 1e-24))
    k = k * (1.0 + (a_lr - 1.0) * ka_ref[...])
    a_vec = -kk
    b_vec = kk * a_lr

    p1 = log_decay.astype(BF16)
    rem = log_decay - p1.astype(F32)
    p2 = rem.astype(BF16)
    p3 = (rem - p2.astype(F32)).astype(BF16)
    cum = _dot(tri3_ref[...], jnp.concatenate([p1, p2, p3], axis=0))

    t_i = lax.broadcasted_iota(jnp.int32, (C, 2 * QUAD), 0)
    s_i = lax.broadcasted_iota(jnp.int32, (C, 2 * QUAD), 1) % C
    strict2 = t_i > s_i
    incl2 = t_i >= s_i
    lane_head = lax.broadcasted_iota(jnp.int32, (C, QUAD), 1) // HEAD_DIM
    row_blk = lax.broadcasted_iota(jnp.int32, (C, QUAD), 0) // INV_BLOCK
    same_blk = row_blk == ((lax.broadcasted_iota(jnp.int32, (C, QUAD), 1) % C) // INV_BLOCK)
    r2 = lax.broadcasted_iota(jnp.int32, (QUAD, QUAD), 0)
    c2 = lax.broadcasted_iota(jnp.int32, (QUAD, QUAD), 1)
    same_head = (r2 // HEAD_DIM) == (c2 // HEAD_DIM)

    def by_head(m):
        zero = jnp.zeros_like(m)
        return jnp.concatenate([jnp.where(lane_head == h, m, zero) for h in range(heads_q)], axis=0)

    def bd_rhs(m, mode):
        if mode == "bf16":
            return by_head(m.astype(BF16))
        hi, lo = _split2(m)
        hi_bd = by_head(hi)
        return jnp.concatenate([hi_bd, by_head(lo), hi_bd], axis=0)

    def mm_bd(a, rhs, mode):
        if mode == "bf16":
            return _dot(a.astype(BF16), rhs)
        ah, al = _split2(a)
        return _dot(jnp.concatenate([ah, ah, al], axis=1), rhs)

    class Work:
        pass

    n_par = n_seq * n_quads
    works = []
    for c in range(n_chunks):
        for sq in range(n_par):
            w = Work()
            s, q = divmod(sq, n_quads)
            ls = slice(q * QUAD, (q + 1) * QUAD)
            rs = slice(s * seq_rows + c * C, s * seq_rows + (c + 1) * C)
            ld = log_decay[rs, ls]
            cm = cum[rs, ls]
            cl = cm[C - 1:C, :]
            e_in = jnp.exp(cm)
            e_neg = jnp.exp(-cm)
            e_hat = jnp.exp(cl - cm)
            w.q, w.c = sq, c
            w.dec = jnp.exp(cl)
            w.vv = v[rs, ls]
            w.ar_t = jnp.concatenate([a_vec[rs, ls] * jnp.exp(cm - ld), r[rs, ls] * e_in], axis=0)
            w.bk_t = jnp.concatenate([bd_rhs(b_vec[rs, ls] * e_neg, "bf16"), bd_rhs(k[rs, ls] * e_neg, "bf16")], axis=0)
            w.bk_h = jnp.concatenate([b_vec[rs, ls] * e_hat, k[rs, ls] * e_hat], axis=0)
            works.append(w)

    for w in works:
        gram = _dot(w.ar_t.astype(BF16), w.bk_t, NT)
        low = jnp.where(strict2, gram[0:C, :], 0.0)
        l_cat = low[:, 0:QUAD]
        rbk = jnp.where(incl2, gram[C:2 * C, :], 0.0)
        w.rb_cat = rbk[:, 0:QUAD]
        w.kv_lhs = jnp.concatenate([low[:, QUAD:], rbk[:, QUAD:]], axis=0)
        w.l_off = jnp.where(same_blk, 0.0, l_cat)
        w.d_inv = jnp.where(same_blk, l_cat, 0.0)

    for w in works:
        w.l_pow = mm_bd(w.d_inv, bd_rhs(w.d_inv, P_INV), P_INV)
    n_sq = int(math.log2(INV_BLOCK))
    for it in range(1, n_sq):
        for w in works:
            if it + 1 < n_sq:
                both = mm_bd(jnp.concatenate([w.d_inv, w.l_pow], axis=0), bd_rhs(w.l_pow, P_INV), P_INV)
                w.d_inv = w.d_inv + w.l_pow + both[0:C]
                w.l_pow = both[C:2 * C]
            else:
                w.d_inv = w.d_inv + w.l_pow + mm_bd(w.d_inv, bd_rhs(w.l_pow, P_INV), P_INV)
    for w in works:
        w.m1 = w.l_off + mm_bd(w.d_inv, bd_rhs(w.l_off, P_SUBST), P_SUBST)
        w.m1_bd = bd_rhs(w.m1, P_SUBST)
    for w in works:
        w.m2 = mm_bd(w.m1, w.m1_bd, P_SUBST)
    for w in works:
        w.nn = w.m1 + w.m2 + mm_bd(w.m2, w.m1_bd, P_SUBST)
    for w in works:
        w.t_inv = w.nn + w.d_inv + mm_bd(w.nn, bd_rhs(w.d_inv, P_SUBST), P_SUBST)
    for w in works:
        w.kv = mm_bd(w.kv_lhs, bd_rhs(w.vv, P_STATE), P_STATE)

    states = [state_sc[sq] for sq in range(n_par)]
    y_parts = [[] for _ in range(n_par)]
    for c in range(n_chunks):
        ws = works[c * n_par:(c + 1) * n_par]
        for w in ws:
            w.base = _mm(w.ar_t, states[w.q], NT, mode=P_STATE) + w.kv
        for w in ws:
            u0 = w.base[0:C]
            w.u = u0 + mm_bd(w.t_inv, bd_rhs(u0, P_STATE), P_STATE)
        for w in ws:
            upd = _mm(jnp.concatenate([w.u, w.vv], axis=0), w.bk_h, TN, mode=P_STATE)
            states[w.q] = states[w.q] * w.dec + jnp.where(same_head, upd, 0.0)
        for w in ws:
            y_parts[w.q].append(w.base[C:2 * C] + mm_bd(w.rb_cat, bd_rhs(w.u, P_STATE), P_STATE))
    for sq in range(n_par):
        state_sc[sq] = states[sq]
    y = jnp.concatenate([jnp.concatenate([jnp.concatenate(y_parts[s * n_quads + q], axis=0)
                                          for q in range(n_quads)], axis=-1)
                         for s in range(n_seq)], axis=0)

    inv_n = 1.0 / HEAD_DIM
    sums = _group_sum(jnp.concatenate([y, r * k * rk_ref[...]], axis=0), ones2)
    mean = sums[0:rows] * inv_n
    yc = y - mean
    var = _group_sum(yc * yc, ones2) * inv_n
    yn = yc * lax.rsqrt(var + LNX_EPS) * lg_ref[...] + lb_ref[...]
    out = ((yn + sums[rows:2 * rows] * v) * gate).astype(o_ref.dtype)
    for s in range(n_seq):
        o_ref[s] = out[s * seq_rows:(s + 1) * seq_rows]


def _rwkv(pr, mu, w0, w_lora, a0, k_k, k_a, r_k, lnx_g, lnx_b, width):
    B, S, n_rwkv = pr.shape
    rows = RWKV_ROWS
    n_seq = RWKV_SEQS
    C = RWKV_CHUNK
    rr = np.arange(n_seq * rows)[:, None]
    cc = np.arange(n_seq * rows)[None, :]
    tri = ((rr // C == cc // C) & (cc <= rr)).astype(np.float32)
    tri3 = jnp.asarray(np.concatenate([tri, tri, tri], axis=1), BF16)
    gg = np.arange(width)
    ones = (gg[:, None] // HEAD_DIM == gg[None, :] // HEAD_DIM).astype(np.float32)
    ones2 = jnp.asarray(np.concatenate([ones, ones], axis=0), BF16)
    fix = lambda shape: pl.BlockSpec(shape, lambda b, i: (0, 0))
    vec = lambda n: fix((1, n))
    return pl.pallas_call(
        functools.partial(_rwkv_kernel, width=width),
        grid=(B // n_seq, S // rows),
        in_specs=[
            pl.BlockSpec((n_seq, rows, n_rwkv), lambda b, i: (b, i, 0)),
            vec(n_rwkv), vec(width), fix(w_lora.shape), vec(width),
            vec(width), vec(width), vec(width), vec(width), vec(width),
            fix(tri3.shape), fix(ones2.shape),
        ],
        out_specs=pl.BlockSpec((n_seq, rows, width), lambda b, i: (b, i, 0)),
        out_shape=jax.ShapeDtypeStruct((B, S, width), BF16),
        scratch_shapes=[
            pltpu.VMEM((n_seq * (width // QUAD), QUAD, QUAD), F32),
            pltpu.VMEM((n_seq, n_rwkv), F32),
        ],
        compiler_params=_cparams(("arbitrary", "arbitrary")),
        name="rwkv7",
    )(pr, mu, w0, w_lora, a0, k_k, k_a, r_k, lnx_g, lnx_b, tri3, ones2)


def _t5_bucket_np(dist):
    n = np.maximum(dist, 0)
    max_exact = NUM_BUCKETS // 2
    nf = np.maximum(n, 1).astype(np.float32)
    large = max_exact + (np.log(nf / max_exact) / math.log(MAX_DISTANCE / max_exact)
                         * (NUM_BUCKETS - max_exact)).astype(np.int32)
    large = np.minimum(large, NUM_BUCKETS - 1)
    return np.where(n < max_exact, n, large).astype(np.int32)


def _near_bucket_tiles(tile):
    kpos = np.arange(tile)[:, None]
    qpos = np.arange(tile)[None, :]
    out = []
    for delta in (0, 1):
        dist = delta * tile + qpos - kpos
        out.append(np.where(dist >= 0, _t5_bucket_np(dist), -1))
    return np.stack(out).astype(np.int32)


def _attn_kernel(relb_ref, bucket_ref, lq1_ref, lk1_ref, lq2_ref, lk2_ref, sg_ref,
                 q_ref, k_ref, vt_ref, o_ref, bias_sc, q2_sc, m_sc, acc_sc, st_sc, *, lam_init):
    tq = q_ref.shape[0]
    n_heads = vt_ref.shape[0]
    heads = range(n_heads)
    qi = pl.program_id(1)

    @pl.when(qi == 0)
    def _():
        for h in heads:
            for d in range(2):
                bk = bucket_ref[d]
                tile = jnp.full(bk.shape, NEG_BIG, F32)
                for j in range(NUM_BUCKETS):
                    tile = jnp.where(bk == j, relb_ref[h, j] * LOG2E, tile)
                bias_sc[h, d, :, 0:tq] = tile
                bias_sc[h, d, :, tq:2 * tq] = tile

    lo = _lane_lo((tq, LANES))
    for h in heads:
        q = q_ref[:, h * LANES:(h + 1) * LANES]
        zero = jnp.zeros_like(q)
        q2_sc[h, 0:tq, :] = jnp.where(lo, q, zero)
        q2_sc[h, tq:2 * tq, :] = jnp.where(lo, zero, q)
    m_sc[...] = jnp.full(m_sc.shape, NEG_BIG, F32)
    acc_sc[...] = jnp.zeros_like(acc_sc)

    def step(k0, tk, near):
        for h in heads:
            st_sc[h, 0:tk, :] = _dot(k_ref[pl.ds(k0, tk), h * LANES:(h + 1) * LANES], q2_sc[h], NT)
        pvs, alphas = [], []
        for h in heads:
            m_old = m_sc[h]
            if near is None:
                st = st_sc[h, 0:tk, :]
                far = relb_ref[h, NUM_BUCKETS - 1] * LOG2E
                m_new = jnp.maximum(m_old, jnp.max(st, axis=0, keepdims=True) + far)
                p = jnp.exp2(st - (m_new - far))
            else:
                st = st_sc[h, 0:tk, :] + bias_sc[h, near]
                m_new = jnp.maximum(m_old, jnp.max(st, axis=0, keepdims=True))
                p = jnp.exp2(st - m_new)
            m_sc[h] = m_new
            pvs.append(_dot(vt_ref[h, :, pl.ds(k0, tk)], p.astype(BF16)))
            alphas.append(jnp.exp2(m_old - m_new))
        for h in heads:
            acc_sc[h] = alphas[h] * acc_sc[h] + pvs[h]

    n_far = jnp.maximum(qi - 1, 0)
    far_tk = 2 * tq

    def far_body(i, carry):
        step(pl.multiple_of(i * far_tk, far_tk), far_tk, None)
        return carry

    lax.fori_loop(0, n_far // 2, far_body, 0)

    @pl.when(n_far % 2 == 1)
    def _():
        step(pl.multiple_of((n_far - 1) * tq, tq), tq, None)

    @pl.when(qi >= 1)
    def _():
        step(pl.multiple_of((qi - 1) * tq, tq), tq, 1)

    step(pl.multiple_of(qi * tq, tq), tq, 0)

    lam = (jnp.exp(jnp.sum(lq1_ref[...] * lk1_ref[...], axis=-1, keepdims=True))
           - jnp.exp(jnp.sum(lq2_ref[...] * lk2_ref[...], axis=-1, keepdims=True)) + lam_init)
    for h in heads:
        acc = acc_sc[h]
        l = acc[ACC_L:ACC_L + 1, :]
        o1 = acc[0:LANES, 0:tq] / l[:, 0:tq]
        o2 = acc[0:LANES, tq:2 * tq] / l[:, tq:2 * tq]
        ot = o1 - lam * o2
        ms = jnp.mean(ot * ot, axis=0, keepdims=True)
        y = ot * lax.rsqrt(ms + NORM_EPS) * sg_ref[...] * (1.0 - lam_init)
        o_ref[:, h * LANES:(h + 1) * LANES] = y.T.astype(o_ref.dtype)


def _diff_attn(q, k, vt, relb_t, lq1, lk1, lq2, lk2, sg_col, lam_init):
    B, S, width = q.shape
    H = width // LANES
    t = ATT_TILE
    buckets = jnp.asarray(_near_bucket_tiles(t))
    vec = pl.BlockSpec((1, HEAD_DIM), lambda b, i: (0, 0))
    return pl.pallas_call(
        functools.partial(_attn_kernel, lam_init=lam_init),
        grid=(B, S // t),
        in_specs=[
            pl.BlockSpec(memory_space=pltpu.SMEM),
            pl.BlockSpec((2, t, t), lambda b, i: (0, 0, 0)),
            vec, vec, vec, vec,
            pl.BlockSpec((LANES, 1), lambda b, i: (0, 0)),
            pl.BlockSpec((None, t, width), lambda b, i: (b, i, 0)),
            pl.BlockSpec((None, S, width), lambda b, i: (b, 0, 0)),
            pl.BlockSpec((None, H, ACC_ROWS, S), lambda b, i: (b, 0, 0, 0)),
        ],
        out_specs=pl.BlockSpec((None, t, width), lambda b, i: (b, i, 0)),
        out_shape=jax.ShapeDtypeStruct((B, S, width), BF16),
        scratch_shapes=[
            pltpu.VMEM((H, 2, t, 2 * t), F32),
            pltpu.VMEM((H, 2 * t, LANES), BF16),
            pltpu.VMEM((H, 1, 2 * t), F32),
            pltpu.VMEM((H, ACC_ROWS, 2 * t), F32),
            pltpu.VMEM((H, 2 * t, 2 * t), F32),
        ],
        compiler_params=_cparams(("arbitrary", "arbitrary")),
        name="diff_attn",
    )(relb_t, buckets, lq1, lk1, lq2, lk2, sg_col, q, k, vt)


def _run_rows(count):
    return jnp.ceil(count * (1.0 / RUN_CHUNK)) * RUN_CHUNK


def _outproj_kernel(x_ref, ya_ref, yb_ref, wa_ref, wb_ref, g_ref, rw_ref, rb_ref,
                    x1_ref, h2_ref, gd_ref, cnt_ref):
    x1 = x_ref[...] + _dot(ya_ref[...], wa_ref[...]) + _dot(yb_ref[...], wb_ref[...])
    x1_ref[...] = x1
    ms = jnp.mean(x1 * x1, axis=-1, keepdims=True)
    h2 = x1 * lax.rsqrt(ms + NORM_EPS) * g_ref[...]
    h2_ref[...] = h2.astype(h2_ref.dtype)
    logits = _mm(h2, rw_ref[...], mode="x3") + rb_ref[...]
    lane = lax.broadcasted_iota(jnp.int32, logits.shape, 1).astype(F32)
    work = logits
    picks = []
    for _ in range(TOP_K):
        m = jnp.max(work, axis=-1, keepdims=True)
        idx = jnp.min(jnp.where(work == m, lane, float(LANES)), axis=-1, keepdims=True)
        hit = lane == idx
        picks.append((m, hit))
        work = jnp.where(hit, NEG_BIG, work)
    m0 = picks[0][0]
    es = [jnp.exp(m - m0) for m, _ in picks]
    denom = es[0] + es[1] + es[2] + es[3]
    gd = jnp.zeros_like(logits)
    for e, (_, hit) in zip(es, picks):
        gd = jnp.where(hit, e / denom, gd)
    gd_ref[...] = gd

    @pl.when(pl.program_id(0) == 0)
    def _():
        cnt_ref[...] = jnp.zeros_like(cnt_ref)

    cnt_ref[0:1, :] = cnt_ref[0:1, :] + _run_rows(jnp.sum(jnp.where(gd > 0.0, 1.0, 0.0), axis=0, keepdims=True))


def _outproj(x2d, ya, yb, wa, wb, g2, rw, rb):
    T, D = x2d.shape
    half = ya.shape[1]
    tm = MOVE_ROWS
    row = lambda i: (i, 0)
    fix = lambda i: (0, 0)
    return pl.pallas_call(
        _outproj_kernel,
        grid=(T // tm,),
        in_specs=[
            pl.BlockSpec((tm, D), row),
            pl.BlockSpec((tm, half), row),
            pl.BlockSpec((tm, half), row),
            pl.BlockSpec((half, D), fix),
            pl.BlockSpec((half, D), fix),
            pl.BlockSpec((1, D), fix),
            pl.BlockSpec((D, LANES), fix),
            pl.BlockSpec((1, LANES), fix),
        ],
        out_specs=[
            pl.BlockSpec((tm, D), row),
            pl.BlockSpec((tm, D), row),
            pl.BlockSpec((tm, LANES), row),
            pl.BlockSpec((8, LANES), fix),
        ],
        out_shape=[
            jax.ShapeDtypeStruct((T, D), F32),
            jax.ShapeDtypeStruct((T, D), BF16),
            jax.ShapeDtypeStruct((T, LANES), F32),
            jax.ShapeDtypeStruct((8, LANES), F32),
        ],
        compiler_params=_cparams(("arbitrary",)),
        name="outproj_router",
    )(x2d, ya, yb, wa, wb, g2, rw, rb)


def _padded_rows(cnt, rb):
    return jnp.ceil(cnt * (1.0 / rb)) * rb


def _route_kernel(gd_ref, tot_ref, lp_ref, gate_ref, tab_ref, cnt_ref, carry_sc, pstart_sc, *, rb):
    i = pl.program_id(0)
    tb = gd_ref.shape[0]
    gd = gd_ref[...]
    sel = gd > 0.0
    self = jnp.where(sel, 1.0, 0.0)
    run_pad = _run_rows(jnp.sum(self, axis=0, keepdims=True))
    r2 = lax.broadcasted_iota(jnp.int32, (LANES, LANES), 0)
    c2 = lax.broadcasted_iota(jnp.int32, (LANES, LANES), 1)

    @pl.when(i == 0)
    def _():
        cnt = tot_ref[0:1, :]
        upper = jnp.where(r2 < c2, 1.0, 0.0)
        pstart = _dot(jnp.broadcast_to(_padded_rows(cnt, rb), (8, LANES)), upper, prec=HP)[0:1, :]
        pstart_sc[...] = pstart
        cnt_ref[0:1, :] = cnt
        cnt_ref[1:2, :] = pstart
        cnt_ref[2:8, :] = jnp.zeros((6, LANES), F32)
        carry_sc[...] = jnp.zeros_like(carry_sc)

    rr = lax.broadcasted_iota(jnp.int32, (tb, tb), 0)
    cc = lax.broadcasted_iota(jnp.int32, (tb, tb), 1)
    lower = jnp.where(cc < rr, 1.0, 0.0).astype(BF16)
    sel_b = self.astype(BF16)
    prefix = _dot(lower, sel_b)
    upper_b = jnp.where(r2 < c2, 1.0, 0.0).astype(BF16)
    off = _dot(jnp.broadcast_to(run_pad, (8, LANES)).astype(BF16), upper_b)[0:1, :]
    lp = off + prefix
    upper_incl = jnp.where(r2 <= c2, 1.0, 0.0).astype(BF16)
    slot = _dot(sel_b, upper_incl)
    lane = lax.broadcasted_iota(jnp.int32, (tb, LANES), 1)
    p_out = jnp.zeros((tb, LANES), F32)
    g_out = jnp.zeros((tb, LANES), F32)
    for kk in range(TOP_K):
        mk = sel & (slot == float(kk + 1))
        p_k = jnp.sum(jnp.where(mk, lp, 0.0), axis=-1, keepdims=True)
        g_k = jnp.sum(jnp.where(mk, gd, 0.0), axis=-1, keepdims=True)
        p_out = jnp.where(lane == kk, p_k, p_out)
        g_out = jnp.where(lane == kk, g_k, g_out)
    lp_ref[...] = p_out
    gate_ref[...] = g_out
    def as_column(v):
        return jnp.sum(jnp.where(r2 == c2, jnp.broadcast_to(v, (LANES, LANES)), 0.0), axis=1, keepdims=True)

    first = as_column(off * (1.0 / RUN_CHUNK))
    count = as_column(run_pad * (1.0 / RUN_CHUNK))
    start = as_column(pstart_sc[...] + carry_sc[...])
    chunk = lax.broadcasted_iota(jnp.int32, (LANES, CHUNK_LANES), 1).astype(F32)
    mine = (chunk >= first) & (chunk < first + count)
    dst = jnp.sum(jnp.where(mine, start + (chunk - first) * RUN_CHUNK, 0.0), axis=0, keepdims=True)
    n_chunks = jnp.sum(count, axis=0, keepdims=True)
    tab_ref[0:1, :] = dst.astype(jnp.int32)
    tab_ref[1:2, :] = jnp.broadcast_to(n_chunks, (1, CHUNK_LANES)).astype(jnp.int32)
    tab_ref[2:8, :] = jnp.zeros((6, CHUNK_LANES), jnp.int32)
    carry_sc[...] = carry_sc[...] + run_pad


def _route(gd, totals, rb):
    T = gd.shape[0]
    tb = MOVE_ROWS
    blk = lambda i: (i, 0)
    fix = lambda i: (0, 0)
    return pl.pallas_call(
        functools.partial(_route_kernel, rb=rb),
        grid=(T // tb,),
        in_specs=[pl.BlockSpec((tb, LANES), blk), pl.BlockSpec((8, LANES), fix)],
        out_specs=[
            pl.BlockSpec((tb, LANES), blk),
            pl.BlockSpec((tb, LANES), blk),
            pl.BlockSpec((8, CHUNK_LANES), blk),
            pl.BlockSpec((8, LANES), fix),
        ],
        out_shape=[
            jax.ShapeDtypeStruct((T, LANES), F32),
            jax.ShapeDtypeStruct((T, LANES), F32),
            jax.ShapeDtypeStruct((T // tb * 8, CHUNK_LANES), jnp.int32),
            jax.ShapeDtypeStruct((8, LANES), F32),
        ],
        scratch_shapes=[pltpu.VMEM((1, LANES), F32), pltpu.VMEM((1, LANES), F32)],
        compiler_params=_cparams(("arbitrary",)),
        name="route_positions",
    )(gd, totals)


def _chunk_copy(src_ref, src_row, dst_ref, dst_row, sem):
    return pltpu.make_async_copy(src_ref.at[pl.ds(src_row, RUN_CHUNK), :],
                                 dst_ref.at[pl.ds(dst_row, RUN_CHUNK), :], sem)


def _start_runs(b, nc_ref, dst_ref, copy_chunk):
    def issue(c, carry):
        copy_chunk(pl.multiple_of(c * RUN_CHUNK, RUN_CHUNK),
                   pl.multiple_of(dst_ref[b * LOCAL_CHUNKS + c], RUN_CHUNK)).start()
        return carry

    lax.fori_loop(0, nc_ref[b], issue, 0)


def _wait_runs(b, nc_ref, wait_rows):
    total = nc_ref[b]
    p = 1
    while p <= LOCAL_CHUNKS:
        @pl.when((total & p) != 0)
        def _(p=p):
            wait_rows(p * RUN_CHUNK).wait()
        p *= 2


def _rows_copy(src_ref, dst_ref, n_rows, sem):
    return pltpu.make_async_copy(src_ref.at[pl.ds(0, n_rows), :], dst_ref.at[pl.ds(0, n_rows), :], sem)


def _local_onehot(lp, lb):
    j = lax.broadcasted_iota(jnp.int32, (lp.shape[0], lb), 1).astype(F32)
    hit = lp[:, 0:1] == j
    for kk in range(1, TOP_K):
        hit = hit | (lp[:, kk:kk + 1] == j)
    return hit


def _scatter_kernel(nc_ref, dst_ref, t0_ref, tc_ref, lp_ref, h_ref, xs_ref, buf, zero_buf, sems):
    b = pl.program_id(0)
    slot = b % 2
    pt = jnp.where(_local_onehot(lp_ref[...], buf.shape[1]), 1.0, 0.0).astype(BF16)
    buf[slot] = _dot(pt, h_ref[...], TN)
    _start_runs(b, nc_ref, dst_ref,
                lambda s, d: _chunk_copy(buf.at[slot], s, xs_ref, d, sems.at[slot]))

    @pl.when(b >= 1)
    def _():
        _wait_runs(b - 1, nc_ref, lambda n: _rows_copy(buf.at[1 - slot], xs_ref, n, sems.at[1 - slot]))

    @pl.when(b == pl.num_programs(0) - 1)
    def _():
        _wait_runs(b, nc_ref, lambda n: _rows_copy(buf.at[slot], xs_ref, n, sems.at[slot]))
        zero_buf[...] = jnp.zeros_like(zero_buf)
        total = 0
        for e in range(N_EXPERTS + 1):
            t0 = t0_ref[e]

            def issue(c, carry, t0=t0):
                _chunk_copy(zero_buf, 0, xs_ref, pl.multiple_of(t0 + c * RUN_CHUNK, RUN_CHUNK), sems.at[0]).start()
                return carry

            lax.fori_loop(0, tc_ref[e], issue, 0)
            total = total + tc_ref[e]

        def drain(c, carry):
            _chunk_copy(zero_buf, 0, xs_ref, 0, sems.at[0]).wait()
            return carry

        lax.fori_loop(0, total, drain, 0)


def _scatter_rows(tabs, tails, lp, h2, n_rows):
    T, D = h2.shape
    tb = MOVE_ROWS
    return pl.pallas_call(
        _scatter_kernel,
        grid_spec=pltpu.PrefetchScalarGridSpec(
            num_scalar_prefetch=4,
            grid=(T // tb,),
            in_specs=[
                pl.BlockSpec((tb, LANES), lambda i, *_: (i, 0)),
                pl.BlockSpec((tb, D), lambda i, *_: (i, 0)),
            ],
            out_specs=pl.BlockSpec(memory_space=pl.ANY),
            scratch_shapes=[pltpu.VMEM((2, LOCAL_ROWS, D), F32), pltpu.VMEM((RUN_CHUNK, D), F32),
                            pltpu.SemaphoreType.DMA((2,))],
        ),
        out_shape=jax.ShapeDtypeStruct((n_rows, D), F32),
        compiler_params=_cparams(("arbitrary",)),
        name="scatter_rows",
    )(*tabs, *tails, lp, h2)


def _moe_kernel(be_ref, na_ref, xs_ref, w1_ref, b1_ref, w2_ref, b2_ref, ys_ref, w1b, w2b):
    j = pl.program_id(0)

    @pl.when((j == 0) | (be_ref[j] != be_ref[jnp.maximum(j - 1, 0)]))
    def _():
        w1b[...] = w1_ref[...].astype(BF16)
        w2b[...] = w2_ref[...].astype(BF16)

    @pl.when(j < na_ref[0])
    def _():
        xb = xs_ref[...].astype(BF16)
        hid = _dot(xb, w1b[...]) + b1_ref[...]
        half = hid.shape[1] // 2
        x_glu = jnp.minimum(hid[:, :half], SWIGLU_LIMIT)
        x_lin = jnp.clip(hid[:, half:], -SWIGLU_LIMIT, SWIGLU_LIMIT)
        act = x_glu * _sigmoid(SWIGLU_ALPHA * x_glu) * (x_lin + 1.0)
        ys_ref[...] = _dot(act.astype(BF16), w2b[...]) + b2_ref[...]

    @pl.when(j >= na_ref[0])
    def _():
        ys_ref[...] = jnp.zeros_like(ys_ref)


def _moe(blk_expert, n_active, xs, w1, b1, w2, b2):
    R, D = xs.shape
    E, _, D2 = w1.shape
    rb = MOE_ROWS
    nb = R // rb
    blk = lambda j, be, na: (jnp.minimum(j, na[0] - 1), 0)
    return pl.pallas_call(
        _moe_kernel,
        grid_spec=pltpu.PrefetchScalarGridSpec(
            num_scalar_prefetch=2,
            grid=(nb,),
            in_specs=[
                pl.BlockSpec((rb, D), blk),
                pl.BlockSpec((None, D, D2), lambda j, be, na: (be[j], 0, 0)),
                pl.BlockSpec((None, 1, D2), lambda j, be, na: (be[j], 0, 0)),
                pl.BlockSpec((None, D2 // 2, D), lambda j, be, na: (be[j], 0, 0)),
                pl.BlockSpec((None, 1, D), lambda j, be, na: (be[j], 0, 0)),
            ],
            out_specs=pl.BlockSpec((rb, D), lambda j, be, na: (j, 0)),
            scratch_shapes=[pltpu.VMEM((D, D2), BF16), pltpu.VMEM((D2 // 2, D), BF16)],
        ),
        out_shape=jax.ShapeDtypeStruct((R, D), F32),
        compiler_params=_cparams(("arbitrary",)),
        name="moe_experts",
    )(blk_expert, n_active, xs, w1, b1, w2, b2)


def _combine_kernel(nc_ref, dst_ref, lp_ref, gate_ref, x1_ref, ys_ref, o_ref, buf, sems):
    b = pl.program_id(0)
    slot = b % 2

    def fetch(blk, sl):
        _start_runs(blk, nc_ref, dst_ref,
                    lambda s, d: _chunk_copy(ys_ref, d, buf.at[sl], s, sems.at[sl]))

    @pl.when(b == 0)
    def _():
        buf[...] = jnp.zeros_like(buf)
        fetch(b, slot)

    @pl.when(b + 1 < pl.num_programs(0))
    def _():
        fetch(b + 1, 1 - slot)

    _wait_runs(b, nc_ref, lambda n: _rows_copy(ys_ref, buf.at[slot], n, sems.at[slot]))
    lp = lp_ref[...]
    g = gate_ref[...]
    _, lb, d = buf.shape
    j = lax.broadcasted_iota(jnp.int32, (lp.shape[0], lb), 1).astype(F32)
    gt = jnp.zeros(j.shape, F32)
    for kk in range(TOP_K):
        gt = jnp.where(lp[:, kk:kk + 1] == j, g[:, kk:kk + 1], gt)
    g_hi, g_lo = _split2(gt)
    ones = jnp.ones((lp.shape[0], LANES), BF16)
    g_row = _dot(g_hi, ones, TN) + _dot(g_lo, ones, TN)
    rows = (buf[slot] * jnp.concatenate([g_row] * (d // LANES), axis=1)).astype(BF16)
    p01 = jnp.where(gt != 0.0, 1.0, 0.0).astype(BF16)
    o_ref[...] = x1_ref[...] + _dot(p01, rows)


def _combine(tabs, lp, gate, x1, ys):
    T, D = x1.shape
    tb = MOVE_ROWS
    return pl.pallas_call(
        _combine_kernel,
        grid_spec=pltpu.PrefetchScalarGridSpec(
            num_scalar_prefetch=2,
            grid=(T // tb,),
            in_specs=[
                pl.BlockSpec((tb, LANES), lambda i, *_: (i, 0)),
                pl.BlockSpec((tb, LANES), lambda i, *_: (i, 0)),
                pl.BlockSpec((tb, D), lambda i, *_: (i, 0)),
                pl.BlockSpec(memory_space=pl.ANY),
            ],
            out_specs=pl.BlockSpec((tb, D), lambda i, *_: (i, 0)),
            scratch_shapes=[pltpu.VMEM((2, LOCAL_ROWS, D), F32), pltpu.SemaphoreType.DMA((2,))],
        ),
        out_shape=jax.ShapeDtypeStruct((T, D), F32),
        compiler_params=_cparams(("arbitrary",)),
        name="combine_rows",
    )(*tabs, lp, gate, x1, ys)


def _pad_cols(a, n):
    return jnp.pad(a, ((0, 0), (0, n - a.shape[1])))


def _pad_rows(a, n):
    return jnp.pad(a, ((0, n - a.shape[0]), (0, 0)))


def _lora_weight(w2, a2, g2, width):
    z = jnp.zeros((LANES, width), F32)
    return jnp.concatenate([
        jnp.concatenate([_pad_rows(w2, LANES), z, z], axis=1),
        jnp.concatenate([z, _pad_rows(a2, LANES), z], axis=1),
        jnp.concatenate([z, z, _pad_rows(g2, LANES)], axis=1),
    ], axis=0).astype(BF16)


def _split_rwkv_cols(a, width):
    o = 3 * width
    return jnp.concatenate([
        a[:, :o],
        _pad_cols(a[:, o:o + DECAY_LORA], LANES),
        _pad_cols(a[:, o + DECAY_LORA:o + DECAY_LORA + AAA_LORA], LANES),
        _pad_cols(a[:, o + DECAY_LORA + AAA_LORA:o + DECAY_LORA + AAA_LORA + GATE_LORA], LANES),
    ], axis=1)


def kernel(x, norm1_g, w_in, mu_shift, w0, w2, a0, a2, g2, k_k, k_a, r_k, lnx_g, lnx_b, qn_g, kn_g,
           lam_q1, lam_k1, lam_q2, lam_k2, subln_g, rel_bias, w_out, norm2_g, router_w, router_b,
           exp_w1, exp_b1, exp_w2, exp_b2):
    B, S, D = x.shape
    T = B * S
    depth = norm1_g.shape[0]
    width = w0.shape[1]
    n_rwkv_cols = 3 * width + DECAY_LORA + AAA_LORA + GATE_LORA
    n_rwkv_pad = 3 * width + 3 * LANES
    n_groups = width // HEAD_DIM
    H = width // LANES
    relb_t = rel_bias.T
    row = lambda a: a.reshape(1, -1)

    for layer in range(depth):
        lam_init = 0.8 - 0.6 * math.exp(-0.3 * layer)
        w_l = w_in[layer]
        w_all = jnp.concatenate([_split_rwkv_cols(w_l[:, :n_rwkv_cols], width), w_l[:, n_rwkv_cols:]],
                                axis=1).astype(BF16)
        mu = _split_rwkv_cols(row(mu_shift[layer]), width)
        qg = row(jnp.tile(qn_g[layer], n_groups))
        kg = row(jnp.tile(kn_g[layer], n_groups))

        pr, q, k, vt = _inproj(x.reshape(T, D), row(norm1_g[layer]), w_all, qg, kg, n_rwkv_pad, width, S)

        y_a = _rwkv(pr.reshape(B, S, n_rwkv_pad), mu, row(w0[layer]),
                    _lora_weight(w2[layer], a2[layer], g2[layer], width), row(a0[layer]),
                    row(k_k[layer]), row(k_a[layer]), row(r_k[layer]), row(lnx_g[layer]),
                    row(lnx_b[layer]), width)

        y_b = _diff_attn(q.reshape(B, S, width), k.reshape(B, S, width), vt, relb_t,
                         row(lam_q1[layer]), row(lam_k1[layer]), row(lam_q2[layer]), row(lam_k2[layer]),
                         subln_g[layer].reshape(-1, 1), lam_init)

        wo = w_out[layer].astype(BF16)
        rw = _pad_cols(router_w[layer], LANES)
        rb = jnp.concatenate([router_b[layer], jnp.full((LANES - N_EXPERTS,), NEG_BIG, F32)]).reshape(1, -1)
        x1, h2, gd, totals = _outproj(x.reshape(T, D), y_a.reshape(T, width), y_b.reshape(T, width),
                                      wo[:width], wo[width:], row(norm2_g[layer]), rw, rb)

        lp, gate, tab, cnt = _route(gd, totals, MOE_ROWS)
        tab = tab.reshape(-1, 8, CHUNK_LANES)
        tabs = (tab[:, 1, 0], tab[:, 0, :LOCAL_CHUNKS].reshape(-1))
        run_end = cnt[1, :N_EXPERTS] + cnt[0, :N_EXPERTS]
        pad_end = cnt[1, :N_EXPERTS] + _padded_rows(cnt[0, :N_EXPERTS], MOE_ROWS)
        run_pad_max = (T // MOVE_ROWS) * N_EXPERTS * (RUN_CHUNK - 1)
        n_blocks = -(-(T * TOP_K + run_pad_max + N_EXPERTS * (MOE_ROWS - 1)) // MOE_ROWS)
        gap_start = jnp.concatenate([run_end, pad_end[-1:]])
        gap_end = jnp.concatenate([pad_end, jnp.full((1,), n_blocks * MOE_ROWS, F32)])
        tails = (gap_start.astype(jnp.int32), ((gap_end - gap_start) / RUN_CHUNK).astype(jnp.int32))
        blk_start = (jnp.arange(n_blocks) * MOE_ROWS).astype(F32)
        blk_expert = jnp.minimum(jnp.sum(pad_end[None, :] <= blk_start[:, None], axis=1), N_EXPERTS - 1)
        n_active = (pad_end[N_EXPERTS - 1] / MOE_ROWS).astype(jnp.int32).reshape(1)

        xs = _scatter_rows(tabs, tails, lp, h2, n_blocks * MOE_ROWS)
        ys = _moe(blk_expert.astype(jnp.int32), n_active, xs,
                  exp_w1[layer], exp_b1[layer][:, None, :], exp_w2[layer], exp_b2[layer][:, None, :])
        x = _combine(tabs, lp, gate, x1, ys).reshape(B, S, D)
    return x
```

```python
import functools
import math

import numpy as np
import jax
import jax.numpy as jnp
from jax import lax
from jax.experimental import pallas as pl
from jax.experimental.pallas import tpu as pltpu

F32 = jnp.float32
BF16 = jnp.bfloat16
HP = lax.Precision.HIGHEST

HEAD_DIM = 64
DECAY_LORA = 32
AAA_LORA = 32
GATE_LORA = 96
NUM_BUCKETS = 32
MAX_DISTANCE = 128
N_EXPERTS = 32
TOP_K = 4
SWIGLU_LIMIT = 7.0
SWIGLU_ALPHA = 1.702
NORM_EPS = 1e-5
LNX_EPS = 64e-5

LANES = 128
VMEM_LIMIT_BYTES = 56 * 1024 * 1024

NEG_BIG = -1e30
LOG2E = math.log2(math.e)
ACC_L = LANES
ACC_ROWS = LANES + 8

PROJ_ROWS = 512
RWKV_CHUNK = 64
RWKV_ROWS = 256
RWKV_SEQS = 2
ATT_TILE = 256
FAR_GROUP = 4
MOE_ROWS = 512
MOVE_ROWS = 256
RUN_CHUNK = 8
LOCAL_ROWS = MOVE_ROWS * 4 + 32 * RUN_CHUNK
LOCAL_CHUNKS = LOCAL_ROWS // RUN_CHUNK
CHUNK_LANES = 256

NN = (((1,), (0,)), ((), ()))
NT = (((1,), (1,)), ((), ()))
TN = (((0,), (0,)), ((), ()))


def _dot(a, b, dims=NN, prec=None):
    return lax.dot_general(a, b, dims, precision=prec, preferred_element_type=F32)


def _cparams(sem):
    return pltpu.CompilerParams(dimension_semantics=sem, vmem_limit_bytes=VMEM_LIMIT_BYTES)


def _sigmoid(x):
    return 1.0 / (1.0 + jnp.exp(-x))


def _lane_lo(shape):
    lane = lax.broadcasted_iota(jnp.int32, shape, len(shape) - 1)
    return (lane % LANES) < HEAD_DIM


def _head_rms_normalize(x, gain, scale):
    outs = []
    for t in range(x.shape[1] // LANES):
        xt = x[:, t * LANES:(t + 1) * LANES]
        lo = _lane_lo(xt.shape)
        x2 = xt * xt
        s0 = jnp.sum(jnp.where(lo, x2, 0.0), axis=-1, keepdims=True)
        s1 = jnp.sum(jnp.where(lo, 0.0, x2), axis=-1, keepdims=True)
        inv = lax.rsqrt(jnp.where(lo, s0, s1) * (1.0 / HEAD_DIM) + NORM_EPS)
        outs.append(xt * inv)
    return jnp.concatenate(outs, axis=-1) * gain * scale


def _inproj_kernel(x_ref, g_ref, w_ref, qg_ref, kg_ref, pr_ref, q_ref, k_ref, vt_ref, *, n_rwkv, width):
    x = x_ref[...]
    ms = jnp.mean(x * x, axis=-1, keepdims=True)
    h = (x * lax.rsqrt(ms + NORM_EPS) * g_ref[...]).astype(BF16)
    step = 384
    for c0 in range(0, n_rwkv, step):
        pr_ref[:, c0:c0 + step] = _dot(h, w_ref[:, c0:c0 + step])
    q = _dot(h, w_ref[:, n_rwkv:n_rwkv + width])
    q_ref[...] = _head_rms_normalize(q, qg_ref[...], HEAD_DIM ** -0.5 * LOG2E).astype(BF16)
    k = _dot(h, w_ref[:, n_rwkv + width:n_rwkv + 2 * width])
    k_ref[...] = _head_rms_normalize(k, kg_ref[...], 1.0).astype(BF16)
    v = _dot(h, w_ref[:, n_rwkv + 2 * width:n_rwkv + 3 * width])
    for hd in range(width // LANES):
        vt_ref[hd, 0:LANES, :] = v[:, hd * LANES:(hd + 1) * LANES].T.astype(BF16)
        vt_ref[hd, LANES:ACC_ROWS, :] = jnp.ones((ACC_ROWS - LANES, v.shape[0]), BF16)


def _inproj(x2d, g, w_all, qg, kg, n_rwkv, width, seq_len):
    T, D = x2d.shape
    n_all = w_all.shape[1]
    tm = PROJ_ROWS
    H = width // LANES
    per_seq = seq_len // tm
    row = lambda i: (i, 0)
    fix = lambda i: (0, 0)
    return pl.pallas_call(
        functools.partial(_inproj_kernel, n_rwkv=n_rwkv, width=width),
        grid=(T // tm,),
        in_specs=[
            pl.BlockSpec((tm, D), row),
            pl.BlockSpec((1, D), fix),
            pl.BlockSpec((D, n_all), fix),
            pl.BlockSpec((1, width), fix),
            pl.BlockSpec((1, width), fix),
        ],
        out_specs=[
            pl.BlockSpec((tm, n_rwkv), row),
            pl.BlockSpec((tm, width), row),
            pl.BlockSpec((tm, width), row),
            pl.BlockSpec((None, H, ACC_ROWS, tm), lambda i: (i // per_seq, 0, 0, i % per_seq)),
        ],
        out_shape=[
            jax.ShapeDtypeStruct((T, n_rwkv), F32),
            jax.ShapeDtypeStruct((T, width), BF16),
            jax.ShapeDtypeStruct((T, width), BF16),
            jax.ShapeDtypeStruct((T // seq_len, H, ACC_ROWS, seq_len), BF16),
        ],
        compiler_params=_cparams(("arbitrary",)),
        name="inproj",
    )(x2d, g, w_all, qg, kg)


def _split2(x):
    hi = x.astype(BF16)
    return hi, (x - hi.astype(F32)).astype(BF16)


def _mm(a, b, dims=NN, mode="bf16"):
    if mode == "bf16":
        return _dot(a.astype(BF16), b.astype(BF16), dims)
    ah, al = _split2(a)
    bh, bl = _split2(b)
    lhs = jnp.concatenate([ah, ah, al], axis=dims[0][0][0])
    rhs = jnp.concatenate([bh, bl, bh], axis=dims[0][1][0])
    return _dot(lhs, rhs, dims)


def _group_sum(x, ones2):
    hi, lo = _split2(x)
    return _dot(jnp.concatenate([hi, lo], axis=1), ones2)


P_INV = "x3"
P_SUBST = "bf16"
P_STATE = "bf16"
INV_BLOCK = 16
QUAD = 256


def _rwkv_kernel(pr_ref, mu_ref, w0_ref, wl_ref, a0_ref, kk_ref, ka_ref, rk_ref, lg_ref, lb_ref,
                 tri3_ref, ones2_ref, o_ref, state_sc, prev_sc, *, width):
    C = RWKV_CHUNK
    n_seq, seq_rows, _ = pr_ref.shape
    rows = n_seq * seq_rows
    n_quads = width // QUAD
    n_chunks = seq_rows // C
    heads_q = QUAD // HEAD_DIM

    @pl.when(pl.program_id(1) == 0)
    def _():
        state_sc[...] = jnp.zeros_like(state_sc)
        prev_sc[...] = jnp.zeros_like(prev_sc)

    x = jnp.concatenate([pr_ref[s] for s in range(n_seq)], axis=0)
    rid = lax.broadcasted_iota(jnp.int32, x.shape, 0)
    prev = pltpu.roll(x, 1, axis=0)
    for s in range(n_seq):
        prev = jnp.where(rid == s * seq_rows, prev_sc[s:s + 1, :], prev)
        prev_sc[s:s + 1, :] = x[(s + 1) * seq_rows - 1:(s + 1) * seq_rows, :]
    xs = x + (prev - x) * mu_ref[...]

    r = xs[:, 0:width]
    k = xs[:, width:2 * width]
    v = xs[:, 2 * width:3 * width]
    o = 3 * width
    lora_in = jnp.concatenate([jnp.tanh(xs[:, o:o + LANES]), xs[:, o + LANES:o + 2 * LANES],
                               _sigmoid(xs[:, o + 2 * LANES:o + 3 * LANES])], axis=1).astype(BF16)
    lora = _dot(lora_in, wl_ref[...])
    z = -(w0_ref[...] + lora[:, 0:width])
    softplus = jnp.maximum(z, 0.0) + jnp.log(1.0 + jnp.exp(-jnp.abs(z)))
    log_decay = -jnp.exp(-softplus - 0.5)
    a_lr = _sigmoid(a0_ref[...] + lora[:, width:2 * width])
    gate = lora[:, 2 * width:3 * width]

    ones2 = ones2_ref[...]
    kk = k * kk_ref[...]
    kk = kk * lax.rsqrt(jnp.maximum(_group_sum(kk * kk, ones2), 1e-24))
    k = k * (1.0 + (a_lr - 1.0) * ka_ref[...])
    a_vec = -kk
    b_vec = kk * a_lr

    p1 = log_decay.astype(BF16)
    rem = log_decay - p1.astype(F32)
    p2 = rem.astype(BF16)
    p3 = (rem - p2.astype(F32)).astype(BF16)
    cum = _dot(tri3_ref[...], jnp.concatenate([p1, p2, p3], axis=0))

    t_i = lax.broadcasted_iota(jnp.int32, (C, 2 * QUAD), 0)
    s_i = lax.broadcasted_iota(jnp.int32, (C, 2 * QUAD), 1) % C
    strict2 = t_i > s_i
    incl2 = t_i >= s_i
    lane_head = lax.broadcasted_iota(jnp.int32, (C, QUAD), 1) // HEAD_DIM
    row_blk = lax.broadcasted_iota(jnp.int32, (C, QUAD), 0) // INV_BLOCK
    same_blk = row_blk == ((lax.broadcasted_iota(jnp.int32, (C, QUAD), 1) % C) // INV_BLOCK)
    r2 = lax.broadcasted_iota(jnp.int32, (QUAD, QUAD), 0)
    c2 = lax.broadcasted_iota(jnp.int32, (QUAD, QUAD), 1)
    same_head = (r2 // HEAD_DIM) == (c2 // HEAD_DIM)

    def by_head(m):
        zero = jnp.zeros_like(m)
        return jnp.concatenate([jnp.where(lane_head == h, m, zero) for h in range(heads_q)], axis=0)

    def bd_rhs(m, mode):
        if mode == "bf16":
            return by_head(m.astype(BF16))
        hi, lo = _split2(m)
        hi_bd = by_head(hi)
        return jnp.concatenate([hi_bd, by_head(lo), hi_bd], axis=0)

    def mm_bd(a, rhs, mode):
        if mode == "bf16":
            return _dot(a.astype(BF16), rhs)
        ah, al = _split2(a)
        return _dot(jnp.concatenate([ah, ah, al], axis=1), rhs)

    class Work:
        pass

    n_par = n_seq * n_quads
    works = []
    for c in range(n_chunks):
        for sq in range(n_par):
            w = Work()
            s, q = divmod(sq, n_quads)
            ls = slice(q * QUAD, (q + 1) * QUAD)
            rs = slice(s * seq_rows + c * C, s * seq_rows + (c + 1) * C)
            ld = log_decay[rs, ls]
            cm = cum[rs, ls]
            cl = cm[C - 1:C, :]
            e_in = jnp.exp(cm)
            e_neg = jnp.exp(-cm)
            e_hat = jnp.exp(cl - cm)
            w.q, w.c = sq, c
            w.dec = jnp.exp(cl)
            w.vv = v[rs, ls]
            w.ar_t = jnp.concatenate([a_vec[rs, ls] * jnp.exp(cm - ld), r[rs, ls] * e_in], axis=0)
            w.bk_t = jnp.concatenate([bd_rhs(b_vec[rs, ls] * e_neg, "bf16"), bd_rhs(k[rs, ls] * e_neg, "bf16")], axis=0)
            w.bk_h = jnp.concatenate([b_vec[rs, ls] * e_hat, k[rs, ls] * e_hat], axis=0)
            works.append(w)

    for w in works:
        gram = _dot(w.ar_t.astype(BF16), w.bk_t, NT)
        low = jnp.where(strict2, gram[0:C, :], 0.0)
        l_cat = low[:, 0:QUAD]
        rbk = jnp.where(incl2, gram[C:2 * C, :], 0.0)
        w.rb_cat = rbk[:, 0:QUAD]
        w.kv_lhs = jnp.concatenate([low[:, QUAD:], rbk[:, QUAD:]], axis=0)
        w.l_off = jnp.where(same_blk, 0.0, l_cat)
        w.d_inv = jnp.where(same_blk, l_cat, 0.0)

    for w in works:
        w.l_pow = mm_bd(w.d_inv, bd_rhs(w.d_inv, P_INV), P_INV)
    n_sq = int(math.log2(INV_BLOCK))
    for it in range(1, n_sq):
        for w in works:
            if it + 1 < n_sq:
                both = mm_bd(jnp.concatenate([w.d_inv, w.l_pow], axis=0), bd_rhs(w.l_pow, P_INV), P_INV)
                w.d_inv = w.d_inv + w.l_pow + both[0:C]
                w.l_pow = both[C:2 * C]
            else:
                w.d_inv = w.d_inv + w.l_pow + mm_bd(w.d_inv, bd_rhs(w.l_pow, P_INV), P_INV)
    for w in works:
        w.m1 = w.l_off + mm_bd(w.d_inv, bd_rhs(w.l_off, P_SUBST), P_SUBST)
        w.m1_bd = bd_rhs(w.m1, P_SUBST)
    for w in works:
        w.m2 = mm_bd(w.m1, w.m1_bd, P_SUBST)
    for w in works:
        w.nn = w.m1 + w.m2 + mm_bd(w.m2, w.m1_bd, P_SUBST)
    for w in works:
        w.t_inv = w.nn + w.d_inv + mm_bd(w.nn, bd_rhs(w.d_inv, P_SUBST), P_SUBST)
    for w in works:
        w.kv = mm_bd(w.kv_lhs, bd_rhs(w.vv, P_STATE), P_STATE)

    states = [state_sc[sq] for sq in range(n_par)]
    y_parts = [[] for _ in range(n_par)]
    for c in range(n_chunks):
        ws = works[c * n_par:(c + 1) * n_par]
        for w in ws:
            w.base = _mm(w.ar_t, states[w.q], NT, mode=P_STATE) + w.kv
        for w in ws:
            u0 = w.base[0:C]
            w.u = u0 + mm_bd(w.t_inv, bd_rhs(u0, P_STATE), P_STATE)
        for w in ws:
            upd = _mm(jnp.concatenate([w.u, w.vv], axis=0), w.bk_h, TN, mode=P_STATE)
            states[w.q] = states[w.q] * w.dec + jnp.where(same_head, upd, 0.0)
        for w in ws:
            y_parts[w.q].append(w.base[C:2 * C] + mm_bd(w.rb_cat, bd_rhs(w.u, P_STATE), P_STATE))
    for sq in range(n_par):
        state_sc[sq] = states[sq]
    y = jnp.concatenate([jnp.concatenate([jnp.concatenate(y_parts[s * n_quads + q], axis=0)
                                          for q in range(n_quads)], axis=-1)
                         for s in range(n_seq)], axis=0)

    inv_n = 1.0 / HEAD_DIM
    sums = _group_sum(jnp.concatenate([y, r * k * rk_ref[...]], axis=0), ones2)
    mean = sums[0:rows] * inv_n
    yc = y - mean
    var = _group_sum(yc * yc, ones2) * inv_n
    yn = yc * lax.rsqrt(var + LNX_EPS) * lg_ref[...] + lb_ref[...]
    out = ((yn + sums[rows:2 * rows] * v) * gate).astype(o_ref.dtype)
    for s in range(n_seq):
        o_ref[s] = out[s * seq_rows:(s + 1) * seq_rows]


def _rwkv(pr, mu, w0, w_lora, a0, k_k, k_a, r_k, lnx_g, lnx_b, width):
    B, S, n_rwkv = pr.shape
    rows = RWKV_ROWS
    n_seq = RWKV_SEQS
    C = RWKV_CHUNK
    rr = np.arange(n_seq * rows)[:, None]
    cc = np.arange(n_seq * rows)[None, :]
    tri = ((rr // C == cc // C) & (cc <= rr)).astype(np.float32)
    tri3 = jnp.asarray(np.concatenate([tri, tri, tri], axis=1), BF16)
    gg = np.arange(width)
    ones = (gg[:, None] // HEAD_DIM == gg[None, :] // HEAD_DIM).astype(np.float32)
    ones2 = jnp.asarray(np.concatenate([ones, ones], axis=0), BF16)
    fix = lambda shape: pl.BlockSpec(shape, lambda b, i: (0, 0))
    vec = lambda n: fix((1, n))
    return pl.pallas_call(
        functools.partial(_rwkv_kernel, width=width),
        grid=(B // n_seq, S // rows),
        in_specs=[
            pl.BlockSpec((n_seq, rows, n_rwkv), lambda b, i: (b, i, 0)),
            vec(n_rwkv), vec(width), fix(w_lora.shape), vec(width),
            vec(width), vec(width), vec(width), vec(width), vec(width),
            fix(tri3.shape), fix(ones2.shape),
        ],
        out_specs=pl.BlockSpec((n_seq, rows, width), lambda b, i: (b, i, 0)),
        out_shape=jax.ShapeDtypeStruct((B, S, width), BF16),
        scratch_shapes=[
            pltpu.VMEM((n_seq * (width // QUAD), QUAD, QUAD), F32),
            pltpu.VMEM((n_seq, n_rwkv), F32),
        ],
        compiler_params=_cparams(("arbitrary", "arbitrary")),
        name="rwkv7",
    )(pr, mu, w0, w_lora, a0, k_k, k_a, r_k, lnx_g, lnx_b, tri3, ones2)


def _t5_bucket_np(dist):
    n = np.maximum(dist, 0)
    max_exact = NUM_BUCKETS // 2
    nf = np.maximum(n, 1).astype(np.float32)
    large = max_exact + (np.log(nf / max_exact) / math.log(MAX_DISTANCE / max_exact)
                         * (NUM_BUCKETS - max_exact)).astype(np.int32)
    large = np.minimum(large, NUM_BUCKETS - 1)
    return np.where(n < max_exact, n, large).astype(np.int32)


def _near_bucket_tiles(tile):
    kpos = np.arange(tile)[:, None]
    qpos = np.arange(tile)[None, :]
    out = []
    for delta in (0, 1):
        dist = delta * tile + qpos - kpos
        out.append(np.where(dist >= 0, _t5_bucket_np(dist), -1))
    return np.stack(out).astype(np.int32)


def _attn_kernel(relb_ref, bucket_ref, lq1_ref, lk1_ref, lq2_ref, lk2_ref, sg_ref,
                 q_ref, k_ref, vt_ref, o_ref, bias_sc, q2_sc, m_sc, acc_sc, st_sc, *, lam_init):
    tq = q_ref.shape[0]
    n_heads = vt_ref.shape[0]
    heads = range(n_heads)
    qi = pl.program_id(1)

    @pl.when((pl.program_id(0) == 0) & (qi == 0))
    def _():
        for h in heads:
            for d in range(2):
                bk = bucket_ref[d]
                tile = jnp.full(bk.shape, NEG_BIG, F32)
                for j in range(NUM_BUCKETS):
                    tile = jnp.where(bk == j, relb_ref[h, j] * LOG2E, tile)
                r0 = (1 - d) * tq
                bias_sc[h, r0:r0 + tq, 0:tq] = tile
                bias_sc[h, r0:r0 + tq, tq:2 * tq] = tile

    lo = _lane_lo((tq, LANES))
    for h in heads:
        q = q_ref[:, h * LANES:(h + 1) * LANES]
        zero = jnp.zeros_like(q)
        q2_sc[h, 0:tq, :] = jnp.where(lo, q, zero)
        q2_sc[h, tq:2 * tq, :] = jnp.where(lo, zero, q)
    m_sc[...] = jnp.full(m_sc.shape, NEG_BIG, F32)
    acc_sc[...] = jnp.zeros_like(acc_sc)

    def step(k0, tk, band):
        for h in heads:
            st_sc[h, 0:tk, :] = _dot(k_ref[pl.ds(k0, tk), h * LANES:(h + 1) * LANES], q2_sc[h], NT)
        pvs, alphas = [], []
        for h in heads:
            m_old = m_sc[h]
            if band is None:
                st = st_sc[h, 0:tk, :]
                far = relb_ref[h, NUM_BUCKETS - 1] * LOG2E
                m_new = jnp.maximum(m_old, jnp.max(st, axis=0, keepdims=True) + far)
                p = jnp.exp2(st - (m_new - far))
            else:
                st = st_sc[h, 0:tk, :] + bias_sc[h, band:band + tk, :]
                m_new = jnp.maximum(m_old, jnp.max(st, axis=0, keepdims=True))
                p = jnp.exp2(st - m_new)
            m_sc[h] = m_new
            pvs.append(_dot(vt_ref[h, :, pl.ds(k0, tk)], p.astype(BF16)))
            alphas.append(jnp.exp2(m_old - m_new))
        for h in heads:
            acc_sc[h] = alphas[h] * acc_sc[h] + pvs[h]

    n_far = jnp.maximum(qi - 1, 0)
    group = FAR_GROUP * tq

    def far_body(i, carry):
        step(pl.multiple_of(i * group, group), group, None)
        return carry

    lax.fori_loop(0, n_far // FAR_GROUP, far_body, 0)
    done = (n_far // FAR_GROUP) * FAR_GROUP
    left = n_far - done

    @pl.when(left >= 2)
    def _():
        step(pl.multiple_of(done * tq, tq), 2 * tq, None)

    @pl.when(left % 2 == 1)
    def _():
        step(pl.multiple_of((n_far - 1) * tq, tq), tq, None)

    @pl.when(qi >= 1)
    def _():
        step(pl.multiple_of((qi - 1) * tq, tq), 2 * tq, 0)

    @pl.when(qi == 0)
    def _():
        step(0, tq, tq)

    lam = (jnp.exp(jnp.sum(lq1_ref[...] * lk1_ref[...], axis=-1, keepdims=True))
           - jnp.exp(jnp.sum(lq2_ref[...] * lk2_ref[...], axis=-1, keepdims=True)) + lam_init)
    for h in heads:
        acc = acc_sc[h]
        l = acc[ACC_L:ACC_L + 1, :]
        o1 = acc[0:LANES, 0:tq] / l[:, 0:tq]
        o2 = acc[0:LANES, tq:2 * tq] / l[:, tq:2 * tq]
        ot = o1 - lam * o2
        ms = jnp.mean(ot * ot, axis=0, keepdims=True)
        y = ot * lax.rsqrt(ms + NORM_EPS) * sg_ref[...] * (1.0 - lam_init)
        o_ref[:, h * LANES:(h + 1) * LANES] = y.T.astype(o_ref.dtype)


def _diff_attn(q, k, vt, relb_t, lq1, lk1, lq2, lk2, sg_col, lam_init):
    B, S, width = q.shape
    H = width // LANES
    t = ATT_TILE
    buckets = jnp.asarray(_near_bucket_tiles(t))
    vec = pl.BlockSpec((1, HEAD_DIM), lambda b, i: (0, 0))
    return pl.pallas_call(
        functools.partial(_attn_kernel, lam_init=lam_init),
        grid=(B, S // t),
        in_specs=[
            pl.BlockSpec(memory_space=pltpu.SMEM),
            pl.BlockSpec((2, t, t), lambda b, i: (0, 0, 0)),
            vec, vec, vec, vec,
            pl.BlockSpec((LANES, 1), lambda b, i: (0, 0)),
            pl.BlockSpec((None, t, width), lambda b, i: (b, i, 0)),
            pl.BlockSpec((None, S, width), lambda b, i: (b, 0, 0)),
            pl.BlockSpec((None, H, ACC_ROWS, S), lambda b, i: (b, 0, 0, 0)),
        ],
        out_specs=pl.BlockSpec((None, t, width), lambda b, i: (b, i, 0)),
        out_shape=jax.ShapeDtypeStruct((B, S, width), BF16),
        scratch_shapes=[
            pltpu.VMEM((H, 2 * t, 2 * t), F32),
            pltpu.VMEM((H, 2 * t, LANES), BF16),
            pltpu.VMEM((H, 1, 2 * t), F32),
            pltpu.VMEM((H, ACC_ROWS, 2 * t), F32),
            pltpu.VMEM((H, FAR_GROUP * t, 2 * t), F32),
        ],
        compiler_params=_cparams(("arbitrary", "arbitrary")),
        name="diff_attn",
    )(relb_t, buckets, lq1, lk1, lq2, lk2, sg_col, q, k, vt)


def _run_rows(count):
    return jnp.ceil(count * (1.0 / RUN_CHUNK)) * RUN_CHUNK


def _outproj_kernel(x_ref, ya_ref, yb_ref, wa_ref, wb_ref, g_ref, rw_ref, rb_ref,
                    x1_ref, h2_ref, gd_ref, cnt_ref):
    x1 = x_ref[...] + _dot(ya_ref[...], wa_ref[...]) + _dot(yb_ref[...], wb_ref[...])
    x1_ref[...] = x1
    ms = jnp.mean(x1 * x1, axis=-1, keepdims=True)
    h2 = x1 * lax.rsqrt(ms + NORM_EPS) * g_ref[...]
    h2_ref[...] = h2.astype(h2_ref.dtype)
    logits = _mm(h2, rw_ref[...], mode="x3") + rb_ref[...]
    lane = lax.broadcasted_iota(jnp.int32, logits.shape, 1).astype(F32)
    work = logits
    picks = []
    for _ in range(TOP_K):
        m = jnp.max(work, axis=-1, keepdims=True)
        idx = jnp.min(jnp.where(work == m, lane, float(LANES)), axis=-1, keepdims=True)
        hit = lane == idx
        picks.append((m, hit))
        work = jnp.where(hit, NEG_BIG, work)
    m0 = picks[0][0]
    es = [jnp.exp(m - m0) for m, _ in picks]
    denom = es[0] + es[1] + es[2] + es[3]
    gd = jnp.zeros_like(logits)
    for e, (_, hit) in zip(es, picks):
        gd = jnp.where(hit, e / denom, gd)
    gd_ref[...] = gd

    @pl.when(pl.program_id(0) == 0)
    def _():
        cnt_ref[...] = jnp.zeros_like(cnt_ref)

    cnt_ref[0:1, :] = cnt_ref[0:1, :] + _run_rows(jnp.sum(jnp.where(gd > 0.0, 1.0, 0.0), axis=0, keepdims=True))


def _outproj(x2d, ya, yb, wa, wb, g2, rw, rb):
    T, D = x2d.shape
    half = ya.shape[1]
    tm = MOVE_ROWS
    row = lambda i: (i, 0)
    fix = lambda i: (0, 0)
    return pl.pallas_call(
        _outproj_kernel,
        grid=(T // tm,),
        in_specs=[
            pl.BlockSpec((tm, D), row),
            pl.BlockSpec((tm, half), row),
            pl.BlockSpec((tm, half), row),
            pl.BlockSpec((half, D), fix),
            pl.BlockSpec((half, D), fix),
            pl.BlockSpec((1, D), fix),
            pl.BlockSpec((D, LANES), fix),
            pl.BlockSpec((1, LANES), fix),
        ],
        out_specs=[
            pl.BlockSpec((tm, D), row),
            pl.BlockSpec((tm, D), row),
            pl.BlockSpec((tm, LANES), row),
            pl.BlockSpec((8, LANES), fix),
        ],
        out_shape=[
            jax.ShapeDtypeStruct((T, D), F32),
            jax.ShapeDtypeStruct((T, D), BF16),
            jax.ShapeDtypeStruct((T, LANES), F32),
            jax.ShapeDtypeStruct((8, LANES), F32),
        ],
        compiler_params=_cparams(("arbitrary",)),
        name="outproj_router",
    )(x2d, ya, yb, wa, wb, g2, rw, rb)


def _padded_rows(cnt, rb):
    return jnp.ceil(cnt * (1.0 / rb)) * rb


def _route_kernel(gd_ref, tot_ref, lp_ref, gate_ref, tab_ref, cnt_ref, carry_sc, pstart_sc, *, rb):
    i = pl.program_id(0)
    tb = gd_ref.shape[0]
    gd = gd_ref[...]
    sel = gd > 0.0
    self = jnp.where(sel, 1.0, 0.0)
    run_pad = _run_rows(jnp.sum(self, axis=0, keepdims=True))
    r2 = lax.broadcasted_iota(jnp.int32, (LANES, LANES), 0)
    c2 = lax.broadcasted_iota(jnp.int32, (LANES, LANES), 1)

    @pl.when(i == 0)
    def _():
        cnt = tot_ref[0:1, :]
        upper = jnp.where(r2 < c2, 1.0, 0.0)
        pstart = _dot(jnp.broadcast_to(_padded_rows(cnt, rb), (8, LANES)), upper, prec=HP)[0:1, :]
        pstart_sc[...] = pstart
        cnt_ref[0:1, :] = cnt
        cnt_ref[1:2, :] = pstart
        cnt_ref[2:8, :] = jnp.zeros((6, LANES), F32)
        carry_sc[...] = jnp.zeros_like(carry_sc)

    rr = lax.broadcasted_iota(jnp.int32, (tb, tb), 0)
    cc = lax.broadcasted_iota(jnp.int32, (tb, tb), 1)
    lower = jnp.where(cc < rr, 1.0, 0.0).astype(BF16)
    sel_b = self.astype(BF16)
    prefix = _dot(lower, sel_b)
    upper_b = jnp.where(r2 < c2, 1.0, 0.0).astype(BF16)
    off = _dot(jnp.broadcast_to(run_pad, (8, LANES)).astype(BF16), upper_b)[0:1, :]
    lp = off + prefix
    upper_incl = jnp.where(r2 <= c2, 1.0, 0.0).astype(BF16)
    slot = _dot(sel_b, upper_incl)
    lane = lax.broadcasted_iota(jnp.int32, (tb, LANES), 1)
    p_out = jnp.zeros((tb, LANES), F32)
    g_out = jnp.zeros((tb, LANES), F32)
    for kk in range(TOP_K):
        mk = sel & (slot == float(kk + 1))
        p_k = jnp.sum(jnp.where(mk, lp, 0.0), axis=-1, keepdims=True)
        g_k = jnp.sum(jnp.where(mk, gd, 0.0), axis=-1, keepdims=True)
        p_out = jnp.where(lane == kk, p_k, p_out)
        g_out = jnp.where(lane == kk, g_k, g_out)
    lp_ref[...] = p_out
    gate_ref[...] = g_out
    def as_column(v):
        return jnp.sum(jnp.where(r2 == c2, jnp.broadcast_to(v, (LANES, LANES)), 0.0), axis=1, keepdims=True)

    first = as_column(off * (1.0 / RUN_CHUNK))
    count = as_column(run_pad * (1.0 / RUN_CHUNK))
    start = as_column(pstart_sc[...] + carry_sc[...])
    chunk = lax.broadcasted_iota(jnp.int32, (LANES, CHUNK_LANES), 1).astype(F32)
    mine = (chunk >= first) & (chunk < first + count)
    dst = jnp.sum(jnp.where(mine, start + (chunk - first) * RUN_CHUNK, 0.0), axis=0, keepdims=True)
    n_chunks = jnp.sum(count, axis=0, keepdims=True)
    tab_ref[0:1, :] = dst.astype(jnp.int32)
    tab_ref[1:2, :] = jnp.broadcast_to(n_chunks, (1, CHUNK_LANES)).astype(jnp.int32)
    tab_ref[2:8, :] = jnp.zeros((6, CHUNK_LANES), jnp.int32)
    carry_sc[...] = carry_sc[...] + run_pad


def _route(gd, totals, rb):
    T = gd.shape[0]
    tb = MOVE_ROWS
    blk = lambda i: (i, 0)
    fix = lambda i: (0, 0)
    return pl.pallas_call(
        functools.partial(_route_kernel, rb=rb),
        grid=(T // tb,),
        in_specs=[pl.BlockSpec((tb, LANES), blk), pl.BlockSpec((8, LANES), fix)],
        out_specs=[
            pl.BlockSpec((tb, LANES), blk),
            pl.BlockSpec((tb, LANES), blk),
            pl.BlockSpec((8, CHUNK_LANES), blk),
            pl.BlockSpec((8, LANES), fix),
        ],
        out_shape=[
            jax.ShapeDtypeStruct((T, LANES), F32),
            jax.ShapeDtypeStruct((T, LANES), F32),
            jax.ShapeDtypeStruct((T // tb * 8, CHUNK_LANES), jnp.int32),
            jax.ShapeDtypeStruct((8, LANES), F32),
        ],
        scratch_shapes=[pltpu.VMEM((1, LANES), F32), pltpu.VMEM((1, LANES), F32)],
        compiler_params=_cparams(("arbitrary",)),
        name="route_positions",
    )(gd, totals)


def _chunk_copy(src_ref, src_row, dst_ref, dst_row, sem):
    return pltpu.make_async_copy(src_ref.at[pl.ds(src_row, RUN_CHUNK), :],
                                 dst_ref.at[pl.ds(dst_row, RUN_CHUNK), :], sem)


def _start_runs(b, nc_ref, dst_ref, copy_chunk):
    def issue(c, carry):
        copy_chunk(pl.multiple_of(c * RUN_CHUNK, RUN_CHUNK),
                   pl.multiple_of(dst_ref[b * LOCAL_CHUNKS + c], RUN_CHUNK)).start()
        return carry

    lax.fori_loop(0, nc_ref[b], issue, 0)


def _wait_runs(b, nc_ref, wait_rows):
    total = nc_ref[b]
    p = 1
    while p <= LOCAL_CHUNKS:
        @pl.when((total & p) != 0)
        def _(p=p):
            wait_rows(p * RUN_CHUNK).wait()
        p *= 2


def _rows_copy(src_ref, dst_ref, n_rows, sem):
    return pltpu.make_async_copy(src_ref.at[pl.ds(0, n_rows), :], dst_ref.at[pl.ds(0, n_rows), :], sem)


def _local_onehot(lp, lb):
    j = lax.broadcasted_iota(jnp.int32, (lp.shape[0], lb), 1).astype(F32)
    hit = lp[:, 0:1] == j
    for kk in range(1, TOP_K):
        hit = hit | (lp[:, kk:kk + 1] == j)
    return hit


def _scatter_kernel(nc_ref, dst_ref, t0_ref, tc_ref, lp_ref, h_ref, xs_ref, buf, zero_buf, sems):
    b = pl.program_id(0)
    slot = b % 2
    pt = jnp.where(_local_onehot(lp_ref[...], buf.shape[1]), 1.0, 0.0).astype(BF16)
    buf[slot] = _dot(pt, h_ref[...], TN)
    _start_runs(b, nc_ref, dst_ref,
                lambda s, d: _chunk_copy(buf.at[slot], s, xs_ref, d, sems.at[slot]))

    @pl.when(b >= 1)
    def _():
        _wait_runs(b - 1, nc_ref, lambda n: _rows_copy(buf.at[1 - slot], xs_ref, n, sems.at[1 - slot]))

    @pl.when(b == pl.num_programs(0) - 1)
    def _():
        _wait_runs(b, nc_ref, lambda n: _rows_copy(buf.at[slot], xs_ref, n, sems.at[slot]))
        zero_buf[...] = jnp.zeros_like(zero_buf)
        total = 0
        for e in range(N_EXPERTS + 1):
            t0 = t0_ref[e]

            def issue(c, carry, t0=t0):
                _chunk_copy(zero_buf, 0, xs_ref, pl.multiple_of(t0 + c * RUN_CHUNK, RUN_CHUNK), sems.at[0]).start()
                return carry

            lax.fori_loop(0, tc_ref[e], issue, 0)
            total = total + tc_ref[e]

        def drain(c, carry):
            _chunk_copy(zero_buf, 0, xs_ref, 0, sems.at[0]).wait()
            return carry

        lax.fori_loop(0, total, drain, 0)


def _scatter_rows(tabs, tails, lp, h2, n_rows):
    T, D = h2.shape
    tb = MOVE_ROWS
    return pl.pallas_call(
        _scatter_kernel,
        grid_spec=pltpu.PrefetchScalarGridSpec(
            num_scalar_prefetch=4,
            grid=(T // tb,),
            in_specs=[
                pl.BlockSpec((tb, LANES), lambda i, *_: (i, 0)),
                pl.BlockSpec((tb, D), lambda i, *_: (i, 0)),
            ],
            out_specs=pl.BlockSpec(memory_space=pl.ANY),
            scratch_shapes=[pltpu.VMEM((2, LOCAL_ROWS, D), F32), pltpu.VMEM((RUN_CHUNK, D), F32),
                            pltpu.SemaphoreType.DMA((2,))],
        ),
        out_shape=jax.ShapeDtypeStruct((n_rows, D), F32),
        compiler_params=_cparams(("arbitrary",)),
        name="scatter_rows",
    )(*tabs, *tails, lp, h2)


def _moe_kernel(be_ref, na_ref, xs_ref, w1_ref, b1_ref, w2_ref, b2_ref, ys_ref, w1b, w2b):
    j = pl.program_id(0)

    @pl.when((j == 0) | (be_ref[j] != be_ref[jnp.maximum(j - 1, 0)]))
    def _():
        w1b[...] = w1_ref[...].astype(BF16)
        w2b[...] = w2_ref[...].astype(BF16)

    @pl.when(j < na_ref[0])
    def _():
        xb = xs_ref[...].astype(BF16)
        hid = _dot(xb, w1b[...]) + b1_ref[...]
        half = hid.shape[1] // 2
        x_glu = jnp.minimum(hid[:, :half], SWIGLU_LIMIT)
        x_lin = jnp.clip(hid[:, half:], -SWIGLU_LIMIT, SWIGLU_LIMIT)
        act = x_glu * _sigmoid(SWIGLU_ALPHA * x_glu) * (x_lin + 1.0)
        ys_ref[...] = _dot(act.astype(BF16), w2b[...]) + b2_ref[...]

    @pl.when(j >= na_ref[0])
    def _():
        ys_ref[...] = jnp.zeros_like(ys_ref)


def _moe(blk_expert, n_active, xs, w1, b1, w2, b2):
    R, D = xs.shape
    E, _, D2 = w1.shape
    rb = MOE_ROWS
    nb = R // rb
    blk = lambda j, be, na: (jnp.minimum(j, na[0] - 1), 0)
    return pl.pallas_call(
        _moe_kernel,
        grid_spec=pltpu.PrefetchScalarGridSpec(
            num_scalar_prefetch=2,
            grid=(nb,),
            in_specs=[
                pl.BlockSpec((rb, D), blk),
                pl.BlockSpec((None, D, D2), lambda j, be, na: (be[j], 0, 0)),
                pl.BlockSpec((None, 1, D2), lambda j, be, na: (be[j], 0, 0)),
                pl.BlockSpec((None, D2 // 2, D), lambda j, be, na: (be[j], 0, 0)),
                pl.BlockSpec((None, 1, D), lambda j, be, na: (be[j], 0, 0)),
            ],
            out_specs=pl.BlockSpec((rb, D), lambda j, be, na: (j, 0)),
            scratch_shapes=[pltpu.VMEM((D, D2), BF16), pltpu.VMEM((D2 // 2, D), BF16)],
        ),
        out_shape=jax.ShapeDtypeStruct((R, D), F32),
        compiler_params=_cparams(("arbitrary",)),
        name="moe_experts",
    )(blk_expert, n_active, xs, w1, b1, w2, b2)


def _combine_kernel(nc_ref, dst_ref, lp_ref, gate_ref, x1_ref, ys_ref, o_ref, buf, sems):
    b = pl.program_id(0)
    slot = b % 2

    def fetch(blk, sl):
        _start_runs(blk, nc_ref, dst_ref,
                    lambda s, d: _chunk_copy(ys_ref, d, buf.at[sl], s, sems.at[sl]))

    @pl.when(b == 0)
    def _():
        buf[...] = jnp.zeros_like(buf)
        fetch(b, slot)

    @pl.when(b + 1 < pl.num_programs(0))
    def _():
        fetch(b + 1, 1 - slot)

    _wait_runs(b, nc_ref, lambda n: _rows_copy(ys_ref, buf.at[slot], n, sems.at[slot]))
    lp = lp_ref[...]
    g = gate_ref[...]
    _, lb, d = buf.shape
    j = lax.broadcasted_iota(jnp.int32, (lp.shape[0], lb), 1).astype(F32)
    gt = jnp.zeros(j.shape, F32)
    for kk in range(TOP_K):
        gt = jnp.where(lp[:, kk:kk + 1] == j, g[:, kk:kk + 1], gt)
    g_hi, g_lo = _split2(gt)
    ones = jnp.ones((lp.shape[0], LANES), BF16)
    g_row = _dot(g_hi, ones, TN) + _dot(g_lo, ones, TN)
    rows = (buf[slot] * jnp.concatenate([g_row] * (d // LANES), axis=1)).astype(BF16)
    p01 = jnp.where(gt != 0.0, 1.0, 0.0).astype(BF16)
    o_ref[...] = x1_ref[...] + _dot(p01, rows)


def _combine(tabs, lp, gate, x1, ys):
    T, D = x1.shape
    tb = MOVE_ROWS
    return pl.pallas_call(
        _combine_kernel,
        grid_spec=pltpu.PrefetchScalarGridSpec(
            num_scalar_prefetch=2,
            grid=(T // tb,),
            in_specs=[
                pl.BlockSpec((tb, LANES), lambda i, *_: (i, 0)),
                pl.BlockSpec((tb, LANES), lambda i, *_: (i, 0)),
                pl.BlockSpec((tb, D), lambda i, *_: (i, 0)),
                pl.BlockSpec(memory_space=pl.ANY),
            ],
            out_specs=pl.BlockSpec((tb, D), lambda i, *_: (i, 0)),
            scratch_shapes=[pltpu.VMEM((2, LOCAL_ROWS, D), F32), pltpu.SemaphoreType.DMA((2,))],
        ),
        out_shape=jax.ShapeDtypeStruct((T, D), F32),
        compiler_params=_cparams(("arbitrary",)),
        name="combine_rows",
    )(*tabs, lp, gate, x1, ys)


def _pad_cols(a, n):
    return jnp.pad(a, ((0, 0), (0, n - a.shape[1])))


def _pad_rows(a, n):
    return jnp.pad(a, ((0, n - a.shape[0]), (0, 0)))


def _lora_weight(w2, a2, g2, width):
    z = jnp.zeros((LANES, width), F32)
    return jnp.concatenate([
        jnp.concatenate([_pad_rows(w2, LANES), z, z], axis=1),
        jnp.concatenate([z, _pad_rows(a2, LANES), z], axis=1),
        jnp.concatenate([z, z, _pad_rows(g2, LANES)], axis=1),
    ], axis=0).astype(BF16)


def _split_rwkv_cols(a, width):
    o = 3 * width
    return jnp.concatenate([
        a[:, :o],
        _pad_cols(a[:, o:o + DECAY_LORA], LANES),
        _pad_cols(a[:, o + DECAY_LORA:o + DECAY_LORA + AAA_LORA], LANES),
        _pad_cols(a[:, o + DECAY_LORA + AAA_LORA:o + DECAY_LORA + AAA_LORA + GATE_LORA], LANES),
    ], axis=1)


def kernel(x, norm1_g, w_in, mu_shift, w0, w2, a0, a2, g2, k_k, k_a, r_k, lnx_g, lnx_b, qn_g, kn_g,
           lam_q1, lam_k1, lam_q2, lam_k2, subln_g, rel_bias, w_out, norm2_g, router_w, router_b,
           exp_w1, exp_b1, exp_w2, exp_b2):
    B, S, D = x.shape
    T = B * S
    depth = norm1_g.shape[0]
    width = w0.shape[1]
    n_rwkv_cols = 3 * width + DECAY_LORA + AAA_LORA + GATE_LORA
    n_rwkv_pad = 3 * width + 3 * LANES
    n_groups = width // HEAD_DIM
    H = width // LANES
    relb_t = rel_bias.T
    row = lambda a: a.reshape(1, -1)

    for layer in range(depth):
        lam_init = 0.8 - 0.6 * math.exp(-0.3 * layer)
        w_l = w_in[layer]
        w_all = jnp.concatenate([_split_rwkv_cols(w_l[:, :n_rwkv_cols], width), w_l[:, n_rwkv_cols:]],
                                axis=1).astype(BF16)
        mu = _split_rwkv_cols(row(mu_shift[layer]), width)
        qg = row(jnp.tile(qn_g[layer], n_groups))
        kg = row(jnp.tile(kn_g[layer], n_groups))

        pr, q, k, vt = _inproj(x.reshape(T, D), row(norm1_g[layer]), w_all, qg, kg, n_rwkv_pad, width, S)

        y_a = _rwkv(pr.reshape(B, S, n_rwkv_pad), mu, row(w0[layer]),
                    _lora_weight(w2[layer], a2[layer], g2[layer], width), row(a0[layer]),
                    row(k_k[layer]), row(k_a[layer]), row(r_k[layer]), row(lnx_g[layer]),
                    row(lnx_b[layer]), width)

        y_b = _diff_attn(q.reshape(B, S, width), k.reshape(B, S, width), vt, relb_t,
                         row(lam_q1[layer]), row(lam_k1[layer]), row(lam_q2[layer]), row(lam_k2[layer]),
                         subln_g[layer].reshape(-1, 1), lam_init)

        wo = w_out[layer].astype(BF16)
        rw = _pad_cols(router_w[layer], LANES)
        rb = jnp.concatenate([router_b[layer], jnp.full((LANES - N_EXPERTS,), NEG_BIG, F32)]).reshape(1, -1)
        x1, h2, gd, totals = _outproj(x.reshape(T, D), y_a.reshape(T, width), y_b.reshape(T, width),
                                      wo[:width], wo[width:], row(norm2_g[layer]), rw, rb)

        lp, gate, tab, cnt = _route(gd, totals, MOE_ROWS)
        tab = tab.reshape(-1, 8, CHUNK_LANES)
        tabs = (tab[:, 1, 0], tab[:, 0, :LOCAL_CHUNKS].reshape(-1))
        run_end = cnt[1, :N_EXPERTS] + cnt[0, :N_EXPERTS]
        pad_end = cnt[1, :N_EXPERTS] + _padded_rows(cnt[0, :N_EXPERTS], MOE_ROWS)
        run_pad_max = (T // MOVE_ROWS) * N_EXPERTS * (RUN_CHUNK - 1)
        n_blocks = -(-(T * TOP_K + run_pad_max + N_EXPERTS * (MOE_ROWS - 1)) // MOE_ROWS)
        gap_start = jnp.concatenate([run_end, pad_end[-1:]])
        gap_end = jnp.concatenate([pad_end, jnp.full((1,), n_blocks * MOE_ROWS, F32)])
        tails = (gap_start.astype(jnp.int32), ((gap_end - gap_start) / RUN_CHUNK).astype(jnp.int32))
        blk_start = (jnp.arange(n_blocks) * MOE_ROWS).astype(F32)
        blk_expert = jnp.minimum(jnp.sum(pad_end[None, :] <= blk_start[:, None], axis=1), N_EXPERTS - 1)
        n_active = (pad_end[N_EXPERTS - 1] / MOE_ROWS).astype(jnp.int32).reshape(1)

        xs = _scatter_rows(tabs, tails, lp, h2, n_blocks * MOE_ROWS)
        ys = _moe(blk_expert.astype(jnp.int32), n_active, xs,
                  exp_w1[layer], exp_b1[layer][:, None, :], exp_w2[layer], exp_b2[layer][:, None, :])
        x = _combine(tabs, lp, gate, x1, ys).reshape(B, S, D)
    return x
```

```python
import functools
import math

import numpy as np
import jax
import jax.numpy as jnp
from jax import lax
from jax.experimental import pallas as pl
from jax.experimental.pallas import tpu as pltpu

F32 = jnp.float32
BF16 = jnp.bfloat16
HP = lax.Precision.HIGHEST

HEAD_DIM = 64
DECAY_LORA = 32
AAA_LORA = 32
GATE_LORA = 96
NUM_BUCKETS = 32
MAX_DISTANCE = 128
N_EXPERTS = 32
TOP_K = 4
SWIGLU_LIMIT = 7.0
SWIGLU_ALPHA = 1.702
NORM_EPS = 1e-5
LNX_EPS = 64e-5

LANES = 128
VMEM_LIMIT_BYTES = 56 * 1024 * 1024

NEG_BIG = -1e30
LOG2E = math.log2(math.e)
ACC_L = LANES
ACC_ROWS = LANES + 8

PROJ_ROWS = 512
RWKV_CHUNK = 64
RWKV_ROWS = 256
RWKV_SEQS = 2
ATT_TILE = 256
FAR_GROUP = 4
MOE_ROWS = 512
MOVE_ROWS = 256
RUN_CHUNK = 8
LOCAL_ROWS = MOVE_ROWS * 4 + 32 * RUN_CHUNK
LOCAL_CHUNKS = LOCAL_ROWS // RUN_CHUNK
CHUNK_LANES = 256

NN = (((1,), (0,)), ((), ()))
NT = (((1,), (1,)), ((), ()))
TN = (((0,), (0,)), ((), ()))


def _dot(a, b, dims=NN, prec=None):
    return lax.dot_general(a, b, dims, precision=prec, preferred_element_type=F32)


def _cparams(sem):
    return pltpu.CompilerParams(dimension_semantics=sem, vmem_limit_bytes=VMEM_LIMIT_BYTES)


def _sigmoid(x):
    return 1.0 / (1.0 + jnp.exp(-x))


def _lane_lo(shape):
    lane = lax.broadcasted_iota(jnp.int32, shape, len(shape) - 1)
    return (lane % LANES) < HEAD_DIM


def _head_rms_normalize(x, gain, scale):
    outs = []
    for t in range(x.shape[1] // LANES):
        xt = x[:, t * LANES:(t + 1) * LANES]
        lo = _lane_lo(xt.shape)
        x2 = xt * xt
        s0 = jnp.sum(jnp.where(lo, x2, 0.0), axis=-1, keepdims=True)
        s1 = jnp.sum(jnp.where(lo, 0.0, x2), axis=-1, keepdims=True)
        inv = lax.rsqrt(jnp.where(lo, s0, s1) * (1.0 / HEAD_DIM) + NORM_EPS)
        outs.append(xt * inv)
    return jnp.concatenate(outs, axis=-1) * gain * scale


def _inproj_kernel(x_ref, g_ref, w_ref, qg_ref, kg_ref, pr_ref, q_ref, k_ref, vt_ref, *, n_rwkv, width):
    x = x_ref[...]
    ms = jnp.mean(x * x, axis=-1, keepdims=True)
    h = (x * lax.rsqrt(ms + NORM_EPS) * g_ref[...]).astype(BF16)
    step = 384
    for c0 in range(0, n_rwkv, step):
        pr_ref[:, c0:c0 + step] = _dot(h, w_ref[:, c0:c0 + step])
    q = _dot(h, w_ref[:, n_rwkv:n_rwkv + width])
    q_ref[...] = _head_rms_normalize(q, qg_ref[...], HEAD_DIM ** -0.5 * LOG2E).astype(BF16)
    k = _dot(h, w_ref[:, n_rwkv + width:n_rwkv + 2 * width])
    k_ref[...] = _head_rms_normalize(k, kg_ref[...], 1.0).astype(BF16)
    v = _dot(h, w_ref[:, n_rwkv + 2 * width:n_rwkv + 3 * width])
    for hd in range(width // LANES):
        vt_ref[hd, 0:LANES, :] = v[:, hd * LANES:(hd + 1) * LANES].T.astype(BF16)
        vt_ref[hd, LANES:ACC_ROWS, :] = jnp.ones((ACC_ROWS - LANES, v.shape[0]), BF16)


def _inproj(x2d, g, w_all, qg, kg, n_rwkv, width, seq_len):
    T, D = x2d.shape
    n_all = w_all.shape[1]
    tm = PROJ_ROWS
    H = width // LANES
    per_seq = seq_len // tm
    row = lambda i: (i, 0)
    fix = lambda i: (0, 0)
    return pl.pallas_call(
        functools.partial(_inproj_kernel, n_rwkv=n_rwkv, width=width),
        grid=(T // tm,),
        in_specs=[
            pl.BlockSpec((tm, D), row),
            pl.BlockSpec((1, D), fix),
            pl.BlockSpec((D, n_all), fix),
            pl.BlockSpec((1, width), fix),
            pl.BlockSpec((1, width), fix),
        ],
        out_specs=[
            pl.BlockSpec((tm, n_rwkv), row),
            pl.BlockSpec((tm, width), row),
            pl.BlockSpec((tm, width), row),
            pl.BlockSpec((None, H, ACC_ROWS, tm), lambda i: (i // per_seq, 0, 0, i % per_seq)),
        ],
        out_shape=[
            jax.ShapeDtypeStruct((T, n_rwkv), F32),
            jax.ShapeDtypeStruct((T, width), BF16),
            jax.ShapeDtypeStruct((T, width), BF16),
            jax.ShapeDtypeStruct((T // seq_len, H, ACC_ROWS, seq_len), BF16),
        ],
        compiler_params=_cparams(("arbitrary",)),
        name="inproj",
    )(x2d, g, w_all, qg, kg)


def _split2(x):
    hi = x.astype(BF16)
    return hi, (x - hi.astype(F32)).astype(BF16)


def _mm(a, b, dims=NN, mode="bf16"):
    if mode == "bf16":
        return _dot(a.astype(BF16), b.astype(BF16), dims)
    ah, al = _split2(a)
    bh, bl = _split2(b)
    lhs = jnp.concatenate([ah, ah, al], axis=dims[0][0][0])
    rhs = jnp.concatenate([bh, bl, bh], axis=dims[0][1][0])
    return _dot(lhs, rhs, dims)


def _group_sum(x, ones2):
    hi, lo = _split2(x)
    return _dot(jnp.concatenate([hi, lo], axis=1), ones2)


P_INV = "x3"
P_SUBST = "bf16"
P_STATE = "bf16"
INV_BLOCK = 16
QUAD = 256


def _rwkv_kernel(pr_ref, mu_ref, w0_ref, wl_ref, a0_ref, kk_ref, ka_ref, rk_ref, lg_ref, lb_ref,
                 tri3_ref, ones2_ref, o_ref, state_sc, prev_sc, *, width):
    C = RWKV_CHUNK
    n_seq, seq_rows, _ = pr_ref.shape
    rows = n_seq * seq_rows
    n_quads = width // QUAD
    n_chunks = seq_rows // C
    heads_q = QUAD // HEAD_DIM

    @pl.when(pl.program_id(1) == 0)
    def _():
        state_sc[...] = jnp.zeros_like(state_sc)
        prev_sc[...] = jnp.zeros_like(prev_sc)

    x = jnp.concatenate([pr_ref[s] for s in range(n_seq)], axis=0)
    rid = lax.broadcasted_iota(jnp.int32, x.shape, 0)
    prev = pltpu.roll(x, 1, axis=0)
    for s in range(n_seq):
        prev = jnp.where(rid == s * seq_rows, prev_sc[s:s + 1, :], prev)
        prev_sc[s:s + 1, :] = x[(s + 1) * seq_rows - 1:(s + 1) * seq_rows, :]
    xs = x + (prev - x) * mu_ref[...]

    r = xs[:, 0:width]
    k = xs[:, width:2 * width]
    v = xs[:, 2 * width:3 * width]
    o = 3 * width
    lora_in = jnp.concatenate([jnp.tanh(xs[:, o:o + LANES]), xs[:, o + LANES:o + 2 * LANES],
                               _sigmoid(xs[:, o + 2 * LANES:o + 3 * LANES])], axis=1).astype(BF16)
    lora = _dot(lora_in, wl_ref[...])
    z = -(w0_ref[...] + lora[:, 0:width])
    softplus = jnp.maximum(z, 0.0) + jnp.log(1.0 + jnp.exp(-jnp.abs(z)))
    log_decay = -jnp.exp(-softplus - 0.5)
    a_lr = _sigmoid(a0_ref[...] + lora[:, width:2 * width])
    gate = lora[:, 2 * width:3 * width]

    ones2 = ones2_ref[...]
    kk = k * kk_ref[...]
    kk = kk * lax.rsqrt(jnp.maximum(_group_sum(kk * kk, ones2), 1e-24))
    k = k * (1.0 + (a_lr - 1.0) * ka_ref[...])
    a_vec = -kk
    b_vec = kk * a_lr

    p1 = log_decay.astype(BF16)
    rem = log_decay - p1.astype(F32)
    p2 = rem.astype(BF16)
    p3 = (rem - p2.astype(F32)).astype(BF16)
    cum = _dot(tri3_ref[...], jnp.concatenate([p1, p2, p3], axis=0))

    t_i = lax.broadcasted_iota(jnp.int32, (C, 2 * QUAD), 0)
    s_i = lax.broadcasted_iota(jnp.int32, (C, 2 * QUAD), 1) % C
    strict2 = t_i > s_i
    incl2 = t_i >= s_i
    lane_head = lax.broadcasted_iota(jnp.int32, (C, QUAD), 1) // HEAD_DIM
    row_blk = lax.broadcasted_iota(jnp.int32, (C, QUAD), 0) // INV_BLOCK
    same_blk = row_blk == ((lax.broadcasted_iota(jnp.int32, (C, QUAD), 1) % C) // INV_BLOCK)
    r2 = lax.broadcasted_iota(jnp.int32, (QUAD, QUAD), 0)
    c2 = lax.broadcasted_iota(jnp.int32, (QUAD, QUAD), 1)
    same_head = (r2 // HEAD_DIM) == (c2 // HEAD_DIM)

    def by_head(m):
        zero = jnp.zeros_like(m)
        return jnp.concatenate([jnp.where(lane_head == h, m, zero) for h in range(heads_q)], axis=0)

    def bd_rhs(m, mode):
        if mode == "bf16":
            return by_head(m.astype(BF16))
        hi, lo = _split2(m)
        hi_bd = by_head(hi)
        return jnp.concatenate([hi_bd, by_head(lo), hi_bd], axis=0)

    def mm_bd(a, rhs, mode):
        if mode == "bf16":
            return _dot(a.astype(BF16), rhs)
        ah, al = _split2(a)
        return _dot(jnp.concatenate([ah, ah, al], axis=1), rhs)

    class Work:
        pass

    def make_works(s):
        ws = []
        for c in range(n_chunks):
            for q in range(n_quads):
                w = Work()
                ls = slice(q * QUAD, (q + 1) * QUAD)
                rs = slice(s * seq_rows + c * C, s * seq_rows + (c + 1) * C)
                ld = log_decay[rs, ls]
                cm = cum[rs, ls]
                cl = cm[C - 1:C, :]
                e_in = jnp.exp(cm)
                e_neg = jnp.exp(-cm)
                e_hat = jnp.exp(cl - cm)
                w.q = q
                w.dec = jnp.exp(cl)
                w.vv = v[rs, ls]
                w.ar_t = jnp.concatenate([a_vec[rs, ls] * jnp.exp(cm - ld), r[rs, ls] * e_in], axis=0)
                w.bk_t = jnp.concatenate([bd_rhs(b_vec[rs, ls] * e_neg, "bf16"),
                                          bd_rhs(k[rs, ls] * e_neg, "bf16")], axis=0)
                w.bk_h = jnp.concatenate([b_vec[rs, ls] * e_hat, k[rs, ls] * e_hat], axis=0)
                ws.append(w)
        return ws

    def pre_stages(ws):
        def gram():
            for w in ws:
                g = _dot(w.ar_t.astype(BF16), w.bk_t, NT)
                low = jnp.where(strict2, g[0:C, :], 0.0)
                l_cat = low[:, 0:QUAD]
                rbk = jnp.where(incl2, g[C:2 * C, :], 0.0)
                w.rb_cat = rbk[:, 0:QUAD]
                w.kv_lhs = jnp.concatenate([low[:, QUAD:], rbk[:, QUAD:]], axis=0)
                w.l_off = jnp.where(same_blk, 0.0, l_cat)
                w.d_inv = jnp.where(same_blk, l_cat, 0.0)

        def square0():
            for w in ws:
                w.l_pow = mm_bd(w.d_inv, bd_rhs(w.d_inv, P_INV), P_INV)

        def square(last):
            def run():
                for w in ws:
                    if last:
                        w.d_inv = w.d_inv + w.l_pow + mm_bd(w.d_inv, bd_rhs(w.l_pow, P_INV), P_INV)
                    else:
                        both = mm_bd(jnp.concatenate([w.d_inv, w.l_pow], axis=0), bd_rhs(w.l_pow, P_INV), P_INV)
                        w.d_inv = w.d_inv + w.l_pow + both[0:C]
                        w.l_pow = both[C:2 * C]
            return run

        def neumann1():
            for w in ws:
                w.m1 = w.l_off + mm_bd(w.d_inv, bd_rhs(w.l_off, P_SUBST), P_SUBST)
                w.m1_bd = bd_rhs(w.m1, P_SUBST)

        def neumann2():
            for w in ws:
                w.m2 = mm_bd(w.m1, w.m1_bd, P_SUBST)

        def neumann3():
            for w in ws:
                w.nn = w.m1 + w.m2 + mm_bd(w.m2, w.m1_bd, P_SUBST)

        def finish():
            for w in ws:
                w.t_inv = w.nn + w.d_inv + mm_bd(w.nn, bd_rhs(w.d_inv, P_SUBST), P_SUBST)

        def values():
            for w in ws:
                w.kv = mm_bd(w.kv_lhs, bd_rhs(w.vv, P_STATE), P_STATE)

        n_sq = int(math.log2(INV_BLOCK))
        return ([gram, square0] + [square(it + 1 == n_sq) for it in range(1, n_sq)]
                + [neumann1, neumann2, neumann3, finish, values])

    def chain_levels(s, ws, y_parts):
        levels = []
        for c in range(n_chunks):
            wc = ws[c * n_quads:(c + 1) * n_quads]

            def base(wc=wc):
                for w in wc:
                    st = state_sc[s * n_quads + w.q]
                    w.base = _mm(w.ar_t, st, NT, mode=P_STATE) + w.kv

            def solve(wc=wc):
                for w in wc:
                    u0 = w.base[0:C]
                    w.u = u0 + mm_bd(w.t_inv, bd_rhs(u0, P_STATE), P_STATE)

            def update(wc=wc):
                for w in wc:
                    upd = _mm(jnp.concatenate([w.u, w.vv], axis=0), w.bk_h, TN, mode=P_STATE)
                    sq = s * n_quads + w.q
                    state_sc[sq] = state_sc[sq] * w.dec + jnp.where(same_head, upd, 0.0)
                for w in wc:
                    y_parts[w.q].append(w.base[C:2 * C] + mm_bd(w.rb_cat, bd_rhs(w.u, P_STATE), P_STATE))

            levels += [base, solve, update]
        return levels

    def post_stages(s, y_parts):
        rs = slice(s * seq_rows, (s + 1) * seq_rows)
        box = Work()

        def sums():
            box.y = jnp.concatenate([jnp.concatenate(y_parts[q], axis=0) for q in range(n_quads)], axis=-1)
            box.sums = _group_sum(jnp.concatenate([box.y, r[rs] * k[rs] * rk_ref[...]], axis=0), ones2)

        def variance():
            box.yc = box.y - box.sums[0:seq_rows] * (1.0 / HEAD_DIM)
            box.var = _group_sum(box.yc * box.yc, ones2) * (1.0 / HEAD_DIM)

        def write():
            yn = box.yc * lax.rsqrt(box.var + LNX_EPS) * lg_ref[...] + lb_ref[...]
            o_ref[s] = ((yn + box.sums[seq_rows:2 * seq_rows] * v[rs]) * gate[rs]).astype(o_ref.dtype)

        return [sums, variance, write]

    works = [make_works(s) for s in range(n_seq)]
    pres = [pre_stages(ws) for ws in works]
    y_all = [[[] for _ in range(n_quads)] for _ in range(n_seq)]
    for group in (pres, [chain_levels(s, works[s], y_all[s]) for s in range(n_seq)],
                  [post_stages(s, y_all[s]) for s in range(n_seq)]):
        for stages in zip(*group):
            for stage in stages:
                stage()


def _rwkv(pr, mu, w0, w_lora, a0, k_k, k_a, r_k, lnx_g, lnx_b, width):
    B, S, n_rwkv = pr.shape
    rows = RWKV_ROWS
    n_seq = RWKV_SEQS
    C = RWKV_CHUNK
    rr = np.arange(n_seq * rows)[:, None]
    cc = np.arange(n_seq * rows)[None, :]
    tri = ((rr // C == cc // C) & (cc <= rr)).astype(np.float32)
    tri3 = jnp.asarray(np.concatenate([tri, tri, tri], axis=1), BF16)
    gg = np.arange(width)
    ones = (gg[:, None] // HEAD_DIM == gg[None, :] // HEAD_DIM).astype(np.float32)
    ones2 = jnp.asarray(np.concatenate([ones, ones], axis=0), BF16)
    fix = lambda shape: pl.BlockSpec(shape, lambda b, i: (0, 0))
    vec = lambda n: fix((1, n))
    return pl.pallas_call(
        functools.partial(_rwkv_kernel, width=width),
        grid=(B // n_seq, S // rows),
        in_specs=[
            pl.BlockSpec((n_seq, rows, n_rwkv), lambda b, i: (b, i, 0)),
            vec(n_rwkv), vec(width), fix(w_lora.shape), vec(width),
            vec(width), vec(width), vec(width), vec(width), vec(width),
            fix(tri3.shape), fix(ones2.shape),
        ],
        out_specs=pl.BlockSpec((n_seq, rows, width), lambda b, i: (b, i, 0)),
        out_shape=jax.ShapeDtypeStruct((B, S, width), BF16),
        scratch_shapes=[
            pltpu.VMEM((n_seq * (width // QUAD), QUAD, QUAD), F32),
            pltpu.VMEM((n_seq, n_rwkv), F32),
        ],
        compiler_params=_cparams(("arbitrary", "arbitrary")),
        name="rwkv7",
    )(pr, mu, w0, w_lora, a0, k_k, k_a, r_k, lnx_g, lnx_b, tri3, ones2)


def _t5_bucket_np(dist):
    n = np.maximum(dist, 0)
    max_exact = NUM_BUCKETS // 2
    nf = np.maximum(n, 1).astype(np.float32)
    large = max_exact + (np.log(nf / max_exact) / math.log(MAX_DISTANCE / max_exact)
                         * (NUM_BUCKETS - max_exact)).astype(np.int32)
    large = np.minimum(large, NUM_BUCKETS - 1)
    return np.where(n < max_exact, n, large).astype(np.int32)


def _near_bucket_tiles(tile):
    kpos = np.arange(tile)[:, None]
    qpos = np.arange(tile)[None, :]
    out = []
    for delta in (0, 1):
        dist = delta * tile + qpos - kpos
        out.append(np.where(dist >= 0, _t5_bucket_np(dist), -1))
    return np.stack(out).astype(np.int32)


def _attn_kernel(relb_ref, bucket_ref, lq1_ref, lk1_ref, lq2_ref, lk2_ref, sg_ref,
                 q_ref, k_ref, vt_ref, o_ref, bias_sc, q2_sc, m_sc, acc_sc, st_sc, *, lam_init):
    tq = q_ref.shape[0]
    n_heads = vt_ref.shape[0]
    heads = range(n_heads)
    qi = pl.program_id(1)

    @pl.when((pl.program_id(0) == 0) & (qi == 0))
    def _():
        for h in heads:
            for d in range(2):
                bk = bucket_ref[d]
                tile = jnp.full(bk.shape, NEG_BIG, F32)
                for j in range(NUM_BUCKETS):
                    tile = jnp.where(bk == j, relb_ref[h, j] * LOG2E, tile)
                r0 = (1 - d) * tq
                bias_sc[h, r0:r0 + tq, 0:tq] = tile
                bias_sc[h, r0:r0 + tq, tq:2 * tq] = tile

    lo = _lane_lo((tq, LANES))
    for h in heads:
        q = q_ref[:, h * LANES:(h + 1) * LANES]
        zero = jnp.zeros_like(q)
        q2_sc[h, 0:tq, :] = jnp.where(lo, q, zero)
        q2_sc[h, tq:2 * tq, :] = jnp.where(lo, zero, q)
    m_sc[...] = jnp.full(m_sc.shape, NEG_BIG, F32)
    acc_sc[...] = jnp.zeros_like(acc_sc)

    def step(k0, tk, band):
        for h in heads:
            st_sc[h, 0:tk, :] = _dot(k_ref[pl.ds(k0, tk), h * LANES:(h + 1) * LANES], q2_sc[h], NT)
        pvs, alphas = [], []
        for h in heads:
            m_old = m_sc[h]
            if band is None:
                st = st_sc[h, 0:tk, :]
                far = relb_ref[h, NUM_BUCKETS - 1] * LOG2E
                m_new = jnp.maximum(m_old, jnp.max(st, axis=0, keepdims=True) + far)
                p = jnp.exp2(st - (m_new - far))
            else:
                st = st_sc[h, 0:tk, :] + bias_sc[h, band:band + tk, :]
                m_new = jnp.maximum(m_old, jnp.max(st, axis=0, keepdims=True))
                p = jnp.exp2(st - m_new)
            m_sc[h] = m_new
            pvs.append(_dot(vt_ref[h, :, pl.ds(k0, tk)], p.astype(BF16)))
            alphas.append(jnp.exp2(m_old - m_new))
        for h in heads:
            acc_sc[h] = alphas[h] * acc_sc[h] + pvs[h]

    n_far = jnp.maximum(qi - 1, 0)
    group = FAR_GROUP * tq

    def far_body(i, carry):
        step(pl.multiple_of(i * group, group), group, None)
        return carry

    lax.fori_loop(0, n_far // FAR_GROUP, far_body, 0)
    done = (n_far // FAR_GROUP) * FAR_GROUP
    left = n_far - done

    @pl.when(left >= 2)
    def _():
        step(pl.multiple_of(done * tq, tq), 2 * tq, None)

    @pl.when(left % 2 == 1)
    def _():
        step(pl.multiple_of((n_far - 1) * tq, tq), tq, None)

    @pl.when(qi >= 1)
    def _():
        step(pl.multiple_of((qi - 1) * tq, tq), 2 * tq, 0)

    @pl.when(qi == 0)
    def _():
        step(0, tq, tq)

    lam = (jnp.exp(jnp.sum(lq1_ref[...] * lk1_ref[...], axis=-1, keepdims=True))
           - jnp.exp(jnp.sum(lq2_ref[...] * lk2_ref[...], axis=-1, keepdims=True)) + lam_init)
    for h in heads:
        acc = acc_sc[h]
        l = acc[ACC_L:ACC_L + 1, :]
        o1 = acc[0:LANES, 0:tq] / l[:, 0:tq]
        o2 = acc[0:LANES, tq:2 * tq] / l[:, tq:2 * tq]
        ot = o1 - lam * o2
        ms = jnp.mean(ot * ot, axis=0, keepdims=True)
        y = ot * lax.rsqrt(ms + NORM_EPS) * sg_ref[...] * (1.0 - lam_init)
        o_ref[:, h * LANES:(h + 1) * LANES] = y.T.astype(o_ref.dtype)


def _diff_attn(q, k, vt, relb_t, lq1, lk1, lq2, lk2, sg_col, lam_init):
    B, S, width = q.shape
    H = width // LANES
    t = ATT_TILE
    buckets = jnp.asarray(_near_bucket_tiles(t))
    vec = pl.BlockSpec((1, HEAD_DIM), lambda b, i: (0, 0))
    return pl.pallas_call(
        functools.partial(_attn_kernel, lam_init=lam_init),
        grid=(B, S // t),
        in_specs=[
            pl.BlockSpec(memory_space=pltpu.SMEM),
            pl.BlockSpec((2, t, t), lambda b, i: (0, 0, 0)),
            vec, vec, vec, vec,
            pl.BlockSpec((LANES, 1), lambda b, i: (0, 0)),
            pl.BlockSpec((None, t, width), lambda b, i: (b, i, 0)),
            pl.BlockSpec((None, S, width), lambda b, i: (b, 0, 0)),
            pl.BlockSpec((None, H, ACC_ROWS, S), lambda b, i: (b, 0, 0, 0)),
        ],
        out_specs=pl.BlockSpec((None, t, width), lambda b, i: (b, i, 0)),
        out_shape=jax.ShapeDtypeStruct((B, S, width), BF16),
        scratch_shapes=[
            pltpu.VMEM((H, 2 * t, 2 * t), F32),
            pltpu.VMEM((H, 2 * t, LANES), BF16),
            pltpu.VMEM((H, 1, 2 * t), F32),
            pltpu.VMEM((H, ACC_ROWS, 2 * t), F32),
            pltpu.VMEM((H, FAR_GROUP * t, 2 * t), F32),
        ],
        compiler_params=_cparams(("arbitrary", "arbitrary")),
        name="diff_attn",
    )(relb_t, buckets, lq1, lk1, lq2, lk2, sg_col, q, k, vt)


def _run_rows(count):
    return jnp.ceil(count * (1.0 / RUN_CHUNK)) * RUN_CHUNK


def _outproj_kernel(x_ref, ya_ref, yb_ref, wa_ref, wb_ref, g_ref, rw_ref, rb_ref,
                    x1_ref, h2_ref, gd_ref, cnt_ref):
    x1 = x_ref[...] + _dot(ya_ref[...], wa_ref[...]) + _dot(yb_ref[...], wb_ref[...])
    x1_ref[...] = x1
    ms = jnp.mean(x1 * x1, axis=-1, keepdims=True)
    h2 = x1 * lax.rsqrt(ms + NORM_EPS) * g_ref[...]
    h2_ref[...] = h2.astype(h2_ref.dtype)
    logits = _mm(h2, rw_ref[...], mode="x3") + rb_ref[...]
    work = logits.T
    expert = lax.broadcasted_iota(jnp.int32, work.shape, 0).astype(F32)
    picks = []
    for _ in range(TOP_K):
        m = jnp.max(work, axis=0, keepdims=True)
        idx = jnp.min(jnp.where(work == m, expert, float(LANES)), axis=0, keepdims=True)
        hit = expert == idx
        picks.append((m, hit))
        work = jnp.where(hit, NEG_BIG, work)
    m0 = picks[0][0]
    es = [jnp.exp(m - m0) for m, _ in picks]
    denom = es[0] + es[1] + es[2] + es[3]
    gd_t = jnp.zeros_like(work)
    for e, (_, hit) in zip(es, picks):
        gd_t = jnp.where(hit, e / denom, gd_t)
    gd = gd_t.T
    gd_ref[...] = gd

    @pl.when(pl.program_id(0) == 0)
    def _():
        cnt_ref[...] = jnp.zeros_like(cnt_ref)

    cnt_ref[0:1, :] = cnt_ref[0:1, :] + _run_rows(jnp.sum(jnp.where(gd > 0.0, 1.0, 0.0), axis=0, keepdims=True))


def _outproj(x2d, ya, yb, wa, wb, g2, rw, rb):
    T, D = x2d.shape
    half = ya.shape[1]
    tm = MOVE_ROWS
    row = lambda i: (i, 0)
    fix = lambda i: (0, 0)
    return pl.pallas_call(
        _outproj_kernel,
        grid=(T // tm,),
        in_specs=[
            pl.BlockSpec((tm, D), row),
            pl.BlockSpec((tm, half), row),
            pl.BlockSpec((tm, half), row),
            pl.BlockSpec((half, D), fix),
            pl.BlockSpec((half, D), fix),
            pl.BlockSpec((1, D), fix),
            pl.BlockSpec((D, LANES), fix),
            pl.BlockSpec((1, LANES), fix),
        ],
        out_specs=[
            pl.BlockSpec((tm, D), row),
            pl.BlockSpec((tm, D), row),
            pl.BlockSpec((tm, LANES), row),
            pl.BlockSpec((8, LANES), fix),
        ],
        out_shape=[
            jax.ShapeDtypeStruct((T, D), F32),
            jax.ShapeDtypeStruct((T, D), BF16),
            jax.ShapeDtypeStruct((T, LANES), F32),
            jax.ShapeDtypeStruct((8, LANES), F32),
        ],
        compiler_params=_cparams(("arbitrary",)),
        name="outproj_router",
    )(x2d, ya, yb, wa, wb, g2, rw, rb)


def _padded_rows(cnt, rb):
    return jnp.ceil(cnt * (1.0 / rb)) * rb


def _route_kernel(gd_ref, tot_ref, lp_ref, gate_ref, tab_ref, cnt_ref, carry_sc, pstart_sc, *, rb):
    i = pl.program_id(0)
    tb = gd_ref.shape[0]
    gd = gd_ref[...]
    sel = gd > 0.0
    self = jnp.where(sel, 1.0, 0.0)
    run_pad = _run_rows(jnp.sum(self, axis=0, keepdims=True))
    r2 = lax.broadcasted_iota(jnp.int32, (LANES, LANES), 0)
    c2 = lax.broadcasted_iota(jnp.int32, (LANES, LANES), 1)

    @pl.when(i == 0)
    def _():
        cnt = tot_ref[0:1, :]
        upper = jnp.where(r2 < c2, 1.0, 0.0)
        pstart = _dot(jnp.broadcast_to(_padded_rows(cnt, rb), (8, LANES)), upper, prec=HP)[0:1, :]
        pstart_sc[...] = pstart
        cnt_ref[0:1, :] = cnt
        cnt_ref[1:2, :] = pstart
        cnt_ref[2:8, :] = jnp.zeros((6, LANES), F32)
        carry_sc[...] = jnp.zeros_like(carry_sc)

    rr = lax.broadcasted_iota(jnp.int32, (tb, tb), 0)
    cc = lax.broadcasted_iota(jnp.int32, (tb, tb), 1)
    lower = jnp.where(cc < rr, 1.0, 0.0).astype(BF16)
    sel_b = self.astype(BF16)
    prefix = _dot(lower, sel_b)
    upper_b = jnp.where(r2 < c2, 1.0, 0.0).astype(BF16)
    off = _dot(jnp.broadcast_to(run_pad, (8, LANES)).astype(BF16), upper_b)[0:1, :]
    lp = off + prefix
    upper_incl = jnp.where(r2 <= c2, 1.0, 0.0).astype(BF16)
    slot = _dot(sel_b, upper_incl)
    lane = lax.broadcasted_iota(jnp.int32, (tb, LANES), 1)
    p_out = jnp.zeros((tb, LANES), F32)
    g_out = jnp.zeros((tb, LANES), F32)
    for kk in range(TOP_K):
        mk = sel & (slot == float(kk + 1))
        p_k = jnp.sum(jnp.where(mk, lp, 0.0), axis=-1, keepdims=True)
        g_k = jnp.sum(jnp.where(mk, gd, 0.0), axis=-1, keepdims=True)
        p_out = jnp.where(lane == kk, p_k, p_out)
        g_out = jnp.where(lane == kk, g_k, g_out)
    lp_ref[...] = p_out
    gate_ref[...] = g_out
    def as_column(v):
        return jnp.sum(jnp.where(r2 == c2, jnp.broadcast_to(v, (LANES, LANES)), 0.0), axis=1, keepdims=True)

    first = as_column(off * (1.0 / RUN_CHUNK))
    count = as_column(run_pad * (1.0 / RUN_CHUNK))
    start = as_column(pstart_sc[...] + carry_sc[...])
    chunk = lax.broadcasted_iota(jnp.int32, (LANES, CHUNK_LANES), 1).astype(F32)
    mine = (chunk >= first) & (chunk < first + count)
    dst = jnp.sum(jnp.where(mine, start + (chunk - first) * RUN_CHUNK, 0.0), axis=0, keepdims=True)
    n_chunks = jnp.sum(count, axis=0, keepdims=True)
    tab_ref[0:1, :] = dst.astype(jnp.int32)
    tab_ref[1:2, :] = jnp.broadcast_to(n_chunks, (1, CHUNK_LANES)).astype(jnp.int32)
    tab_ref[2:8, :] = jnp.zeros((6, CHUNK_LANES), jnp.int32)
    carry_sc[...] = carry_sc[...] + run_pad


def _route(gd, totals, rb):
    T = gd.shape[0]
    tb = MOVE_ROWS
    blk = lambda i: (i, 0)
    fix = lambda i: (0, 0)
    return pl.pallas_call(
        functools.partial(_route_kernel, rb=rb),
        grid=(T // tb,),
        in_specs=[pl.BlockSpec((tb, LANES), blk), pl.BlockSpec((8, LANES), fix)],
        out_specs=[
            pl.BlockSpec((tb, LANES), blk),
            pl.BlockSpec((tb, LANES), blk),
            pl.BlockSpec((8, CHUNK_LANES), blk),
            pl.BlockSpec((8, LANES), fix),
        ],
        out_shape=[
            jax.ShapeDtypeStruct((T, LANES), F32),
            jax.ShapeDtypeStruct((T, LANES), F32),
            jax.ShapeDtypeStruct((T // tb * 8, CHUNK_LANES), jnp.int32),
            jax.ShapeDtypeStruct((8, LANES), F32),
        ],
        scratch_shapes=[pltpu.VMEM((1, LANES), F32), pltpu.VMEM((1, LANES), F32)],
        compiler_params=_cparams(("arbitrary",)),
        name="route_positions",
    )(gd, totals)


def _chunk_copy(src_ref, src_row, dst_ref, dst_row, sem):
    return pltpu.make_async_copy(src_ref.at[pl.ds(src_row, RUN_CHUNK), :],
                                 dst_ref.at[pl.ds(dst_row, RUN_CHUNK), :], sem)


def _start_runs(b, nc_ref, dst_ref, copy_chunk):
    def issue(c, carry):
        copy_chunk(pl.multiple_of(c * RUN_CHUNK, RUN_CHUNK),
                   pl.multiple_of(dst_ref[b * LOCAL_CHUNKS + c], RUN_CHUNK)).start()
        return carry

    lax.fori_loop(0, nc_ref[b], issue, 0)


def _wait_runs(b, nc_ref, wait_rows):
    total = nc_ref[b]
    p = 1
    while p <= LOCAL_CHUNKS:
        @pl.when((total & p) != 0)
        def _(p=p):
            wait_rows(p * RUN_CHUNK).wait()
        p *= 2


def _rows_copy(src_ref, dst_ref, n_rows, sem):
    return pltpu.make_async_copy(src_ref.at[pl.ds(0, n_rows), :], dst_ref.at[pl.ds(0, n_rows), :], sem)


def _local_onehot(lp, lb):
    j = lax.broadcasted_iota(jnp.int32, (lp.shape[0], lb), 1).astype(F32)
    hit = lp[:, 0:1] == j
    for kk in range(1, TOP_K):
        hit = hit | (lp[:, kk:kk + 1] == j)
    return hit


def _scatter_kernel(nc_ref, dst_ref, t0_ref, tc_ref, lp_ref, h_ref, xs_ref, buf, zero_buf, sems):
    b = pl.program_id(0)
    slot = b % 2
    pt = jnp.where(_local_onehot(lp_ref[...], buf.shape[1]), 1.0, 0.0).astype(BF16)
    buf[slot] = _dot(pt, h_ref[...], TN)
    _start_runs(b, nc_ref, dst_ref,
                lambda s, d: _chunk_copy(buf.at[slot], s, xs_ref, d, sems.at[slot]))

    @pl.when(b >= 1)
    def _():
        _wait_runs(b - 1, nc_ref, lambda n: _rows_copy(buf.at[1 - slot], xs_ref, n, sems.at[1 - slot]))

    @pl.when(b == pl.num_programs(0) - 1)
    def _():
        _wait_runs(b, nc_ref, lambda n: _rows_copy(buf.at[slot], xs_ref, n, sems.at[slot]))
        zero_buf[...] = jnp.zeros_like(zero_buf)
        total = 0
        for e in range(N_EXPERTS + 1):
            t0 = t0_ref[e]

            def issue(c, carry, t0=t0):
                _chunk_copy(zero_buf, 0, xs_ref, pl.multiple_of(t0 + c * RUN_CHUNK, RUN_CHUNK), sems.at[0]).start()
                return carry

            lax.fori_loop(0, tc_ref[e], issue, 0)
            total = total + tc_ref[e]

        def drain(c, carry):
            _chunk_copy(zero_buf, 0, xs_ref, 0, sems.at[0]).wait()
            return carry

        lax.fori_loop(0, total, drain, 0)


def _scatter_rows(tabs, tails, lp, h2, n_rows):
    T, D = h2.shape
    tb = MOVE_ROWS
    return pl.pallas_call(
        _scatter_kernel,
        grid_spec=pltpu.PrefetchScalarGridSpec(
            num_scalar_prefetch=4,
            grid=(T // tb,),
            in_specs=[
                pl.BlockSpec((tb, LANES), lambda i, *_: (i, 0)),
                pl.BlockSpec((tb, D), lambda i, *_: (i, 0)),
            ],
            out_specs=pl.BlockSpec(memory_space=pl.ANY),
            scratch_shapes=[pltpu.VMEM((2, LOCAL_ROWS, D), F32), pltpu.VMEM((RUN_CHUNK, D), F32),
                            pltpu.SemaphoreType.DMA((2,))],
        ),
        out_shape=jax.ShapeDtypeStruct((n_rows, D), F32),
        compiler_params=_cparams(("arbitrary",)),
        name="scatter_rows",
    )(*tabs, *tails, lp, h2)


def _moe_kernel(be_ref, na_ref, slot_ref, nxt_ref, xs_ref, w1_hbm, b1_ref, w2_hbm, b2_ref, ys_ref,
                w1f, w2f, w1b, w2b, sems):
    j = pl.program_id(0)
    e = be_ref[j]
    slot = slot_ref[j]

    def fetch(expert, sl):
        return (pltpu.make_async_copy(w1_hbm.at[expert], w1f.at[sl], sems.at[0, sl]),
                pltpu.make_async_copy(w2_hbm.at[expert], w2f.at[sl], sems.at[1, sl]))

    @pl.when(j == 0)
    def _():
        for cp in fetch(e, slot):
            cp.start()

    @pl.when((j == 0) | (e != be_ref[jnp.maximum(j - 1, 0)]))
    def _():
        for cp in fetch(e, slot):
            cp.wait()
        w1b[...] = w1f[slot].astype(BF16)
        w2b[...] = w2f[slot].astype(BF16)

        @pl.when(nxt_ref[j] >= 0)
        def _():
            for cp in fetch(nxt_ref[j], 1 - slot):
                cp.start()

    @pl.when(j < na_ref[0])
    def _():
        xb = xs_ref[...].astype(BF16)
        hid = _dot(xb, w1b[...]) + b1_ref[...]
        half = hid.shape[1] // 2
        x_glu = jnp.minimum(hid[:, :half], SWIGLU_LIMIT)
        x_lin = jnp.clip(hid[:, half:], -SWIGLU_LIMIT, SWIGLU_LIMIT)
        act = x_glu * _sigmoid(SWIGLU_ALPHA * x_glu) * (x_lin + 1.0)
        ys_ref[...] = _dot(act.astype(BF16), w2b[...]) + b2_ref[...]

    @pl.when(j >= na_ref[0])
    def _():
        ys_ref[...] = jnp.zeros_like(ys_ref)


def _expert_runs(blk_expert):
    n = blk_expert.shape[0]
    idx = jnp.arange(n)
    starts = jnp.concatenate([jnp.ones((1,), bool), blk_expert[1:] != blk_expert[:-1]])
    slot = (jnp.cumsum(starts) - 1) % 2
    next_start = lax.cummin(jnp.where(starts, idx, n)[::-1])[::-1]
    after = jnp.concatenate([next_start[1:], jnp.full((1,), n)])
    nxt = jnp.where(after < n, blk_expert[jnp.minimum(after, n - 1)], -1)
    return slot.astype(jnp.int32), nxt.astype(jnp.int32)


def _moe(blk_expert, n_active, xs, w1, b1, w2, b2):
    R, D = xs.shape
    E, _, D2 = w1.shape
    rb = MOE_ROWS
    nb = R // rb
    slot, nxt = _expert_runs(blk_expert)
    blk = lambda j, be, na, *_: (jnp.minimum(j, na[0] - 1), 0)
    return pl.pallas_call(
        _moe_kernel,
        grid_spec=pltpu.PrefetchScalarGridSpec(
            num_scalar_prefetch=4,
            grid=(nb,),
            in_specs=[
                pl.BlockSpec((rb, D), blk),
                pl.BlockSpec(memory_space=pl.ANY),
                pl.BlockSpec((None, 1, D2), lambda j, be, *_: (be[j], 0, 0)),
                pl.BlockSpec(memory_space=pl.ANY),
                pl.BlockSpec((None, 1, D), lambda j, be, *_: (be[j], 0, 0)),
            ],
            out_specs=pl.BlockSpec((rb, D), lambda j, *_: (j, 0)),
            scratch_shapes=[pltpu.VMEM((2, D, D2), F32), pltpu.VMEM((2, D2 // 2, D), F32),
                            pltpu.VMEM((D, D2), BF16), pltpu.VMEM((D2 // 2, D), BF16),
                            pltpu.SemaphoreType.DMA((2, 2))],
        ),
        out_shape=jax.ShapeDtypeStruct((R, D), F32),
        compiler_params=_cparams(("arbitrary",)),
        name="moe_experts",
    )(blk_expert, n_active, slot, nxt, xs, w1, b1, w2, b2)


def _combine_kernel(nc_ref, dst_ref, lp_ref, gate_ref, x1_ref, ys_ref, o_ref, buf, sems):
    b = pl.program_id(0)
    slot = b % 2

    def fetch(blk, sl):
        _start_runs(blk, nc_ref, dst_ref,
                    lambda s, d: _chunk_copy(ys_ref, d, buf.at[sl], s, sems.at[sl]))

    @pl.when(b == 0)
    def _():
        buf[...] = jnp.zeros_like(buf)
        fetch(b, slot)

    @pl.when(b + 1 < pl.num_programs(0))
    def _():
        fetch(b + 1, 1 - slot)

    _wait_runs(b, nc_ref, lambda n: _rows_copy(ys_ref, buf.at[slot], n, sems.at[slot]))
    lp = lp_ref[...]
    g = gate_ref[...]
    _, lb, d = buf.shape
    j = lax.broadcasted_iota(jnp.int32, (lp.shape[0], lb), 1).astype(F32)
    gt = jnp.zeros(j.shape, F32)
    for kk in range(TOP_K):
        gt = jnp.where(lp[:, kk:kk + 1] == j, g[:, kk:kk + 1], gt)
    g_hi, g_lo = _split2(gt)
    ones = jnp.ones((lp.shape[0], LANES), BF16)
    g_row = _dot(g_hi, ones, TN) + _dot(g_lo, ones, TN)
    rows = (buf[slot] * jnp.concatenate([g_row] * (d // LANES), axis=1)).astype(BF16)
    p01 = jnp.where(gt != 0.0, 1.0, 0.0).astype(BF16)
    o_ref[...] = x1_ref[...] + _dot(p01, rows)


def _combine(tabs, lp, gate, x1, ys):
    T, D = x1.shape
    tb = MOVE_ROWS
    return pl.pallas_call(
        _combine_kernel,
        grid_spec=pltpu.PrefetchScalarGridSpec(
            num_scalar_prefetch=2,
            grid=(T // tb,),
            in_specs=[
                pl.BlockSpec((tb, LANES), lambda i, *_: (i, 0)),
                pl.BlockSpec((tb, LANES), lambda i, *_: (i, 0)),
                pl.BlockSpec((tb, D), lambda i, *_: (i, 0)),
                pl.BlockSpec(memory_space=pl.ANY),
            ],
            out_specs=pl.BlockSpec((tb, D), lambda i, *_: (i, 0)),
            scratch_shapes=[pltpu.VMEM((2, LOCAL_ROWS, D), F32), pltpu.SemaphoreType.DMA((2,))],
        ),
        out_shape=jax.ShapeDtypeStruct((T, D), F32),
        compiler_params=_cparams(("arbitrary",)),
        name="combine_rows",
    )(*tabs, lp, gate, x1, ys)


def _pad_cols(a, n):
    return jnp.pad(a, ((0, 0), (0, n - a.shape[1])))


def _pad_rows(a, n):
    return jnp.pad(a, ((0, n - a.shape[0]), (0, 0)))


def _lora_weight(w2, a2, g2, width):
    z = jnp.zeros((LANES, width), F32)
    return jnp.concatenate([
        jnp.concatenate([_pad_rows(w2, LANES), z, z], axis=1),
        jnp.concatenate([z, _pad_rows(a2, LANES), z], axis=1),
        jnp.concatenate([z, z, _pad_rows(g2, LANES)], axis=1),
    ], axis=0).astype(BF16)


def _split_rwkv_cols(a, width):
    o = 3 * width
    return jnp.concatenate([
        a[:, :o],
        _pad_cols(a[:, o:o + DECAY_LORA], LANES),
        _pad_cols(a[:, o + DECAY_LORA:o + DECAY_LORA + AAA_LORA], LANES),
        _pad_cols(a[:, o + DECAY_LORA + AAA_LORA:o + DECAY_LORA + AAA_LORA + GATE_LORA], LANES),
    ], axis=1)


def kernel(x, norm1_g, w_in, mu_shift, w0, w2, a0, a2, g2, k_k, k_a, r_k, lnx_g, lnx_b, qn_g, kn_g,
           lam_q1, lam_k1, lam_q2, lam_k2, subln_g, rel_bias, w_out, norm2_g, router_w, router_b,
           exp_w1, exp_b1, exp_w2, exp_b2):
    B, S, D = x.shape
    T = B * S
    depth = norm1_g.shape[0]
    width = w0.shape[1]
    n_rwkv_cols = 3 * width + DECAY_LORA + AAA_LORA + GATE_LORA
    n_rwkv_pad = 3 * width + 3 * LANES
    n_groups = width // HEAD_DIM
    H = width // LANES
    relb_t = rel_bias.T
    row = lambda a: a.reshape(1, -1)

    for layer in range(depth):
        lam_init = 0.8 - 0.6 * math.exp(-0.3 * layer)
        w_l = w_in[layer]
        w_all = jnp.concatenate([_split_rwkv_cols(w_l[:, :n_rwkv_cols], width), w_l[:, n_rwkv_cols:]],
                                axis=1).astype(BF16)
        mu = _split_rwkv_cols(row(mu_shift[layer]), width)
        qg = row(jnp.tile(qn_g[layer], n_groups))
        kg = row(jnp.tile(kn_g[layer], n_groups))

        pr, q, k, vt = _inproj(x.reshape(T, D), row(norm1_g[layer]), w_all, qg, kg, n_rwkv_pad, width, S)

        y_a = _rwkv(pr.reshape(B, S, n_rwkv_pad), mu, row(w0[layer]),
                    _lora_weight(w2[layer], a2[layer], g2[layer], width), row(a0[layer]),
                    row(k_k[layer]), row(k_a[layer]), row(r_k[layer]), row(lnx_g[layer]),
                    row(lnx_b[layer]), width)

        y_b = _diff_attn(q.reshape(B, S, width), k.reshape(B, S, width), vt, relb_t,
                         row(lam_q1[layer]), row(lam_k1[layer]), row(lam_q2[layer]), row(lam_k2[layer]),
                         subln_g[layer].reshape(-1, 1), lam_init)

        wo = w_out[layer].astype(BF16)
        rw = _pad_cols(router_w[layer], LANES)
        rb = jnp.concatenate([router_b[layer], jnp.full((LANES - N_EXPERTS,), NEG_BIG, F32)]).reshape(1, -1)
        x1, h2, gd, totals = _outproj(x.reshape(T, D), y_a.reshape(T, width), y_b.reshape(T, width),
                                      wo[:width], wo[width:], row(norm2_g[layer]), rw, rb)

        lp, gate, tab, cnt = _route(gd, totals, MOE_ROWS)
        tab = tab.reshape(-1, 8, CHUNK_LANES)
        tabs = (tab[:, 1, 0], tab[:, 0, :LOCAL_CHUNKS].reshape(-1))
        run_end = cnt[1, :N_EXPERTS] + cnt[0, :N_EXPERTS]
        pad_end = cnt[1, :N_EXPERTS] + _padded_rows(cnt[0, :N_EXPERTS], MOE_ROWS)
        run_pad_max = (T // MOVE_ROWS) * N_EXPERTS * (RUN_CHUNK - 1)
        n_blocks = -(-(T * TOP_K + run_pad_max + N_EXPERTS * (MOE_ROWS - 1)) // MOE_ROWS)
        gap_start = jnp.concatenate([run_end, pad_end[-1:]])
        gap_end = jnp.concatenate([pad_end, jnp.full((1,), n_blocks * MOE_ROWS, F32)])
        tails = (gap_start.astype(jnp.int32), ((gap_end - gap_start) / RUN_CHUNK).astype(jnp.int32))
        blk_start = (jnp.arange(n_blocks) * MOE_ROWS).astype(F32)
        blk_expert = jnp.minimum(jnp.sum(pad_end[None, :] <= blk_start[:, None], axis=1), N_EXPERTS - 1)
        n_active = (pad_end[N_EXPERTS - 1] / MOE_ROWS).astype(jnp.int32).reshape(1)

        xs = _scatter_rows(tabs, tails, lp, h2, n_blocks * MOE_ROWS)
        ys = _moe(blk_expert.astype(jnp.int32), n_active, xs,
                  exp_w1[layer], exp_b1[layer][:, None, :], exp_w2[layer], exp_b2[layer][:, None, :])
        x = _combine(tabs, lp, gate, x1, ys).reshape(B, S, D)
    return x
```

```python
import functools
import math

import numpy as np
import jax
import jax.numpy as jnp
from jax import lax
from jax.experimental import pallas as pl
from jax.experimental.pallas import tpu as pltpu

F32 = jnp.float32
BF16 = jnp.bfloat16
HP = lax.Precision.HIGHEST

HEAD_DIM = 64
DECAY_LORA = 32
AAA_LORA = 32
GATE_LORA = 96
NUM_BUCKETS = 32
MAX_DISTANCE = 128
N_EXPERTS = 32
TOP_K = 4
SWIGLU_LIMIT = 7.0
SWIGLU_ALPHA = 1.702
NORM_EPS = 1e-5
LNX_EPS = 64e-5

LANES = 128
VMEM_LIMIT_BYTES = 56 * 1024 * 1024

NEG_BIG = -1e30
LOG2E = math.log2(math.e)
ACC_L = LANES
ACC_ROWS = LANES + 8

PROJ_ROWS = 512
RWKV_CHUNK = 64
RWKV_ROWS = 256
RWKV_SEQS = 2
ATT_TILE = 256
FAR_GROUP = 4
MOE_ROWS = 512
MOVE_ROWS = 256
RUN_CHUNK = 8
LOCAL_ROWS = MOVE_ROWS * 4 + 32 * RUN_CHUNK
LOCAL_CHUNKS = LOCAL_ROWS // RUN_CHUNK
CHUNK_LANES = 256

NN = (((1,), (0,)), ((), ()))
NT = (((1,), (1,)), ((), ()))
TN = (((0,), (0,)), ((), ()))


def _dot(a, b, dims=NN, prec=None):
    return lax.dot_general(a, b, dims, precision=prec, preferred_element_type=F32)


def _cparams(sem):
    return pltpu.CompilerParams(dimension_semantics=sem, vmem_limit_bytes=VMEM_LIMIT_BYTES)


def _sigmoid(x):
    return 1.0 / (1.0 + jnp.exp(-x))


def _lane_lo(shape):
    lane = lax.broadcasted_iota(jnp.int32, shape, len(shape) - 1)
    return (lane % LANES) < HEAD_DIM


def _head_rms_normalize(x, gain, scale):
    outs = []
    for t in range(x.shape[1] // LANES):
        xt = x[:, t * LANES:(t + 1) * LANES]
        lo = _lane_lo(xt.shape)
        x2 = xt * xt
        s0 = jnp.sum(jnp.where(lo, x2, 0.0), axis=-1, keepdims=True)
        s1 = jnp.sum(jnp.where(lo, 0.0, x2), axis=-1, keepdims=True)
        inv = lax.rsqrt(jnp.where(lo, s0, s1) * (1.0 / HEAD_DIM) + NORM_EPS)
        outs.append(xt * inv)
    return jnp.concatenate(outs, axis=-1) * gain * scale


def _inproj_kernel(x_ref, g_ref, w_ref, qg_ref, kg_ref, pr_ref, q_ref, k_ref, vt_ref, *, n_rwkv, width):
    x = x_ref[...]
    ms = jnp.mean(x * x, axis=-1, keepdims=True)
    h = (x * lax.rsqrt(ms + NORM_EPS) * g_ref[...]).astype(BF16)
    step = 384
    for c0 in range(0, n_rwkv, step):
        pr_ref[:, c0:c0 + step] = _dot(h, w_ref[:, c0:c0 + step])
    q = _dot(h, w_ref[:, n_rwkv:n_rwkv + width])
    q_ref[...] = _head_rms_normalize(q, qg_ref[...], HEAD_DIM ** -0.5 * LOG2E).astype(BF16)
    k = _dot(h, w_ref[:, n_rwkv + width:n_rwkv + 2 * width])
    k_ref[...] = _head_rms_normalize(k, kg_ref[...], 1.0).astype(BF16)
    v = _dot(h, w_ref[:, n_rwkv + 2 * width:n_rwkv + 3 * width])
    for hd in range(width // LANES):
        vt_ref[hd, 0:LANES, :] = v[:, hd * LANES:(hd + 1) * LANES].T.astype(BF16)
        vt_ref[hd, LANES:ACC_ROWS, :] = jnp.ones((ACC_ROWS - LANES, v.shape[0]), BF16)


def _inproj(x2d, g, w_all, qg, kg, n_rwkv, width, seq_len):
    T, D = x2d.shape
    n_all = w_all.shape[1]
    tm = PROJ_ROWS
    H = width // LANES
    per_seq = seq_len // tm
    row = lambda i: (i, 0)
    fix = lambda i: (0, 0)
    return pl.pallas_call(
        functools.partial(_inproj_kernel, n_rwkv=n_rwkv, width=width),
        grid=(T // tm,),
        in_specs=[
            pl.BlockSpec((tm, D), row),
            pl.BlockSpec((1, D), fix),
            pl.BlockSpec((D, n_all), fix),
            pl.BlockSpec((1, width), fix),
            pl.BlockSpec((1, width), fix),
        ],
        out_specs=[
            pl.BlockSpec((tm, n_rwkv), row),
            pl.BlockSpec((tm, width), row),
            pl.BlockSpec((tm, width), row),
            pl.BlockSpec((None, H, ACC_ROWS, tm), lambda i: (i // per_seq, 0, 0, i % per_seq)),
        ],
        out_shape=[
            jax.ShapeDtypeStruct((T, n_rwkv), F32),
            jax.ShapeDtypeStruct((T, width), BF16),
            jax.ShapeDtypeStruct((T, width), BF16),
            jax.ShapeDtypeStruct((T // seq_len, H, ACC_ROWS, seq_len), BF16),
        ],
        compiler_params=_cparams(("arbitrary",)),
        name="inproj",
    )(x2d, g, w_all, qg, kg)


def _split2(x):
    hi = x.astype(BF16)
    return hi, (x - hi.astype(F32)).astype(BF16)


def _mm(a, b, dims=NN, mode="bf16"):
    if mode == "bf16":
        return _dot(a.astype(BF16), b.astype(BF16), dims)
    ah, al = _split2(a)
    bh, bl = _split2(b)
    lhs = jnp.concatenate([ah, ah, al], axis=dims[0][0][0])
    rhs = jnp.concatenate([bh, bl, bh], axis=dims[0][1][0])
    return _dot(lhs, rhs, dims)


def _group_sum(x, ones2):
    hi, lo = _split2(x)
    return _dot(jnp.concatenate([hi, lo], axis=1), ones2)


P_INV = "x3"
P_SUBST = "bf16"
P_STATE = "bf16"
INV_BLOCK = 16
QUAD = 256


def _rwkv_kernel(pr_ref, mu_ref, w0_ref, wl_ref, a0_ref, kk_ref, ka_ref, rk_ref, lg_ref, lb_ref,
                 tri3_ref, ones2_ref, o_ref, state_sc, prev_sc, *, width):
    C = RWKV_CHUNK
    n_seq, seq_rows, _ = pr_ref.shape
    rows = n_seq * seq_rows
    n_quads = width // QUAD
    n_chunks = seq_rows // C
    heads_q = QUAD // HEAD_DIM

    @pl.when(pl.program_id(1) == 0)
    def _():
        state_sc[...] = jnp.zeros_like(state_sc)
        prev_sc[...] = jnp.zeros_like(prev_sc)

    x = jnp.concatenate([pr_ref[s] for s in range(n_seq)], axis=0)
    rid = lax.broadcasted_iota(jnp.int32, x.shape, 0)
    prev = pltpu.roll(x, 1, axis=0)
    for s in range(n_seq):
        prev = jnp.where(rid == s * seq_rows, prev_sc[s:s + 1, :], prev)
        prev_sc[s:s + 1, :] = x[(s + 1) * seq_rows - 1:(s + 1) * seq_rows, :]
    xs = x + (prev - x) * mu_ref[...]

    r = xs[:, 0:width]
    k = xs[:, width:2 * width]
    v = xs[:, 2 * width:3 * width]
    o = 3 * width
    lora_in = jnp.concatenate([jnp.tanh(xs[:, o:o + LANES]), xs[:, o + LANES:o + 2 * LANES],
                               _sigmoid(xs[:, o + 2 * LANES:o + 3 * LANES])], axis=1).astype(BF16)
    lora = _dot(lora_in, wl_ref[...])
    z = -(w0_ref[...] + lora[:, 0:width])
    softplus = jnp.maximum(z, 0.0) + jnp.log(1.0 + jnp.exp(-jnp.abs(z)))
    log_decay = -jnp.exp(-softplus - 0.5)
    a_lr = _sigmoid(a0_ref[...] + lora[:, width:2 * width])
    gate = lora[:, 2 * width:3 * width]

    ones2 = ones2_ref[...]
    kk = k * kk_ref[...]
    kk = kk * lax.rsqrt(jnp.maximum(_group_sum(kk * kk, ones2), 1e-24))
    k = k * (1.0 + (a_lr - 1.0) * ka_ref[...])
    a_vec = -kk
    b_vec = kk * a_lr

    p1 = log_decay.astype(BF16)
    rem = log_decay - p1.astype(F32)
    p2 = rem.astype(BF16)
    p3 = (rem - p2.astype(F32)).astype(BF16)
    cum = _dot(tri3_ref[...], jnp.concatenate([p1, p2, p3], axis=0))

    t_i = lax.broadcasted_iota(jnp.int32, (C, 2 * QUAD), 0)
    s_i = lax.broadcasted_iota(jnp.int32, (C, 2 * QUAD), 1) % C
    strict2 = t_i > s_i
    incl2 = t_i >= s_i
    lane_head = lax.broadcasted_iota(jnp.int32, (C, QUAD), 1) // HEAD_DIM
    row_blk = lax.broadcasted_iota(jnp.int32, (C, QUAD), 0) // INV_BLOCK
    same_blk = row_blk == ((lax.broadcasted_iota(jnp.int32, (C, QUAD), 1) % C) // INV_BLOCK)
    r2 = lax.broadcasted_iota(jnp.int32, (QUAD, QUAD), 0)
    c2 = lax.broadcasted_iota(jnp.int32, (QUAD, QUAD), 1)
    same_head = (r2 // HEAD_DIM) == (c2 // HEAD_DIM)

    def by_head(m):
        zero = jnp.zeros_like(m)
        return jnp.concatenate([jnp.where(lane_head == h, m, zero) for h in range(heads_q)], axis=0)

    def bd_rhs(m, mode):
        if mode == "bf16":
            return by_head(m.astype(BF16))
        hi, lo = _split2(m)
        hi_bd = by_head(hi)
        return jnp.concatenate([hi_bd, by_head(lo), hi_bd], axis=0)

    def mm_bd(a, rhs, mode):
        if mode == "bf16":
            return _dot(a.astype(BF16), rhs)
        ah, al = _split2(a)
        return _dot(jnp.concatenate([ah, ah, al], axis=1), rhs)

    class Work:
        pass

    def make_works(s):
        ws = []
        for c in range(n_chunks):
            for q in range(n_quads):
                w = Work()
                ls = slice(q * QUAD, (q + 1) * QUAD)
                rs = slice(s * seq_rows + c * C, s * seq_rows + (c + 1) * C)
                ld = log_decay[rs, ls]
                cm = cum[rs, ls]
                cl = cm[C - 1:C, :]
                e_in = jnp.exp(cm)
                e_neg = jnp.exp(-cm)
                e_hat = jnp.exp(cl - cm)
                w.q = q
                w.dec = jnp.exp(cl)
                w.vv = v[rs, ls]
                w.ar_t = jnp.concatenate([a_vec[rs, ls] * jnp.exp(cm - ld), r[rs, ls] * e_in], axis=0)
                w.bk_t = jnp.concatenate([bd_rhs(b_vec[rs, ls] * e_neg, "bf16"),
                                          bd_rhs(k[rs, ls] * e_neg, "bf16")], axis=0)
                w.bk_h = jnp.concatenate([b_vec[rs, ls] * e_hat, k[rs, ls] * e_hat], axis=0)
                ws.append(w)
        return ws

    def pre_stages(ws):
        def gram():
            for w in ws:
                g = _dot(w.ar_t.astype(BF16), w.bk_t, NT)
                low = jnp.where(strict2, g[0:C, :], 0.0)
                l_cat = low[:, 0:QUAD]
                rbk = jnp.where(incl2, g[C:2 * C, :], 0.0)
                w.rb_cat = rbk[:, 0:QUAD]
                w.kv_lhs = jnp.concatenate([low[:, QUAD:], rbk[:, QUAD:]], axis=0)
                w.l_off = jnp.where(same_blk, 0.0, l_cat)
                w.d_inv = jnp.where(same_blk, l_cat, 0.0)

        def square0():
            for w in ws:
                w.l_pow = mm_bd(w.d_inv, bd_rhs(w.d_inv, P_INV), P_INV)

        def square(last):
            def run():
                for w in ws:
                    if last:
                        w.d_inv = w.d_inv + w.l_pow + mm_bd(w.d_inv, bd_rhs(w.l_pow, P_INV), P_INV)
                    else:
                        both = mm_bd(jnp.concatenate([w.d_inv, w.l_pow], axis=0), bd_rhs(w.l_pow, P_INV), P_INV)
                        w.d_inv = w.d_inv + w.l_pow + both[0:C]
                        w.l_pow = both[C:2 * C]
            return run

        def neumann1():
            for w in ws:
                w.m1 = w.l_off + mm_bd(w.d_inv, bd_rhs(w.l_off, P_SUBST), P_SUBST)
                w.m1_bd = bd_rhs(w.m1, P_SUBST)

        def neumann2():
            for w in ws:
                w.m2 = mm_bd(w.m1, w.m1_bd, P_SUBST)

        def neumann3():
            for w in ws:
                w.nn = w.m1 + w.m2 + mm_bd(w.m2, w.m1_bd, P_SUBST)

        def finish():
            for w in ws:
                w.t_inv = w.nn + w.d_inv + mm_bd(w.nn, bd_rhs(w.d_inv, P_SUBST), P_SUBST)

        def values():
            for w in ws:
                w.kv = mm_bd(w.kv_lhs, bd_rhs(w.vv, P_STATE), P_STATE)

        n_sq = int(math.log2(INV_BLOCK))
        return ([gram, square0] + [square(it + 1 == n_sq) for it in range(1, n_sq)]
                + [neumann1, neumann2, neumann3, finish, values])

    def chain_levels(s, ws, y_parts):
        levels = []
        for c in range(n_chunks):
            wc = ws[c * n_quads:(c + 1) * n_quads]

            def base(wc=wc):
                for w in wc:
                    st = state_sc[s * n_quads + w.q]
                    w.base = _mm(w.ar_t, st, NT, mode=P_STATE) + w.kv

            def solve(wc=wc):
                for w in wc:
                    u0 = w.base[0:C]
                    w.u = u0 + mm_bd(w.t_inv, bd_rhs(u0, P_STATE), P_STATE)

            def update(wc=wc):
                for w in wc:
                    upd = _mm(jnp.concatenate([w.u, w.vv], axis=0), w.bk_h, TN, mode=P_STATE)
                    sq = s * n_quads + w.q
                    state_sc[sq] = state_sc[sq] * w.dec + jnp.where(same_head, upd, 0.0)
                for w in wc:
                    y_parts[w.q].append(w.base[C:2 * C] + mm_bd(w.rb_cat, bd_rhs(w.u, P_STATE), P_STATE))

            levels += [base, solve, update]
        return levels

    def post_stages(s, y_parts):
        rs = slice(s * seq_rows, (s + 1) * seq_rows)
        box = Work()

        def sums():
            box.y = jnp.concatenate([jnp.concatenate(y_parts[q], axis=0) for q in range(n_quads)], axis=-1)
            box.sums = _group_sum(jnp.concatenate([box.y, r[rs] * k[rs] * rk_ref[...]], axis=0), ones2)

        def variance():
            box.yc = box.y - box.sums[0:seq_rows] * (1.0 / HEAD_DIM)
            box.var = _group_sum(box.yc * box.yc, ones2) * (1.0 / HEAD_DIM)

        def write():
            yn = box.yc * lax.rsqrt(box.var + LNX_EPS) * lg_ref[...] + lb_ref[...]
            o_ref[s] = ((yn + box.sums[seq_rows:2 * seq_rows] * v[rs]) * gate[rs]).astype(o_ref.dtype)

        return [sums, variance, write]

    works = [make_works(s) for s in range(n_seq)]
    pres = [pre_stages(ws) for ws in works]
    y_all = [[[] for _ in range(n_quads)] for _ in range(n_seq)]
    for group in (pres, [chain_levels(s, works[s], y_all[s]) for s in range(n_seq)],
                  [post_stages(s, y_all[s]) for s in range(n_seq)]):
        for stages in zip(*group):
            for stage in stages:
                stage()


def _rwkv(pr, mu, w0, w_lora, a0, k_k, k_a, r_k, lnx_g, lnx_b, width):
    B, S, n_rwkv = pr.shape
    rows = RWKV_ROWS
    n_seq = RWKV_SEQS
    C = RWKV_CHUNK
    rr = np.arange(n_seq * rows)[:, None]
    cc = np.arange(n_seq * rows)[None, :]
    tri = ((rr // C == cc // C) & (cc <= rr)).astype(np.float32)
    tri3 = jnp.asarray(np.concatenate([tri, tri, tri], axis=1), BF16)
    gg = np.arange(width)
    ones = (gg[:, None] // HEAD_DIM == gg[None, :] // HEAD_DIM).astype(np.float32)
    ones2 = jnp.asarray(np.concatenate([ones, ones], axis=0), BF16)
    fix = lambda shape: pl.BlockSpec(shape, lambda b, i: (0, 0))
    vec = lambda n: fix((1, n))
    return pl.pallas_call(
        functools.partial(_rwkv_kernel, width=width),
        grid=(B // n_seq, S // rows),
        in_specs=[
            pl.BlockSpec((n_seq, rows, n_rwkv), lambda b, i: (b, i, 0)),
            vec(n_rwkv), vec(width), fix(w_lora.shape), vec(width),
            vec(width), vec(width), vec(width), vec(width), vec(width),
            fix(tri3.shape), fix(ones2.shape),
        ],
        out_specs=pl.BlockSpec((n_seq, rows, width), lambda b, i: (b, i, 0)),
        out_shape=jax.ShapeDtypeStruct((B, S, width), BF16),
        scratch_shapes=[
            pltpu.VMEM((n_seq * (width // QUAD), QUAD, QUAD), F32),
            pltpu.VMEM((n_seq, n_rwkv), F32),
        ],
        compiler_params=_cparams(("arbitrary", "arbitrary")),
        name="rwkv7",
    )(pr, mu, w0, w_lora, a0, k_k, k_a, r_k, lnx_g, lnx_b, tri3, ones2)


def _t5_bucket_np(dist):
    n = np.maximum(dist, 0)
    max_exact = NUM_BUCKETS // 2
    nf = np.maximum(n, 1).astype(np.float32)
    large = max_exact + (np.log(nf / max_exact) / math.log(MAX_DISTANCE / max_exact)
                         * (NUM_BUCKETS - max_exact)).astype(np.int32)
    large = np.minimum(large, NUM_BUCKETS - 1)
    return np.where(n < max_exact, n, large).astype(np.int32)


def _near_bucket_tiles(tile):
    kpos = np.arange(tile)[:, None]
    qpos = np.arange(tile)[None, :]
    out = []
    for delta in (0, 1):
        dist = delta * tile + qpos - kpos
        out.append(np.where(dist >= 0, _t5_bucket_np(dist), -1))
    return np.stack(out).astype(np.int32)


def _attn_kernel(relb_ref, bucket_ref, lq1_ref, lk1_ref, lq2_ref, lk2_ref, sg_ref,
                 q_ref, k_ref, vt_ref, o_ref, bias_sc, q2_sc, m_sc, acc_sc, st_sc, *, lam_init):
    tq = q_ref.shape[0]
    n_heads = vt_ref.shape[0]
    heads = range(n_heads)
    qi = pl.program_id(1)

    @pl.when((pl.program_id(0) == 0) & (qi == 0))
    def _():
        for h in heads:
            for d in range(2):
                bk = bucket_ref[d]
                tile = jnp.full(bk.shape, NEG_BIG, F32)
                for j in range(NUM_BUCKETS):
                    tile = jnp.where(bk == j, relb_ref[h, j] * LOG2E, tile)
                r0 = (1 - d) * tq
                bias_sc[h, r0:r0 + tq, 0:tq] = tile
                bias_sc[h, r0:r0 + tq, tq:2 * tq] = tile

    lo = _lane_lo((tq, LANES))
    for h in heads:
        q = q_ref[:, h * LANES:(h + 1) * LANES]
        zero = jnp.zeros_like(q)
        q2_sc[h, 0:tq, :] = jnp.where(lo, q, zero)
        q2_sc[h, tq:2 * tq, :] = jnp.where(lo, zero, q)
    m_sc[...] = jnp.full(m_sc.shape, NEG_BIG, F32)
    acc_sc[...] = jnp.zeros_like(acc_sc)

    def step(k0, tk, band):
        for h in heads:
            st_sc[h, 0:tk, :] = _dot(k_ref[pl.ds(k0, tk), h * LANES:(h + 1) * LANES], q2_sc[h], NT)
        pvs, alphas = [], []
        for h in heads:
            m_old = m_sc[h]
            if band is None:
                st = st_sc[h, 0:tk, :]
                far = relb_ref[h, NUM_BUCKETS - 1] * LOG2E
                m_new = jnp.maximum(m_old, jnp.max(st, axis=0, keepdims=True) + far)
                p = jnp.exp2(st - (m_new - far))
            else:
                st = st_sc[h, 0:tk, :] + bias_sc[h, band:band + tk, :]
                m_new = jnp.maximum(m_old, jnp.max(st, axis=0, keepdims=True))
                p = jnp.exp2(st - m_new)
            m_sc[h] = m_new
            pvs.append(_dot(vt_ref[h, :, pl.ds(k0, tk)], p.astype(BF16)))
            alphas.append(jnp.exp2(m_old - m_new))
        for h in heads:
            acc_sc[h] = alphas[h] * acc_sc[h] + pvs[h]

    n_far = jnp.maximum(qi - 1, 0)
    group = FAR_GROUP * tq

    def far_body(i, carry):
        step(pl.multiple_of(i * group, group), group, None)
        return carry

    lax.fori_loop(0, n_far // FAR_GROUP, far_body, 0)
    done = (n_far // FAR_GROUP) * FAR_GROUP
    left = n_far - done

    @pl.when(left >= 2)
    def _():
        step(pl.multiple_of(done * tq, tq), 2 * tq, None)

    @pl.when(left % 2 == 1)
    def _():
        step(pl.multiple_of((n_far - 1) * tq, tq), tq, None)

    @pl.when(qi >= 1)
    def _():
        step(pl.multiple_of((qi - 1) * tq, tq), 2 * tq, 0)

    @pl.when(qi == 0)
    def _():
        step(0, tq, tq)

    lam = (jnp.exp(jnp.sum(lq1_ref[...] * lk1_ref[...], axis=-1, keepdims=True))
           - jnp.exp(jnp.sum(lq2_ref[...] * lk2_ref[...], axis=-1, keepdims=True)) + lam_init)
    for h in heads:
        acc = acc_sc[h]
        l = acc[ACC_L:ACC_L + 1, :]
        o1 = acc[0:LANES, 0:tq] / l[:, 0:tq]
        o2 = acc[0:LANES, tq:2 * tq] / l[:, tq:2 * tq]
        ot = o1 - lam * o2
        ms = jnp.mean(ot * ot, axis=0, keepdims=True)
        y = ot * lax.rsqrt(ms + NORM_EPS) * sg_ref[...] * (1.0 - lam_init)
        o_ref[:, h * LANES:(h + 1) * LANES] = y.T.astype(o_ref.dtype)


def _diff_attn(q, k, vt, relb_t, lq1, lk1, lq2, lk2, sg_col, lam_init):
    B, S, width = q.shape
    H = width // LANES
    t = ATT_TILE
    buckets = jnp.asarray(_near_bucket_tiles(t))
    vec = pl.BlockSpec((1, HEAD_DIM), lambda b, i: (0, 0))
    return pl.pallas_call(
        functools.partial(_attn_kernel, lam_init=lam_init),
        grid=(B, S // t),
        in_specs=[
            pl.BlockSpec(memory_space=pltpu.SMEM),
            pl.BlockSpec((2, t, t), lambda b, i: (0, 0, 0)),
            vec, vec, vec, vec,
            pl.BlockSpec((LANES, 1), lambda b, i: (0, 0)),
            pl.BlockSpec((None, t, width), lambda b, i: (b, i, 0)),
            pl.BlockSpec((None, S, width), lambda b, i: (b, 0, 0)),
            pl.BlockSpec((None, H, ACC_ROWS, S), lambda b, i: (b, 0, 0, 0)),
        ],
        out_specs=pl.BlockSpec((None, t, width), lambda b, i: (b, i, 0)),
        out_shape=jax.ShapeDtypeStruct((B, S, width), BF16),
        scratch_shapes=[
            pltpu.VMEM((H, 2 * t, 2 * t), F32),
            pltpu.VMEM((H, 2 * t, LANES), BF16),
            pltpu.VMEM((H, 1, 2 * t), F32),
            pltpu.VMEM((H, ACC_ROWS, 2 * t), F32),
            pltpu.VMEM((H, FAR_GROUP * t, 2 * t), F32),
        ],
        compiler_params=_cparams(("arbitrary", "arbitrary")),
        name="diff_attn",
    )(relb_t, buckets, lq1, lk1, lq2, lk2, sg_col, q, k, vt)


def _run_rows(count):
    return jnp.ceil(count * (1.0 / RUN_CHUNK)) * RUN_CHUNK


def _outproj_kernel(x_ref, ya_ref, yb_ref, wa_ref, wb_ref, g_ref, rw_ref, rb_ref,
                    x1_ref, h2_ref, gd_ref, cnt_ref):
    x1 = x_ref[...] + _dot(ya_ref[...], wa_ref[...]) + _dot(yb_ref[...], wb_ref[...])
    x1_ref[...] = x1
    ms = jnp.mean(x1 * x1, axis=-1, keepdims=True)
    h2 = x1 * lax.rsqrt(ms + NORM_EPS) * g_ref[...]
    h2_ref[...] = h2.astype(h2_ref.dtype)
    logits = _mm(h2, rw_ref[...], mode="x3") + rb_ref[...]
    work = logits.T
    expert = lax.broadcasted_iota(jnp.int32, work.shape, 0).astype(F32)
    picks = []
    for _ in range(TOP_K):
        m = jnp.max(work, axis=0, keepdims=True)
        idx = jnp.min(jnp.where(work == m, expert, float(LANES)), axis=0, keepdims=True)
        hit = expert == idx
        picks.append((m, hit))
        work = jnp.where(hit, NEG_BIG, work)
    m0 = picks[0][0]
    es = [jnp.exp(m - m0) for m, _ in picks]
    denom = es[0] + es[1] + es[2] + es[3]
    gd_t = jnp.zeros_like(work)
    for e, (_, hit) in zip(es, picks):
        gd_t = jnp.where(hit, e / denom, gd_t)
    gd = gd_t.T
    gd_ref[...] = gd

    @pl.when(pl.program_id(0) == 0)
    def _():
        cnt_ref[...] = jnp.zeros_like(cnt_ref)

    cnt_ref[0:1, :] = cnt_ref[0:1, :] + _run_rows(jnp.sum(jnp.where(gd > 0.0, 1.0, 0.0), axis=0, keepdims=True))


def _outproj(x2d, ya, yb, wa, wb, g2, rw, rb):
    T, D = x2d.shape
    half = ya.shape[1]
    tm = MOVE_ROWS
    row = lambda i: (i, 0)
    fix = lambda i: (0, 0)
    return pl.pallas_call(
        _outproj_kernel,
        grid=(T // tm,),
        in_specs=[
            pl.BlockSpec((tm, D), row),
            pl.BlockSpec((tm, half), row),
            pl.BlockSpec((tm, half), row),
            pl.BlockSpec((half, D), fix),
            pl.BlockSpec((half, D), fix),
            pl.BlockSpec((1, D), fix),
            pl.BlockSpec((D, LANES), fix),
            pl.BlockSpec((1, LANES), fix),
        ],
        out_specs=[
            pl.BlockSpec((tm, D), row),
            pl.BlockSpec((tm, D), row),
            pl.BlockSpec((tm, LANES), row),
            pl.BlockSpec((8, LANES), fix),
        ],
        out_shape=[
            jax.ShapeDtypeStruct((T, D), F32),
            jax.ShapeDtypeStruct((T, D), BF16),
            jax.ShapeDtypeStruct((T, LANES), F32),
            jax.ShapeDtypeStruct((8, LANES), F32),
        ],
        compiler_params=_cparams(("arbitrary",)),
        name="outproj_router",
    )(x2d, ya, yb, wa, wb, g2, rw, rb)


def _padded_rows(cnt, rb):
    return jnp.ceil(cnt * (1.0 / rb)) * rb


def _route_kernel(gd_ref, tot_ref, lp_ref, gate_ref, tab_ref, cnt_ref, carry_sc, pstart_sc, *, rb):
    i = pl.program_id(0)
    tb = gd_ref.shape[0]
    gd = gd_ref[...]
    sel = gd > 0.0
    self = jnp.where(sel, 1.0, 0.0)
    run_pad = _run_rows(jnp.sum(self, axis=0, keepdims=True))
    r2 = lax.broadcasted_iota(jnp.int32, (LANES, LANES), 0)
    c2 = lax.broadcasted_iota(jnp.int32, (LANES, LANES), 1)

    @pl.when(i == 0)
    def _():
        cnt = tot_ref[0:1, :]
        upper = jnp.where(r2 < c2, 1.0, 0.0)
        pstart = _dot(jnp.broadcast_to(_padded_rows(cnt, rb), (8, LANES)), upper, prec=HP)[0:1, :]
        pstart_sc[...] = pstart
        cnt_ref[0:1, :] = cnt
        cnt_ref[1:2, :] = pstart
        cnt_ref[2:8, :] = jnp.zeros((6, LANES), F32)
        carry_sc[...] = jnp.zeros_like(carry_sc)

    rr = lax.broadcasted_iota(jnp.int32, (tb, tb), 0)
    cc = lax.broadcasted_iota(jnp.int32, (tb, tb), 1)
    lower = jnp.where(cc < rr, 1.0, 0.0).astype(BF16)
    sel_b = self.astype(BF16)
    prefix = _dot(lower, sel_b)
    upper_b = jnp.where(r2 < c2, 1.0, 0.0).astype(BF16)
    off = _dot(jnp.broadcast_to(run_pad, (8, LANES)).astype(BF16), upper_b)[0:1, :]
    lp = off + prefix
    upper_incl = jnp.where(r2 <= c2, 1.0, 0.0).astype(BF16)
    slot = _dot(sel_b, upper_incl)
    lane = lax.broadcasted_iota(jnp.int32, (tb, LANES), 1)
    p_out = jnp.zeros((tb, LANES), F32)
    g_out = jnp.zeros((tb, LANES), F32)
    for kk in range(TOP_K):
        mk = sel & (slot == float(kk + 1))
        p_k = jnp.sum(jnp.where(mk, lp, 0.0), axis=-1, keepdims=True)
        g_k = jnp.sum(jnp.where(mk, gd, 0.0), axis=-1, keepdims=True)
        p_out = jnp.where(lane == kk, p_k, p_out)
        g_out = jnp.where(lane == kk, g_k, g_out)
    lp_ref[...] = p_out
    gate_ref[...] = g_out
    def as_column(v):
        return jnp.sum(jnp.where(r2 == c2, jnp.broadcast_to(v, (LANES, LANES)), 0.0), axis=1, keepdims=True)

    first = as_column(off * (1.0 / RUN_CHUNK))
    count = as_column(run_pad * (1.0 / RUN_CHUNK))
    start = as_column(pstart_sc[...] + carry_sc[...])
    chunk = lax.broadcasted_iota(jnp.int32, (LANES, CHUNK_LANES), 1).astype(F32)
    mine = (chunk >= first) & (chunk < first + count)
    dst = jnp.sum(jnp.where(mine, start + (chunk - first) * RUN_CHUNK, 0.0), axis=0, keepdims=True)
    n_chunks = jnp.sum(count, axis=0, keepdims=True)
    tab_ref[0:1, :] = dst.astype(jnp.int32)
    tab_ref[1:2, :] = jnp.broadcast_to(n_chunks, (1, CHUNK_LANES)).astype(jnp.int32)
    tab_ref[2:8, :] = jnp.zeros((6, CHUNK_LANES), jnp.int32)
    carry_sc[...] = carry_sc[...] + run_pad


def _route(gd, totals, rb):
    T = gd.shape[0]
    tb = MOVE_ROWS
    blk = lambda i: (i, 0)
    fix = lambda i: (0, 0)
    return pl.pallas_call(
        functools.partial(_route_kernel, rb=rb),
        grid=(T // tb,),
        in_specs=[pl.BlockSpec((tb, LANES), blk), pl.BlockSpec((8, LANES), fix)],
        out_specs=[
            pl.BlockSpec((tb, LANES), blk),
            pl.BlockSpec((tb, LANES), blk),
            pl.BlockSpec((8, CHUNK_LANES), blk),
            pl.BlockSpec((8, LANES), fix),
        ],
        out_shape=[
            jax.ShapeDtypeStruct((T, LANES), F32),
            jax.ShapeDtypeStruct((T, LANES), F32),
            jax.ShapeDtypeStruct((T // tb * 8, CHUNK_LANES), jnp.int32),
            jax.ShapeDtypeStruct((8, LANES), F32),
        ],
        scratch_shapes=[pltpu.VMEM((1, LANES), F32), pltpu.VMEM((1, LANES), F32)],
        compiler_params=_cparams(("arbitrary",)),
        name="route_positions",
    )(gd, totals)


def _chunk_copy(src_ref, src_row, dst_ref, dst_row, sem):
    return pltpu.make_async_copy(src_ref.at[pl.ds(src_row, RUN_CHUNK), :],
                                 dst_ref.at[pl.ds(dst_row, RUN_CHUNK), :], sem)


def _start_runs(b, nc_ref, dst_ref, copy_chunk):
    def issue(c, carry):
        copy_chunk(pl.multiple_of(c * RUN_CHUNK, RUN_CHUNK),
                   pl.multiple_of(dst_ref[b * LOCAL_CHUNKS + c], RUN_CHUNK)).start()
        return carry

    lax.fori_loop(0, nc_ref[b], issue, 0)


def _wait_runs(b, nc_ref, wait_rows):
    total = nc_ref[b]
    p = 1
    while p <= LOCAL_CHUNKS:
        @pl.when((total & p) != 0)
        def _(p=p):
            wait_rows(p * RUN_CHUNK).wait()
        p *= 2


def _rows_copy(src_ref, dst_ref, n_rows, sem):
    return pltpu.make_async_copy(src_ref.at[pl.ds(0, n_rows), :], dst_ref.at[pl.ds(0, n_rows), :], sem)


def _local_onehot(lp, lb):
    j = lax.broadcasted_iota(jnp.int32, (lp.shape[0], lb), 1).astype(F32)
    hit = lp[:, 0:1] == j
    for kk in range(1, TOP_K):
        hit = hit | (lp[:, kk:kk + 1] == j)
    return hit


def _pack_bf16_pairs(x):
    n = x.shape[1] // 2
    hi = pltpu.bitcast(x[:, :n], jnp.uint32)
    lo = pltpu.bitcast(x[:, n:], jnp.uint32)
    return (hi & jnp.uint32(0xFFFF0000)) | (lo >> jnp.uint32(16))


def _unpack_bf16_pairs(p):
    hi = pltpu.bitcast(p & jnp.uint32(0xFFFF0000), F32)
    lo = pltpu.bitcast(p << jnp.uint32(16), F32)
    return jnp.concatenate([hi, lo], axis=1).astype(BF16)


def _scatter_kernel(nc_ref, dst_ref, t0_ref, tc_ref, lp_ref, h_ref, xs_ref, buf, zero_buf, sems):
    b = pl.program_id(0)
    slot = b % 2
    pt = jnp.where(_local_onehot(lp_ref[...], buf.shape[1]), 1.0, 0.0).astype(BF16)
    buf[slot] = _pack_bf16_pairs(_dot(pt, h_ref[...], TN))
    _start_runs(b, nc_ref, dst_ref,
                lambda s, d: _chunk_copy(buf.at[slot], s, xs_ref, d, sems.at[slot]))

    @pl.when(b >= 1)
    def _():
        _wait_runs(b - 1, nc_ref, lambda n: _rows_copy(buf.at[1 - slot], xs_ref, n, sems.at[1 - slot]))

    @pl.when(b == pl.num_programs(0) - 1)
    def _():
        _wait_runs(b, nc_ref, lambda n: _rows_copy(buf.at[slot], xs_ref, n, sems.at[slot]))
        zero_buf[...] = jnp.zeros_like(zero_buf)
        total = 0
        for e in range(N_EXPERTS + 1):
            t0 = t0_ref[e]

            def issue(c, carry, t0=t0):
                _chunk_copy(zero_buf, 0, xs_ref, pl.multiple_of(t0 + c * RUN_CHUNK, RUN_CHUNK), sems.at[0]).start()
                return carry

            lax.fori_loop(0, tc_ref[e], issue, 0)
            total = total + tc_ref[e]

        def drain(c, carry):
            _chunk_copy(zero_buf, 0, xs_ref, 0, sems.at[0]).wait()
            return carry

        lax.fori_loop(0, total, drain, 0)


def _scatter_rows(tabs, tails, lp, h2, n_rows):
    T, D = h2.shape
    tb = MOVE_ROWS
    return pl.pallas_call(
        _scatter_kernel,
        grid_spec=pltpu.PrefetchScalarGridSpec(
            num_scalar_prefetch=4,
            grid=(T // tb,),
            in_specs=[
                pl.BlockSpec((tb, LANES), lambda i, *_: (i, 0)),
                pl.BlockSpec((tb, D), lambda i, *_: (i, 0)),
            ],
            out_specs=pl.BlockSpec(memory_space=pl.ANY),
            scratch_shapes=[pltpu.VMEM((2, LOCAL_ROWS, D // 2), jnp.uint32),
                            pltpu.VMEM((RUN_CHUNK, D // 2), jnp.uint32), pltpu.SemaphoreType.DMA((2,))],
        ),
        out_shape=jax.ShapeDtypeStruct((n_rows, D // 2), jnp.uint32),
        compiler_params=_cparams(("arbitrary",)),
        name="scatter_rows",
    )(*tabs, *tails, lp, h2)


def _moe_kernel(be_ref, na_ref, slot_ref, nxt_ref, xs_ref, w1_hbm, b1_ref, w2_hbm, b2_ref, ys_ref,
                w1f, w2f, w1b, w2b, sems):
    j = pl.program_id(0)
    e = be_ref[j]
    slot = slot_ref[j]

    def fetch(expert, sl):
        return (pltpu.make_async_copy(w1_hbm.at[expert], w1f.at[sl], sems.at[0, sl]),
                pltpu.make_async_copy(w2_hbm.at[expert], w2f.at[sl], sems.at[1, sl]))

    @pl.when(j == 0)
    def _():
        for cp in fetch(e, slot):
            cp.start()

    @pl.when((j == 0) | (e != be_ref[jnp.maximum(j - 1, 0)]))
    def _():
        for cp in fetch(e, slot):
            cp.wait()
        w1b[...] = w1f[slot].astype(BF16)
        w2b[...] = w2f[slot].astype(BF16)

        @pl.when(nxt_ref[j] >= 0)
        def _():
            for cp in fetch(nxt_ref[j], 1 - slot):
                cp.start()

    @pl.when(j < na_ref[0])
    def _():
        hid = _dot(_unpack_bf16_pairs(xs_ref[...]), w1b[...]) + b1_ref[...]
        half = hid.shape[1] // 2
        x_glu = jnp.minimum(hid[:, :half], SWIGLU_LIMIT)
        x_lin = jnp.clip(hid[:, half:], -SWIGLU_LIMIT, SWIGLU_LIMIT)
        act = x_glu * _sigmoid(SWIGLU_ALPHA * x_glu) * (x_lin + 1.0)
        ys_ref[...] = _dot(act.astype(BF16), w2b[...]) + b2_ref[...]

    @pl.when(j >= na_ref[0])
    def _():
        ys_ref[...] = jnp.zeros_like(ys_ref)


def _expert_runs(blk_expert):
    n = blk_expert.shape[0]
    idx = jnp.arange(n)
    starts = jnp.concatenate([jnp.ones((1,), bool), blk_expert[1:] != blk_expert[:-1]])
    slot = (jnp.cumsum(starts) - 1) % 2
    next_start = lax.cummin(jnp.where(starts, idx, n)[::-1])[::-1]
    after = jnp.concatenate([next_start[1:], jnp.full((1,), n)])
    nxt = jnp.where(after < n, blk_expert[jnp.minimum(after, n - 1)], -1)
    return slot.astype(jnp.int32), nxt.astype(jnp.int32)


def _moe(blk_expert, n_active, xs, w1, b1, w2, b2):
    R = xs.shape[0]
    E, D, D2 = w1.shape
    rb = MOE_ROWS
    nb = R // rb
    slot, nxt = _expert_runs(blk_expert)
    blk = lambda j, be, na, *_: (jnp.minimum(j, na[0] - 1), 0)
    return pl.pallas_call(
        _moe_kernel,
        grid_spec=pltpu.PrefetchScalarGridSpec(
            num_scalar_prefetch=4,
            grid=(nb,),
            in_specs=[
                pl.BlockSpec((rb, D // 2), blk),
                pl.BlockSpec(memory_space=pl.ANY),
                pl.BlockSpec((None, 1, D2), lambda j, be, *_: (be[j], 0, 0)),
                pl.BlockSpec(memory_space=pl.ANY),
                pl.BlockSpec((None, 1, D), lambda j, be, *_: (be[j], 0, 0)),
            ],
            out_specs=pl.BlockSpec((rb, D), lambda j, *_: (j, 0)),
            scratch_shapes=[pltpu.VMEM((2, D, D2), F32), pltpu.VMEM((2, D2 // 2, D), F32),
                            pltpu.VMEM((D, D2), BF16), pltpu.VMEM((D2 // 2, D), BF16),
                            pltpu.SemaphoreType.DMA((2, 2))],
        ),
        out_shape=jax.ShapeDtypeStruct((R, D), F32),
        compiler_params=_cparams(("arbitrary",)),
        name="moe_experts",
    )(blk_expert, n_active, slot, nxt, xs, w1, b1, w2, b2)


def _combine_kernel(nc_ref, dst_ref, lp_ref, gate_ref, x1_ref, ys_ref, o_ref, buf, sems):
    b = pl.program_id(0)
    slot = b % 2

    def fetch(blk, sl):
        _start_runs(blk, nc_ref, dst_ref,
                    lambda s, d: _chunk_copy(ys_ref, d, buf.at[sl], s, sems.at[sl]))

    @pl.when(b == 0)
    def _():
        buf[...] = jnp.zeros_like(buf)
        fetch(b, slot)

    @pl.when(b + 1 < pl.num_programs(0))
    def _():
        fetch(b + 1, 1 - slot)

    _wait_runs(b, nc_ref, lambda n: _rows_copy(ys_ref, buf.at[slot], n, sems.at[slot]))
    lp = lp_ref[...]
    g = gate_ref[...]
    _, lb, d = buf.shape
    j = lax.broadcasted_iota(jnp.int32, (lp.shape[0], lb), 1).astype(F32)
    gt = jnp.zeros(j.shape, F32)
    for kk in range(TOP_K):
        gt = jnp.where(lp[:, kk:kk + 1] == j, g[:, kk:kk + 1], gt)
    g_hi, g_lo = _split2(gt)
    ones = jnp.ones((lp.shape[0], LANES), BF16)
    g_row = _dot(g_hi, ones, TN) + _dot(g_lo, ones, TN)
    rows = (buf[slot] * jnp.concatenate([g_row] * (d // LANES), axis=1)).astype(BF16)
    p01 = jnp.where(gt != 0.0, 1.0, 0.0).astype(BF16)
    o_ref[...] = x1_ref[...] + _dot(p01, rows)


def _combine(tabs, lp, gate, x1, ys):
    T, D = x1.shape
    tb = MOVE_ROWS
    return pl.pallas_call(
        _combine_kernel,
        grid_spec=pltpu.PrefetchScalarGridSpec(
            num_scalar_prefetch=2,
            grid=(T // tb,),
            in_specs=[
                pl.BlockSpec((tb, LANES), lambda i, *_: (i, 0)),
                pl.BlockSpec((tb, LANES), lambda i, *_: (i, 0)),
                pl.BlockSpec((tb, D), lambda i, *_: (i, 0)),
                pl.BlockSpec(memory_space=pl.ANY),
            ],
            out_specs=pl.BlockSpec((tb, D), lambda i, *_: (i, 0)),
            scratch_shapes=[pltpu.VMEM((2, LOCAL_ROWS, D), F32), pltpu.SemaphoreType.DMA((2,))],
        ),
        out_shape=jax.ShapeDtypeStruct((T, D), F32),
        compiler_params=_cparams(("arbitrary",)),
        name="combine_rows",
    )(*tabs, lp, gate, x1, ys)


def _pad_cols(a, n):
    return jnp.pad(a, ((0, 0), (0, n - a.shape[1])))


def _pad_rows(a, n):
    return jnp.pad(a, ((0, n - a.shape[0]), (0, 0)))


def _lora_weight(w2, a2, g2, width):
    z = jnp.zeros((LANES, width), F32)
    return jnp.concatenate([
        jnp.concatenate([_pad_rows(w2, LANES), z, z], axis=1),
        jnp.concatenate([z, _pad_rows(a2, LANES), z], axis=1),
        jnp.concatenate([z, z, _pad_rows(g2, LANES)], axis=1),
    ], axis=0).astype(BF16)


def _split_rwkv_cols(a, width):
    o = 3 * width
    return jnp.concatenate([
        a[:, :o],
        _pad_cols(a[:, o:o + DECAY_LORA], LANES),
        _pad_cols(a[:, o + DECAY_LORA:o + DECAY_LORA + AAA_LORA], LANES),
        _pad_cols(a[:, o + DECAY_LORA + AAA_LORA:o + DECAY_LORA + AAA_LORA + GATE_LORA], LANES),
    ], axis=1)


def kernel(x, norm1_g, w_in, mu_shift, w0, w2, a0, a2, g2, k_k, k_a, r_k, lnx_g, lnx_b, qn_g, kn_g,
           lam_q1, lam_k1, lam_q2, lam_k2, subln_g, rel_bias, w_out, norm2_g, router_w, router_b,
           exp_w1, exp_b1, exp_w2, exp_b2):
    B, S, D = x.shape
    T = B * S
    depth = norm1_g.shape[0]
    width = w0.shape[1]
    n_rwkv_cols = 3 * width + DECAY_LORA + AAA_LORA + GATE_LORA
    n_rwkv_pad = 3 * width + 3 * LANES
    n_groups = width // HEAD_DIM
    H = width // LANES
    relb_t = rel_bias.T
    row = lambda a: a.reshape(1, -1)

    for layer in range(depth):
        lam_init = 0.8 - 0.6 * math.exp(-0.3 * layer)
        w_l = w_in[layer]
        w_all = jnp.concatenate([_split_rwkv_cols(w_l[:, :n_rwkv_cols], width), w_l[:, n_rwkv_cols:]],
                                axis=1).astype(BF16)
        mu = _split_rwkv_cols(row(mu_shift[layer]), width)
        qg = row(jnp.tile(qn_g[layer], n_groups))
        kg = row(jnp.tile(kn_g[layer], n_groups))

        pr, q, k, vt = _inproj(x.reshape(T, D), row(norm1_g[layer]), w_all, qg, kg, n_rwkv_pad, width, S)

        y_a = _rwkv(pr.reshape(B, S, n_rwkv_pad), mu, row(w0[layer]),
                    _lora_weight(w2[layer], a2[layer], g2[layer], width), row(a0[layer]),
                    row(k_k[layer]), row(k_a[layer]), row(r_k[layer]), row(lnx_g[layer]),
                    row(lnx_b[layer]), width)

        y_b = _diff_attn(q.reshape(B, S, width), k.reshape(B, S, width), vt, relb_t,
                         row(lam_q1[layer]), row(lam_k1[layer]), row(lam_q2[layer]), row(lam_k2[layer]),
                         subln_g[layer].reshape(-1, 1), lam_init)

        wo = w_out[layer].astype(BF16)
        rw = _pad_cols(router_w[layer], LANES)
        rb = jnp.concatenate([router_b[layer], jnp.full((LANES - N_EXPERTS,), NEG_BIG, F32)]).reshape(1, -1)
        x1, h2, gd, totals = _outproj(x.reshape(T, D), y_a.reshape(T, width), y_b.reshape(T, width),
                                      wo[:width], wo[width:], row(norm2_g[layer]), rw, rb)

        lp, gate, tab, cnt = _route(gd, totals, MOE_ROWS)
        tab = tab.reshape(-1, 8, CHUNK_LANES)
        tabs = (tab[:, 1, 0], tab[:, 0, :LOCAL_CHUNKS].reshape(-1))
        run_end = cnt[1, :N_EXPERTS] + cnt[0, :N_EXPERTS]
        pad_end = cnt[1, :N_EXPERTS] + _padded_rows(cnt[0, :N_EXPERTS], MOE_ROWS)
        run_pad_max = (T // MOVE_ROWS) * N_EXPERTS * (RUN_CHUNK - 1)
        n_blocks = -(-(T * TOP_K + run_pad_max + N_EXPERTS * (MOE_ROWS - 1)) // MOE_ROWS)
        gap_start = jnp.concatenate([run_end, pad_end[-1:]])
        gap_end = jnp.concatenate([pad_end, jnp.full((1,), n_blocks * MOE_ROWS, F32)])
        tails = (gap_start.astype(jnp.int32), ((gap_end - gap_start) / RUN_CHUNK).astype(jnp.int32))
        blk_start = (jnp.arange(n_blocks) * MOE_ROWS).astype(F32)
        blk_expert = jnp.minimum(jnp.sum(pad_end[None, :] <= blk_start[:, None], axis=1), N_EXPERTS - 1)
        n_active = (pad_end[N_EXPERTS - 1] / MOE_ROWS).astype(jnp.int32).reshape(1)

        xs = _scatter_rows(tabs, tails, lp, h2, n_blocks * MOE_ROWS)
        ys = _moe(blk_expert.astype(jnp.int32), n_active, xs,
                  exp_w1[layer], exp_b1[layer][:, None, :], exp_w2[layer], exp_b2[layer][:, None, :])
        x = _combine(tabs, lp, gate, x1, ys).reshape(B, S, D)
    return x
```

```python
import functools
import math

import numpy as np
import jax
import jax.numpy as jnp
from jax import lax
from jax.experimental import pallas as pl
from jax.experimental.pallas import tpu as pltpu

F32 = jnp.float32
BF16 = jnp.bfloat16
HP = lax.Precision.HIGHEST

HEAD_DIM = 64
DECAY_LORA = 32
AAA_LORA = 32
GATE_LORA = 96
NUM_BUCKETS = 32
MAX_DISTANCE = 128
N_EXPERTS = 32
TOP_K = 4
SWIGLU_LIMIT = 7.0
SWIGLU_ALPHA = 1.702
NORM_EPS = 1e-5
LNX_EPS = 64e-5

LANES = 128
VMEM_LIMIT_BYTES = 56 * 1024 * 1024

NEG_BIG = -1e30
LOG2E = math.log2(math.e)
ACC_L = LANES
ACC_ROWS = LANES + 8

PROJ_ROWS = 512
RWKV_CHUNK = 64
RWKV_ROWS = 256
RWKV_SEQS = 2
ATT_TILE = 256
FAR_GROUP = 4
MOE_ROWS = 512
MOVE_ROWS = 256
RUN_CHUNK = 8
LOCAL_ROWS = MOVE_ROWS * 4 + 32 * RUN_CHUNK
LOCAL_CHUNKS = LOCAL_ROWS // RUN_CHUNK
CHUNK_LANES = 256
BIG_COPY = 4
BIG_LIST = 64
SMALL_LIST = 128

NN = (((1,), (0,)), ((), ()))
NT = (((1,), (1,)), ((), ()))
TN = (((0,), (0,)), ((), ()))


def _dot(a, b, dims=NN, prec=None):
    return lax.dot_general(a, b, dims, precision=prec, preferred_element_type=F32)


def _cparams(sem):
    return pltpu.CompilerParams(dimension_semantics=sem, vmem_limit_bytes=VMEM_LIMIT_BYTES)


def _sigmoid(x):
    return 1.0 / (1.0 + jnp.exp(-x))


def _lane_lo(shape):
    lane = lax.broadcasted_iota(jnp.int32, shape, len(shape) - 1)
    return (lane % LANES) < HEAD_DIM


def _head_rms_normalize(x, gain, scale):
    outs = []
    for t in range(x.shape[1] // LANES):
        xt = x[:, t * LANES:(t + 1) * LANES]
        lo = _lane_lo(xt.shape)
        x2 = xt * xt
        s0 = jnp.sum(jnp.where(lo, x2, 0.0), axis=-1, keepdims=True)
        s1 = jnp.sum(jnp.where(lo, 0.0, x2), axis=-1, keepdims=True)
        inv = lax.rsqrt(jnp.where(lo, s0, s1) * (1.0 / HEAD_DIM) + NORM_EPS)
        outs.append(xt * inv)
    return jnp.concatenate(outs, axis=-1) * gain * scale


def _inproj_kernel(x_ref, g_ref, w_ref, qg_ref, kg_ref, pr_ref, q_ref, k_ref, vt_ref, *, n_rwkv, width):
    x = x_ref[...]
    ms = jnp.mean(x * x, axis=-1, keepdims=True)
    h = (x * lax.rsqrt(ms + NORM_EPS) * g_ref[...]).astype(BF16)
    step = 384
    for c0 in range(0, n_rwkv, step):
        pr_ref[:, c0:c0 + step] = _dot(h, w_ref[:, c0:c0 + step])
    q = _dot(h, w_ref[:, n_rwkv:n_rwkv + width])
    q_ref[...] = _head_rms_normalize(q, qg_ref[...], HEAD_DIM ** -0.5 * LOG2E).astype(BF16)
    k = _dot(h, w_ref[:, n_rwkv + width:n_rwkv + 2 * width])
    k_ref[...] = _head_rms_normalize(k, kg_ref[...], 1.0).astype(BF16)
    v = _dot(h, w_ref[:, n_rwkv + 2 * width:n_rwkv + 3 * width])
    for hd in range(width // LANES):
        vt_ref[hd, 0:LANES, :] = v[:, hd * LANES:(hd + 1) * LANES].T.astype(BF16)
        vt_ref[hd, LANES:ACC_ROWS, :] = jnp.ones((ACC_ROWS - LANES, v.shape[0]), BF16)


def _inproj(x2d, g, w_all, qg, kg, n_rwkv, width, seq_len):
    T, D = x2d.shape
    n_all = w_all.shape[1]
    tm = PROJ_ROWS
    H = width // LANES
    per_seq = seq_len // tm
    row = lambda i: (i, 0)
    fix = lambda i: (0, 0)
    return pl.pallas_call(
        functools.partial(_inproj_kernel, n_rwkv=n_rwkv, width=width),
        grid=(T // tm,),
        in_specs=[
            pl.BlockSpec((tm, D), row),
            pl.BlockSpec((1, D), fix),
            pl.BlockSpec((D, n_all), fix),
            pl.BlockSpec((1, width), fix),
            pl.BlockSpec((1, width), fix),
        ],
        out_specs=[
            pl.BlockSpec((tm, n_rwkv), row),
            pl.BlockSpec((tm, width), row),
            pl.BlockSpec((tm, width), row),
            pl.BlockSpec((None, H, ACC_ROWS, tm), lambda i: (i // per_seq, 0, 0, i % per_seq)),
        ],
        out_shape=[
            jax.ShapeDtypeStruct((T, n_rwkv), F32),
            jax.ShapeDtypeStruct((T, width), BF16),
            jax.ShapeDtypeStruct((T, width), BF16),
            jax.ShapeDtypeStruct((T // seq_len, H, ACC_ROWS, seq_len), BF16),
        ],
        compiler_params=_cparams(("arbitrary",)),
        name="inproj",
    )(x2d, g, w_all, qg, kg)


def _split2(x):
    hi = x.astype(BF16)
    return hi, (x - hi.astype(F32)).astype(BF16)


def _mm(a, b, dims=NN, mode="bf16"):
    if mode == "bf16":
        return _dot(a.astype(BF16), b.astype(BF16), dims)
    ah, al = _split2(a)
    bh, bl = _split2(b)
    lhs = jnp.concatenate([ah, ah, al], axis=dims[0][0][0])
    rhs = jnp.concatenate([bh, bl, bh], axis=dims[0][1][0])
    return _dot(lhs, rhs, dims)


def _group_sum(x, ones2):
    hi, lo = _split2(x)
    return _dot(jnp.concatenate([hi, lo], axis=1), ones2)


P_INV = "x3"
P_SUBST = "bf16"
P_STATE = "bf16"
INV_BLOCK = 16
QUAD = 256


def _rwkv_kernel(pr_ref, mu_ref, w0_ref, wl_ref, a0_ref, kk_ref, ka_ref, rk_ref, lg_ref, lb_ref,
                 tri3_ref, ones2_ref, o_ref, state_sc, prev_sc, *, width):
    C = RWKV_CHUNK
    n_seq, seq_rows, _ = pr_ref.shape
    rows = n_seq * seq_rows
    n_quads = width // QUAD
    n_chunks = seq_rows // C
    heads_q = QUAD // HEAD_DIM

    @pl.when(pl.program_id(1) == 0)
    def _():
        state_sc[...] = jnp.zeros_like(state_sc)
        prev_sc[...] = jnp.zeros_like(prev_sc)

    x = jnp.concatenate([pr_ref[s] for s in range(n_seq)], axis=0)
    rid = lax.broadcasted_iota(jnp.int32, x.shape, 0)
    prev = pltpu.roll(x, 1, axis=0)
    for s in range(n_seq):
        prev = jnp.where(rid == s * seq_rows, prev_sc[s:s + 1, :], prev)
        prev_sc[s:s + 1, :] = x[(s + 1) * seq_rows - 1:(s + 1) * seq_rows, :]
    xs = x + (prev - x) * mu_ref[...]

    r = xs[:, 0:width]
    k = xs[:, width:2 * width]
    v = xs[:, 2 * width:3 * width]
    o = 3 * width
    lora_in = jnp.concatenate([jnp.tanh(xs[:, o:o + LANES]), xs[:, o + LANES:o + 2 * LANES],
                               _sigmoid(xs[:, o + 2 * LANES:o + 3 * LANES])], axis=1).astype(BF16)
    lora = _dot(lora_in, wl_ref[...])
    z = -(w0_ref[...] + lora[:, 0:width])
    softplus = jnp.maximum(z, 0.0) + jnp.log(1.0 + jnp.exp(-jnp.abs(z)))
    log_decay = -jnp.exp(-softplus - 0.5)
    a_lr = _sigmoid(a0_ref[...] + lora[:, width:2 * width])
    gate = lora[:, 2 * width:3 * width]

    ones2 = ones2_ref[...]
    kk = k * kk_ref[...]
    kk = kk * lax.rsqrt(jnp.maximum(_group_sum(kk * kk, ones2), 1e-24))
    k = k * (1.0 + (a_lr - 1.0) * ka_ref[...])
    a_vec = -kk
    b_vec = kk * a_lr

    p1 = log_decay.astype(BF16)
    rem = log_decay - p1.astype(F32)
    p2 = rem.astype(BF16)
    p3 = (rem - p2.astype(F32)).astype(BF16)
    cum = _dot(tri3_ref[...], jnp.concatenate([p1, p2, p3], axis=0))

    t_i = lax.broadcasted_iota(jnp.int32, (C, 2 * QUAD), 0)
    s_i = lax.broadcasted_iota(jnp.int32, (C, 2 * QUAD), 1) % C
    strict2 = t_i > s_i
    incl2 = t_i >= s_i
    lane_head = lax.broadcasted_iota(jnp.int32, (C, QUAD), 1) // HEAD_DIM
    row_blk = lax.broadcasted_iota(jnp.int32, (C, QUAD), 0) // INV_BLOCK
    same_blk = row_blk == ((lax.broadcasted_iota(jnp.int32, (C, QUAD), 1) % C) // INV_BLOCK)
    r2 = lax.broadcasted_iota(jnp.int32, (QUAD, QUAD), 0)
    c2 = lax.broadcasted_iota(jnp.int32, (QUAD, QUAD), 1)
    same_head = (r2 // HEAD_DIM) == (c2 // HEAD_DIM)

    def by_head(m):
        zero = jnp.zeros_like(m)
        return jnp.concatenate([jnp.where(lane_head == h, m, zero) for h in range(heads_q)], axis=0)

    def bd_rhs(m, mode):
        if mode == "bf16":
            return by_head(m.astype(BF16))
        hi, lo = _split2(m)
        hi_bd = by_head(hi)
        return jnp.concatenate([hi_bd, by_head(lo), hi_bd], axis=0)

    def mm_bd(a, rhs, mode):
        if mode == "bf16":
            return _dot(a.astype(BF16), rhs)
        ah, al = _split2(a)
        return _dot(jnp.concatenate([ah, ah, al], axis=1), rhs)

    class Work:
        pass

    def make_works(s):
        ws = []
        for c in range(n_chunks):
            for q in range(n_quads):
                w = Work()
                ls = slice(q * QUAD, (q + 1) * QUAD)
                rs = slice(s * seq_rows + c * C, s * seq_rows + (c + 1) * C)
                ld = log_decay[rs, ls]
                cm = cum[rs, ls]
                cl = cm[C - 1:C, :]
                e_in = jnp.exp(cm)
                e_neg = jnp.exp(-cm)
                e_hat = jnp.exp(cl - cm)
                w.q = q
                w.dec = jnp.exp(cl)
                w.vv = v[rs, ls]
                w.ar_t = jnp.concatenate([a_vec[rs, ls] * jnp.exp(cm - ld), r[rs, ls] * e_in], axis=0)
                w.bk_t = jnp.concatenate([bd_rhs(b_vec[rs, ls] * e_neg, "bf16"),
                                          bd_rhs(k[rs, ls] * e_neg, "bf16")], axis=0)
                w.bk_h = jnp.concatenate([b_vec[rs, ls] * e_hat, k[rs, ls] * e_hat], axis=0)
                ws.append(w)
        return ws

    def pre_stages(ws):
        def gram():
            for w in ws:
                g = _dot(w.ar_t.astype(BF16), w.bk_t, NT)
                low = jnp.where(strict2, g[0:C, :], 0.0)
                l_cat = low[:, 0:QUAD]
                rbk = jnp.where(incl2, g[C:2 * C, :], 0.0)
                w.rb_cat = rbk[:, 0:QUAD]
                w.kv_lhs = jnp.concatenate([low[:, QUAD:], rbk[:, QUAD:]], axis=0)
                w.l_off = jnp.where(same_blk, 0.0, l_cat)
                w.d_inv = jnp.where(same_blk, l_cat, 0.0)

        def square0():
            for w in ws:
                w.l_pow = mm_bd(w.d_inv, bd_rhs(w.d_inv, P_INV), P_INV)

        def square(last):
            def run():
                for w in ws:
                    if last:
                        w.d_inv = w.d_inv + w.l_pow + mm_bd(w.d_inv, bd_rhs(w.l_pow, P_INV), P_INV)
                    else:
                        both = mm_bd(jnp.concatenate([w.d_inv, w.l_pow], axis=0), bd_rhs(w.l_pow, P_INV), P_INV)
                        w.d_inv = w.d_inv + w.l_pow + both[0:C]
                        w.l_pow = both[C:2 * C]
            return run

        def neumann1():
            for w in ws:
                w.m1 = w.l_off + mm_bd(w.d_inv, bd_rhs(w.l_off, P_SUBST), P_SUBST)
                w.m1_bd = bd_rhs(w.m1, P_SUBST)

        def neumann2():
            for w in ws:
                w.m2 = mm_bd(w.m1, w.m1_bd, P_SUBST)

        def neumann3():
            for w in ws:
                w.nn = w.m1 + w.m2 + mm_bd(w.m2, w.m1_bd, P_SUBST)

        def finish():
            for w in ws:
                w.t_inv = w.nn + w.d_inv + mm_bd(w.nn, bd_rhs(w.d_inv, P_SUBST), P_SUBST)

        def values():
            for w in ws:
                w.kv = mm_bd(w.kv_lhs, bd_rhs(w.vv, P_STATE), P_STATE)

        n_sq = int(math.log2(INV_BLOCK))
        return ([gram, square0] + [square(it + 1 == n_sq) for it in range(1, n_sq)]
                + [neumann1, neumann2, neumann3, finish, values])

    def chain_levels(s, ws, y_parts):
        levels = []
        for c in range(n_chunks):
            wc = ws[c * n_quads:(c + 1) * n_quads]

            def base(wc=wc):
                for w in wc:
                    st = state_sc[s * n_quads + w.q]
                    w.base = _mm(w.ar_t, st, NT, mode=P_STATE) + w.kv

            def solve(wc=wc):
                for w in wc:
                    u0 = w.base[0:C]
                    w.u = u0 + mm_bd(w.t_inv, bd_rhs(u0, P_STATE), P_STATE)

            def update(wc=wc):
                for w in wc:
                    upd = _mm(jnp.concatenate([w.u, w.vv], axis=0), w.bk_h, TN, mode=P_STATE)
                    sq = s * n_quads + w.q
                    state_sc[sq] = state_sc[sq] * w.dec + jnp.where(same_head, upd, 0.0)
                for w in wc:
                    y_parts[w.q].append(w.base[C:2 * C] + mm_bd(w.rb_cat, bd_rhs(w.u, P_STATE), P_STATE))

            levels += [base, solve, update]
        return levels

    def post_stages(s, y_parts):
        rs = slice(s * seq_rows, (s + 1) * seq_rows)
        box = Work()

        def sums():
            box.y = jnp.concatenate([jnp.concatenate(y_parts[q], axis=0) for q in range(n_quads)], axis=-1)
            box.sums = _group_sum(jnp.concatenate([box.y, r[rs] * k[rs] * rk_ref[...]], axis=0), ones2)

        def variance():
            box.yc = box.y - box.sums[0:seq_rows] * (1.0 / HEAD_DIM)
            box.var = _group_sum(box.yc * box.yc, ones2) * (1.0 / HEAD_DIM)

        def write():
            yn = box.yc * lax.rsqrt(box.var + LNX_EPS) * lg_ref[...] + lb_ref[...]
            o_ref[s] = ((yn + box.sums[seq_rows:2 * seq_rows] * v[rs]) * gate[rs]).astype(o_ref.dtype)

        return [sums, variance, write]

    works = [make_works(s) for s in range(n_seq)]
    pres = [pre_stages(ws) for ws in works]
    y_all = [[[] for _ in range(n_quads)] for _ in range(n_seq)]
    for group in (pres, [chain_levels(s, works[s], y_all[s]) for s in range(n_seq)],
                  [post_stages(s, y_all[s]) for s in range(n_seq)]):
        for stages in zip(*group):
            for stage in stages:
                stage()


def _rwkv(pr, mu, w0, w_lora, a0, k_k, k_a, r_k, lnx_g, lnx_b, width):
    B, S, n_rwkv = pr.shape
    rows = RWKV_ROWS
    n_seq = RWKV_SEQS
    C = RWKV_CHUNK
    rr = np.arange(n_seq * rows)[:, None]
    cc = np.arange(n_seq * rows)[None, :]
    tri = ((rr // C == cc // C) & (cc <= rr)).astype(np.float32)
    tri3 = jnp.asarray(np.concatenate([tri, tri, tri], axis=1), BF16)
    gg = np.arange(width)
    ones = (gg[:, None] // HEAD_DIM == gg[None, :] // HEAD_DIM).astype(np.float32)
    ones2 = jnp.asarray(np.concatenate([ones, ones], axis=0), BF16)
    fix = lambda shape: pl.BlockSpec(shape, lambda b, i: (0, 0))
    vec = lambda n: fix((1, n))
    return pl.pallas_call(
        functools.partial(_rwkv_kernel, width=width),
        grid=(B // n_seq, S // rows),
        in_specs=[
            pl.BlockSpec((n_seq, rows, n_rwkv), lambda b, i: (b, i, 0)),
            vec(n_rwkv), vec(width), fix(w_lora.shape), vec(width),
            vec(width), vec(width), vec(width), vec(width), vec(width),
            fix(tri3.shape), fix(ones2.shape),
        ],
        out_specs=pl.BlockSpec((n_seq, rows, width), lambda b, i: (b, i, 0)),
        out_shape=jax.ShapeDtypeStruct((B, S, width), BF16),
        scratch_shapes=[
            pltpu.VMEM((n_seq * (width // QUAD), QUAD, QUAD), F32),
            pltpu.VMEM((n_seq, n_rwkv), F32),
        ],
        compiler_params=_cparams(("arbitrary", "arbitrary")),
        name="rwkv7",
    )(pr, mu, w0, w_lora, a0, k_k, k_a, r_k, lnx_g, lnx_b, tri3, ones2)


def _t5_bucket_np(dist):
    n = np.maximum(dist, 0)
    max_exact = NUM_BUCKETS // 2
    nf = np.maximum(n, 1).astype(np.float32)
    large = max_exact + (np.log(nf / max_exact) / math.log(MAX_DISTANCE / max_exact)
                         * (NUM_BUCKETS - max_exact)).astype(np.int32)
    large = np.minimum(large, NUM_BUCKETS - 1)
    return np.where(n < max_exact, n, large).astype(np.int32)


def _near_bucket_tiles(tile):
    kpos = np.arange(tile)[:, None]
    qpos = np.arange(tile)[None, :]
    out = []
    for delta in (0, 1):
        dist = delta * tile + qpos - kpos
        out.append(np.where(dist >= 0, _t5_bucket_np(dist), -1))
    return np.stack(out).astype(np.int32)


def _attn_kernel(relb_ref, bucket_ref, lq1_ref, lk1_ref, lq2_ref, lk2_ref, sg_ref,
                 q_ref, k_ref, vt_ref, o_ref, bias_sc, q2_sc, m_sc, acc_sc, st_sc, *, lam_init):
    tq = q_ref.shape[0]
    n_heads = vt_ref.shape[0]
    heads = range(n_heads)
    qi = pl.program_id(1)

    @pl.when((pl.program_id(0) == 0) & (qi == 0))
    def _():
        for h in heads:
            for d in range(2):
                bk = bucket_ref[d]
                tile = jnp.full(bk.shape, NEG_BIG, F32)
                for j in range(NUM_BUCKETS):
                    tile = jnp.where(bk == j, relb_ref[h, j] * LOG2E, tile)
                r0 = (1 - d) * tq
                bias_sc[h, r0:r0 + tq, 0:tq] = tile
                bias_sc[h, r0:r0 + tq, tq:2 * tq] = tile

    lo = _lane_lo((tq, LANES))
    for h in heads:
        q = q_ref[:, h * LANES:(h + 1) * LANES]
        zero = jnp.zeros_like(q)
        q2_sc[h, 0:tq, :] = jnp.where(lo, q, zero)
        q2_sc[h, tq:2 * tq, :] = jnp.where(lo, zero, q)
    m_sc[...] = jnp.full(m_sc.shape, NEG_BIG, F32)
    acc_sc[...] = jnp.zeros_like(acc_sc)

    def step(k0, tk, band):
        for h in heads:
            st_sc[h, 0:tk, :] = _dot(k_ref[pl.ds(k0, tk), h * LANES:(h + 1) * LANES], q2_sc[h], NT)
        pvs, alphas = [], []
        for h in heads:
            m_old = m_sc[h]
            if band is None:
                st = st_sc[h, 0:tk, :]
                far = relb_ref[h, NUM_BUCKETS - 1] * LOG2E
                m_new = jnp.maximum(m_old, jnp.max(st, axis=0, keepdims=True) + far)
                p = jnp.exp2(st - (m_new - far))
            else:
                st = st_sc[h, 0:tk, :] + bias_sc[h, band:band + tk, :]
                m_new = jnp.maximum(m_old, jnp.max(st, axis=0, keepdims=True))
                p = jnp.exp2(st - m_new)
            m_sc[h] = m_new
            pvs.append(_dot(vt_ref[h, :, pl.ds(k0, tk)], p.astype(BF16)))
            alphas.append(jnp.exp2(m_old - m_new))
        for h in heads:
            acc_sc[h] = alphas[h] * acc_sc[h] + pvs[h]

    n_far = jnp.maximum(qi - 1, 0)
    group = FAR_GROUP * tq

    def far_body(i, carry):
        step(pl.multiple_of(i * group, group), group, None)
        return carry

    lax.fori_loop(0, n_far // FAR_GROUP, far_body, 0)
    done = (n_far // FAR_GROUP) * FAR_GROUP
    left = n_far - done

    @pl.when(left >= 2)
    def _():
        step(pl.multiple_of(done * tq, tq), 2 * tq, None)

    @pl.when(left % 2 == 1)
    def _():
        step(pl.multiple_of((n_far - 1) * tq, tq), tq, None)

    @pl.when(qi >= 1)
    def _():
        step(pl.multiple_of((qi - 1) * tq, tq), 2 * tq, 0)

    @pl.when(qi == 0)
    def _():
        step(0, tq, tq)

    lam = (jnp.exp(jnp.sum(lq1_ref[...] * lk1_ref[...], axis=-1, keepdims=True))
           - jnp.exp(jnp.sum(lq2_ref[...] * lk2_ref[...], axis=-1, keepdims=True)) + lam_init)
    for h in heads:
        acc = acc_sc[h]
        l = acc[ACC_L:ACC_L + 1, :]
        o1 = acc[0:LANES, 0:tq] / l[:, 0:tq]
        o2 = acc[0:LANES, tq:2 * tq] / l[:, tq:2 * tq]
        ot = o1 - lam * o2
        ms = jnp.mean(ot * ot, axis=0, keepdims=True)
        y = ot * lax.rsqrt(ms + NORM_EPS) * sg_ref[...] * (1.0 - lam_init)
        o_ref[:, h * LANES:(h + 1) * LANES] = y.T.astype(o_ref.dtype)


def _diff_attn(q, k, vt, relb_t, lq1, lk1, lq2, lk2, sg_col, lam_init):
    B, S, width = q.shape
    H = width // LANES
    t = ATT_TILE
    buckets = jnp.asarray(_near_bucket_tiles(t))
    vec = pl.BlockSpec((1, HEAD_DIM), lambda b, i: (0, 0))
    return pl.pallas_call(
        functools.partial(_attn_kernel, lam_init=lam_init),
        grid=(B, S // t),
        in_specs=[
            pl.BlockSpec(memory_space=pltpu.SMEM),
            pl.BlockSpec((2, t, t), lambda b, i: (0, 0, 0)),
            vec, vec, vec, vec,
            pl.BlockSpec((LANES, 1), lambda b, i: (0, 0)),
            pl.BlockSpec((None, t, width), lambda b, i: (b, i, 0)),
            pl.BlockSpec((None, S, width), lambda b, i: (b, 0, 0)),
            pl.BlockSpec((None, H, ACC_ROWS, S), lambda b, i: (b, 0, 0, 0)),
        ],
        out_specs=pl.BlockSpec((None, t, width), lambda b, i: (b, i, 0)),
        out_shape=jax.ShapeDtypeStruct((B, S, width), BF16),
        scratch_shapes=[
            pltpu.VMEM((H, 2 * t, 2 * t), F32),
            pltpu.VMEM((H, 2 * t, LANES), BF16),
            pltpu.VMEM((H, 1, 2 * t), F32),
            pltpu.VMEM((H, ACC_ROWS, 2 * t), F32),
            pltpu.VMEM((H, FAR_GROUP * t, 2 * t), F32),
        ],
        compiler_params=_cparams(("arbitrary", "arbitrary")),
        name="diff_attn",
    )(relb_t, buckets, lq1, lk1, lq2, lk2, sg_col, q, k, vt)


def _run_rows(count):
    return jnp.ceil(count * (1.0 / RUN_CHUNK)) * RUN_CHUNK


def _outproj_kernel(x_ref, ya_ref, yb_ref, wa_ref, wb_ref, g_ref, rw_ref, rb_ref,
                    x1_ref, h2_ref, gd_ref, cnt_ref):
    x1 = x_ref[...] + _dot(ya_ref[...], wa_ref[...]) + _dot(yb_ref[...], wb_ref[...])
    x1_ref[...] = x1
    ms = jnp.mean(x1 * x1, axis=-1, keepdims=True)
    h2 = x1 * lax.rsqrt(ms + NORM_EPS) * g_ref[...]
    h2_ref[...] = h2.astype(h2_ref.dtype)
    logits = _mm(h2, rw_ref[...], mode="x3") + rb_ref[...]
    work = logits.T
    expert = lax.broadcasted_iota(jnp.int32, work.shape, 0).astype(F32)
    picks = []
    for _ in range(TOP_K):
        m = jnp.max(work, axis=0, keepdims=True)
        idx = jnp.min(jnp.where(work == m, expert, float(LANES)), axis=0, keepdims=True)
        hit = expert == idx
        picks.append((m, hit))
        work = jnp.where(hit, NEG_BIG, work)
    m0 = picks[0][0]
    es = [jnp.exp(m - m0) for m, _ in picks]
    denom = es[0] + es[1] + es[2] + es[3]
    gd_t = jnp.zeros_like(work)
    for e, (_, hit) in zip(es, picks):
        gd_t = jnp.where(hit, e / denom, gd_t)
    gd = gd_t.T
    gd_ref[...] = gd

    @pl.when(pl.program_id(0) == 0)
    def _():
        cnt_ref[...] = jnp.zeros_like(cnt_ref)

    cnt_ref[0:1, :] = cnt_ref[0:1, :] + _run_rows(jnp.sum(jnp.where(gd > 0.0, 1.0, 0.0), axis=0, keepdims=True))


def _outproj(x2d, ya, yb, wa, wb, g2, rw, rb):
    T, D = x2d.shape
    half = ya.shape[1]
    tm = MOVE_ROWS
    row = lambda i: (i, 0)
    fix = lambda i: (0, 0)
    return pl.pallas_call(
        _outproj_kernel,
        grid=(T // tm,),
        in_specs=[
            pl.BlockSpec((tm, D), row),
            pl.BlockSpec((tm, half), row),
            pl.BlockSpec((tm, half), row),
            pl.BlockSpec((half, D), fix),
            pl.BlockSpec((half, D), fix),
            pl.BlockSpec((1, D), fix),
            pl.BlockSpec((D, LANES), fix),
            pl.BlockSpec((1, LANES), fix),
        ],
        out_specs=[
            pl.BlockSpec((tm, D), row),
            pl.BlockSpec((tm, D), row),
            pl.BlockSpec((tm, LANES), row),
            pl.BlockSpec((8, LANES), fix),
        ],
        out_shape=[
            jax.ShapeDtypeStruct((T, D), F32),
            jax.ShapeDtypeStruct((T, D), BF16),
            jax.ShapeDtypeStruct((T, LANES), F32),
            jax.ShapeDtypeStruct((8, LANES), F32),
        ],
        compiler_params=_cparams(("arbitrary",)),
        name="outproj_router",
    )(x2d, ya, yb, wa, wb, g2, rw, rb)


def _padded_rows(cnt, rb):
    return jnp.ceil(cnt * (1.0 / rb)) * rb


def _route_kernel(gd_ref, tot_ref, lp_ref, gate_ref, tab_ref, cnt_ref, carry_sc, pstart_sc, *, rb):
    i = pl.program_id(0)
    tb = gd_ref.shape[0]
    gd = gd_ref[...]
    sel = gd > 0.0
    self = jnp.where(sel, 1.0, 0.0)
    run_pad = _run_rows(jnp.sum(self, axis=0, keepdims=True))
    r2 = lax.broadcasted_iota(jnp.int32, (LANES, LANES), 0)
    c2 = lax.broadcasted_iota(jnp.int32, (LANES, LANES), 1)

    @pl.when(i == 0)
    def _():
        cnt = tot_ref[0:1, :]
        upper = jnp.where(r2 < c2, 1.0, 0.0)
        pstart = _dot(jnp.broadcast_to(_padded_rows(cnt, rb), (8, LANES)), upper, prec=HP)[0:1, :]
        pstart_sc[...] = pstart
        cnt_ref[0:1, :] = cnt
        cnt_ref[1:2, :] = pstart
        cnt_ref[2:8, :] = jnp.zeros((6, LANES), F32)
        carry_sc[...] = jnp.zeros_like(carry_sc)

    rr = lax.broadcasted_iota(jnp.int32, (tb, tb), 0)
    cc = lax.broadcasted_iota(jnp.int32, (tb, tb), 1)
    lower = jnp.where(cc < rr, 1.0, 0.0).astype(BF16)
    sel_b = self.astype(BF16)
    prefix = _dot(lower, sel_b)
    upper_b = jnp.where(r2 < c2, 1.0, 0.0).astype(BF16)
    off = _dot(jnp.broadcast_to(run_pad, (8, LANES)).astype(BF16), upper_b)[0:1, :]
    lp = off + prefix
    upper_incl = jnp.where(r2 <= c2, 1.0, 0.0).astype(BF16)
    slot = _dot(sel_b, upper_incl)
    lane = lax.broadcasted_iota(jnp.int32, (tb, LANES), 1)
    p_out = jnp.zeros((tb, LANES), F32)
    g_out = jnp.zeros((tb, LANES), F32)
    for kk in range(TOP_K):
        mk = sel & (slot == float(kk + 1))
        p_k = jnp.sum(jnp.where(mk, lp, 0.0), axis=-1, keepdims=True)
        g_k = jnp.sum(jnp.where(mk, gd, 0.0), axis=-1, keepdims=True)
        p_out = jnp.where(lane == kk, p_k, p_out)
        g_out = jnp.where(lane == kk, g_k, g_out)
    lp_ref[...] = p_out
    gate_ref[...] = g_out
    def as_column(v):
        return jnp.sum(jnp.where(r2 == c2, jnp.broadcast_to(v, (LANES, LANES)), 0.0), axis=1, keepdims=True)

    first = as_column(off * (1.0 / RUN_CHUNK))
    count = as_column(run_pad * (1.0 / RUN_CHUNK))
    start = as_column(pstart_sc[...] + carry_sc[...])
    chunk = lax.broadcasted_iota(jnp.int32, (LANES, CHUNK_LANES), 1).astype(F32)
    mine = (chunk >= first) & (chunk < first + count)
    dst = jnp.sum(jnp.where(mine, start + (chunk - first) * RUN_CHUNK, 0.0), axis=0, keepdims=True)
    pos_in_run = jnp.sum(jnp.where(mine, chunk - first, 0.0), axis=0, keepdims=True)
    run_len = jnp.sum(jnp.where(mine, count, 0.0), axis=0, keepdims=True)
    used = run_len > 0.0
    n_chunks = jnp.sum(count, axis=0, keepdims=True)
    big_part = jnp.floor(run_len * (1.0 / BIG_COPY)) * BIG_COPY
    is_big = used & (pos_in_run < big_part) & (pos_in_run == jnp.floor(pos_in_run * (1.0 / BIG_COPY)) * BIG_COPY)
    is_small = used & (pos_in_run >= big_part)
    code = dst * (256.0 / RUN_CHUNK) + chunk[0:1, :]
    c_hi = jnp.floor(code * (1.0 / 65536.0))
    c_mid = jnp.floor(code * (1.0 / 256.0)) - c_hi * 256.0
    c_lo = code - jnp.floor(code * (1.0 / 256.0)) * 256.0
    pieces = jnp.concatenate([c_hi, c_mid, c_lo, jnp.zeros((5, CHUNK_LANES), F32)], axis=0).astype(BF16)
    rq = lax.broadcasted_iota(jnp.int32, (CHUNK_LANES, CHUNK_LANES), 0)
    cq = lax.broadcasted_iota(jnp.int32, (CHUNK_LANES, CHUNK_LANES), 1)
    before = jnp.where(rq < cq, 1.0, 0.0).astype(BF16)
    slot_id = rq.astype(F32)

    def compact(mask):
        flag = jnp.where(mask, 1.0, 0.0)
        rank = _dot(jnp.broadcast_to(flag, (8, CHUNK_LANES)).astype(BF16), before)[0:1, :]
        pick = jnp.where(mask & (rank == slot_id), 1.0, 0.0).astype(BF16)
        got = _dot(pieces, pick, NT)
        return got[0:1, :] * 65536.0 + got[1:2, :] * 256.0 + got[2:3, :], jnp.sum(flag, axis=1, keepdims=True)

    big_list, n_big = compact(is_big)
    small_list, n_small = compact(is_small)
    lane_c = lax.broadcasted_iota(jnp.int32, (1, CHUNK_LANES), 1)
    counts = jnp.where(lane_c == 0, n_chunks, jnp.where(lane_c == 1, n_big, jnp.where(lane_c == 2, n_small, 0.0)))
    tab_ref[0:1, :] = big_list.astype(jnp.int32)
    tab_ref[1:2, :] = small_list.astype(jnp.int32)
    tab_ref[2:3, :] = counts.astype(jnp.int32)
    tab_ref[3:8, :] = jnp.zeros((5, CHUNK_LANES), jnp.int32)
    carry_sc[...] = carry_sc[...] + run_pad


def _route(gd, totals, rb):
    T = gd.shape[0]
    tb = MOVE_ROWS
    blk = lambda i: (i, 0)
    fix = lambda i: (0, 0)
    return pl.pallas_call(
        functools.partial(_route_kernel, rb=rb),
        grid=(T // tb,),
        in_specs=[pl.BlockSpec((tb, LANES), blk), pl.BlockSpec((8, LANES), fix)],
        out_specs=[
            pl.BlockSpec((tb, LANES), blk),
            pl.BlockSpec((tb, LANES), blk),
            pl.BlockSpec((8, CHUNK_LANES), blk),
            pl.BlockSpec((8, LANES), fix),
        ],
        out_shape=[
            jax.ShapeDtypeStruct((T, LANES), F32),
            jax.ShapeDtypeStruct((T, LANES), F32),
            jax.ShapeDtypeStruct((T // tb * 8, CHUNK_LANES), jnp.int32),
            jax.ShapeDtypeStruct((8, LANES), F32),
        ],
        scratch_shapes=[pltpu.VMEM((1, LANES), F32), pltpu.VMEM((1, LANES), F32)],
        compiler_params=_cparams(("arbitrary",)),
        name="route_positions",
    )(gd, totals)


def _rows_at(src_ref, src_row, dst_ref, dst_row, n_rows, sem):
    return pltpu.make_async_copy(src_ref.at[pl.ds(src_row, n_rows), :], dst_ref.at[pl.ds(dst_row, n_rows), :], sem)


def _start_runs(b, cnt_ref, big_ref, small_ref, copy_rows):
    def issue(tab_ref, stride, n_rows):
        def body(i, carry):
            code = tab_ref[b * stride + i]
            local = (code & 255) * RUN_CHUNK
            glob = lax.shift_right_logical(code, 8) * RUN_CHUNK
            copy_rows(pl.multiple_of(local, RUN_CHUNK), pl.multiple_of(glob, RUN_CHUNK), n_rows).start()
            return carry
        return body

    lax.fori_loop(0, cnt_ref[b * 4 + 1], issue(big_ref, BIG_LIST, BIG_COPY * RUN_CHUNK), 0)
    lax.fori_loop(0, cnt_ref[b * 4 + 2], issue(small_ref, SMALL_LIST, RUN_CHUNK), 0)


def _wait_runs(b, nc_ref, wait_rows):
    total = nc_ref[b * 4]
    p = 1
    while p <= LOCAL_CHUNKS:
        @pl.when((total & p) != 0)
        def _(p=p):
            wait_rows(p * RUN_CHUNK).wait()
        p *= 2


def _rows_copy(src_ref, dst_ref, n_rows, sem):
    return pltpu.make_async_copy(src_ref.at[pl.ds(0, n_rows), :], dst_ref.at[pl.ds(0, n_rows), :], sem)


def _local_onehot(lp, lb):
    j = lax.broadcasted_iota(jnp.int32, (lp.shape[0], lb), 1).astype(F32)
    hit = lp[:, 0:1] == j
    for kk in range(1, TOP_K):
        hit = hit | (lp[:, kk:kk + 1] == j)
    return hit


def _scatter_kernel(nc_ref, big_ref, small_ref, t0_ref, tc_ref, lp_ref, h_ref, xs_ref, buf, zero_buf, sems):
    b = pl.program_id(0)
    slot = b % 2
    pt = jnp.where(_local_onehot(lp_ref[...], buf.shape[1]), 1.0, 0.0).astype(BF16)
    buf[slot] = _dot(pt, h_ref[...], TN)
    _start_runs(b, nc_ref, big_ref, small_ref,
                lambda s, d, n: _rows_at(buf.at[slot], s, xs_ref, d, n, sems.at[slot]))

    @pl.when(b >= 1)
    def _():
        _wait_runs(b - 1, nc_ref, lambda n: _rows_copy(buf.at[1 - slot], xs_ref, n, sems.at[1 - slot]))

    @pl.when(b == pl.num_programs(0) - 1)
    def _():
        _wait_runs(b, nc_ref, lambda n: _rows_copy(buf.at[slot], xs_ref, n, sems.at[slot]))
        zero_buf[...] = jnp.zeros_like(zero_buf)
        total = 0
        for e in range(N_EXPERTS + 1):
            t0 = t0_ref[e]

            def issue(c, carry, t0=t0):
                _rows_at(zero_buf, 0, xs_ref, pl.multiple_of(t0 + c * RUN_CHUNK, RUN_CHUNK), RUN_CHUNK,
                         sems.at[0]).start()
                return carry

            lax.fori_loop(0, tc_ref[e], issue, 0)
            total = total + tc_ref[e]

        def drain(c, carry):
            _rows_at(zero_buf, 0, xs_ref, 0, RUN_CHUNK, sems.at[0]).wait()
            return carry

        lax.fori_loop(0, total, drain, 0)


def _scatter_rows(tabs, tails, lp, h2, n_rows):
    T, D = h2.shape
    tb = MOVE_ROWS
    return pl.pallas_call(
        _scatter_kernel,
        grid_spec=pltpu.PrefetchScalarGridSpec(
            num_scalar_prefetch=5,
            grid=(T // tb,),
            in_specs=[
                pl.BlockSpec((tb, LANES), lambda i, *_: (i, 0)),
                pl.BlockSpec((tb, D), lambda i, *_: (i, 0)),
            ],
            out_specs=pl.BlockSpec(memory_space=pl.ANY),
            scratch_shapes=[pltpu.VMEM((2, LOCAL_ROWS, D), F32), pltpu.VMEM((RUN_CHUNK, D), F32),
                            pltpu.SemaphoreType.DMA((2,))],
        ),
        out_shape=jax.ShapeDtypeStruct((n_rows, D), F32),
        compiler_params=_cparams(("arbitrary",)),
        name="scatter_rows",
    )(*tabs, *tails, lp, h2)


def _moe_kernel(be_ref, na_ref, slot_ref, nxt_ref, xs_ref, w1_hbm, b1_ref, w2_hbm, b2_ref, ys_ref,
                w1f, w2f, w1b, w2b, sems):
    j = pl.program_id(0)
    e = be_ref[j]
    slot = slot_ref[j]

    def fetch(expert, sl):
        return (pltpu.make_async_copy(w1_hbm.at[expert], w1f.at[sl], sems.at[0, sl]),
                pltpu.make_async_copy(w2_hbm.at[expert], w2f.at[sl], sems.at[1, sl]))

    @pl.when(j == 0)
    def _():
        for cp in fetch(e, slot):
            cp.start()

    @pl.when((j == 0) | (e != be_ref[jnp.maximum(j - 1, 0)]))
    def _():
        for cp in fetch(e, slot):
            cp.wait()
        w1b[...] = w1f[slot].astype(BF16)
        w2b[...] = w2f[slot].astype(BF16)

        @pl.when(nxt_ref[j] >= 0)
        def _():
            for cp in fetch(nxt_ref[j], 1 - slot):
                cp.start()

    @pl.when(j < na_ref[0])
    def _():
        xb = xs_ref[...].astype(BF16)
        hid = _dot(xb, w1b[...]) + b1_ref[...]
        half = hid.shape[1] // 2
        x_glu = jnp.minimum(hid[:, :half], SWIGLU_LIMIT)
        x_lin = jnp.clip(hid[:, half:], -SWIGLU_LIMIT, SWIGLU_LIMIT)
        act = x_glu * _sigmoid(SWIGLU_ALPHA * x_glu) * (x_lin + 1.0)
        ys_ref[...] = _dot(act.astype(BF16), w2b[...]) + b2_ref[...]

    @pl.when(j >= na_ref[0])
    def _():
        ys_ref[...] = jnp.zeros_like(ys_ref)


def _expert_runs(blk_expert):
    n = blk_expert.shape[0]
    idx = jnp.arange(n)
    starts = jnp.concatenate([jnp.ones((1,), bool), blk_expert[1:] != blk_expert[:-1]])
    slot = (jnp.cumsum(starts) - 1) % 2
    next_start = lax.cummin(jnp.where(starts, idx, n)[::-1])[::-1]
    after = jnp.concatenate([next_start[1:], jnp.full((1,), n)])
    nxt = jnp.where(after < n, blk_expert[jnp.minimum(after, n - 1)], -1)
    return slot.astype(jnp.int32), nxt.astype(jnp.int32)


def _moe(blk_expert, n_active, xs, w1, b1, w2, b2):
    R, D = xs.shape
    E, _, D2 = w1.shape
    rb = MOE_ROWS
    nb = R // rb
    slot, nxt = _expert_runs(blk_expert)
    blk = lambda j, be, na, *_: (jnp.minimum(j, na[0] - 1), 0)
    return pl.pallas_call(
        _moe_kernel,
        grid_spec=pltpu.PrefetchScalarGridSpec(
            num_scalar_prefetch=4,
            grid=(nb,),
            in_specs=[
                pl.BlockSpec((rb, D), blk),
                pl.BlockSpec(memory_space=pl.ANY),
                pl.BlockSpec((None, 1, D2), lambda j, be, *_: (be[j], 0, 0)),
                pl.BlockSpec(memory_space=pl.ANY),
                pl.BlockSpec((None, 1, D), lambda j, be, *_: (be[j], 0, 0)),
            ],
            out_specs=pl.BlockSpec((rb, D), lambda j, *_: (j, 0)),
            scratch_shapes=[pltpu.VMEM((2, D, D2), F32), pltpu.VMEM((2, D2 // 2, D), F32),
                            pltpu.VMEM((D, D2), BF16), pltpu.VMEM((D2 // 2, D), BF16),
                            pltpu.SemaphoreType.DMA((2, 2))],
        ),
        out_shape=jax.ShapeDtypeStruct((R, D), F32),
        compiler_params=_cparams(("arbitrary",)),
        name="moe_experts",
    )(blk_expert, n_active, slot, nxt, xs, w1, b1, w2, b2)


def _combine_kernel(nc_ref, big_ref, small_ref, lp_ref, gate_ref, x1_ref, ys_ref, o_ref, buf, sems):
    b = pl.program_id(0)
    slot = b % 2

    def fetch(blk, sl):
        _start_runs(blk, nc_ref, big_ref, small_ref,
                    lambda s, d, n: _rows_at(ys_ref, d, buf.at[sl], s, n, sems.at[sl]))

    @pl.when(b == 0)
    def _():
        buf[...] = jnp.zeros_like(buf)
        fetch(b, slot)

    @pl.when(b + 1 < pl.num_programs(0))
    def _():
        fetch(b + 1, 1 - slot)

    _wait_runs(b, nc_ref, lambda n: _rows_copy(ys_ref, buf.at[slot], n, sems.at[slot]))
    lp = lp_ref[...]
    g = gate_ref[...]
    _, lb, d = buf.shape
    j = lax.broadcasted_iota(jnp.int32, (lp.shape[0], lb), 1).astype(F32)
    gt = jnp.zeros(j.shape, F32)
    for kk in range(TOP_K):
        gt = jnp.where(lp[:, kk:kk + 1] == j, g[:, kk:kk + 1], gt)
    g_hi, g_lo = _split2(gt)
    ones = jnp.ones((lp.shape[0], LANES), BF16)
    g_row = _dot(g_hi, ones, TN) + _dot(g_lo, ones, TN)
    rows = (buf[slot] * jnp.concatenate([g_row] * (d // LANES), axis=1)).astype(BF16)
    p01 = jnp.where(gt != 0.0, 1.0, 0.0).astype(BF16)
    o_ref[...] = x1_ref[...] + _dot(p01, rows)


def _combine(tabs, lp, gate, x1, ys):
    T, D = x1.shape
    tb = MOVE_ROWS
    return pl.pallas_call(
        _combine_kernel,
        grid_spec=pltpu.PrefetchScalarGridSpec(
            num_scalar_prefetch=3,
            grid=(T // tb,),
            in_specs=[
                pl.BlockSpec((tb, LANES), lambda i, *_: (i, 0)),
                pl.BlockSpec((tb, LANES), lambda i, *_: (i, 0)),
                pl.BlockSpec((tb, D), lambda i, *_: (i, 0)),
                pl.BlockSpec(memory_space=pl.ANY),
            ],
            out_specs=pl.BlockSpec((tb, D), lambda i, *_: (i, 0)),
            scratch_shapes=[pltpu.VMEM((2, LOCAL_ROWS, D), F32), pltpu.SemaphoreType.DMA((2,))],
        ),
        out_shape=jax.ShapeDtypeStruct((T, D), F32),
        compiler_params=_cparams(("arbitrary",)),
        name="combine_rows",
    )(*tabs, lp, gate, x1, ys)


def _pad_cols(a, n):
    return jnp.pad(a, ((0, 0), (0, n - a.shape[1])))


def _pad_rows(a, n):
    return jnp.pad(a, ((0, n - a.shape[0]), (0, 0)))


def _lora_weight(w2, a2, g2, width):
    z = jnp.zeros((LANES, width), F32)
    return jnp.concatenate([
        jnp.concatenate([_pad_rows(w2, LANES), z, z], axis=1),
        jnp.concatenate([z, _pad_rows(a2, LANES), z], axis=1),
        jnp.concatenate([z, z, _pad_rows(g2, LANES)], axis=1),
    ], axis=0).astype(BF16)


def _split_rwkv_cols(a, width):
    o = 3 * width
    return jnp.concatenate([
        a[:, :o],
        _pad_cols(a[:, o:o + DECAY_LORA], LANES),
        _pad_cols(a[:, o + DECAY_LORA:o + DECAY_LORA + AAA_LORA], LANES),
        _pad_cols(a[:, o + DECAY_LORA + AAA_LORA:o + DECAY_LORA + AAA_LORA + GATE_LORA], LANES),
    ], axis=1)


def kernel(x, norm1_g, w_in, mu_shift, w0, w2, a0, a2, g2, k_k, k_a, r_k, lnx_g, lnx_b, qn_g, kn_g,
           lam_q1, lam_k1, lam_q2, lam_k2, subln_g, rel_bias, w_out, norm2_g, router_w, router_b,
           exp_w1, exp_b1, exp_w2, exp_b2):
    B, S, D = x.shape
    T = B * S
    depth = norm1_g.shape[0]
    width = w0.shape[1]
    n_rwkv_cols = 3 * width + DECAY_LORA + AAA_LORA + GATE_LORA
    n_rwkv_pad = 3 * width + 3 * LANES
    n_groups = width // HEAD_DIM
    H = width // LANES
    relb_t = rel_bias.T
    row = lambda a: a.reshape(1, -1)

    for layer in range(depth):
        lam_init = 0.8 - 0.6 * math.exp(-0.3 * layer)
        w_l = w_in[layer]
        w_all = jnp.concatenate([_split_rwkv_cols(w_l[:, :n_rwkv_cols], width), w_l[:, n_rwkv_cols:]],
                                axis=1).astype(BF16)
        mu = _split_rwkv_cols(row(mu_shift[layer]), width)
        qg = row(jnp.tile(qn_g[layer], n_groups))
        kg = row(jnp.tile(kn_g[layer], n_groups))

        pr, q, k, vt = _inproj(x.reshape(T, D), row(norm1_g[layer]), w_all, qg, kg, n_rwkv_pad, width, S)

        y_a = _rwkv(pr.reshape(B, S, n_rwkv_pad), mu, row(w0[layer]),
                    _lora_weight(w2[layer], a2[layer], g2[layer], width), row(a0[layer]),
                    row(k_k[layer]), row(k_a[layer]), row(r_k[layer]), row(lnx_g[layer]),
                    row(lnx_b[layer]), width)

        y_b = _diff_attn(q.reshape(B, S, width), k.reshape(B, S, width), vt, relb_t,
                         row(lam_q1[layer]), row(lam_k1[layer]), row(lam_q2[layer]), row(lam_k2[layer]),
                         subln_g[layer].reshape(-1, 1), lam_init)

        wo = w_out[layer].astype(BF16)
        rw = _pad_cols(router_w[layer], LANES)
        rb = jnp.concatenate([router_b[layer], jnp.full((LANES - N_EXPERTS,), NEG_BIG, F32)]).reshape(1, -1)
        x1, h2, gd, totals = _outproj(x.reshape(T, D), y_a.reshape(T, width), y_b.reshape(T, width),
                                      wo[:width], wo[width:], row(norm2_g[layer]), rw, rb)

        lp, gate, tab, cnt = _route(gd, totals, MOE_ROWS)
        tab = tab.reshape(-1, 8, CHUNK_LANES)
        tabs = (tab[:, 2, :4].reshape(-1),
                tab[:, 0, :BIG_LIST].reshape(-1), tab[:, 1, :SMALL_LIST].reshape(-1))
        run_end = cnt[1, :N_EXPERTS] + cnt[0, :N_EXPERTS]
        pad_end = cnt[1, :N_EXPERTS] + _padded_rows(cnt[0, :N_EXPERTS], MOE_ROWS)
        run_pad_max = (T // MOVE_ROWS) * N_EXPERTS * (RUN_CHUNK - 1)
        n_blocks = -(-(T * TOP_K + run_pad_max + N_EXPERTS * (MOE_ROWS - 1)) // MOE_ROWS)
        gap_start = jnp.concatenate([run_end, pad_end[-1:]])
        gap_end = jnp.concatenate([pad_end, jnp.full((1,), n_blocks * MOE_ROWS, F32)])
        tails = (gap_start.astype(jnp.int32), ((gap_end - gap_start) / RUN_CHUNK).astype(jnp.int32))
        blk_start = (jnp.arange(n_blocks) * MOE_ROWS).astype(F32)
        blk_expert = jnp.minimum(jnp.sum(pad_end[None, :] <= blk_start[:, None], axis=1), N_EXPERTS - 1)
        n_active = (pad_end[N_EXPERTS - 1] / MOE_ROWS).astype(jnp.int32).reshape(1)

        xs = _scatter_rows(tabs, tails, lp, h2, n_blocks * MOE_ROWS)
        ys = _moe(blk_expert.astype(jnp.int32), n_active, xs,
                  exp_w1[layer], exp_b1[layer][:, None, :], exp_w2[layer], exp_b2[layer][:, None, :])
        x = _combine(tabs, lp, gate, x1, ys).reshape(B, S, D)
    return x
```

```python
import functools
import math

import numpy as np
import jax
import jax.numpy as jnp
from jax import lax
from jax.experimental import pallas as pl
from jax.experimental.pallas import tpu as pltpu

F32 = jnp.float32
BF16 = jnp.bfloat16
HP = lax.Precision.HIGHEST

HEAD_DIM = 64
DECAY_LORA = 32
AAA_LORA = 32
GATE_LORA = 96
NUM_BUCKETS = 32
MAX_DISTANCE = 128
N_EXPERTS = 32
TOP_K = 4
SWIGLU_LIMIT = 7.0
SWIGLU_ALPHA = 1.702
NORM_EPS = 1e-5
LNX_EPS = 64e-5

LANES = 128
VMEM_LIMIT_BYTES = 56 * 1024 * 1024

NEG_BIG = -1e30
LOG2E = math.log2(math.e)
ACC_L = LANES
ACC_ROWS = LANES + 8

PROJ_ROWS = 512
RWKV_CHUNK = 64
RWKV_ROWS = 256
RWKV_SEQS = 2
ATT_TILE = 256
FAR_GROUP = 4
MOE_ROWS = 512
MOVE_ROWS = 256
OUT_ROWS = 256
RUN_CHUNK = 8
LOCAL_ROWS = MOVE_ROWS * 4 + 32 * RUN_CHUNK
LOCAL_CHUNKS = LOCAL_ROWS // RUN_CHUNK
CHUNK_LANES = 256
BIG_COPY = 4
BIG_LIST = 64
SMALL_LIST = 128

NN = (((1,), (0,)), ((), ()))
NT = (((1,), (1,)), ((), ()))
TN = (((0,), (0,)), ((), ()))


def _dot(a, b, dims=NN, prec=None):
    return lax.dot_general(a, b, dims, precision=prec, preferred_element_type=F32)


def _cparams(sem):
    return pltpu.CompilerParams(dimension_semantics=sem, vmem_limit_bytes=VMEM_LIMIT_BYTES)


def _sigmoid(x):
    return 1.0 / (1.0 + jnp.exp(-x))


def _lane_lo(shape):
    lane = lax.broadcasted_iota(jnp.int32, shape, len(shape) - 1)
    return (lane % LANES) < HEAD_DIM


def _head_rms_normalize(x, gain, scale):
    outs = []
    for t in range(x.shape[1] // LANES):
        xt = x[:, t * LANES:(t + 1) * LANES]
        lo = _lane_lo(xt.shape)
        x2 = xt * xt
        s0 = jnp.sum(jnp.where(lo, x2, 0.0), axis=-1, keepdims=True)
        s1 = jnp.sum(jnp.where(lo, 0.0, x2), axis=-1, keepdims=True)
        inv = lax.rsqrt(jnp.where(lo, s0, s1) * (1.0 / HEAD_DIM) + NORM_EPS)
        outs.append(xt * inv)
    return jnp.concatenate(outs, axis=-1) * gain * scale


def _inproj_kernel(x_ref, g_ref, w_ref, qg_ref, kg_ref, pr_ref, q_ref, k_ref, vt_ref, *, n_rwkv, width):
    x = x_ref[...]
    ms = jnp.mean(x * x, axis=-1, keepdims=True)
    h = (x * lax.rsqrt(ms + NORM_EPS) * g_ref[...]).astype(BF16)
    step = 384
    for c0 in range(0, n_rwkv, step):
        pr_ref[:, c0:c0 + step] = _dot(h, w_ref[:, c0:c0 + step])
    q = _dot(h, w_ref[:, n_rwkv:n_rwkv + width])
    q_ref[...] = _head_rms_normalize(q, qg_ref[...], HEAD_DIM ** -0.5 * LOG2E).astype(BF16)
    k = _dot(h, w_ref[:, n_rwkv + width:n_rwkv + 2 * width])
    k_ref[...] = _head_rms_normalize(k, kg_ref[...], 1.0).astype(BF16)
    v = _dot(h, w_ref[:, n_rwkv + 2 * width:n_rwkv + 3 * width])
    for hd in range(width // LANES):
        vt_ref[hd, 0:LANES, :] = v[:, hd * LANES:(hd + 1) * LANES].T.astype(BF16)
        vt_ref[hd, LANES:ACC_ROWS, :] = jnp.ones((ACC_ROWS - LANES, v.shape[0]), BF16)


def _inproj(x2d, g, w_all, qg, kg, n_rwkv, width, seq_len):
    T, D = x2d.shape
    n_all = w_all.shape[1]
    tm = PROJ_ROWS
    H = width // LANES
    per_seq = seq_len // tm
    row = lambda i: (i, 0)
    fix = lambda i: (0, 0)
    return pl.pallas_call(
        functools.partial(_inproj_kernel, n_rwkv=n_rwkv, width=width),
        grid=(T // tm,),
        in_specs=[
            pl.BlockSpec((tm, D), row),
            pl.BlockSpec((1, D), fix),
            pl.BlockSpec((D, n_all), fix),
            pl.BlockSpec((1, width), fix),
            pl.BlockSpec((1, width), fix),
        ],
        out_specs=[
            pl.BlockSpec((tm, n_rwkv), row),
            pl.BlockSpec((tm, width), row),
            pl.BlockSpec((tm, width), row),
            pl.BlockSpec((None, H, ACC_ROWS, tm), lambda i: (i // per_seq, 0, 0, i % per_seq)),
        ],
        out_shape=[
            jax.ShapeDtypeStruct((T, n_rwkv), F32),
            jax.ShapeDtypeStruct((T, width), BF16),
            jax.ShapeDtypeStruct((T, width), BF16),
            jax.ShapeDtypeStruct((T // seq_len, H, ACC_ROWS, seq_len), BF16),
        ],
        compiler_params=_cparams(("arbitrary",)),
        name="inproj",
    )(x2d, g, w_all, qg, kg)


def _split2(x):
    hi = x.astype(BF16)
    return hi, (x - hi.astype(F32)).astype(BF16)


def _mm(a, b, dims=NN, mode="bf16"):
    if mode == "bf16":
        return _dot(a.astype(BF16), b.astype(BF16), dims)
    ah, al = _split2(a)
    bh, bl = _split2(b)
    lhs = jnp.concatenate([ah, ah, al], axis=dims[0][0][0])
    rhs = jnp.concatenate([bh, bl, bh], axis=dims[0][1][0])
    return _dot(lhs, rhs, dims)


def _group_sum(x, ones2):
    hi, lo = _split2(x)
    return _dot(jnp.concatenate([hi, lo], axis=1), ones2)


P_INV = "x3"
P_SUBST = "bf16"
P_STATE = "bf16"
INV_BLOCK = 16
QUAD = 256


def _rwkv_kernel(pr_ref, mu_ref, w0_ref, wl_ref, a0_ref, kk_ref, ka_ref, rk_ref, lg_ref, lb_ref,
                 tri3_ref, ones2_ref, o_ref, state_sc, prev_sc, *, width):
    C = RWKV_CHUNK
    n_seq, seq_rows, _ = pr_ref.shape
    rows = n_seq * seq_rows
    n_quads = width // QUAD
    n_chunks = seq_rows // C
    heads_q = QUAD // HEAD_DIM

    @pl.when(pl.program_id(1) == 0)
    def _():
        state_sc[...] = jnp.zeros_like(state_sc)
        prev_sc[...] = jnp.zeros_like(prev_sc)

    x = jnp.concatenate([pr_ref[s] for s in range(n_seq)], axis=0)
    rid = lax.broadcasted_iota(jnp.int32, x.shape, 0)
    prev = pltpu.roll(x, 1, axis=0)
    for s in range(n_seq):
        prev = jnp.where(rid == s * seq_rows, prev_sc[s:s + 1, :], prev)
        prev_sc[s:s + 1, :] = x[(s + 1) * seq_rows - 1:(s + 1) * seq_rows, :]
    xs = x + (prev - x) * mu_ref[...]

    r = xs[:, 0:width]
    k = xs[:, width:2 * width]
    v = xs[:, 2 * width:3 * width]
    o = 3 * width
    lora_in = jnp.concatenate([jnp.tanh(xs[:, o:o + LANES]), xs[:, o + LANES:o + 2 * LANES],
                               _sigmoid(xs[:, o + 2 * LANES:o + 3 * LANES])], axis=1).astype(BF16)
    lora = _dot(lora_in, wl_ref[...])
    z = -(w0_ref[...] + lora[:, 0:width])
    softplus = jnp.maximum(z, 0.0) + jnp.log(1.0 + jnp.exp(-jnp.abs(z)))
    log_decay = -jnp.exp(-softplus - 0.5)
    a_lr = _sigmoid(a0_ref[...] + lora[:, width:2 * width])
    gate = lora[:, 2 * width:3 * width]

    ones2 = ones2_ref[...]
    kk = k * kk_ref[...]
    kk = kk * lax.rsqrt(jnp.maximum(_group_sum(kk * kk, ones2), 1e-24))
    k = k * (1.0 + (a_lr - 1.0) * ka_ref[...])
    a_vec = -kk
    b_vec = kk * a_lr

    p1 = log_decay.astype(BF16)
    rem = log_decay - p1.astype(F32)
    p2 = rem.astype(BF16)
    p3 = (rem - p2.astype(F32)).astype(BF16)
    cum = _dot(tri3_ref[...], jnp.concatenate([p1, p2, p3], axis=0))

    t_i = lax.broadcasted_iota(jnp.int32, (C, 2 * QUAD), 0)
    s_i = lax.broadcasted_iota(jnp.int32, (C, 2 * QUAD), 1) % C
    strict2 = t_i > s_i
    incl2 = t_i >= s_i
    lane_head = lax.broadcasted_iota(jnp.int32, (C, QUAD), 1) // HEAD_DIM
    row_blk = lax.broadcasted_iota(jnp.int32, (C, QUAD), 0) // INV_BLOCK
    same_blk = row_blk == ((lax.broadcasted_iota(jnp.int32, (C, QUAD), 1) % C) // INV_BLOCK)
    r2 = lax.broadcasted_iota(jnp.int32, (QUAD, QUAD), 0)
    c2 = lax.broadcasted_iota(jnp.int32, (QUAD, QUAD), 1)
    same_head = (r2 // HEAD_DIM) == (c2 // HEAD_DIM)

    def by_head(m):
        zero = jnp.zeros_like(m)
        return jnp.concatenate([jnp.where(lane_head == h, m, zero) for h in range(heads_q)], axis=0)

    def bd_rhs(m, mode):
        if mode == "bf16":
            return by_head(m.astype(BF16))
        hi, lo = _split2(m)
        hi_bd = by_head(hi)
        return jnp.concatenate([hi_bd, by_head(lo), hi_bd], axis=0)

    def mm_bd(a, rhs, mode):
        if mode == "bf16":
            return _dot(a.astype(BF16), rhs)
        ah, al = _split2(a)
        return _dot(jnp.concatenate([ah, ah, al], axis=1), rhs)

    class Work:
        pass

    def make_works(s):
        ws = []
        for c in range(n_chunks):
            for q in range(n_quads):
                w = Work()
                ls = slice(q * QUAD, (q + 1) * QUAD)
                rs = slice(s * seq_rows + c * C, s * seq_rows + (c + 1) * C)
                ld = log_decay[rs, ls]
                cm = cum[rs, ls]
                cl = cm[C - 1:C, :]
                e_in = jnp.exp(cm)
                e_neg = jnp.exp(-cm)
                e_hat = jnp.exp(cl - cm)
                w.q = q
                w.dec = jnp.exp(cl)
                w.vv = v[rs, ls]
                w.ar_t = jnp.concatenate([a_vec[rs, ls] * jnp.exp(cm - ld), r[rs, ls] * e_in], axis=0)
                w.bk_t = jnp.concatenate([bd_rhs(b_vec[rs, ls] * e_neg, "bf16"),
                                          bd_rhs(k[rs, ls] * e_neg, "bf16")], axis=0)
                w.bk_h = jnp.concatenate([b_vec[rs, ls] * e_hat, k[rs, ls] * e_hat], axis=0)
                ws.append(w)
        return ws

    def pre_stages(ws):
        def gram():
            for w in ws:
                g = _dot(w.ar_t.astype(BF16), w.bk_t, NT)
                low = jnp.where(strict2, g[0:C, :], 0.0)
                l_cat = low[:, 0:QUAD]
                rbk = jnp.where(incl2, g[C:2 * C, :], 0.0)
                w.rb_cat = rbk[:, 0:QUAD]
                w.kv_lhs = jnp.concatenate([low[:, QUAD:], rbk[:, QUAD:]], axis=0)
                w.l_off = jnp.where(same_blk, 0.0, l_cat)
                w.d_inv = jnp.where(same_blk, l_cat, 0.0)

        def square0():
            for w in ws:
                w.l_pow = mm_bd(w.d_inv, bd_rhs(w.d_inv, P_INV), P_INV)

        def square(last):
            def run():
                for w in ws:
                    if last:
                        w.d_inv = w.d_inv + w.l_pow + mm_bd(w.d_inv, bd_rhs(w.l_pow, P_INV), P_INV)
                    else:
                        both = mm_bd(jnp.concatenate([w.d_inv, w.l_pow], axis=0), bd_rhs(w.l_pow, P_INV), P_INV)
                        w.d_inv = w.d_inv + w.l_pow + both[0:C]
                        w.l_pow = both[C:2 * C]
            return run

        def neumann1():
            for w in ws:
                w.m1 = w.l_off + mm_bd(w.d_inv, bd_rhs(w.l_off, P_SUBST), P_SUBST)
                w.m1_bd = bd_rhs(w.m1, P_SUBST)

        def neumann2():
            for w in ws:
                w.m2 = mm_bd(w.m1, w.m1_bd, P_SUBST)

        def neumann3():
            for w in ws:
                w.nn = w.m1 + w.m2 + mm_bd(w.m2, w.m1_bd, P_SUBST)

        def finish():
            for w in ws:
                w.t_inv = w.nn + w.d_inv + mm_bd(w.nn, bd_rhs(w.d_inv, P_SUBST), P_SUBST)

        def values():
            for w in ws:
                w.kv = mm_bd(w.kv_lhs, bd_rhs(w.vv, P_STATE), P_STATE)

        n_sq = int(math.log2(INV_BLOCK))
        return ([gram, square0] + [square(it + 1 == n_sq) for it in range(1, n_sq)]
                + [neumann1, neumann2, neumann3, finish, values])

    def chain_levels(s, ws, y_parts):
        levels = []
        for c in range(n_chunks):
            wc = ws[c * n_quads:(c + 1) * n_quads]

            def base(wc=wc):
                for w in wc:
                    st = state_sc[s * n_quads + w.q]
                    w.base = _mm(w.ar_t, st, NT, mode=P_STATE) + w.kv

            def solve(wc=wc):
                for w in wc:
                    u0 = w.base[0:C]
                    w.u = u0 + mm_bd(w.t_inv, bd_rhs(u0, P_STATE), P_STATE)

            def update(wc=wc):
                for w in wc:
                    upd = _mm(jnp.concatenate([w.u, w.vv], axis=0), w.bk_h, TN, mode=P_STATE)
                    sq = s * n_quads + w.q
                    state_sc[sq] = state_sc[sq] * w.dec + jnp.where(same_head, upd, 0.0)
                for w in wc:
                    y_parts[w.q].append(w.base[C:2 * C] + mm_bd(w.rb_cat, bd_rhs(w.u, P_STATE), P_STATE))

            levels += [base, solve, update]
        return levels

    def post_stages(s, y_parts):
        rs = slice(s * seq_rows, (s + 1) * seq_rows)
        box = Work()

        def sums():
            box.y = jnp.concatenate([jnp.concatenate(y_parts[q], axis=0) for q in range(n_quads)], axis=-1)
            box.sums = _group_sum(jnp.concatenate([box.y, r[rs] * k[rs] * rk_ref[...]], axis=0), ones2)

        def variance():
            box.yc = box.y - box.sums[0:seq_rows] * (1.0 / HEAD_DIM)
            box.var = _group_sum(box.yc * box.yc, ones2) * (1.0 / HEAD_DIM)

        def write():
            yn = box.yc * lax.rsqrt(box.var + LNX_EPS) * lg_ref[...] + lb_ref[...]
            o_ref[s] = ((yn + box.sums[seq_rows:2 * seq_rows] * v[rs]) * gate[rs]).astype(o_ref.dtype)

        return [sums, variance, write]

    works = [make_works(s) for s in range(n_seq)]
    pres = [pre_stages(ws) for ws in works]
    y_all = [[[] for _ in range(n_quads)] for _ in range(n_seq)]
    for group in (pres, [chain_levels(s, works[s], y_all[s]) for s in range(n_seq)],
                  [post_stages(s, y_all[s]) for s in range(n_seq)]):
        for stages in zip(*group):
            for stage in stages:
                stage()


def _rwkv(pr, mu, w0, w_lora, a0, k_k, k_a, r_k, lnx_g, lnx_b, width):
    B, S, n_rwkv = pr.shape
    rows = RWKV_ROWS
    n_seq = RWKV_SEQS
    C = RWKV_CHUNK
    rr = np.arange(n_seq * rows)[:, None]
    cc = np.arange(n_seq * rows)[None, :]
    tri = ((rr // C == cc // C) & (cc <= rr)).astype(np.float32)
    tri3 = jnp.asarray(np.concatenate([tri, tri, tri], axis=1), BF16)
    gg = np.arange(width)
    ones = (gg[:, None] // HEAD_DIM == gg[None, :] // HEAD_DIM).astype(np.float32)
    ones2 = jnp.asarray(np.concatenate([ones, ones], axis=0), BF16)
    fix = lambda shape: pl.BlockSpec(shape, lambda b, i: (0, 0))
    vec = lambda n: fix((1, n))
    return pl.pallas_call(
        functools.partial(_rwkv_kernel, width=width),
        grid=(B // n_seq, S // rows),
        in_specs=[
            pl.BlockSpec((n_seq, rows, n_rwkv), lambda b, i: (b, i, 0)),
            vec(n_rwkv), vec(width), fix(w_lora.shape), vec(width),
            vec(width), vec(width), vec(width), vec(width), vec(width),
            fix(tri3.shape), fix(ones2.shape),
        ],
        out_specs=pl.BlockSpec((n_seq, rows, width), lambda b, i: (b, i, 0)),
        out_shape=jax.ShapeDtypeStruct((B, S, width), BF16),
        scratch_shapes=[
            pltpu.VMEM((n_seq * (width // QUAD), QUAD, QUAD), F32),
            pltpu.VMEM((n_seq, n_rwkv), F32),
        ],
        compiler_params=_cparams(("arbitrary", "arbitrary")),
        name="rwkv7",
    )(pr, mu, w0, w_lora, a0, k_k, k_a, r_k, lnx_g, lnx_b, tri3, ones2)


def _t5_bucket_np(dist):
    n = np.maximum(dist, 0)
    max_exact = NUM_BUCKETS // 2
    nf = np.maximum(n, 1).astype(np.float32)
    large = max_exact + (np.log(nf / max_exact) / math.log(MAX_DISTANCE / max_exact)
                         * (NUM_BUCKETS - max_exact)).astype(np.int32)
    large = np.minimum(large, NUM_BUCKETS - 1)
    return np.where(n < max_exact, n, large).astype(np.int32)


def _near_bucket_tiles(tile):
    kpos = np.arange(tile)[:, None]
    qpos = np.arange(tile)[None, :]
    out = []
    for delta in (0, 1):
        dist = delta * tile + qpos - kpos
        out.append(np.where(dist >= 0, _t5_bucket_np(dist), -1))
    return np.stack(out).astype(np.int32)


def _attn_kernel(relb_ref, bucket_ref, lq1_ref, lk1_ref, lq2_ref, lk2_ref, sg_ref,
                 q_ref, k_ref, vt_ref, o_ref, bias_sc, q2_sc, m_sc, acc_sc, st_sc, *, lam_init):
    tq = q_ref.shape[0]
    n_heads = vt_ref.shape[0]
    heads = range(n_heads)
    qi = pl.program_id(1)

    @pl.when((pl.program_id(0) == 0) & (qi == 0))
    def _():
        for h in heads:
            for d in range(2):
                bk = bucket_ref[d]
                tile = jnp.full(bk.shape, NEG_BIG, F32)
                for j in range(NUM_BUCKETS):
                    tile = jnp.where(bk == j, relb_ref[h, j] * LOG2E, tile)
                r0 = (1 - d) * tq
                bias_sc[h, r0:r0 + tq, 0:tq] = tile
                bias_sc[h, r0:r0 + tq, tq:2 * tq] = tile

    lo = _lane_lo((tq, LANES))
    for h in heads:
        q = q_ref[:, h * LANES:(h + 1) * LANES]
        zero = jnp.zeros_like(q)
        q2_sc[h, 0:tq, :] = jnp.where(lo, q, zero)
        q2_sc[h, tq:2 * tq, :] = jnp.where(lo, zero, q)
    m_sc[...] = jnp.full(m_sc.shape, NEG_BIG, F32)
    acc_sc[...] = jnp.zeros_like(acc_sc)

    def step(k0, tk, band):
        for h in heads:
            st_sc[h, 0:tk, :] = _dot(k_ref[pl.ds(k0, tk), h * LANES:(h + 1) * LANES], q2_sc[h], NT)
        pvs, alphas = [], []
        for h in heads:
            m_old = m_sc[h]
            if band is None:
                st = st_sc[h, 0:tk, :]
                far = relb_ref[h, NUM_BUCKETS - 1] * LOG2E
                m_new = jnp.maximum(m_old, jnp.max(st, axis=0, keepdims=True) + far)
                p = jnp.exp2(st - (m_new - far))
            else:
                st = st_sc[h, 0:tk, :] + bias_sc[h, band:band + tk, :]
                m_new = jnp.maximum(m_old, jnp.max(st, axis=0, keepdims=True))
                p = jnp.exp2(st - m_new)
            m_sc[h] = m_new
            pvs.append(_dot(vt_ref[h, :, pl.ds(k0, tk)], p.astype(BF16)))
            alphas.append(jnp.exp2(m_old - m_new))
        for h in heads:
            acc_sc[h] = alphas[h] * acc_sc[h] + pvs[h]

    n_far = jnp.maximum(qi - 1, 0)
    group = FAR_GROUP * tq

    def far_body(i, carry):
        step(pl.multiple_of(i * group, group), group, None)
        return carry

    lax.fori_loop(0, n_far // FAR_GROUP, far_body, 0)
    pos = (n_far // FAR_GROUP) * FAR_GROUP
    left = n_far - pos
    size = FAR_GROUP // 2
    while size >= 1:
        take = (left // size) % 2 == 1

        @pl.when(take)
        def _(pos=pos, size=size):
            step(pl.multiple_of(pos * tq, tq), size * tq, None)

        pos = pos + jnp.where(take, size, 0)
        size //= 2

    @pl.when(qi >= 1)
    def _():
        step(pl.multiple_of((qi - 1) * tq, tq), 2 * tq, 0)

    @pl.when(qi == 0)
    def _():
        step(0, tq, tq)

    lam = (jnp.exp(jnp.sum(lq1_ref[...] * lk1_ref[...], axis=-1, keepdims=True))
           - jnp.exp(jnp.sum(lq2_ref[...] * lk2_ref[...], axis=-1, keepdims=True)) + lam_init)
    for h in heads:
        acc = acc_sc[h]
        l = acc[ACC_L:ACC_L + 1, :]
        o1 = acc[0:LANES, 0:tq] / l[:, 0:tq]
        o2 = acc[0:LANES, tq:2 * tq] / l[:, tq:2 * tq]
        ot = o1 - lam * o2
        ms = jnp.mean(ot * ot, axis=0, keepdims=True)
        y = ot * lax.rsqrt(ms + NORM_EPS) * sg_ref[...] * (1.0 - lam_init)
        o_ref[:, h * LANES:(h + 1) * LANES] = y.T.astype(o_ref.dtype)


def _diff_attn(q, k, vt, relb_t, lq1, lk1, lq2, lk2, sg_col, lam_init):
    B, S, width = q.shape
    H = width // LANES
    t = ATT_TILE
    buckets = jnp.asarray(_near_bucket_tiles(t))
    vec = pl.BlockSpec((1, HEAD_DIM), lambda b, i: (0, 0))
    return pl.pallas_call(
        functools.partial(_attn_kernel, lam_init=lam_init),
        grid=(B, S // t),
        in_specs=[
            pl.BlockSpec(memory_space=pltpu.SMEM),
            pl.BlockSpec((2, t, t), lambda b, i: (0, 0, 0)),
            vec, vec, vec, vec,
            pl.BlockSpec((LANES, 1), lambda b, i: (0, 0)),
            pl.BlockSpec((None, t, width), lambda b, i: (b, i, 0)),
            pl.BlockSpec((None, S, width), lambda b, i: (b, 0, 0)),
            pl.BlockSpec((None, H, ACC_ROWS, S), lambda b, i: (b, 0, 0, 0)),
        ],
        out_specs=pl.BlockSpec((None, t, width), lambda b, i: (b, i, 0)),
        out_shape=jax.ShapeDtypeStruct((B, S, width), BF16),
        scratch_shapes=[
            pltpu.VMEM((H, 2 * t, 2 * t), F32),
            pltpu.VMEM((H, 2 * t, LANES), BF16),
            pltpu.VMEM((H, 1, 2 * t), F32),
            pltpu.VMEM((H, ACC_ROWS, 2 * t), F32),
            pltpu.VMEM((H, FAR_GROUP * t, 2 * t), F32),
        ],
        compiler_params=_cparams(("arbitrary", "arbitrary")),
        name="diff_attn",
    )(relb_t, buckets, lq1, lk1, lq2, lk2, sg_col, q, k, vt)


def _run_rows(count):
    return jnp.ceil(count * (1.0 / RUN_CHUNK)) * RUN_CHUNK


def _outproj_kernel(x_ref, ya_ref, yb_ref, wa_ref, wb_ref, g_ref, rw_ref, rb_ref,
                    x1_ref, h2_ref, gd_ref, cnt_ref):
    x1 = x_ref[...] + _dot(ya_ref[...], wa_ref[...]) + _dot(yb_ref[...], wb_ref[...])
    x1_ref[...] = x1
    ms = jnp.mean(x1 * x1, axis=-1, keepdims=True)
    h2 = x1 * lax.rsqrt(ms + NORM_EPS) * g_ref[...]
    h2_ref[...] = h2.astype(h2_ref.dtype)
    logits = _mm(h2, rw_ref[...], mode="x3") + rb_ref[...]
    work = logits.T
    expert = lax.broadcasted_iota(jnp.int32, work.shape, 0).astype(F32)
    picks = []
    for _ in range(TOP_K):
        m = jnp.max(work, axis=0, keepdims=True)
        idx = jnp.min(jnp.where(work == m, expert, float(LANES)), axis=0, keepdims=True)
        hit = expert == idx
        picks.append((m, hit))
        work = jnp.where(hit, NEG_BIG, work)
    m0 = picks[0][0]
    es = [jnp.exp(m - m0) for m, _ in picks]
    denom = es[0] + es[1] + es[2] + es[3]
    gd_t = jnp.zeros_like(work)
    for e, (_, hit) in zip(es, picks):
        gd_t = jnp.where(hit, e / denom, gd_t)
    gd = gd_t.T
    gd_ref[...] = gd

    @pl.when(pl.program_id(0) == 0)
    def _():
        cnt_ref[...] = jnp.zeros_like(cnt_ref)

    picked = jnp.where(gd > 0.0, 1.0, 0.0)
    for t0 in range(0, gd.shape[0], MOVE_ROWS):
        cnt_ref[0:1, :] = cnt_ref[0:1, :] + _run_rows(jnp.sum(picked[t0:t0 + MOVE_ROWS], axis=0, keepdims=True))


def _outproj(x2d, ya, yb, wa, wb, g2, rw, rb):
    T, D = x2d.shape
    half = ya.shape[1]
    tm = OUT_ROWS
    row = lambda i: (i, 0)
    fix = lambda i: (0, 0)
    return pl.pallas_call(
        _outproj_kernel,
        grid=(T // tm,),
        in_specs=[
            pl.BlockSpec((tm, D), row),
            pl.BlockSpec((tm, half), row),
            pl.BlockSpec((tm, half), row),
            pl.BlockSpec((half, D), fix),
            pl.BlockSpec((half, D), fix),
            pl.BlockSpec((1, D), fix),
            pl.BlockSpec((D, LANES), fix),
            pl.BlockSpec((1, LANES), fix),
        ],
        out_specs=[
            pl.BlockSpec((tm, D), row),
            pl.BlockSpec((tm, D), row),
            pl.BlockSpec((tm, LANES), row),
            pl.BlockSpec((8, LANES), fix),
        ],
        out_shape=[
            jax.ShapeDtypeStruct((T, D), F32),
            jax.ShapeDtypeStruct((T, D), BF16),
            jax.ShapeDtypeStruct((T, LANES), F32),
            jax.ShapeDtypeStruct((8, LANES), F32),
        ],
        compiler_params=_cparams(("arbitrary",)),
        name="outproj_router",
    )(x2d, ya, yb, wa, wb, g2, rw, rb)


def _padded_rows(cnt, rb):
    return jnp.ceil(cnt * (1.0 / rb)) * rb


def _route_kernel(gd_ref, tot_ref, lp_ref, gate_ref, tab_ref, cnt_ref, carry_sc, pstart_sc, *, rb):
    i = pl.program_id(0)
    tb = gd_ref.shape[0]
    gd = gd_ref[...]
    sel = gd > 0.0
    self = jnp.where(sel, 1.0, 0.0)
    run_pad = _run_rows(jnp.sum(self, axis=0, keepdims=True))
    r2 = lax.broadcasted_iota(jnp.int32, (LANES, LANES), 0)
    c2 = lax.broadcasted_iota(jnp.int32, (LANES, LANES), 1)

    @pl.when(i == 0)
    def _():
        cnt = tot_ref[0:1, :]
        upper = jnp.where(r2 < c2, 1.0, 0.0)
        pstart = _dot(jnp.broadcast_to(_padded_rows(cnt, rb), (8, LANES)), upper, prec=HP)[0:1, :]
        pstart_sc[...] = pstart
        cnt_ref[0:1, :] = cnt
        cnt_ref[1:2, :] = pstart
        cnt_ref[2:8, :] = jnp.zeros((6, LANES), F32)
        carry_sc[...] = jnp.zeros_like(carry_sc)

    rr = lax.broadcasted_iota(jnp.int32, (tb, tb), 0)
    cc = lax.broadcasted_iota(jnp.int32, (tb, tb), 1)
    lower = jnp.where(cc < rr, 1.0, 0.0).astype(BF16)
    sel_b = self.astype(BF16)
    prefix = _dot(lower, sel_b)
    upper_b = jnp.where(r2 < c2, 1.0, 0.0).astype(BF16)
    off = _dot(jnp.broadcast_to(run_pad, (8, LANES)).astype(BF16), upper_b)[0:1, :]
    lp = off + prefix
    upper_incl = jnp.where(r2 <= c2, 1.0, 0.0).astype(BF16)
    slot = _dot(sel_b, upper_incl)
    lane = lax.broadcasted_iota(jnp.int32, (tb, LANES), 1)
    p_out = jnp.zeros((tb, LANES), F32)
    g_out = jnp.zeros((tb, LANES), F32)
    for kk in range(TOP_K):
        mk = sel & (slot == float(kk + 1))
        p_k = jnp.sum(jnp.where(mk, lp, 0.0), axis=-1, keepdims=True)
        g_k = jnp.sum(jnp.where(mk, gd, 0.0), axis=-1, keepdims=True)
        p_out = jnp.where(lane == kk, p_k, p_out)
        g_out = jnp.where(lane == kk, g_k, g_out)
    lp_ref[...] = p_out
    gate_ref[...] = g_out
    def as_column(v):
        return jnp.sum(jnp.where(r2 == c2, jnp.broadcast_to(v, (LANES, LANES)), 0.0), axis=1, keepdims=True)

    first = as_column(off * (1.0 / RUN_CHUNK))
    count = as_column(run_pad * (1.0 / RUN_CHUNK))
    start = as_column(pstart_sc[...] + carry_sc[...])
    chunk = lax.broadcasted_iota(jnp.int32, (LANES, CHUNK_LANES), 1).astype(F32)
    mine = (chunk >= first) & (chunk < first + count)
    dst = jnp.sum(jnp.where(mine, start + (chunk - first) * RUN_CHUNK, 0.0), axis=0, keepdims=True)
    pos_in_run = jnp.sum(jnp.where(mine, chunk - first, 0.0), axis=0, keepdims=True)
    run_len = jnp.sum(jnp.where(mine, count, 0.0), axis=0, keepdims=True)
    used = run_len > 0.0
    n_chunks = jnp.sum(count, axis=0, keepdims=True)
    big_part = jnp.floor(run_len * (1.0 / BIG_COPY)) * BIG_COPY
    is_big = used & (pos_in_run < big_part) & (pos_in_run == jnp.floor(pos_in_run * (1.0 / BIG_COPY)) * BIG_COPY)
    is_small = used & (pos_in_run >= big_part)
    code = dst * (256.0 / RUN_CHUNK) + chunk[0:1, :]
    c_hi = jnp.floor(code * (1.0 / 65536.0))
    c_mid = jnp.floor(code * (1.0 / 256.0)) - c_hi * 256.0
    c_lo = code - jnp.floor(code * (1.0 / 256.0)) * 256.0
    pieces = jnp.concatenate([c_hi, c_mid, c_lo, jnp.zeros((5, CHUNK_LANES), F32)], axis=0).astype(BF16)
    rq = lax.broadcasted_iota(jnp.int32, (CHUNK_LANES, CHUNK_LANES), 0)
    cq = lax.broadcasted_iota(jnp.int32, (CHUNK_LANES, CHUNK_LANES), 1)
    before = jnp.where(rq < cq, 1.0, 0.0).astype(BF16)
    slot_id = rq.astype(F32)

    def compact(mask):
        flag = jnp.where(mask, 1.0, 0.0)
        rank = _dot(jnp.broadcast_to(flag, (8, CHUNK_LANES)).astype(BF16), before)[0:1, :]
        pick = jnp.where(mask & (rank == slot_id), 1.0, 0.0).astype(BF16)
        got = _dot(pieces, pick, NT)
        return got[0:1, :] * 65536.0 + got[1:2, :] * 256.0 + got[2:3, :], jnp.sum(flag, axis=1, keepdims=True)

    big_list, n_big = compact(is_big)
    small_list, n_small = compact(is_small)
    lane_c = lax.broadcasted_iota(jnp.int32, (1, CHUNK_LANES), 1)
    counts = jnp.where(lane_c == 0, n_chunks, jnp.where(lane_c == 1, n_big, jnp.where(lane_c == 2, n_small, 0.0)))
    tab_ref[0:1, :] = big_list.astype(jnp.int32)
    tab_ref[1:2, :] = small_list.astype(jnp.int32)
    tab_ref[2:3, :] = counts.astype(jnp.int32)
    tab_ref[3:8, :] = jnp.zeros((5, CHUNK_LANES), jnp.int32)
    carry_sc[...] = carry_sc[...] + run_pad


def _route(gd, totals, rb):
    T = gd.shape[0]
    tb = MOVE_ROWS
    blk = lambda i: (i, 0)
    fix = lambda i: (0, 0)
    return pl.pallas_call(
        functools.partial(_route_kernel, rb=rb),
        grid=(T // tb,),
        in_specs=[pl.BlockSpec((tb, LANES), blk), pl.BlockSpec((8, LANES), fix)],
        out_specs=[
            pl.BlockSpec((tb, LANES), blk),
            pl.BlockSpec((tb, LANES), blk),
            pl.BlockSpec((8, CHUNK_LANES), blk),
            pl.BlockSpec((8, LANES), fix),
        ],
        out_shape=[
            jax.ShapeDtypeStruct((T, LANES), F32),
            jax.ShapeDtypeStruct((T, LANES), F32),
            jax.ShapeDtypeStruct((T // tb * 8, CHUNK_LANES), jnp.int32),
            jax.ShapeDtypeStruct((8, LANES), F32),
        ],
        scratch_shapes=[pltpu.VMEM((1, LANES), F32), pltpu.VMEM((1, LANES), F32)],
        compiler_params=_cparams(("arbitrary",)),
        name="route_positions",
    )(gd, totals)


def _rows_at(src_ref, src_row, dst_ref, dst_row, n_rows, sem):
    return pltpu.make_async_copy(src_ref.at[pl.ds(src_row, n_rows), :], dst_ref.at[pl.ds(dst_row, n_rows), :], sem)


def _start_runs(b, cnt_ref, big_ref, small_ref, copy_rows):
    def issue(tab_ref, stride, n_rows):
        def body(i, carry):
            code = tab_ref[b * stride + i]
            local = (code & 255) * RUN_CHUNK
            glob = lax.shift_right_logical(code, 8) * RUN_CHUNK
            copy_rows(pl.multiple_of(local, RUN_CHUNK), pl.multiple_of(glob, RUN_CHUNK), n_rows).start()
            return carry
        return body

    lax.fori_loop(0, cnt_ref[b * 4 + 1], issue(big_ref, BIG_LIST, BIG_COPY * RUN_CHUNK), 0)
    lax.fori_loop(0, cnt_ref[b * 4 + 2], issue(small_ref, SMALL_LIST, RUN_CHUNK), 0)


def _wait_runs(b, nc_ref, wait_rows):
    total = nc_ref[b * 4]
    p = 1
    while p <= LOCAL_CHUNKS:
        @pl.when((total & p) != 0)
        def _(p=p):
            wait_rows(p * RUN_CHUNK).wait()
        p *= 2


def _rows_copy(src_ref, dst_ref, n_rows, sem):
    return pltpu.make_async_copy(src_ref.at[pl.ds(0, n_rows), :], dst_ref.at[pl.ds(0, n_rows), :], sem)


def _local_onehot(lp, lb):
    j = lax.broadcasted_iota(jnp.int32, (lp.shape[0], lb), 1).astype(F32)
    hit = lp[:, 0:1] == j
    for kk in range(1, TOP_K):
        hit = hit | (lp[:, kk:kk + 1] == j)
    return hit


def _scatter_kernel(nc_ref, big_ref, small_ref, t0_ref, tc_ref, lp_ref, h_ref, xs_ref, buf, zero_buf, sems):
    b = pl.program_id(0)
    slot = b % 2
    pt = jnp.where(_local_onehot(lp_ref[...], buf.shape[1]), 1.0, 0.0).astype(BF16)
    buf[slot] = _dot(pt, h_ref[...], TN)
    _start_runs(b, nc_ref, big_ref, small_ref,
                lambda s, d, n: _rows_at(buf.at[slot], s, xs_ref, d, n, sems.at[slot]))

    @pl.when(b >= 1)
    def _():
        _wait_runs(b - 1, nc_ref, lambda n: _rows_copy(buf.at[1 - slot], xs_ref, n, sems.at[1 - slot]))

    @pl.when(b == pl.num_programs(0) - 1)
    def _():
        _wait_runs(b, nc_ref, lambda n: _rows_copy(buf.at[slot], xs_ref, n, sems.at[slot]))
        zero_buf[...] = jnp.zeros_like(zero_buf)
        total = 0
        for e in range(N_EXPERTS + 1):
            t0 = t0_ref[e]

            def issue(c, carry, t0=t0):
                _rows_at(zero_buf, 0, xs_ref, pl.multiple_of(t0 + c * RUN_CHUNK, RUN_CHUNK), RUN_CHUNK,
                         sems.at[0]).start()
                return carry

            lax.fori_loop(0, tc_ref[e], issue, 0)
            total = total + tc_ref[e]

        def drain(c, carry):
            _rows_at(zero_buf, 0, xs_ref, 0, RUN_CHUNK, sems.at[0]).wait()
            return carry

        lax.fori_loop(0, total, drain, 0)


def _scatter_rows(tabs, tails, lp, h2, n_rows):
    T, D = h2.shape
    tb = MOVE_ROWS
    return pl.pallas_call(
        _scatter_kernel,
        grid_spec=pltpu.PrefetchScalarGridSpec(
            num_scalar_prefetch=5,
            grid=(T // tb,),
            in_specs=[
                pl.BlockSpec((tb, LANES), lambda i, *_: (i, 0)),
                pl.BlockSpec((tb, D), lambda i, *_: (i, 0)),
            ],
            out_specs=pl.BlockSpec(memory_space=pl.ANY),
            scratch_shapes=[pltpu.VMEM((2, LOCAL_ROWS, D), F32), pltpu.VMEM((RUN_CHUNK, D), F32),
                            pltpu.SemaphoreType.DMA((2,))],
        ),
        out_shape=jax.ShapeDtypeStruct((n_rows, D), F32),
        compiler_params=_cparams(("arbitrary",)),
        name="scatter_rows",
    )(*tabs, *tails, lp, h2)


def _moe_kernel(be_ref, na_ref, slot_ref, nxt_ref, valid_ref, xs_ref, w1_hbm, b1_ref, w2_hbm, b2_ref, ys_ref,
                w1f, w2f, w1b, w2b, sems):
    j = pl.program_id(0)
    e = be_ref[j]
    slot = slot_ref[j]

    def fetch(expert, sl):
        return (pltpu.make_async_copy(w1_hbm.at[expert], w1f.at[sl], sems.at[0, sl]),
                pltpu.make_async_copy(w2_hbm.at[expert], w2f.at[sl], sems.at[1, sl]))

    @pl.when(j == 0)
    def _():
        for cp in fetch(e, slot):
            cp.start()

    @pl.when((j == 0) | (e != be_ref[jnp.maximum(j - 1, 0)]))
    def _():
        for cp in fetch(e, slot):
            cp.wait()
        w1b[...] = w1f[slot].astype(BF16)
        w2b[...] = w2f[slot].astype(BF16)

        @pl.when(nxt_ref[j] >= 0)
        def _():
            for cp in fetch(nxt_ref[j], 1 - slot):
                cp.start()

    def expert_mlp(rows):
        xb = xs_ref[0:rows, :].astype(BF16)
        hid = _dot(xb, w1b[...]) + b1_ref[...]
        half = hid.shape[1] // 2
        x_glu = jnp.minimum(hid[:, :half], SWIGLU_LIMIT)
        x_lin = jnp.clip(hid[:, half:], -SWIGLU_LIMIT, SWIGLU_LIMIT)
        act = x_glu * _sigmoid(SWIGLU_ALPHA * x_glu) * (x_lin + 1.0)
        ys_ref[0:rows, :] = _dot(act.astype(BF16), w2b[...]) + b2_ref[...]

    rb = xs_ref.shape[0]
    valid = valid_ref[j]

    @pl.when(valid > rb // 2)
    def _():
        expert_mlp(rb)

    @pl.when((valid > 0) & (valid <= rb // 2))
    def _():
        expert_mlp(rb // 2)
        ys_ref[rb // 2:rb, :] = jnp.zeros((rb - rb // 2, ys_ref.shape[1]), ys_ref.dtype)

    @pl.when(valid <= 0)
    def _():
        ys_ref[...] = jnp.zeros_like(ys_ref)


def _expert_runs(blk_expert):
    n = blk_expert.shape[0]
    idx = jnp.arange(n)
    starts = jnp.concatenate([jnp.ones((1,), bool), blk_expert[1:] != blk_expert[:-1]])
    slot = (jnp.cumsum(starts) - 1) % 2
    next_start = lax.cummin(jnp.where(starts, idx, n)[::-1])[::-1]
    after = jnp.concatenate([next_start[1:], jnp.full((1,), n)])
    nxt = jnp.where(after < n, blk_expert[jnp.minimum(after, n - 1)], -1)
    return slot.astype(jnp.int32), nxt.astype(jnp.int32)


def _moe(blk_expert, n_active, valid, xs, w1, b1, w2, b2):
    R, D = xs.shape
    E, _, D2 = w1.shape
    rb = MOE_ROWS
    nb = R // rb
    slot, nxt = _expert_runs(blk_expert)
    blk = lambda j, be, na, *_: (jnp.minimum(j, na[0] - 1), 0)
    return pl.pallas_call(
        _moe_kernel,
        grid_spec=pltpu.PrefetchScalarGridSpec(
            num_scalar_prefetch=5,
            grid=(nb,),
            in_specs=[
                pl.BlockSpec((rb, D), blk),
                pl.BlockSpec(memory_space=pl.ANY),
                pl.BlockSpec((None, 1, D2), lambda j, be, *_: (be[j], 0, 0)),
                pl.BlockSpec(memory_space=pl.ANY),
                pl.BlockSpec((None, 1, D), lambda j, be, *_: (be[j], 0, 0)),
            ],
            out_specs=pl.BlockSpec((rb, D), lambda j, *_: (j, 0)),
            scratch_shapes=[pltpu.VMEM((2, D, D2), F32), pltpu.VMEM((2, D2 // 2, D), F32),
                            pltpu.VMEM((D, D2), BF16), pltpu.VMEM((D2 // 2, D), BF16),
                            pltpu.SemaphoreType.DMA((2, 2))],
        ),
        out_shape=jax.ShapeDtypeStruct((R, D), F32),
        compiler_params=_cparams(("arbitrary",)),
        name="moe_experts",
    )(blk_expert, n_active, slot, nxt, valid, xs, w1, b1, w2, b2)


def _combine_kernel(nc_ref, big_ref, small_ref, lp_ref, gate_ref, x1_ref, ys_ref, o_ref, buf, sems):
    b = pl.program_id(0)
    slot = b % 2

    def fetch(blk, sl):
        _start_runs(blk, nc_ref, big_ref, small_ref,
                    lambda s, d, n: _rows_at(ys_ref, d, buf.at[sl], s, n, sems.at[sl]))

    @pl.when(b == 0)
    def _():
        buf[...] = jnp.zeros_like(buf)
        fetch(b, slot)

    @pl.when(b + 1 < pl.num_programs(0))
    def _():
        fetch(b + 1, 1 - slot)

    _wait_runs(b, nc_ref, lambda n: _rows_copy(ys_ref, buf.at[slot], n, sems.at[slot]))
    lp = lp_ref[...]
    g = gate_ref[...]
    _, lb, d = buf.shape
    j = lax.broadcasted_iota(jnp.int32, (lp.shape[0], lb), 1).astype(F32)
    gt = jnp.zeros(j.shape, F32)
    for kk in range(TOP_K):
        gt = jnp.where(lp[:, kk:kk + 1] == j, g[:, kk:kk + 1], gt)
    g_hi, g_lo = _split2(gt)
    ones = jnp.ones((lp.shape[0], LANES), BF16)
    g_row = _dot(g_hi, ones, TN) + _dot(g_lo, ones, TN)
    rows = (buf[slot] * jnp.concatenate([g_row] * (d // LANES), axis=1)).astype(BF16)
    p01 = jnp.where(gt != 0.0, 1.0, 0.0).astype(BF16)
    o_ref[...] = x1_ref[...] + _dot(p01, rows)


def _combine(tabs, lp, gate, x1, ys):
    T, D = x1.shape
    tb = MOVE_ROWS
    return pl.pallas_call(
        _combine_kernel,
        grid_spec=pltpu.PrefetchScalarGridSpec(
            num_scalar_prefetch=3,
            grid=(T // tb,),
            in_specs=[
                pl.BlockSpec((tb, LANES), lambda i, *_: (i, 0)),
                pl.BlockSpec((tb, LANES), lambda i, *_: (i, 0)),
                pl.BlockSpec((tb, D), lambda i, *_: (i, 0)),
                pl.BlockSpec(memory_space=pl.ANY),
            ],
            out_specs=pl.BlockSpec((tb, D), lambda i, *_: (i, 0)),
            scratch_shapes=[pltpu.VMEM((2, LOCAL_ROWS, D), F32), pltpu.SemaphoreType.DMA((2,))],
        ),
        out_shape=jax.ShapeDtypeStruct((T, D), F32),
        compiler_params=_cparams(("arbitrary",)),
        name="combine_rows",
    )(*tabs, lp, gate, x1, ys)


def _pad_cols(a, n):
    return jnp.pad(a, ((0, 0), (0, n - a.shape[1])))


def _pad_rows(a, n):
    return jnp.pad(a, ((0, n - a.shape[0]), (0, 0)))


def _lora_weight(w2, a2, g2, width):
    z = jnp.zeros((LANES, width), F32)
    return jnp.concatenate([
        jnp.concatenate([_pad_rows(w2, LANES), z, z], axis=1),
        jnp.concatenate([z, _pad_rows(a2, LANES), z], axis=1),
        jnp.concatenate([z, z, _pad_rows(g2, LANES)], axis=1),
    ], axis=0).astype(BF16)


def _split_rwkv_cols(a, width):
    o = 3 * width
    return jnp.concatenate([
        a[:, :o],
        _pad_cols(a[:, o:o + DECAY_LORA], LANES),
        _pad_cols(a[:, o + DECAY_LORA:o + DECAY_LORA + AAA_LORA], LANES),
        _pad_cols(a[:, o + DECAY_LORA + AAA_LORA:o + DECAY_LORA + AAA_LORA + GATE_LORA], LANES),
    ], axis=1)


def kernel(x, norm1_g, w_in, mu_shift, w0, w2, a0, a2, g2, k_k, k_a, r_k, lnx_g, lnx_b, qn_g, kn_g,
           lam_q1, lam_k1, lam_q2, lam_k2, subln_g, rel_bias, w_out, norm2_g, router_w, router_b,
           exp_w1, exp_b1, exp_w2, exp_b2):
    B, S, D = x.shape
    T = B * S
    depth = norm1_g.shape[0]
    width = w0.shape[1]
    n_rwkv_cols = 3 * width + DECAY_LORA + AAA_LORA + GATE_LORA
    n_rwkv_pad = 3 * width + 3 * LANES
    n_groups = width // HEAD_DIM
    H = width // LANES
    relb_t = rel_bias.T
    row = lambda a: a.reshape(1, -1)

    for layer in range(depth):
        lam_init = 0.8 - 0.6 * math.exp(-0.3 * layer)
        w_l = w_in[layer]
        w_all = jnp.concatenate([_split_rwkv_cols(w_l[:, :n_rwkv_cols], width), w_l[:, n_rwkv_cols:]],
                                axis=1).astype(BF16)
        mu = _split_rwkv_cols(row(mu_shift[layer]), width)
        qg = row(jnp.tile(qn_g[layer], n_groups))
        kg = row(jnp.tile(kn_g[layer], n_groups))

        pr, q, k, vt = _inproj(x.reshape(T, D), row(norm1_g[layer]), w_all, qg, kg, n_rwkv_pad, width, S)

        y_a = _rwkv(pr.reshape(B, S, n_rwkv_pad), mu, row(w0[layer]),
                    _lora_weight(w2[layer], a2[layer], g2[layer], width), row(a0[layer]),
                    row(k_k[layer]), row(k_a[layer]), row(r_k[layer]), row(lnx_g[layer]),
                    row(lnx_b[layer]), width)

        y_b = _diff_attn(q.reshape(B, S, width), k.reshape(B, S, width), vt, relb_t,
                         row(lam_q1[layer]), row(lam_k1[layer]), row(lam_q2[layer]), row(lam_k2[layer]),
                         subln_g[layer].reshape(-1, 1), lam_init)

        wo = w_out[layer].astype(BF16)
        rw = _pad_cols(router_w[layer], LANES)
        rb = jnp.concatenate([router_b[layer], jnp.full((LANES - N_EXPERTS,), NEG_BIG, F32)]).reshape(1, -1)
        x1, h2, gd, totals = _outproj(x.reshape(T, D), y_a.reshape(T, width), y_b.reshape(T, width),
                                      wo[:width], wo[width:], row(norm2_g[layer]), rw, rb)

        lp, gate, tab, cnt = _route(gd, totals, MOE_ROWS)
        tab = tab.reshape(-1, 8, CHUNK_LANES)
        tabs = (tab[:, 2, :4].reshape(-1),
                tab[:, 0, :BIG_LIST].reshape(-1), tab[:, 1, :SMALL_LIST].reshape(-1))
        run_end = cnt[1, :N_EXPERTS] + cnt[0, :N_EXPERTS]
        pad_end = cnt[1, :N_EXPERTS] + _padded_rows(cnt[0, :N_EXPERTS], MOE_ROWS)
        run_pad_max = (T // MOVE_ROWS) * N_EXPERTS * (RUN_CHUNK - 1)
        n_blocks = -(-(T * TOP_K + run_pad_max + N_EXPERTS * (MOE_ROWS - 1)) // MOE_ROWS)
        gap_start = jnp.concatenate([run_end, pad_end[-1:]])
        gap_end = jnp.concatenate([pad_end, jnp.full((1,), n_blocks * MOE_ROWS, F32)])
        tails = (gap_start.astype(jnp.int32), ((gap_end - gap_start) / RUN_CHUNK).astype(jnp.int32))
        blk_start = (jnp.arange(n_blocks) * MOE_ROWS).astype(F32)
        blk_expert = jnp.minimum(jnp.sum(pad_end[None, :] <= blk_start[:, None], axis=1), N_EXPERTS - 1)
        n_active = (pad_end[N_EXPERTS - 1] / MOE_ROWS).astype(jnp.int32).reshape(1)

        xs = _scatter_rows(tabs, tails, lp, h2, n_blocks * MOE_ROWS)
        blk_valid = jnp.clip(run_end[blk_expert] - blk_start, 0, MOE_ROWS)
        ys = _moe(blk_expert.astype(jnp.int32), n_active, blk_valid.astype(jnp.int32), xs,
                  exp_w1[layer], exp_b1[layer][:, None, :], exp_w2[layer], exp_b2[layer][:, None, :])
        x = _combine(tabs, lp, gate, x1, ys).reshape(B, S, D)
    return x
```

```python
import functools
import math

import numpy as np
import jax
import jax.numpy as jnp
from jax import lax
from jax.experimental import pallas as pl
from jax.experimental.pallas import tpu as pltpu

F32 = jnp.float32
BF16 = jnp.bfloat16
HP = lax.Precision.HIGHEST

HEAD_DIM = 64
DECAY_LORA = 32
AAA_LORA = 32
GATE_LORA = 96
NUM_BUCKETS = 32
MAX_DISTANCE = 128
N_EXPERTS = 32
TOP_K = 4
SWIGLU_LIMIT = 7.0
SWIGLU_ALPHA = 1.702
NORM_EPS = 1e-5
LNX_EPS = 64e-5

LANES = 128
VMEM_LIMIT_BYTES = 56 * 1024 * 1024

NEG_BIG = -1e30
LOG2E = math.log2(math.e)
ACC_L = LANES
ACC_ROWS = LANES + 8

PROJ_ROWS = 512
RWKV_CHUNK = 64
RWKV_ROWS = 128
RWKV_SEQS = 4
ATT_TILE = 256
FAR_GROUP = 4
MOE_ROWS = 512
MOVE_ROWS = 256
OUT_ROWS = 256
RUN_CHUNK = 8
LOCAL_ROWS = MOVE_ROWS * 4 + 32 * RUN_CHUNK
LOCAL_CHUNKS = LOCAL_ROWS // RUN_CHUNK
CHUNK_LANES = 256
BIG_COPY = 4
BIG_LIST = 64
SMALL_LIST = 128

NN = (((1,), (0,)), ((), ()))
NT = (((1,), (1,)), ((), ()))
TN = (((0,), (0,)), ((), ()))


def _dot(a, b, dims=NN, prec=None):
    return lax.dot_general(a, b, dims, precision=prec, preferred_element_type=F32)


def _cparams(sem):
    return pltpu.CompilerParams(dimension_semantics=sem, vmem_limit_bytes=VMEM_LIMIT_BYTES)


def _sigmoid(x):
    return 1.0 / (1.0 + jnp.exp(-x))


def _lane_lo(shape):
    lane = lax.broadcasted_iota(jnp.int32, shape, len(shape) - 1)
    return (lane % LANES) < HEAD_DIM


def _head_rms_normalize(x, gain, scale):
    outs = []
    for t in range(x.shape[1] // LANES):
        xt = x[:, t * LANES:(t + 1) * LANES]
        lo = _lane_lo(xt.shape)
        x2 = xt * xt
        s0 = jnp.sum(jnp.where(lo, x2, 0.0), axis=-1, keepdims=True)
        s1 = jnp.sum(jnp.where(lo, 0.0, x2), axis=-1, keepdims=True)
        inv = lax.rsqrt(jnp.where(lo, s0, s1) * (1.0 / HEAD_DIM) + NORM_EPS)
        outs.append(xt * inv)
    return jnp.concatenate(outs, axis=-1) * gain * scale


def _inproj_kernel(x_ref, g_ref, w_ref, qg_ref, kg_ref, pr_ref, q_ref, k_ref, vt_ref, *, n_rwkv, width):
    x = x_ref[...]
    ms = jnp.mean(x * x, axis=-1, keepdims=True)
    h = (x * lax.rsqrt(ms + NORM_EPS) * g_ref[...]).astype(BF16)
    step = 384
    for c0 in range(0, n_rwkv, step):
        pr_ref[:, c0:c0 + step] = _dot(h, w_ref[:, c0:c0 + step])
    q = _dot(h, w_ref[:, n_rwkv:n_rwkv + width])
    q_ref[...] = _head_rms_normalize(q, qg_ref[...], HEAD_DIM ** -0.5 * LOG2E).astype(BF16)
    k = _dot(h, w_ref[:, n_rwkv + width:n_rwkv + 2 * width])
    k_ref[...] = _head_rms_normalize(k, kg_ref[...], 1.0).astype(BF16)
    v = _dot(h, w_ref[:, n_rwkv + 2 * width:n_rwkv + 3 * width])
    for hd in range(width // LANES):
        vt_ref[hd, 0:LANES, :] = v[:, hd * LANES:(hd + 1) * LANES].T.astype(BF16)
        vt_ref[hd, LANES:ACC_ROWS, :] = jnp.ones((ACC_ROWS - LANES, v.shape[0]), BF16)


def _inproj(x2d, g, w_all, qg, kg, n_rwkv, width, seq_len):
    T, D = x2d.shape
    n_all = w_all.shape[1]
    tm = PROJ_ROWS
    H = width // LANES
    per_seq = seq_len // tm
    row = lambda i: (i, 0)
    fix = lambda i: (0, 0)
    return pl.pallas_call(
        functools.partial(_inproj_kernel, n_rwkv=n_rwkv, width=width),
        grid=(T // tm,),
        in_specs=[
            pl.BlockSpec((tm, D), row),
            pl.BlockSpec((1, D), fix),
            pl.BlockSpec((D, n_all), fix),
            pl.BlockSpec((1, width), fix),
            pl.BlockSpec((1, width), fix),
        ],
        out_specs=[
            pl.BlockSpec((tm, n_rwkv), row),
            pl.BlockSpec((tm, width), row),
            pl.BlockSpec((tm, width), row),
            pl.BlockSpec((None, H, ACC_ROWS, tm), lambda i: (i // per_seq, 0, 0, i % per_seq)),
        ],
        out_shape=[
            jax.ShapeDtypeStruct((T, n_rwkv), F32),
            jax.ShapeDtypeStruct((T, width), BF16),
            jax.ShapeDtypeStruct((T, width), BF16),
            jax.ShapeDtypeStruct((T // seq_len, H, ACC_ROWS, seq_len), BF16),
        ],
        compiler_params=_cparams(("arbitrary",)),
        name="inproj",
    )(x2d, g, w_all, qg, kg)


def _split2(x):
    hi = x.astype(BF16)
    return hi, (x - hi.astype(F32)).astype(BF16)


def _mm(a, b, dims=NN, mode="bf16"):
    if mode == "bf16":
        return _dot(a.astype(BF16), b.astype(BF16), dims)
    ah, al = _split2(a)
    bh, bl = _split2(b)
    lhs = jnp.concatenate([ah, ah, al], axis=dims[0][0][0])
    rhs = jnp.concatenate([bh, bl, bh], axis=dims[0][1][0])
    return _dot(lhs, rhs, dims)


def _group_sum(x, ones2):
    hi, lo = _split2(x)
    return _dot(jnp.concatenate([hi, lo], axis=1), ones2)


P_INV = "x3"
P_SUBST = "bf16"
P_STATE = "bf16"
INV_BLOCK = 16
QUAD = 256


def _rwkv_kernel(pr_ref, mu_ref, w0_ref, wl_ref, a0_ref, kk_ref, ka_ref, rk_ref, lg_ref, lb_ref,
                 tri3_ref, ones2_ref, o_ref, state_sc, prev_sc, *, width):
    C = RWKV_CHUNK
    n_seq, seq_rows, _ = pr_ref.shape
    rows = n_seq * seq_rows
    n_quads = width // QUAD
    n_chunks = seq_rows // C
    heads_q = QUAD // HEAD_DIM

    @pl.when(pl.program_id(1) == 0)
    def _():
        state_sc[...] = jnp.zeros_like(state_sc)
        prev_sc[...] = jnp.zeros_like(prev_sc)

    x = jnp.concatenate([pr_ref[s] for s in range(n_seq)], axis=0)
    rid = lax.broadcasted_iota(jnp.int32, x.shape, 0)
    prev = pltpu.roll(x, 1, axis=0)
    for s in range(n_seq):
        prev = jnp.where(rid == s * seq_rows, prev_sc[s:s + 1, :], prev)
        prev_sc[s:s + 1, :] = x[(s + 1) * seq_rows - 1:(s + 1) * seq_rows, :]
    xs = x + (prev - x) * mu_ref[...]

    r = xs[:, 0:width]
    k = xs[:, width:2 * width]
    v = xs[:, 2 * width:3 * width]
    o = 3 * width
    lora_in = jnp.concatenate([jnp.tanh(xs[:, o:o + LANES]), xs[:, o + LANES:o + 2 * LANES],
                               _sigmoid(xs[:, o + 2 * LANES:o + 3 * LANES])], axis=1).astype(BF16)
    lora = _dot(lora_in, wl_ref[...])
    z = -(w0_ref[...] + lora[:, 0:width])
    softplus = jnp.maximum(z, 0.0) + jnp.log(1.0 + jnp.exp(-jnp.abs(z)))
    log_decay = -jnp.exp(-softplus - 0.5)
    a_lr = _sigmoid(a0_ref[...] + lora[:, width:2 * width])
    gate = lora[:, 2 * width:3 * width]

    ones2 = ones2_ref[...]
    kk = k * kk_ref[...]
    kk = kk * lax.rsqrt(jnp.maximum(_group_sum(kk * kk, ones2), 1e-24))
    k = k * (1.0 + (a_lr - 1.0) * ka_ref[...])
    a_vec = -kk
    b_vec = kk * a_lr

    p1 = log_decay.astype(BF16)
    rem = log_decay - p1.astype(F32)
    p2 = rem.astype(BF16)
    p3 = (rem - p2.astype(F32)).astype(BF16)
    cum = _dot(tri3_ref[...], jnp.concatenate([p1, p2, p3], axis=0))

    t_i = lax.broadcasted_iota(jnp.int32, (C, 2 * QUAD), 0)
    s_i = lax.broadcasted_iota(jnp.int32, (C, 2 * QUAD), 1) % C
    strict2 = t_i > s_i
    incl2 = t_i >= s_i
    lane_head = lax.broadcasted_iota(jnp.int32, (C, QUAD), 1) // HEAD_DIM
    row_blk = lax.broadcasted_iota(jnp.int32, (C, QUAD), 0) // INV_BLOCK
    same_blk = row_blk == ((lax.broadcasted_iota(jnp.int32, (C, QUAD), 1) % C) // INV_BLOCK)
    r2 = lax.broadcasted_iota(jnp.int32, (QUAD, QUAD), 0)
    c2 = lax.broadcasted_iota(jnp.int32, (QUAD, QUAD), 1)
    same_head = (r2 // HEAD_DIM) == (c2 // HEAD_DIM)

    def by_head(m):
        zero = jnp.zeros_like(m)
        return jnp.concatenate([jnp.where(lane_head == h, m, zero) for h in range(heads_q)], axis=0)

    def bd_rhs(m, mode):
        if mode == "bf16":
            return by_head(m.astype(BF16))
        hi, lo = _split2(m)
        hi_bd = by_head(hi)
        return jnp.concatenate([hi_bd, by_head(lo), hi_bd], axis=0)

    def mm_bd(a, rhs, mode):
        if mode == "bf16":
            return _dot(a.astype(BF16), rhs)
        ah, al = _split2(a)
        return _dot(jnp.concatenate([ah, ah, al], axis=1), rhs)

    class Work:
        pass

    def make_works(s):
        ws = []
        for c in range(n_chunks):
            for q in range(n_quads):
                w = Work()
                ls = slice(q * QUAD, (q + 1) * QUAD)
                rs = slice(s * seq_rows + c * C, s * seq_rows + (c + 1) * C)
                ld = log_decay[rs, ls]
                cm = cum[rs, ls]
                cl = cm[C - 1:C, :]
                e_in = jnp.exp(cm)
                e_neg = jnp.exp(-cm)
                e_hat = jnp.exp(cl - cm)
                w.q = q
                w.dec = jnp.exp(cl)
                w.vv = v[rs, ls]
                w.ar_t = jnp.concatenate([a_vec[rs, ls] * jnp.exp(cm - ld), r[rs, ls] * e_in], axis=0)
                w.bk_t = jnp.concatenate([bd_rhs(b_vec[rs, ls] * e_neg, "bf16"),
                                          bd_rhs(k[rs, ls] * e_neg, "bf16")], axis=0)
                w.bk_h = jnp.concatenate([b_vec[rs, ls] * e_hat, k[rs, ls] * e_hat], axis=0)
                ws.append(w)
        return ws

    def pre_stages(ws):
        def gram():
            for w in ws:
                g = _dot(w.ar_t.astype(BF16), w.bk_t, NT)
                low = jnp.where(strict2, g[0:C, :], 0.0)
                l_cat = low[:, 0:QUAD]
                rbk = jnp.where(incl2, g[C:2 * C, :], 0.0)
                w.rb_cat = rbk[:, 0:QUAD]
                w.kv_lhs = jnp.concatenate([low[:, QUAD:], rbk[:, QUAD:]], axis=0)
                w.l_off = jnp.where(same_blk, 0.0, l_cat)
                w.d_inv = jnp.where(same_blk, l_cat, 0.0)

        def square0():
            for w in ws:
                w.l_pow = mm_bd(w.d_inv, bd_rhs(w.d_inv, P_INV), P_INV)

        def square(last):
            def run():
                for w in ws:
                    if last:
                        w.d_inv = w.d_inv + w.l_pow + mm_bd(w.d_inv, bd_rhs(w.l_pow, P_INV), P_INV)
                    else:
                        both = mm_bd(jnp.concatenate([w.d_inv, w.l_pow], axis=0), bd_rhs(w.l_pow, P_INV), P_INV)
                        w.d_inv = w.d_inv + w.l_pow + both[0:C]
                        w.l_pow = both[C:2 * C]
            return run

        def neumann1():
            for w in ws:
                w.m1 = w.l_off + mm_bd(w.d_inv, bd_rhs(w.l_off, P_SUBST), P_SUBST)
                w.m1_bd = bd_rhs(w.m1, P_SUBST)

        def neumann2():
            for w in ws:
                w.m2 = mm_bd(w.m1, w.m1_bd, P_SUBST)

        def neumann3():
            for w in ws:
                w.nn = w.m1 + w.m2 + mm_bd(w.m2, w.m1_bd, P_SUBST)

        def finish():
            for w in ws:
                w.t_inv = w.nn + w.d_inv + mm_bd(w.nn, bd_rhs(w.d_inv, P_SUBST), P_SUBST)

        def values():
            for w in ws:
                w.kv = mm_bd(w.kv_lhs, bd_rhs(w.vv, P_STATE), P_STATE)

        n_sq = int(math.log2(INV_BLOCK))
        return ([gram, square0] + [square(it + 1 == n_sq) for it in range(1, n_sq)]
                + [neumann1, neumann2, neumann3, finish, values])

    def chain_levels(s, ws, y_parts):
        levels = []
        for c in range(n_chunks):
            wc = ws[c * n_quads:(c + 1) * n_quads]

            def base(wc=wc):
                for w in wc:
                    st = state_sc[s * n_quads + w.q]
                    w.base = _mm(w.ar_t, st, NT, mode=P_STATE) + w.kv

            def solve(wc=wc):
                for w in wc:
                    u0 = w.base[0:C]
                    w.u = u0 + mm_bd(w.t_inv, bd_rhs(u0, P_STATE), P_STATE)

            def update(wc=wc):
                for w in wc:
                    upd = _mm(jnp.concatenate([w.u, w.vv], axis=0), w.bk_h, TN, mode=P_STATE)
                    sq = s * n_quads + w.q
                    state_sc[sq] = state_sc[sq] * w.dec + jnp.where(same_head, upd, 0.0)
                for w in wc:
                    y_parts[w.q].append(w.base[C:2 * C] + mm_bd(w.rb_cat, bd_rhs(w.u, P_STATE), P_STATE))

            levels += [base, solve, update]
        return levels

    def post_stages(s, y_parts):
        rs = slice(s * seq_rows, (s + 1) * seq_rows)
        box = Work()

        def sums():
            box.y = jnp.concatenate([jnp.concatenate(y_parts[q], axis=0) for q in range(n_quads)], axis=-1)
            box.sums = _group_sum(jnp.concatenate([box.y, r[rs] * k[rs] * rk_ref[...]], axis=0), ones2)

        def variance():
            box.yc = box.y - box.sums[0:seq_rows] * (1.0 / HEAD_DIM)
            box.var = _group_sum(box.yc * box.yc, ones2) * (1.0 / HEAD_DIM)

        def write():
            yn = box.yc * lax.rsqrt(box.var + LNX_EPS) * lg_ref[...] + lb_ref[...]
            o_ref[s] = ((yn + box.sums[seq_rows:2 * seq_rows] * v[rs]) * gate[rs]).astype(o_ref.dtype)

        return [sums, variance, write]

    works = [make_works(s) for s in range(n_seq)]
    pres = [pre_stages(ws) for ws in works]
    y_all = [[[] for _ in range(n_quads)] for _ in range(n_seq)]
    for group in (pres, [chain_levels(s, works[s], y_all[s]) for s in range(n_seq)],
                  [post_stages(s, y_all[s]) for s in range(n_seq)]):
        for stages in zip(*group):
            for stage in stages:
                stage()


def _rwkv(pr, mu, w0, w_lora, a0, k_k, k_a, r_k, lnx_g, lnx_b, width):
    B, S, n_rwkv = pr.shape
    rows = RWKV_ROWS
    n_seq = RWKV_SEQS
    C = RWKV_CHUNK
    rr = np.arange(n_seq * rows)[:, None]
    cc = np.arange(n_seq * rows)[None, :]
    tri = ((rr // C == cc // C) & (cc <= rr)).astype(np.float32)
    tri3 = jnp.asarray(np.concatenate([tri, tri, tri], axis=1), BF16)
    gg = np.arange(width)
    ones = (gg[:, None] // HEAD_DIM == gg[None, :] // HEAD_DIM).astype(np.float32)
    ones2 = jnp.asarray(np.concatenate([ones, ones], axis=0), BF16)
    fix = lambda shape: pl.BlockSpec(shape, lambda b, i: (0, 0))
    vec = lambda n: fix((1, n))
    return pl.pallas_call(
        functools.partial(_rwkv_kernel, width=width),
        grid=(B // n_seq, S // rows),
        in_specs=[
            pl.BlockSpec((n_seq, rows, n_rwkv), lambda b, i: (b, i, 0)),
            vec(n_rwkv), vec(width), fix(w_lora.shape), vec(width),
            vec(width), vec(width), vec(width), vec(width), vec(width),
            fix(tri3.shape), fix(ones2.shape),
        ],
        out_specs=pl.BlockSpec((n_seq, rows, width), lambda b, i: (b, i, 0)),
        out_shape=jax.ShapeDtypeStruct((B, S, width), BF16),
        scratch_shapes=[
            pltpu.VMEM((n_seq * (width // QUAD), QUAD, QUAD), F32),
            pltpu.VMEM((n_seq, n_rwkv), F32),
        ],
        compiler_params=_cparams(("arbitrary", "arbitrary")),
        name="rwkv7",
    )(pr, mu, w0, w_lora, a0, k_k, k_a, r_k, lnx_g, lnx_b, tri3, ones2)


def _t5_bucket_np(dist):
    n = np.maximum(dist, 0)
    max_exact = NUM_BUCKETS // 2
    nf = np.maximum(n, 1).astype(np.float32)
    large = max_exact + (np.log(nf / max_exact) / math.log(MAX_DISTANCE / max_exact)
                         * (NUM_BUCKETS - max_exact)).astype(np.int32)
    large = np.minimum(large, NUM_BUCKETS - 1)
    return np.where(n < max_exact, n, large).astype(np.int32)


def _near_bucket_tiles(tile):
    kpos = np.arange(tile)[:, None]
    qpos = np.arange(tile)[None, :]
    out = []
    for delta in (0, 1):
        dist = delta * tile + qpos - kpos
        out.append(np.where(dist >= 0, _t5_bucket_np(dist), -1))
    return np.stack(out).astype(np.int32)


def _attn_kernel(relb_ref, bucket_ref, lq1_ref, lk1_ref, lq2_ref, lk2_ref, sg_ref,
                 q_ref, k_ref, vt_ref, o_ref, bias_sc, q2_sc, m_sc, acc_sc, st_sc, *, lam_init):
    tq = q_ref.shape[0]
    n_heads = vt_ref.shape[0]
    heads = range(n_heads)
    qi = pl.program_id(1)

    @pl.when((pl.program_id(0) == 0) & (qi == 0))
    def _():
        for h in heads:
            for d in range(2):
                bk = bucket_ref[d]
                tile = jnp.full(bk.shape, NEG_BIG, F32)
                for j in range(NUM_BUCKETS):
                    tile = jnp.where(bk == j, relb_ref[h, j] * LOG2E, tile)
                r0 = (1 - d) * tq
                bias_sc[h, r0:r0 + tq, 0:tq] = tile
                bias_sc[h, r0:r0 + tq, tq:2 * tq] = tile

    lo = _lane_lo((tq, LANES))
    for h in heads:
        q = q_ref[:, h * LANES:(h + 1) * LANES]
        zero = jnp.zeros_like(q)
        q2_sc[h, 0:tq, :] = jnp.where(lo, q, zero)
        q2_sc[h, tq:2 * tq, :] = jnp.where(lo, zero, q)
    m_sc[...] = jnp.full(m_sc.shape, NEG_BIG, F32)
    acc_sc[...] = jnp.zeros_like(acc_sc)

    def step(k0, tk, band):
        for h in heads:
            st_sc[h, 0:tk, :] = _dot(k_ref[pl.ds(k0, tk), h * LANES:(h + 1) * LANES], q2_sc[h], NT)
        pvs, alphas = [], []
        for h in heads:
            m_old = m_sc[h]
            if band is None:
                st = st_sc[h, 0:tk, :]
                far = relb_ref[h, NUM_BUCKETS - 1] * LOG2E
                m_new = jnp.maximum(m_old, jnp.max(st, axis=0, keepdims=True) + far)
                p = jnp.exp2(st - (m_new - far))
            else:
                st = st_sc[h, 0:tk, :] + bias_sc[h, band:band + tk, :]
                m_new = jnp.maximum(m_old, jnp.max(st, axis=0, keepdims=True))
                p = jnp.exp2(st - m_new)
            m_sc[h] = m_new
            pvs.append(_dot(vt_ref[h, :, pl.ds(k0, tk)], p.astype(BF16)))
            alphas.append(jnp.exp2(m_old - m_new))
        for h in heads:
            acc_sc[h] = alphas[h] * acc_sc[h] + pvs[h]

    n_far = jnp.maximum(qi - 1, 0)
    group = FAR_GROUP * tq

    def far_body(i, carry):
        step(pl.multiple_of(i * group, group), group, None)
        return carry

    lax.fori_loop(0, n_far // FAR_GROUP, far_body, 0)
    pos = (n_far // FAR_GROUP) * FAR_GROUP
    left = n_far - pos
    size = FAR_GROUP // 2
    while size >= 1:
        take = (left // size) % 2 == 1

        @pl.when(take)
        def _(pos=pos, size=size):
            step(pl.multiple_of(pos * tq, tq), size * tq, None)

        pos = pos + jnp.where(take, size, 0)
        size //= 2

    @pl.when(qi >= 1)
    def _():
        step(pl.multiple_of((qi - 1) * tq, tq), 2 * tq, 0)

    @pl.when(qi == 0)
    def _():
        step(0, tq, tq)

    lam = (jnp.exp(jnp.sum(lq1_ref[...] * lk1_ref[...], axis=-1, keepdims=True))
           - jnp.exp(jnp.sum(lq2_ref[...] * lk2_ref[...], axis=-1, keepdims=True)) + lam_init)
    for h in heads:
        acc = acc_sc[h]
        l = acc[ACC_L:ACC_L + 1, :]
        o1 = acc[0:LANES, 0:tq] / l[:, 0:tq]
        o2 = acc[0:LANES, tq:2 * tq] / l[:, tq:2 * tq]
        ot = o1 - lam * o2
        ms = jnp.mean(ot * ot, axis=0, keepdims=True)
        y = ot * lax.rsqrt(ms + NORM_EPS) * sg_ref[...] * (1.0 - lam_init)
        o_ref[:, h * LANES:(h + 1) * LANES] = y.T.astype(o_ref.dtype)


def _diff_attn(q, k, vt, relb_t, lq1, lk1, lq2, lk2, sg_col, lam_init):
    B, S, width = q.shape
    H = width // LANES
    t = ATT_TILE
    buckets = jnp.asarray(_near_bucket_tiles(t))
    vec = pl.BlockSpec((1, HEAD_DIM), lambda b, i: (0, 0))
    return pl.pallas_call(
        functools.partial(_attn_kernel, lam_init=lam_init),
        grid=(B, S // t),
        in_specs=[
            pl.BlockSpec(memory_space=pltpu.SMEM),
            pl.BlockSpec((2, t, t), lambda b, i: (0, 0, 0)),
            vec, vec, vec, vec,
            pl.BlockSpec((LANES, 1), lambda b, i: (0, 0)),
            pl.BlockSpec((None, t, width), lambda b, i: (b, i, 0)),
            pl.BlockSpec((None, S, width), lambda b, i: (b, 0, 0)),
            pl.BlockSpec((None, H, ACC_ROWS, S), lambda b, i: (b, 0, 0, 0)),
        ],
        out_specs=pl.BlockSpec((None, t, width), lambda b, i: (b, i, 0)),
        out_shape=jax.ShapeDtypeStruct((B, S, width), BF16),
        scratch_shapes=[
            pltpu.VMEM((H, 2 * t, 2 * t), F32),
            pltpu.VMEM((H, 2 * t, LANES), BF16),
            pltpu.VMEM((H, 1, 2 * t), F32),
            pltpu.VMEM((H, ACC_ROWS, 2 * t), F32),
            pltpu.VMEM((H, FAR_GROUP * t, 2 * t), F32),
        ],
        compiler_params=_cparams(("arbitrary", "arbitrary")),
        name="diff_attn",
    )(relb_t, buckets, lq1, lk1, lq2, lk2, sg_col, q, k, vt)


def _run_rows(count):
    return jnp.ceil(count * (1.0 / RUN_CHUNK)) * RUN_CHUNK


def _outproj_kernel(x_ref, ya_ref, yb_ref, wa_ref, wb_ref, g_ref, rw_ref, rb_ref,
                    x1_ref, h2_ref, gd_ref, cnt_ref):
    x1 = x_ref[...] + _dot(ya_ref[...], wa_ref[...]) + _dot(yb_ref[...], wb_ref[...])
    x1_ref[...] = x1
    ms = jnp.mean(x1 * x1, axis=-1, keepdims=True)
    h2 = x1 * lax.rsqrt(ms + NORM_EPS) * g_ref[...]
    h2_ref[...] = h2.astype(h2_ref.dtype)
    logits = _mm(h2, rw_ref[...], mode="x3") + rb_ref[...]
    work = logits.T
    expert = lax.broadcasted_iota(jnp.int32, work.shape, 0).astype(F32)
    picks = []
    for _ in range(TOP_K):
        m = jnp.max(work, axis=0, keepdims=True)
        idx = jnp.min(jnp.where(work == m, expert, float(LANES)), axis=0, keepdims=True)
        hit = expert == idx
        picks.append((m, hit))
        work = jnp.where(hit, NEG_BIG, work)
    m0 = picks[0][0]
    es = [jnp.exp(m - m0) for m, _ in picks]
    denom = es[0] + es[1] + es[2] + es[3]
    gd_t = jnp.zeros_like(work)
    for e, (_, hit) in zip(es, picks):
        gd_t = jnp.where(hit, e / denom, gd_t)
    gd = gd_t.T
    gd_ref[...] = gd

    @pl.when(pl.program_id(0) == 0)
    def _():
        cnt_ref[...] = jnp.zeros_like(cnt_ref)

    picked = jnp.where(gd > 0.0, 1.0, 0.0)
    for t0 in range(0, gd.shape[0], MOVE_ROWS):
        cnt_ref[0:1, :] = cnt_ref[0:1, :] + _run_rows(jnp.sum(picked[t0:t0 + MOVE_ROWS], axis=0, keepdims=True))


def _outproj(x2d, ya, yb, wa, wb, g2, rw, rb):
    T, D = x2d.shape
    half = ya.shape[1]
    tm = OUT_ROWS
    row = lambda i: (i, 0)
    fix = lambda i: (0, 0)
    return pl.pallas_call(
        _outproj_kernel,
        grid=(T // tm,),
        in_specs=[
            pl.BlockSpec((tm, D), row),
            pl.BlockSpec((tm, half), row),
            pl.BlockSpec((tm, half), row),
            pl.BlockSpec((half, D), fix),
            pl.BlockSpec((half, D), fix),
            pl.BlockSpec((1, D), fix),
            pl.BlockSpec((D, LANES), fix),
            pl.BlockSpec((1, LANES), fix),
        ],
        out_specs=[
            pl.BlockSpec((tm, D), row),
            pl.BlockSpec((tm, D), row),
            pl.BlockSpec((tm, LANES), row),
            pl.BlockSpec((8, LANES), fix),
        ],
        out_shape=[
            jax.ShapeDtypeStruct((T, D), F32),
            jax.ShapeDtypeStruct((T, D), BF16),
            jax.ShapeDtypeStruct((T, LANES), F32),
            jax.ShapeDtypeStruct((8, LANES), F32),
        ],
        compiler_params=_cparams(("arbitrary",)),
        name="outproj_router",
    )(x2d, ya, yb, wa, wb, g2, rw, rb)


def _padded_rows(cnt, rb):
    return jnp.ceil(cnt * (1.0 / rb)) * rb


def _route_kernel(gd_ref, tot_ref, lp_ref, gate_ref, tab_ref, cnt_ref, carry_sc, pstart_sc, *, rb):
    i = pl.program_id(0)
    tb = gd_ref.shape[0]
    gd = gd_ref[...]
    sel = gd > 0.0
    self = jnp.where(sel, 1.0, 0.0)
    run_pad = _run_rows(jnp.sum(self, axis=0, keepdims=True))
    r2 = lax.broadcasted_iota(jnp.int32, (LANES, LANES), 0)
    c2 = lax.broadcasted_iota(jnp.int32, (LANES, LANES), 1)

    @pl.when(i == 0)
    def _():
        cnt = tot_ref[0:1, :]
        upper = jnp.where(r2 < c2, 1.0, 0.0)
        pstart = _dot(jnp.broadcast_to(_padded_rows(cnt, rb), (8, LANES)), upper, prec=HP)[0:1, :]
        pstart_sc[...] = pstart
        cnt_ref[0:1, :] = cnt
        cnt_ref[1:2, :] = pstart
        cnt_ref[2:8, :] = jnp.zeros((6, LANES), F32)
        carry_sc[...] = jnp.zeros_like(carry_sc)

    rr = lax.broadcasted_iota(jnp.int32, (tb, tb), 0)
    cc = lax.broadcasted_iota(jnp.int32, (tb, tb), 1)
    lower = jnp.where(cc < rr, 1.0, 0.0).astype(BF16)
    sel_b = self.astype(BF16)
    prefix = _dot(lower, sel_b)
    upper_b = jnp.where(r2 < c2, 1.0, 0.0).astype(BF16)
    off = _dot(jnp.broadcast_to(run_pad, (8, LANES)).astype(BF16), upper_b)[0:1, :]
    lp = off + prefix
    upper_incl = jnp.where(r2 <= c2, 1.0, 0.0).astype(BF16)
    slot = _dot(sel_b, upper_incl)
    lane = lax.broadcasted_iota(jnp.int32, (tb, LANES), 1)
    p_out = jnp.zeros((tb, LANES), F32)
    g_out = jnp.zeros((tb, LANES), F32)
    for kk in range(TOP_K):
        mk = sel & (slot == float(kk + 1))
        p_k = jnp.sum(jnp.where(mk, lp, 0.0), axis=-1, keepdims=True)
        g_k = jnp.sum(jnp.where(mk, gd, 0.0), axis=-1, keepdims=True)
        p_out = jnp.where(lane == kk, p_k, p_out)
        g_out = jnp.where(lane == kk, g_k, g_out)
    lp_ref[...] = p_out
    gate_ref[...] = g_out
    def as_column(v):
        return jnp.sum(jnp.where(r2 == c2, jnp.broadcast_to(v, (LANES, LANES)), 0.0), axis=1, keepdims=True)

    first = as_column(off * (1.0 / RUN_CHUNK))
    count = as_column(run_pad * (1.0 / RUN_CHUNK))
    start = as_column(pstart_sc[...] + carry_sc[...])
    chunk = lax.broadcasted_iota(jnp.int32, (LANES, CHUNK_LANES), 1).astype(F32)
    mine = (chunk >= first) & (chunk < first + count)
    dst = jnp.sum(jnp.where(mine, start + (chunk - first) * RUN_CHUNK, 0.0), axis=0, keepdims=True)
    pos_in_run = jnp.sum(jnp.where(mine, chunk - first, 0.0), axis=0, keepdims=True)
    run_len = jnp.sum(jnp.where(mine, count, 0.0), axis=0, keepdims=True)
    used = run_len > 0.0
    n_chunks = jnp.sum(count, axis=0, keepdims=True)
    big_part = jnp.floor(run_len * (1.0 / BIG_COPY)) * BIG_COPY
    is_big = used & (pos_in_run < big_part) & (pos_in_run == jnp.floor(pos_in_run * (1.0 / BIG_COPY)) * BIG_COPY)
    is_small = used & (pos_in_run >= big_part)
    code = dst * (256.0 / RUN_CHUNK) + chunk[0:1, :]
    c_hi = jnp.floor(code * (1.0 / 65536.0))
    c_mid = jnp.floor(code * (1.0 / 256.0)) - c_hi * 256.0
    c_lo = code - jnp.floor(code * (1.0 / 256.0)) * 256.0
    pieces = jnp.concatenate([c_hi, c_mid, c_lo, jnp.zeros((5, CHUNK_LANES), F32)], axis=0).astype(BF16)
    rq = lax.broadcasted_iota(jnp.int32, (CHUNK_LANES, CHUNK_LANES), 0)
    cq = lax.broadcasted_iota(jnp.int32, (CHUNK_LANES, CHUNK_LANES), 1)
    before = jnp.where(rq < cq, 1.0, 0.0).astype(BF16)
    slot_id = rq.astype(F32)

    def compact(mask):
        flag = jnp.where(mask, 1.0, 0.0)
        rank = _dot(jnp.broadcast_to(flag, (8, CHUNK_LANES)).astype(BF16), before)[0:1, :]
        pick = jnp.where(mask & (rank == slot_id), 1.0, 0.0).astype(BF16)
        got = _dot(pieces, pick, NT)
        return got[0:1, :] * 65536.0 + got[1:2, :] * 256.0 + got[2:3, :], jnp.sum(flag, axis=1, keepdims=True)

    big_list, n_big = compact(is_big)
    small_list, n_small = compact(is_small)
    lane_c = lax.broadcasted_iota(jnp.int32, (1, CHUNK_LANES), 1)
    counts = jnp.where(lane_c == 0, n_chunks, jnp.where(lane_c == 1, n_big, jnp.where(lane_c == 2, n_small, 0.0)))
    tab_ref[0:1, :] = big_list.astype(jnp.int32)
    tab_ref[1:2, :] = small_list.astype(jnp.int32)
    tab_ref[2:3, :] = counts.astype(jnp.int32)
    tab_ref[3:8, :] = jnp.zeros((5, CHUNK_LANES), jnp.int32)
    carry_sc[...] = carry_sc[...] + run_pad


def _route(gd, totals, rb):
    T = gd.shape[0]
    tb = MOVE_ROWS
    blk = lambda i: (i, 0)
    fix = lambda i: (0, 0)
    return pl.pallas_call(
        functools.partial(_route_kernel, rb=rb),
        grid=(T // tb,),
        in_specs=[pl.BlockSpec((tb, LANES), blk), pl.BlockSpec((8, LANES), fix)],
        out_specs=[
            pl.BlockSpec((tb, LANES), blk),
            pl.BlockSpec((tb, LANES), blk),
            pl.BlockSpec((8, CHUNK_LANES), blk),
            pl.BlockSpec((8, LANES), fix),
        ],
        out_shape=[
            jax.ShapeDtypeStruct((T, LANES), F32),
            jax.ShapeDtypeStruct((T, LANES), F32),
            jax.ShapeDtypeStruct((T // tb * 8, CHUNK_LANES), jnp.int32),
            jax.ShapeDtypeStruct((8, LANES), F32),
        ],
        scratch_shapes=[pltpu.VMEM((1, LANES), F32), pltpu.VMEM((1, LANES), F32)],
        compiler_params=_cparams(("arbitrary",)),
        name="route_positions",
    )(gd, totals)


def _rows_at(src_ref, src_row, dst_ref, dst_row, n_rows, sem):
    return pltpu.make_async_copy(src_ref.at[pl.ds(src_row, n_rows), :], dst_ref.at[pl.ds(dst_row, n_rows), :], sem)


def _start_runs(b, cnt_ref, big_ref, small_ref, copy_rows):
    def issue(tab_ref, stride, n_rows):
        def body(i, carry):
            code = tab_ref[b * stride + i]
            local = (code & 255) * RUN_CHUNK
            glob = lax.shift_right_logical(code, 8) * RUN_CHUNK
            copy_rows(pl.multiple_of(local, RUN_CHUNK), pl.multiple_of(glob, RUN_CHUNK), n_rows).start()
            return carry
        return body

    lax.fori_loop(0, cnt_ref[b * 4 + 1], issue(big_ref, BIG_LIST, BIG_COPY * RUN_CHUNK), 0)
    lax.fori_loop(0, cnt_ref[b * 4 + 2], issue(small_ref, SMALL_LIST, RUN_CHUNK), 0)


def _wait_runs(b, nc_ref, wait_rows):
    total = nc_ref[b * 4]
    p = 1
    while p <= LOCAL_CHUNKS:
        @pl.when((total & p) != 0)
        def _(p=p):
            wait_rows(p * RUN_CHUNK).wait()
        p *= 2


def _rows_copy(src_ref, dst_ref, n_rows, sem):
    return pltpu.make_async_copy(src_ref.at[pl.ds(0, n_rows), :], dst_ref.at[pl.ds(0, n_rows), :], sem)


def _local_onehot(lp, lb):
    lpt = lp.T
    j = lax.broadcasted_iota(jnp.int32, (lb, lp.shape[0]), 0).astype(F32)
    hit = lpt[0:1, :] == j
    for kk in range(1, TOP_K):
        hit = hit | (lpt[kk:kk + 1, :] == j)
    return hit


def _scatter_kernel(nc_ref, big_ref, small_ref, t0_ref, tc_ref, lp_ref, h_ref, xs_ref, buf, zero_buf, sems):
    b = pl.program_id(0)
    slot = b % 2
    pick = jnp.where(_local_onehot(lp_ref[...], buf.shape[1]), 1.0, 0.0).astype(BF16)
    buf[slot] = _dot(pick, h_ref[...])
    _start_runs(b, nc_ref, big_ref, small_ref,
                lambda s, d, n: _rows_at(buf.at[slot], s, xs_ref, d, n, sems.at[slot]))

    @pl.when(b >= 1)
    def _():
        _wait_runs(b - 1, nc_ref, lambda n: _rows_copy(buf.at[1 - slot], xs_ref, n, sems.at[1 - slot]))

    @pl.when(b == pl.num_programs(0) - 1)
    def _():
        _wait_runs(b, nc_ref, lambda n: _rows_copy(buf.at[slot], xs_ref, n, sems.at[slot]))
        zero_buf[...] = jnp.zeros_like(zero_buf)
        total = 0
        for e in range(N_EXPERTS + 1):
            t0 = t0_ref[e]

            def issue(c, carry, t0=t0):
                _rows_at(zero_buf, 0, xs_ref, pl.multiple_of(t0 + c * RUN_CHUNK, RUN_CHUNK), RUN_CHUNK,
                         sems.at[0]).start()
                return carry

            lax.fori_loop(0, tc_ref[e], issue, 0)
            total = total + tc_ref[e]

        def drain(c, carry):
            _rows_at(zero_buf, 0, xs_ref, 0, RUN_CHUNK, sems.at[0]).wait()
            return carry

        lax.fori_loop(0, total, drain, 0)


def _scatter_rows(tabs, tails, lp, h2, n_rows):
    T, D = h2.shape
    tb = MOVE_ROWS
    return pl.pallas_call(
        _scatter_kernel,
        grid_spec=pltpu.PrefetchScalarGridSpec(
            num_scalar_prefetch=5,
            grid=(T // tb,),
            in_specs=[
                pl.BlockSpec((tb, LANES), lambda i, *_: (i, 0)),
                pl.BlockSpec((tb, D), lambda i, *_: (i, 0)),
            ],
            out_specs=pl.BlockSpec(memory_space=pl.ANY),
            scratch_shapes=[pltpu.VMEM((2, LOCAL_ROWS, D), F32), pltpu.VMEM((RUN_CHUNK, D), F32),
                            pltpu.SemaphoreType.DMA((2,))],
        ),
        out_shape=jax.ShapeDtypeStruct((n_rows, D), F32),
        compiler_params=_cparams(("arbitrary",)),
        name="scatter_rows",
    )(*tabs, *tails, lp, h2)


def _moe_kernel(be_ref, na_ref, slot_ref, nxt_ref, valid_ref, xs_ref, w1_hbm, b1_ref, w2_hbm, b2_ref, ys_ref,
                w1f, w2f, w1b, w2b, sems):
    j = pl.program_id(0)
    e = be_ref[j]
    slot = slot_ref[j]

    def fetch(expert, sl):
        return (pltpu.make_async_copy(w1_hbm.at[expert], w1f.at[sl], sems.at[0, sl]),
                pltpu.make_async_copy(w2_hbm.at[expert], w2f.at[sl], sems.at[1, sl]))

    @pl.when(j == 0)
    def _():
        for cp in fetch(e, slot):
            cp.start()

    @pl.when((j == 0) | (e != be_ref[jnp.maximum(j - 1, 0)]))
    def _():
        for cp in fetch(e, slot):
            cp.wait()
        w1b[...] = w1f[slot].astype(BF16)
        w2b[...] = w2f[slot].astype(BF16)

        @pl.when(nxt_ref[j] >= 0)
        def _():
            for cp in fetch(nxt_ref[j], 1 - slot):
                cp.start()

    def expert_mlp(rows):
        xb = xs_ref[0:rows, :].astype(BF16)
        hid = _dot(xb, w1b[...]) + b1_ref[...]
        half = hid.shape[1] // 2
        x_glu = jnp.minimum(hid[:, :half], SWIGLU_LIMIT)
        x_lin = jnp.clip(hid[:, half:], -SWIGLU_LIMIT, SWIGLU_LIMIT)
        act = x_glu * _sigmoid(SWIGLU_ALPHA * x_glu) * (x_lin + 1.0)
        ys_ref[0:rows, :] = _dot(act.astype(BF16), w2b[...]) + b2_ref[...]

    rb = xs_ref.shape[0]
    valid = valid_ref[j]

    @pl.when(valid > rb // 2)
    def _():
        expert_mlp(rb)

    @pl.when((valid > 0) & (valid <= rb // 2))
    def _():
        expert_mlp(rb // 2)
        ys_ref[rb // 2:rb, :] = jnp.zeros((rb - rb // 2, ys_ref.shape[1]), ys_ref.dtype)

    @pl.when(valid <= 0)
    def _():
        ys_ref[...] = jnp.zeros_like(ys_ref)


def _expert_runs(blk_expert):
    n = blk_expert.shape[0]
    idx = jnp.arange(n)
    starts = jnp.concatenate([jnp.ones((1,), bool), blk_expert[1:] != blk_expert[:-1]])
    slot = (jnp.cumsum(starts) - 1) % 2
    next_start = lax.cummin(jnp.where(starts, idx, n)[::-1])[::-1]
    after = jnp.concatenate([next_start[1:], jnp.full((1,), n)])
    nxt = jnp.where(after < n, blk_expert[jnp.minimum(after, n - 1)], -1)
    return slot.astype(jnp.int32), nxt.astype(jnp.int32)


def _moe(blk_expert, n_active, valid, xs, w1, b1, w2, b2):
    R, D = xs.shape
    E, _, D2 = w1.shape
    rb = MOE_ROWS
    nb = R // rb
    slot, nxt = _expert_runs(blk_expert)
    blk = lambda j, be, na, *_: (jnp.minimum(j, na[0] - 1), 0)
    return pl.pallas_call(
        _moe_kernel,
        grid_spec=pltpu.PrefetchScalarGridSpec(
            num_scalar_prefetch=5,
            grid=(nb,),
            in_specs=[
                pl.BlockSpec((rb, D), blk),
                pl.BlockSpec(memory_space=pl.ANY),
                pl.BlockSpec((None, 1, D2), lambda j, be, *_: (be[j], 0, 0)),
                pl.BlockSpec(memory_space=pl.ANY),
                pl.BlockSpec((None, 1, D), lambda j, be, *_: (be[j], 0, 0)),
            ],
            out_specs=pl.BlockSpec((rb, D), lambda j, *_: (j, 0)),
            scratch_shapes=[pltpu.VMEM((2, D, D2), F32), pltpu.VMEM((2, D2 // 2, D), F32),
                            pltpu.VMEM((D, D2), BF16), pltpu.VMEM((D2 // 2, D), BF16),
                            pltpu.SemaphoreType.DMA((2, 2))],
        ),
        out_shape=jax.ShapeDtypeStruct((R, D), F32),
        compiler_params=_cparams(("arbitrary",)),
        name="moe_experts",
    )(blk_expert, n_active, slot, nxt, valid, xs, w1, b1, w2, b2)


def _combine_kernel(nc_ref, big_ref, small_ref, lp_ref, gate_ref, x1_ref, ys_ref, o_ref, buf, sems):
    b = pl.program_id(0)
    slot = b % 2

    def fetch(blk, sl):
        _start_runs(blk, nc_ref, big_ref, small_ref,
                    lambda s, d, n: _rows_at(ys_ref, d, buf.at[sl], s, n, sems.at[sl]))

    @pl.when(b == 0)
    def _():
        buf[...] = jnp.zeros_like(buf)
        fetch(b, slot)

    @pl.when(b + 1 < pl.num_programs(0))
    def _():
        fetch(b + 1, 1 - slot)

    _wait_runs(b, nc_ref, lambda n: _rows_copy(ys_ref, buf.at[slot], n, sems.at[slot]))
    lp = lp_ref[...]
    _, lb, d = buf.shape
    tb = lp.shape[0]
    lpt = lp.T
    gt = gate_ref[...].T
    j_row = lax.broadcasted_iota(jnp.int32, (lb, tb), 0).astype(F32)
    row_gate = jnp.zeros((lb, tb), F32)
    for kk in range(TOP_K):
        row_gate = jnp.where(lpt[kk:kk + 1, :] == j_row, gt[kk:kk + 1, :], row_gate)
    g_hi, g_lo = _split2(row_gate)
    ones = jnp.ones((tb, LANES), BF16)
    g_row = _dot(g_hi, ones) + _dot(g_lo, ones)
    rows = (buf[slot] * jnp.concatenate([g_row] * (d // LANES), axis=1)).astype(BF16)
    j_col = lax.broadcasted_iota(jnp.int32, (tb, lb), 1).astype(F32)
    hit = lp[:, 0:1] == j_col
    for kk in range(1, TOP_K):
        hit = hit | (lp[:, kk:kk + 1] == j_col)
    o_ref[...] = x1_ref[...] + _dot(jnp.where(hit, 1.0, 0.0).astype(BF16), rows)


def _combine(tabs, lp, gate, x1, ys):
    T, D = x1.shape
    tb = MOVE_ROWS
    return pl.pallas_call(
        _combine_kernel,
        grid_spec=pltpu.PrefetchScalarGridSpec(
            num_scalar_prefetch=3,
            grid=(T // tb,),
            in_specs=[
                pl.BlockSpec((tb, LANES), lambda i, *_: (i, 0)),
                pl.BlockSpec((tb, LANES), lambda i, *_: (i, 0)),
                pl.BlockSpec((tb, D), lambda i, *_: (i, 0)),
                pl.BlockSpec(memory_space=pl.ANY),
            ],
            out_specs=pl.BlockSpec((tb, D), lambda i, *_: (i, 0)),
            scratch_shapes=[pltpu.VMEM((2, LOCAL_ROWS, D), F32), pltpu.SemaphoreType.DMA((2,))],
        ),
        out_shape=jax.ShapeDtypeStruct((T, D), F32),
        compiler_params=_cparams(("arbitrary",)),
        name="combine_rows",
    )(*tabs, lp, gate, x1, ys)


def _pad_cols(a, n):
    return jnp.pad(a, ((0, 0), (0, n - a.shape[1])))


def _pad_rows(a, n):
    return jnp.pad(a, ((0, n - a.shape[0]), (0, 0)))


def _lora_weight(w2, a2, g2, width):
    z = jnp.zeros((LANES, width), F32)
    return jnp.concatenate([
        jnp.concatenate([_pad_rows(w2, LANES), z, z], axis=1),
        jnp.concatenate([z, _pad_rows(a2, LANES), z], axis=1),
        jnp.concatenate([z, z, _pad_rows(g2, LANES)], axis=1),
    ], axis=0).astype(BF16)


def _split_rwkv_cols(a, width):
    o = 3 * width
    return jnp.concatenate([
        a[:, :o],
        _pad_cols(a[:, o:o + DECAY_LORA], LANES),
        _pad_cols(a[:, o + DECAY_LORA:o + DECAY_LORA + AAA_LORA], LANES),
        _pad_cols(a[:, o + DECAY_LORA + AAA_LORA:o + DECAY_LORA + AAA_LORA + GATE_LORA], LANES),
    ], axis=1)


def kernel(x, norm1_g, w_in, mu_shift, w0, w2, a0, a2, g2, k_k, k_a, r_k, lnx_g, lnx_b, qn_g, kn_g,
           lam_q1, lam_k1, lam_q2, lam_k2, subln_g, rel_bias, w_out, norm2_g, router_w, router_b,
           exp_w1, exp_b1, exp_w2, exp_b2):
    B, S, D = x.shape
    T = B * S
    depth = norm1_g.shape[0]
    width = w0.shape[1]
    n_rwkv_cols = 3 * width + DECAY_LORA + AAA_LORA + GATE_LORA
    n_rwkv_pad = 3 * width + 3 * LANES
    n_groups = width // HEAD_DIM
    H = width // LANES
    relb_t = rel_bias.T
    row = lambda a: a.reshape(1, -1)

    for layer in range(depth):
        lam_init = 0.8 - 0.6 * math.exp(-0.3 * layer)
        w_l = w_in[layer]
        w_all = jnp.concatenate([_split_rwkv_cols(w_l[:, :n_rwkv_cols], width), w_l[:, n_rwkv_cols:]],
                                axis=1).astype(BF16)
        mu = _split_rwkv_cols(row(mu_shift[layer]), width)
        qg = row(jnp.tile(qn_g[layer], n_groups))
        kg = row(jnp.tile(kn_g[layer], n_groups))

        pr, q, k, vt = _inproj(x.reshape(T, D), row(norm1_g[layer]), w_all, qg, kg, n_rwkv_pad, width, S)

        y_a = _rwkv(pr.reshape(B, S, n_rwkv_pad), mu, row(w0[layer]),
                    _lora_weight(w2[layer], a2[layer], g2[layer], width), row(a0[layer]),
                    row(k_k[layer]), row(k_a[layer]), row(r_k[layer]), row(lnx_g[layer]),
                    row(lnx_b[layer]), width)

        y_b = _diff_attn(q.reshape(B, S, width), k.reshape(B, S, width), vt, relb_t,
                         row(lam_q1[layer]), row(lam_k1[layer]), row(lam_q2[layer]), row(lam_k2[layer]),
                         subln_g[layer].reshape(-1, 1), lam_init)

        wo = w_out[layer].astype(BF16)
        rw = _pad_cols(router_w[layer], LANES)
        rb = jnp.concatenate([router_b[layer], jnp.full((LANES - N_EXPERTS,), NEG_BIG, F32)]).reshape(1, -1)
        x1, h2, gd, totals = _outproj(x.reshape(T, D), y_a.reshape(T, width), y_b.reshape(T, width),
                                      wo[:width], wo[width:], row(norm2_g[layer]), rw, rb)

        lp, gate, tab, cnt = _route(gd, totals, MOE_ROWS)
        tab = tab.reshape(-1, 8, CHUNK_LANES)
        tabs = (tab[:, 2, :4].reshape(-1),
                tab[:, 0, :BIG_LIST].reshape(-1), tab[:, 1, :SMALL_LIST].reshape(-1))
        run_end = cnt[1, :N_EXPERTS] + cnt[0, :N_EXPERTS]
        pad_end = cnt[1, :N_EXPERTS] + _padded_rows(cnt[0, :N_EXPERTS], MOE_ROWS)
        run_pad_max = (T // MOVE_ROWS) * N_EXPERTS * (RUN_CHUNK - 1)
        n_blocks = -(-(T * TOP_K + run_pad_max + N_EXPERTS * (MOE_ROWS - 1)) // MOE_ROWS)
        gap_start = jnp.concatenate([run_end, pad_end[-1:]])
        gap_end = jnp.concatenate([pad_end, jnp.full((1,), n_blocks * MOE_ROWS, F32)])
        tails = (gap_start.astype(jnp.int32), ((gap_end - gap_start) / RUN_CHUNK).astype(jnp.int32))
        blk_start = (jnp.arange(n_blocks) * MOE_ROWS).astype(F32)
        blk_expert = jnp.minimum(jnp.sum(pad_end[None, :] <= blk_start[:, None], axis=1), N_EXPERTS - 1)
        n_active = (pad_end[N_EXPERTS - 1] / MOE_ROWS).astype(jnp.int32).reshape(1)

        xs = _scatter_rows(tabs, tails, lp, h2, n_blocks * MOE_ROWS)
        blk_valid = jnp.clip(run_end[blk_expert] - blk_start, 0, MOE_ROWS)
        ys = _moe(blk_expert.astype(jnp.int32), n_active, blk_valid.astype(jnp.int32), xs,
                  exp_w1[layer], exp_b1[layer][:, None, :], exp_w2[layer], exp_b2[layer][:, None, :])
        x = _combine(tabs, lp, gate, x1, ys).reshape(B, S, D)
    return x
```

```python
import functools
import math

import numpy as np
import jax
import jax.numpy as jnp
from jax import lax
from jax.experimental import pallas as pl
from jax.experimental.pallas import tpu as pltpu

F32 = jnp.float32
BF16 = jnp.bfloat16
HP = lax.Precision.HIGHEST

HEAD_DIM = 64
DECAY_LORA = 32
AAA_LORA = 32
GATE_LORA = 96
NUM_BUCKETS = 32
MAX_DISTANCE = 128
N_EXPERTS = 32
TOP_K = 4
SWIGLU_LIMIT = 7.0
SWIGLU_ALPHA = 1.702
NORM_EPS = 1e-5
LNX_EPS = 64e-5

LANES = 128
VMEM_LIMIT_BYTES = 56 * 1024 * 1024

NEG_BIG = -1e30
LOG2E = math.log2(math.e)
ACC_L = LANES
ACC_ROWS = LANES + 8

PROJ_ROWS = 1024
RWKV_CHUNK = 64
RWKV_ROWS = 128
RWKV_SEQS = 4
ATT_TILE = 256
FAR_GROUP = 4
MOE_ROWS = 512
MOVE_ROWS = 256
OUT_ROWS = 256
RUN_CHUNK = 8
LOCAL_ROWS = MOVE_ROWS * 4 + 32 * RUN_CHUNK
LOCAL_CHUNKS = LOCAL_ROWS // RUN_CHUNK
CHUNK_LANES = 256
BIG_COPY = 4
BIG_LIST = 64
SMALL_LIST = 128

NN = (((1,), (0,)), ((), ()))
NT = (((1,), (1,)), ((), ()))
TN = (((0,), (0,)), ((), ()))


def _dot(a, b, dims=NN, prec=None):
    return lax.dot_general(a, b, dims, precision=prec, preferred_element_type=F32)


def _cparams(sem):
    return pltpu.CompilerParams(dimension_semantics=sem, vmem_limit_bytes=VMEM_LIMIT_BYTES)


def _sigmoid(x):
    return 1.0 / (1.0 + jnp.exp(-x))


def _lane_lo(shape):
    lane = lax.broadcasted_iota(jnp.int32, shape, len(shape) - 1)
    return (lane % LANES) < HEAD_DIM


def _head_rms_normalize(x, gain, scale):
    outs = []
    for t in range(x.shape[1] // LANES):
        xt = x[:, t * LANES:(t + 1) * LANES]
        lo = _lane_lo(xt.shape)
        x2 = xt * xt
        s0 = jnp.sum(jnp.where(lo, x2, 0.0), axis=-1, keepdims=True)
        s1 = jnp.sum(jnp.where(lo, 0.0, x2), axis=-1, keepdims=True)
        inv = lax.rsqrt(jnp.where(lo, s0, s1) * (1.0 / HEAD_DIM) + NORM_EPS)
        outs.append(xt * inv)
    return jnp.concatenate(outs, axis=-1) * gain * scale


def _inproj_kernel(x_ref, g_ref, w_ref, qg_ref, kg_ref, pr_ref, q_ref, k_ref, vt_ref, *, n_rwkv, width):
    x = x_ref[...]
    ms = jnp.mean(x * x, axis=-1, keepdims=True)
    h = (x * lax.rsqrt(ms + NORM_EPS) * g_ref[...]).astype(BF16)
    step = 384
    for c0 in range(0, n_rwkv, step):
        pr_ref[:, c0:c0 + step] = _dot(h, w_ref[:, c0:c0 + step])
    q = _dot(h, w_ref[:, n_rwkv:n_rwkv + width])
    q_ref[...] = _head_rms_normalize(q, qg_ref[...], HEAD_DIM ** -0.5 * LOG2E).astype(BF16)
    k = _dot(h, w_ref[:, n_rwkv + width:n_rwkv + 2 * width])
    k_ref[...] = _head_rms_normalize(k, kg_ref[...], 1.0).astype(BF16)
    v = _dot(h, w_ref[:, n_rwkv + 2 * width:n_rwkv + 3 * width])
    for hd in range(width // LANES):
        vt_ref[hd, 0:LANES, :] = v[:, hd * LANES:(hd + 1) * LANES].T.astype(BF16)
        vt_ref[hd, LANES:ACC_ROWS, :] = jnp.ones((ACC_ROWS - LANES, v.shape[0]), BF16)


def _inproj(x2d, g, w_all, qg, kg, n_rwkv, width, seq_len):
    T, D = x2d.shape
    n_all = w_all.shape[1]
    tm = PROJ_ROWS
    H = width // LANES
    per_seq = seq_len // tm
    row = lambda i: (i, 0)
    fix = lambda i: (0, 0)
    return pl.pallas_call(
        functools.partial(_inproj_kernel, n_rwkv=n_rwkv, width=width),
        grid=(T // tm,),
        in_specs=[
            pl.BlockSpec((tm, D), row),
            pl.BlockSpec((1, D), fix),
            pl.BlockSpec((D, n_all), fix),
            pl.BlockSpec((1, width), fix),
            pl.BlockSpec((1, width), fix),
        ],
        out_specs=[
            pl.BlockSpec((tm, n_rwkv), row),
            pl.BlockSpec((tm, width), row),
            pl.BlockSpec((tm, width), row),
            pl.BlockSpec((None, H, ACC_ROWS, tm), lambda i: (i // per_seq, 0, 0, i % per_seq)),
        ],
        out_shape=[
            jax.ShapeDtypeStruct((T, n_rwkv), F32),
            jax.ShapeDtypeStruct((T, width), BF16),
            jax.ShapeDtypeStruct((T, width), BF16),
            jax.ShapeDtypeStruct((T // seq_len, H, ACC_ROWS, seq_len), BF16),
        ],
        compiler_params=_cparams(("arbitrary",)),
        name="inproj",
    )(x2d, g, w_all, qg, kg)


def _split2(x):
    hi = x.astype(BF16)
    return hi, (x - hi.astype(F32)).astype(BF16)


def _mm(a, b, dims=NN, mode="bf16"):
    if mode == "bf16":
        return _dot(a.astype(BF16), b.astype(BF16), dims)
    ah, al = _split2(a)
    bh, bl = _split2(b)
    lhs = jnp.concatenate([ah, ah, al], axis=dims[0][0][0])
    rhs = jnp.concatenate([bh, bl, bh], axis=dims[0][1][0])
    return _dot(lhs, rhs, dims)


def _group_sum(x, ones2):
    hi, lo = _split2(x)
    return _dot(jnp.concatenate([hi, lo], axis=1), ones2)


P_INV = "x3"
P_SUBST = "bf16"
P_STATE = "bf16"
INV_BLOCK = 16
QUAD = 256


def _rwkv_kernel(pr_ref, mu_ref, w0_ref, wl_ref, a0_ref, kk_ref, ka_ref, rk_ref, lg_ref, lb_ref,
                 tri3_ref, ones2_ref, o_ref, state_sc, prev_sc, *, width):
    C = RWKV_CHUNK
    n_seq, seq_rows, _ = pr_ref.shape
    rows = n_seq * seq_rows
    n_quads = width // QUAD
    n_chunks = seq_rows // C
    heads_q = QUAD // HEAD_DIM

    @pl.when(pl.program_id(1) == 0)
    def _():
        state_sc[...] = jnp.zeros_like(state_sc)
        prev_sc[...] = jnp.zeros_like(prev_sc)

    x = jnp.concatenate([pr_ref[s] for s in range(n_seq)], axis=0)
    rid = lax.broadcasted_iota(jnp.int32, x.shape, 0)
    prev = pltpu.roll(x, 1, axis=0)
    for s in range(n_seq):
        prev = jnp.where(rid == s * seq_rows, prev_sc[s:s + 1, :], prev)
        prev_sc[s:s + 1, :] = x[(s + 1) * seq_rows - 1:(s + 1) * seq_rows, :]
    xs = x + (prev - x) * mu_ref[...]

    r = xs[:, 0:width]
    k = xs[:, width:2 * width]
    v = xs[:, 2 * width:3 * width]
    o = 3 * width
    lora_in = jnp.concatenate([jnp.tanh(xs[:, o:o + LANES]), xs[:, o + LANES:o + 2 * LANES],
                               _sigmoid(xs[:, o + 2 * LANES:o + 3 * LANES])], axis=1).astype(BF16)
    lora = _dot(lora_in, wl_ref[...])
    z = -(w0_ref[...] + lora[:, 0:width])
    softplus = jnp.maximum(z, 0.0) + jnp.log(1.0 + jnp.exp(-jnp.abs(z)))
    log_decay = -jnp.exp(-softplus - 0.5)
    a_lr = _sigmoid(a0_ref[...] + lora[:, width:2 * width])
    gate = lora[:, 2 * width:3 * width]

    ones2 = ones2_ref[...]
    kk = k * kk_ref[...]
    kk = kk * lax.rsqrt(jnp.maximum(_group_sum(kk * kk, ones2), 1e-24))
    k = k * (1.0 + (a_lr - 1.0) * ka_ref[...])
    a_vec = -kk
    b_vec = kk * a_lr

    p1 = log_decay.astype(BF16)
    rem = log_decay - p1.astype(F32)
    p2 = rem.astype(BF16)
    p3 = (rem - p2.astype(F32)).astype(BF16)
    cum = _dot(tri3_ref[...], jnp.concatenate([p1, p2, p3], axis=0))

    t_i = lax.broadcasted_iota(jnp.int32, (C, 2 * QUAD), 0)
    s_i = lax.broadcasted_iota(jnp.int32, (C, 2 * QUAD), 1) % C
    strict2 = t_i > s_i
    incl2 = t_i >= s_i
    lane_head = lax.broadcasted_iota(jnp.int32, (C, QUAD), 1) // HEAD_DIM
    row_blk = lax.broadcasted_iota(jnp.int32, (C, QUAD), 0) // INV_BLOCK
    same_blk = row_blk == ((lax.broadcasted_iota(jnp.int32, (C, QUAD), 1) % C) // INV_BLOCK)
    r2 = lax.broadcasted_iota(jnp.int32, (QUAD, QUAD), 0)
    c2 = lax.broadcasted_iota(jnp.int32, (QUAD, QUAD), 1)
    same_head = (r2 // HEAD_DIM) == (c2 // HEAD_DIM)

    def by_head(m):
        zero = jnp.zeros_like(m)
        return jnp.concatenate([jnp.where(lane_head == h, m, zero) for h in range(heads_q)], axis=0)

    def bd_rhs(m, mode):
        if mode == "bf16":
            return by_head(m.astype(BF16))
        hi, lo = _split2(m)
        hi_bd = by_head(hi)
        return jnp.concatenate([hi_bd, by_head(lo), hi_bd], axis=0)

    def mm_bd(a, rhs, mode):
        if mode == "bf16":
            return _dot(a.astype(BF16), rhs)
        ah, al = _split2(a)
        return _dot(jnp.concatenate([ah, ah, al], axis=1), rhs)

    class Work:
        pass

    def make_works(s):
        ws = []
        for c in range(n_chunks):
            for q in range(n_quads):
                w = Work()
                ls = slice(q * QUAD, (q + 1) * QUAD)
                rs = slice(s * seq_rows + c * C, s * seq_rows + (c + 1) * C)
                ld = log_decay[rs, ls]
                cm = cum[rs, ls]
                cl = cm[C - 1:C, :]
                e_in = jnp.exp(cm)
                e_neg = jnp.exp(-cm)
                e_hat = jnp.exp(cl - cm)
                w.q = q
                w.dec = jnp.exp(cl)
                w.vv = v[rs, ls]
                w.ar_t = jnp.concatenate([a_vec[rs, ls] * jnp.exp(cm - ld), r[rs, ls] * e_in], axis=0)
                w.bk_t = jnp.concatenate([bd_rhs(b_vec[rs, ls] * e_neg, "bf16"),
                                          bd_rhs(k[rs, ls] * e_neg, "bf16")], axis=0)
                w.bk_h = jnp.concatenate([b_vec[rs, ls] * e_hat, k[rs, ls] * e_hat], axis=0)
                ws.append(w)
        return ws

    def pre_stages(ws):
        def gram():
            for w in ws:
                g = _dot(w.ar_t.astype(BF16), w.bk_t, NT)
                low = jnp.where(strict2, g[0:C, :], 0.0)
                l_cat = low[:, 0:QUAD]
                rbk = jnp.where(incl2, g[C:2 * C, :], 0.0)
                w.rb_cat = rbk[:, 0:QUAD]
                w.kv_lhs = jnp.concatenate([low[:, QUAD:], rbk[:, QUAD:]], axis=0)
                w.l_off = jnp.where(same_blk, 0.0, l_cat)
                w.d_inv = jnp.where(same_blk, l_cat, 0.0)

        def square0():
            for w in ws:
                w.l_pow = mm_bd(w.d_inv, bd_rhs(w.d_inv, P_INV), P_INV)

        def square(last):
            def run():
                for w in ws:
                    if last:
                        w.d_inv = w.d_inv + w.l_pow + mm_bd(w.d_inv, bd_rhs(w.l_pow, P_INV), P_INV)
                    else:
                        both = mm_bd(jnp.concatenate([w.d_inv, w.l_pow], axis=0), bd_rhs(w.l_pow, P_INV), P_INV)
                        w.d_inv = w.d_inv + w.l_pow + both[0:C]
                        w.l_pow = both[C:2 * C]
            return run

        def neumann1():
            for w in ws:
                w.m1 = w.l_off + mm_bd(w.d_inv, bd_rhs(w.l_off, P_SUBST), P_SUBST)
                w.m1_bd = bd_rhs(w.m1, P_SUBST)

        def neumann2():
            for w in ws:
                w.m2 = mm_bd(w.m1, w.m1_bd, P_SUBST)

        def neumann3():
            for w in ws:
                w.nn = w.m1 + w.m2 + mm_bd(w.m2, w.m1_bd, P_SUBST)

        def finish():
            for w in ws:
                w.t_inv = w.nn + w.d_inv + mm_bd(w.nn, bd_rhs(w.d_inv, P_SUBST), P_SUBST)

        def values():
            for w in ws:
                w.kv = mm_bd(w.kv_lhs, bd_rhs(w.vv, P_STATE), P_STATE)

        n_sq = int(math.log2(INV_BLOCK))
        return ([gram, square0] + [square(it + 1 == n_sq) for it in range(1, n_sq)]
                + [neumann1, neumann2, neumann3, finish, values])

    def chain_levels(s, ws, y_parts):
        levels = []
        for c in range(n_chunks):
            wc = ws[c * n_quads:(c + 1) * n_quads]

            def base(wc=wc):
                for w in wc:
                    st = state_sc[s * n_quads + w.q]
                    w.base = _mm(w.ar_t, st, NT, mode=P_STATE) + w.kv

            def solve(wc=wc):
                for w in wc:
                    u0 = w.base[0:C]
                    w.u = u0 + mm_bd(w.t_inv, bd_rhs(u0, P_STATE), P_STATE)

            def update(wc=wc):
                for w in wc:
                    upd = _mm(jnp.concatenate([w.u, w.vv], axis=0), w.bk_h, TN, mode=P_STATE)
                    sq = s * n_quads + w.q
                    state_sc[sq] = state_sc[sq] * w.dec + jnp.where(same_head, upd, 0.0)
                for w in wc:
                    y_parts[w.q].append(w.base[C:2 * C] + mm_bd(w.rb_cat, bd_rhs(w.u, P_STATE), P_STATE))

            levels += [base, solve, update]
        return levels

    def post_stages(s, y_parts):
        rs = slice(s * seq_rows, (s + 1) * seq_rows)
        box = Work()

        def sums():
            box.y = jnp.concatenate([jnp.concatenate(y_parts[q], axis=0) for q in range(n_quads)], axis=-1)
            box.sums = _group_sum(jnp.concatenate([box.y, r[rs] * k[rs] * rk_ref[...]], axis=0), ones2)

        def variance():
            box.yc = box.y - box.sums[0:seq_rows] * (1.0 / HEAD_DIM)
            box.var = _group_sum(box.yc * box.yc, ones2) * (1.0 / HEAD_DIM)

        def write():
            yn = box.yc * lax.rsqrt(box.var + LNX_EPS) * lg_ref[...] + lb_ref[...]
            o_ref[s] = ((yn + box.sums[seq_rows:2 * seq_rows] * v[rs]) * gate[rs]).astype(o_ref.dtype)

        return [sums, variance, write]

    works = [make_works(s) for s in range(n_seq)]
    pres = [pre_stages(ws) for ws in works]
    y_all = [[[] for _ in range(n_quads)] for _ in range(n_seq)]
    for group in (pres, [chain_levels(s, works[s], y_all[s]) for s in range(n_seq)],
                  [post_stages(s, y_all[s]) for s in range(n_seq)]):
        for stages in zip(*group):
            for stage in stages:
                stage()


def _rwkv(pr, mu, w0, w_lora, a0, k_k, k_a, r_k, lnx_g, lnx_b, width):
    B, S, n_rwkv = pr.shape
    rows = RWKV_ROWS
    n_seq = RWKV_SEQS
    C = RWKV_CHUNK
    rr = np.arange(n_seq * rows)[:, None]
    cc = np.arange(n_seq * rows)[None, :]
    tri = ((rr // C == cc // C) & (cc <= rr)).astype(np.float32)
    tri3 = jnp.asarray(np.concatenate([tri, tri, tri], axis=1), BF16)
    gg = np.arange(width)
    ones = (gg[:, None] // HEAD_DIM == gg[None, :] // HEAD_DIM).astype(np.float32)
    ones2 = jnp.asarray(np.concatenate([ones, ones], axis=0), BF16)
    fix = lambda shape: pl.BlockSpec(shape, lambda b, i: (0, 0))
    vec = lambda n: fix((1, n))
    return pl.pallas_call(
        functools.partial(_rwkv_kernel, width=width),
        grid=(B // n_seq, S // rows),
        in_specs=[
            pl.BlockSpec((n_seq, rows, n_rwkv), lambda b, i: (b, i, 0)),
            vec(n_rwkv), vec(width), fix(w_lora.shape), vec(width),
            vec(width), vec(width), vec(width), vec(width), vec(width),
            fix(tri3.shape), fix(ones2.shape),
        ],
        out_specs=pl.BlockSpec((n_seq, rows, width), lambda b, i: (b, i, 0)),
        out_shape=jax.ShapeDtypeStruct((B, S, width), BF16),
        scratch_shapes=[
            pltpu.VMEM((n_seq * (width // QUAD), QUAD, QUAD), F32),
            pltpu.VMEM((n_seq, n_rwkv), F32),
        ],
        compiler_params=_cparams(("arbitrary", "arbitrary")),
        name="rwkv7",
    )(pr, mu, w0, w_lora, a0, k_k, k_a, r_k, lnx_g, lnx_b, tri3, ones2)


def _t5_bucket_np(dist):
    n = np.maximum(dist, 0)
    max_exact = NUM_BUCKETS // 2
    nf = np.maximum(n, 1).astype(np.float32)
    large = max_exact + (np.log(nf / max_exact) / math.log(MAX_DISTANCE / max_exact)
                         * (NUM_BUCKETS - max_exact)).astype(np.int32)
    large = np.minimum(large, NUM_BUCKETS - 1)
    return np.where(n < max_exact, n, large).astype(np.int32)


def _near_bucket_tiles(tile):
    kpos = np.arange(tile)[:, None]
    qpos = np.arange(tile)[None, :]
    out = []
    for delta in (0, 1):
        dist = delta * tile + qpos - kpos
        out.append(np.where(dist >= 0, _t5_bucket_np(dist), -1))
    return np.stack(out).astype(np.int32)


def _attn_kernel(relb_ref, bucket_ref, lq1_ref, lk1_ref, lq2_ref, lk2_ref, sg_ref,
                 q_ref, k_ref, vt_ref, o_ref, bias_sc, q2_sc, m_sc, acc_sc, st_sc, *, lam_init):
    tq = q_ref.shape[0]
    n_heads = vt_ref.shape[0]
    heads = range(n_heads)
    qi = pl.program_id(1)

    @pl.when((pl.program_id(0) == 0) & (qi == 0))
    def _():
        for h in heads:
            for d in range(2):
                bk = bucket_ref[d]
                tile = jnp.full(bk.shape, NEG_BIG, F32)
                for j in range(NUM_BUCKETS):
                    tile = jnp.where(bk == j, relb_ref[h, j] * LOG2E, tile)
                r0 = (1 - d) * tq
                bias_sc[h, r0:r0 + tq, 0:tq] = tile
                bias_sc[h, r0:r0 + tq, tq:2 * tq] = tile

    lo = _lane_lo((tq, LANES))
    for h in heads:
        q = q_ref[:, h * LANES:(h + 1) * LANES]
        zero = jnp.zeros_like(q)
        q2_sc[h, 0:tq, :] = jnp.where(lo, q, zero)
        q2_sc[h, tq:2 * tq, :] = jnp.where(lo, zero, q)
    m_sc[...] = jnp.full(m_sc.shape, NEG_BIG, F32)
    acc_sc[...] = jnp.zeros_like(acc_sc)

    def step(k0, tk, band):
        for h in heads:
            st_sc[h, 0:tk, :] = _dot(k_ref[pl.ds(k0, tk), h * LANES:(h + 1) * LANES], q2_sc[h], NT)
        pvs, alphas = [], []
        for h in heads:
            m_old = m_sc[h]
            if band is None:
                st = st_sc[h, 0:tk, :]
                far = relb_ref[h, NUM_BUCKETS - 1] * LOG2E
                m_new = jnp.maximum(m_old, jnp.max(st, axis=0, keepdims=True) + far)
                p = jnp.exp2(st - (m_new - far))
            else:
                st = st_sc[h, 0:tk, :] + bias_sc[h, band:band + tk, :]
                m_new = jnp.maximum(m_old, jnp.max(st, axis=0, keepdims=True))
                p = jnp.exp2(st - m_new)
            m_sc[h] = m_new
            pvs.append(_dot(vt_ref[h, :, pl.ds(k0, tk)], p.astype(BF16)))
            alphas.append(jnp.exp2(m_old - m_new))
        for h in heads:
            acc_sc[h] = alphas[h] * acc_sc[h] + pvs[h]

    n_far = jnp.maximum(qi - 1, 0)
    group = FAR_GROUP * tq

    def far_body(i, carry):
        step(pl.multiple_of(i * group, group), group, None)
        return carry

    lax.fori_loop(0, n_far // FAR_GROUP, far_body, 0)
    pos = (n_far // FAR_GROUP) * FAR_GROUP
    left = n_far - pos
    size = FAR_GROUP // 2
    while size >= 1:
        take = (left // size) % 2 == 1

        @pl.when(take)
        def _(pos=pos, size=size):
            step(pl.multiple_of(pos * tq, tq), size * tq, None)

        pos = pos + jnp.where(take, size, 0)
        size //= 2

    @pl.when(qi >= 1)
    def _():
        step(pl.multiple_of((qi - 1) * tq, tq), 2 * tq, 0)

    @pl.when(qi == 0)
    def _():
        step(0, tq, tq)

    lam = (jnp.exp(jnp.sum(lq1_ref[...] * lk1_ref[...], axis=-1, keepdims=True))
           - jnp.exp(jnp.sum(lq2_ref[...] * lk2_ref[...], axis=-1, keepdims=True)) + lam_init)
    for h in heads:
        acc = acc_sc[h]
        l = acc[ACC_L:ACC_L + 1, :]
        o1 = acc[0:LANES, 0:tq] / l[:, 0:tq]
        o2 = acc[0:LANES, tq:2 * tq] / l[:, tq:2 * tq]
        ot = o1 - lam * o2
        ms = jnp.mean(ot * ot, axis=0, keepdims=True)
        y = ot * lax.rsqrt(ms + NORM_EPS) * sg_ref[...] * (1.0 - lam_init)
        o_ref[:, h * LANES:(h + 1) * LANES] = y.T.astype(o_ref.dtype)


def _diff_attn(q, k, vt, relb_t, lq1, lk1, lq2, lk2, sg_col, lam_init):
    B, S, width = q.shape
    H = width // LANES
    t = ATT_TILE
    buckets = jnp.asarray(_near_bucket_tiles(t))
    vec = pl.BlockSpec((1, HEAD_DIM), lambda b, i: (0, 0))
    return pl.pallas_call(
        functools.partial(_attn_kernel, lam_init=lam_init),
        grid=(B, S // t),
        in_specs=[
            pl.BlockSpec(memory_space=pltpu.SMEM),
            pl.BlockSpec((2, t, t), lambda b, i: (0, 0, 0)),
            vec, vec, vec, vec,
            pl.BlockSpec((LANES, 1), lambda b, i: (0, 0)),
            pl.BlockSpec((None, t, width), lambda b, i: (b, i, 0)),
            pl.BlockSpec((None, S, width), lambda b, i: (b, 0, 0)),
            pl.BlockSpec((None, H, ACC_ROWS, S), lambda b, i: (b, 0, 0, 0)),
        ],
        out_specs=pl.BlockSpec((None, t, width), lambda b, i: (b, i, 0)),
        out_shape=jax.ShapeDtypeStruct((B, S, width), BF16),
        scratch_shapes=[
            pltpu.VMEM((H, 2 * t, 2 * t), F32),
            pltpu.VMEM((H, 2 * t, LANES), BF16),
            pltpu.VMEM((H, 1, 2 * t), F32),
            pltpu.VMEM((H, ACC_ROWS, 2 * t), F32),
            pltpu.VMEM((H, FAR_GROUP * t, 2 * t), F32),
        ],
        compiler_params=_cparams(("arbitrary", "arbitrary")),
        name="diff_attn",
    )(relb_t, buckets, lq1, lk1, lq2, lk2, sg_col, q, k, vt)


def _run_rows(count):
    return jnp.ceil(count * (1.0 / RUN_CHUNK)) * RUN_CHUNK


def _outproj_kernel(x_ref, ya_ref, yb_ref, wa_ref, wb_ref, g_ref, rw_ref, rb_ref,
                    x1_ref, h2_ref, gd_ref, cnt_ref):
    x1 = x_ref[...] + _dot(ya_ref[...], wa_ref[...]) + _dot(yb_ref[...], wb_ref[...])
    x1_ref[...] = x1
    ms = jnp.mean(x1 * x1, axis=-1, keepdims=True)
    h2 = x1 * lax.rsqrt(ms + NORM_EPS) * g_ref[...]
    h2_ref[...] = h2.astype(h2_ref.dtype)
    logits = _mm(h2, rw_ref[...], mode="x3") + rb_ref[...]
    work = logits.T
    expert = lax.broadcasted_iota(jnp.int32, work.shape, 0).astype(F32)
    picks = []
    for _ in range(TOP_K):
        m = jnp.max(work, axis=0, keepdims=True)
        idx = jnp.min(jnp.where(work == m, expert, float(LANES)), axis=0, keepdims=True)
        hit = expert == idx
        picks.append((m, hit))
        work = jnp.where(hit, NEG_BIG, work)
    m0 = picks[0][0]
    es = [jnp.exp(m - m0) for m, _ in picks]
    denom = es[0] + es[1] + es[2] + es[3]
    gd_t = jnp.zeros_like(work)
    for e, (_, hit) in zip(es, picks):
        gd_t = jnp.where(hit, e / denom, gd_t)
    gd = gd_t.T
    gd_ref[...] = gd

    @pl.when(pl.program_id(0) == 0)
    def _():
        cnt_ref[...] = jnp.zeros_like(cnt_ref)

    picked = jnp.where(gd > 0.0, 1.0, 0.0)
    for t0 in range(0, gd.shape[0], MOVE_ROWS):
        cnt_ref[0:1, :] = cnt_ref[0:1, :] + _run_rows(jnp.sum(picked[t0:t0 + MOVE_ROWS], axis=0, keepdims=True))


def _outproj(x2d, ya, yb, wa, wb, g2, rw, rb):
    T, D = x2d.shape
    half = ya.shape[1]
    tm = OUT_ROWS
    row = lambda i: (i, 0)
    fix = lambda i: (0, 0)
    return pl.pallas_call(
        _outproj_kernel,
        grid=(T // tm,),
        in_specs=[
            pl.BlockSpec((tm, D), row),
            pl.BlockSpec((tm, half), row),
            pl.BlockSpec((tm, half), row),
            pl.BlockSpec((half, D), fix),
            pl.BlockSpec((half, D), fix),
            pl.BlockSpec((1, D), fix),
            pl.BlockSpec((D, LANES), fix),
            pl.BlockSpec((1, LANES), fix),
        ],
        out_specs=[
            pl.BlockSpec((tm, D), row),
            pl.BlockSpec((tm, D), row),
            pl.BlockSpec((tm, LANES), row),
            pl.BlockSpec((8, LANES), fix),
        ],
        out_shape=[
            jax.ShapeDtypeStruct((T, D), F32),
            jax.ShapeDtypeStruct((T, D), BF16),
            jax.ShapeDtypeStruct((T, LANES), F32),
            jax.ShapeDtypeStruct((8, LANES), F32),
        ],
        compiler_params=_cparams(("arbitrary",)),
        name="outproj_router",
    )(x2d, ya, yb, wa, wb, g2, rw, rb)


def _padded_rows(cnt, rb):
    return jnp.ceil(cnt * (1.0 / rb)) * rb


def _route_kernel(gd_ref, tot_ref, lp_ref, gate_ref, tab_ref, cnt_ref, carry_sc, pstart_sc, *, rb):
    i = pl.program_id(0)
    tb = gd_ref.shape[0]
    gd = gd_ref[...]
    sel = gd > 0.0
    self = jnp.where(sel, 1.0, 0.0)
    run_pad = _run_rows(jnp.sum(self, axis=0, keepdims=True))
    r2 = lax.broadcasted_iota(jnp.int32, (LANES, LANES), 0)
    c2 = lax.broadcasted_iota(jnp.int32, (LANES, LANES), 1)

    @pl.when(i == 0)
    def _():
        cnt = tot_ref[0:1, :]
        upper = jnp.where(r2 < c2, 1.0, 0.0)
        pstart = _dot(jnp.broadcast_to(_padded_rows(cnt, rb), (8, LANES)), upper, prec=HP)[0:1, :]
        pstart_sc[...] = pstart
        cnt_ref[0:1, :] = cnt
        cnt_ref[1:2, :] = pstart
        cnt_ref[2:8, :] = jnp.zeros((6, LANES), F32)
        carry_sc[...] = jnp.zeros_like(carry_sc)

    rr = lax.broadcasted_iota(jnp.int32, (tb, tb), 0)
    cc = lax.broadcasted_iota(jnp.int32, (tb, tb), 1)
    lower = jnp.where(cc < rr, 1.0, 0.0).astype(BF16)
    sel_b = self.astype(BF16)
    prefix = _dot(lower, sel_b)
    upper_b = jnp.where(r2 < c2, 1.0, 0.0).astype(BF16)
    off = _dot(jnp.broadcast_to(run_pad, (8, LANES)).astype(BF16), upper_b)[0:1, :]
    lp = off + prefix
    upper_incl = jnp.where(r2 <= c2, 1.0, 0.0).astype(BF16)
    slot = _dot(sel_b, upper_incl)
    lane = lax.broadcasted_iota(jnp.int32, (tb, LANES), 1)
    p_out = jnp.zeros((tb, LANES), F32)
    g_out = jnp.zeros((tb, LANES), F32)
    for kk in range(TOP_K):
        mk = sel & (slot == float(kk + 1))
        p_k = jnp.sum(jnp.where(mk, lp, 0.0), axis=-1, keepdims=True)
        g_k = jnp.sum(jnp.where(mk, gd, 0.0), axis=-1, keepdims=True)
        p_out = jnp.where(lane == kk, p_k, p_out)
        g_out = jnp.where(lane == kk, g_k, g_out)
    lp_ref[...] = p_out
    gate_ref[...] = g_out
    def as_column(v):
        return jnp.sum(jnp.where(r2 == c2, jnp.broadcast_to(v, (LANES, LANES)), 0.0), axis=1, keepdims=True)

    first = as_column(off * (1.0 / RUN_CHUNK))
    count = as_column(run_pad * (1.0 / RUN_CHUNK))
    start = as_column(pstart_sc[...] + carry_sc[...])
    chunk = lax.broadcasted_iota(jnp.int32, (LANES, CHUNK_LANES), 1).astype(F32)
    mine = (chunk >= first) & (chunk < first + count)
    dst = jnp.sum(jnp.where(mine, start + (chunk - first) * RUN_CHUNK, 0.0), axis=0, keepdims=True)
    pos_in_run = jnp.sum(jnp.where(mine, chunk - first, 0.0), axis=0, keepdims=True)
    run_len = jnp.sum(jnp.where(mine, count, 0.0), axis=0, keepdims=True)
    used = run_len > 0.0
    n_chunks = jnp.sum(count, axis=0, keepdims=True)
    big_part = jnp.floor(run_len * (1.0 / BIG_COPY)) * BIG_COPY
    is_big = used & (pos_in_run < big_part) & (pos_in_run == jnp.floor(pos_in_run * (1.0 / BIG_COPY)) * BIG_COPY)
    is_small = used & (pos_in_run >= big_part)
    code = dst * (256.0 / RUN_CHUNK) + chunk[0:1, :]
    c_hi = jnp.floor(code * (1.0 / 65536.0))
    c_mid = jnp.floor(code * (1.0 / 256.0)) - c_hi * 256.0
    c_lo = code - jnp.floor(code * (1.0 / 256.0)) * 256.0
    pieces = jnp.concatenate([c_hi, c_mid, c_lo, jnp.zeros((5, CHUNK_LANES), F32)], axis=0).astype(BF16)
    rq = lax.broadcasted_iota(jnp.int32, (CHUNK_LANES, CHUNK_LANES), 0)
    cq = lax.broadcasted_iota(jnp.int32, (CHUNK_LANES, CHUNK_LANES), 1)
    before = jnp.where(rq < cq, 1.0, 0.0).astype(BF16)
    slot_id = rq.astype(F32)

    def compact(mask):
        flag = jnp.where(mask, 1.0, 0.0)
        rank = _dot(jnp.broadcast_to(flag, (8, CHUNK_LANES)).astype(BF16), before)[0:1, :]
        pick = jnp.where(mask & (rank == slot_id), 1.0, 0.0).astype(BF16)
        got = _dot(pieces, pick, NT)
        return got[0:1, :] * 65536.0 + got[1:2, :] * 256.0 + got[2:3, :], jnp.sum(flag, axis=1, keepdims=True)

    big_list, n_big = compact(is_big)
    small_list, n_small = compact(is_small)
    lane_c = lax.broadcasted_iota(jnp.int32, (1, CHUNK_LANES), 1)
    counts = jnp.where(lane_c == 0, n_chunks, jnp.where(lane_c == 1, n_big, jnp.where(lane_c == 2, n_small, 0.0)))
    tab_ref[0:1, :] = big_list.astype(jnp.int32)
    tab_ref[1:2, :] = small_list.astype(jnp.int32)
    tab_ref[2:3, :] = counts.astype(jnp.int32)
    tab_ref[3:8, :] = jnp.zeros((5, CHUNK_LANES), jnp.int32)
    carry_sc[...] = carry_sc[...] + run_pad


def _route(gd, totals, rb):
    T = gd.shape[0]
    tb = MOVE_ROWS
    blk = lambda i: (i, 0)
    fix = lambda i: (0, 0)
    return pl.pallas_call(
        functools.partial(_route_kernel, rb=rb),
        grid=(T // tb,),
        in_specs=[pl.BlockSpec((tb, LANES), blk), pl.BlockSpec((8, LANES), fix)],
        out_specs=[
            pl.BlockSpec((tb, LANES), blk),
            pl.BlockSpec((tb, LANES), blk),
            pl.BlockSpec((8, CHUNK_LANES), blk),
            pl.BlockSpec((8, LANES), fix),
        ],
        out_shape=[
            jax.ShapeDtypeStruct((T, LANES), F32),
            jax.ShapeDtypeStruct((T, LANES), F32),
            jax.ShapeDtypeStruct((T // tb * 8, CHUNK_LANES), jnp.int32),
            jax.ShapeDtypeStruct((8, LANES), F32),
        ],
        scratch_shapes=[pltpu.VMEM((1, LANES), F32), pltpu.VMEM((1, LANES), F32)],
        compiler_params=_cparams(("arbitrary",)),
        name="route_positions",
    )(gd, totals)


def _rows_at(src_ref, src_row, dst_ref, dst_row, n_rows, sem):
    return pltpu.make_async_copy(src_ref.at[pl.ds(src_row, n_rows), :], dst_ref.at[pl.ds(dst_row, n_rows), :], sem)


def _start_runs(b, cnt_ref, big_ref, small_ref, copy_rows):
    def issue(tab_ref, stride, n_rows):
        def body(i, carry):
            code = tab_ref[b * stride + i]
            local = (code & 255) * RUN_CHUNK
            glob = lax.shift_right_logical(code, 8) * RUN_CHUNK
            copy_rows(pl.multiple_of(local, RUN_CHUNK), pl.multiple_of(glob, RUN_CHUNK), n_rows).start()
            return carry
        return body

    lax.fori_loop(0, cnt_ref[b * 4 + 1], issue(big_ref, BIG_LIST, BIG_COPY * RUN_CHUNK), 0)
    lax.fori_loop(0, cnt_ref[b * 4 + 2], issue(small_ref, SMALL_LIST, RUN_CHUNK), 0)


def _wait_runs(b, nc_ref, wait_rows):
    total = nc_ref[b * 4]
    p = 1
    while p <= LOCAL_CHUNKS:
        @pl.when((total & p) != 0)
        def _(p=p):
            wait_rows(p * RUN_CHUNK).wait()
        p *= 2


def _rows_copy(src_ref, dst_ref, n_rows, sem):
    return pltpu.make_async_copy(src_ref.at[pl.ds(0, n_rows), :], dst_ref.at[pl.ds(0, n_rows), :], sem)


def _local_onehot(lp, lb):
    lpt = lp.T
    j = lax.broadcasted_iota(jnp.int32, (lb, lp.shape[0]), 0).astype(F32)
    hit = lpt[0:1, :] == j
    for kk in range(1, TOP_K):
        hit = hit | (lpt[kk:kk + 1, :] == j)
    return hit


def _scatter_kernel(nc_ref, big_ref, small_ref, t0_ref, tc_ref, lp_ref, h_ref, xs_ref, buf, zero_buf, sems):
    b = pl.program_id(0)
    slot = b % 2
    pick = jnp.where(_local_onehot(lp_ref[...], buf.shape[1]), 1.0, 0.0).astype(BF16)
    buf[slot] = _dot(pick, h_ref[...])
    _start_runs(b, nc_ref, big_ref, small_ref,
                lambda s, d, n: _rows_at(buf.at[slot], s, xs_ref, d, n, sems.at[slot]))

    @pl.when(b >= 1)
    def _():
        _wait_runs(b - 1, nc_ref, lambda n: _rows_copy(buf.at[1 - slot], xs_ref, n, sems.at[1 - slot]))

    @pl.when(b == pl.num_programs(0) - 1)
    def _():
        _wait_runs(b, nc_ref, lambda n: _rows_copy(buf.at[slot], xs_ref, n, sems.at[slot]))
        zero_buf[...] = jnp.zeros_like(zero_buf)
        total = 0
        for e in range(N_EXPERTS + 1):
            t0 = t0_ref[e]

            def issue(c, carry, t0=t0):
                _rows_at(zero_buf, 0, xs_ref, pl.multiple_of(t0 + c * RUN_CHUNK, RUN_CHUNK), RUN_CHUNK,
                         sems.at[0]).start()
                return carry

            lax.fori_loop(0, tc_ref[e], issue, 0)
            total = total + tc_ref[e]

        def drain(c, carry):
            _rows_at(zero_buf, 0, xs_ref, 0, RUN_CHUNK, sems.at[0]).wait()
            return carry

        lax.fori_loop(0, total, drain, 0)


def _scatter_rows(tabs, tails, lp, h2, n_rows):
    T, D = h2.shape
    tb = MOVE_ROWS
    return pl.pallas_call(
        _scatter_kernel,
        grid_spec=pltpu.PrefetchScalarGridSpec(
            num_scalar_prefetch=5,
            grid=(T // tb,),
            in_specs=[
                pl.BlockSpec((tb, LANES), lambda i, *_: (i, 0)),
                pl.BlockSpec((tb, D), lambda i, *_: (i, 0)),
            ],
            out_specs=pl.BlockSpec(memory_space=pl.ANY),
            scratch_shapes=[pltpu.VMEM((2, LOCAL_ROWS, D), F32), pltpu.VMEM((RUN_CHUNK, D), F32),
                            pltpu.SemaphoreType.DMA((2,))],
        ),
        out_shape=jax.ShapeDtypeStruct((n_rows, D), F32),
        compiler_params=_cparams(("arbitrary",)),
        name="scatter_rows",
    )(*tabs, *tails, lp, h2)


def _moe_kernel(be_ref, na_ref, slot_ref, nxt_ref, valid_ref, xs_ref, w1_hbm, b1_ref, w2_hbm, b2_ref, ys_ref,
                w1f, w2f, w1b, w2b, sems):
    j = pl.program_id(0)
    e = be_ref[j]
    slot = slot_ref[j]

    def fetch(expert, sl):
        return (pltpu.make_async_copy(w1_hbm.at[expert], w1f.at[sl], sems.at[0, sl]),
                pltpu.make_async_copy(w2_hbm.at[expert], w2f.at[sl], sems.at[1, sl]))

    @pl.when(j == 0)
    def _():
        for cp in fetch(e, slot):
            cp.start()

    @pl.when((j == 0) | (e != be_ref[jnp.maximum(j - 1, 0)]))
    def _():
        for cp in fetch(e, slot):
            cp.wait()
        w1b[...] = w1f[slot].astype(BF16)
        w2b[...] = w2f[slot].astype(BF16)

        @pl.when(nxt_ref[j] >= 0)
        def _():
            for cp in fetch(nxt_ref[j], 1 - slot):
                cp.start()

    def expert_mlp(rows):
        xb = xs_ref[0:rows, :].astype(BF16)
        hid = _dot(xb, w1b[...]) + b1_ref[...]
        half = hid.shape[1] // 2
        x_glu = jnp.minimum(hid[:, :half], SWIGLU_LIMIT)
        x_lin = jnp.clip(hid[:, half:], -SWIGLU_LIMIT, SWIGLU_LIMIT)
        act = x_glu * _sigmoid(SWIGLU_ALPHA * x_glu) * (x_lin + 1.0)
        ys_ref[0:rows, :] = _dot(act.astype(BF16), w2b[...]) + b2_ref[...]

    rb = xs_ref.shape[0]
    valid = valid_ref[j]

    @pl.when(valid > rb // 2)
    def _():
        expert_mlp(rb)

    @pl.when((valid > 0) & (valid <= rb // 2))
    def _():
        expert_mlp(rb // 2)
        ys_ref[rb // 2:rb, :] = jnp.zeros((rb - rb // 2, ys_ref.shape[1]), ys_ref.dtype)

    @pl.when(valid <= 0)
    def _():
        ys_ref[...] = jnp.zeros_like(ys_ref)


def _expert_runs(blk_expert):
    n = blk_expert.shape[0]
    idx = jnp.arange(n)
    starts = jnp.concatenate([jnp.ones((1,), bool), blk_expert[1:] != blk_expert[:-1]])
    slot = (jnp.cumsum(starts) - 1) % 2
    next_start = lax.cummin(jnp.where(starts, idx, n)[::-1])[::-1]
    after = jnp.concatenate([next_start[1:], jnp.full((1,), n)])
    nxt = jnp.where(after < n, blk_expert[jnp.minimum(after, n - 1)], -1)
    return slot.astype(jnp.int32), nxt.astype(jnp.int32)


def _moe(blk_expert, n_active, valid, xs, w1, b1, w2, b2):
    R, D = xs.shape
    E, _, D2 = w1.shape
    rb = MOE_ROWS
    nb = R // rb
    slot, nxt = _expert_runs(blk_expert)
    blk = lambda j, be, na, *_: (jnp.minimum(j, na[0] - 1), 0)
    return pl.pallas_call(
        _moe_kernel,
        grid_spec=pltpu.PrefetchScalarGridSpec(
            num_scalar_prefetch=5,
            grid=(nb,),
            in_specs=[
                pl.BlockSpec((rb, D), blk),
                pl.BlockSpec(memory_space=pl.ANY),
                pl.BlockSpec((None, 1, D2), lambda j, be, *_: (be[j], 0, 0)),
                pl.BlockSpec(memory_space=pl.ANY),
                pl.BlockSpec((None, 1, D), lambda j, be, *_: (be[j], 0, 0)),
            ],
            out_specs=pl.BlockSpec((rb, D), lambda j, *_: (j, 0)),
            scratch_shapes=[pltpu.VMEM((2, D, D2), F32), pltpu.VMEM((2, D2 // 2, D), F32),
                            pltpu.VMEM((D, D2), BF16), pltpu.VMEM((D2 // 2, D), BF16),
                            pltpu.SemaphoreType.DMA((2, 2))],
        ),
        out_shape=jax.ShapeDtypeStruct((R, D), F32),
        compiler_params=_cparams(("arbitrary",)),
        name="moe_experts",
    )(blk_expert, n_active, slot, nxt, valid, xs, w1, b1, w2, b2)


def _combine_kernel(nc_ref, big_ref, small_ref, lp_ref, gate_ref, x1_ref, ys_ref, o_ref, buf, sems):
    b = pl.program_id(0)
    slot = b % 2

    def fetch(blk, sl):
        _start_runs(blk, nc_ref, big_ref, small_ref,
                    lambda s, d, n: _rows_at(ys_ref, d, buf.at[sl], s, n, sems.at[sl]))

    @pl.when(b == 0)
    def _():
        buf[...] = jnp.zeros_like(buf)
        fetch(b, slot)

    @pl.when(b + 1 < pl.num_programs(0))
    def _():
        fetch(b + 1, 1 - slot)

    _wait_runs(b, nc_ref, lambda n: _rows_copy(ys_ref, buf.at[slot], n, sems.at[slot]))
    lp = lp_ref[...]
    _, lb, d = buf.shape
    tb = lp.shape[0]
    lpt = lp.T
    gt = gate_ref[...].T
    j_row = lax.broadcasted_iota(jnp.int32, (lb, tb), 0).astype(F32)
    row_gate = jnp.zeros((lb, tb), F32)
    for kk in range(TOP_K):
        row_gate = jnp.where(lpt[kk:kk + 1, :] == j_row, gt[kk:kk + 1, :], row_gate)
    g_hi, g_lo = _split2(row_gate)
    ones = jnp.ones((tb, LANES), BF16)
    g_row = _dot(g_hi, ones) + _dot(g_lo, ones)
    rows = (buf[slot] * jnp.concatenate([g_row] * (d // LANES), axis=1)).astype(BF16)
    j_col = lax.broadcasted_iota(jnp.int32, (tb, lb), 1).astype(F32)
    hit = lp[:, 0:1] == j_col
    for kk in range(1, TOP_K):
        hit = hit | (lp[:, kk:kk + 1] == j_col)
    o_ref[...] = x1_ref[...] + _dot(jnp.where(hit, 1.0, 0.0).astype(BF16), rows)


def _combine(tabs, lp, gate, x1, ys):
    T, D = x1.shape
    tb = MOVE_ROWS
    return pl.pallas_call(
        _combine_kernel,
        grid_spec=pltpu.PrefetchScalarGridSpec(
            num_scalar_prefetch=3,
            grid=(T // tb,),
            in_specs=[
                pl.BlockSpec((tb, LANES), lambda i, *_: (i, 0)),
                pl.BlockSpec((tb, LANES), lambda i, *_: (i, 0)),
                pl.BlockSpec((tb, D), lambda i, *_: (i, 0)),
                pl.BlockSpec(memory_space=pl.ANY),
            ],
            out_specs=pl.BlockSpec((tb, D), lambda i, *_: (i, 0)),
            scratch_shapes=[pltpu.VMEM((2, LOCAL_ROWS, D), F32), pltpu.SemaphoreType.DMA((2,))],
        ),
        out_shape=jax.ShapeDtypeStruct((T, D), F32),
        compiler_params=_cparams(("arbitrary",)),
        name="combine_rows",
    )(*tabs, lp, gate, x1, ys)


def _pad_cols(a, n):
    return jnp.pad(a, ((0, 0), (0, n - a.shape[1])))


def _pad_rows(a, n):
    return jnp.pad(a, ((0, n - a.shape[0]), (0, 0)))


def _lora_weight(w2, a2, g2, width):
    z = jnp.zeros((LANES, width), F32)
    return jnp.concatenate([
        jnp.concatenate([_pad_rows(w2, LANES), z, z], axis=1),
        jnp.concatenate([z, _pad_rows(a2, LANES), z], axis=1),
        jnp.concatenate([z, z, _pad_rows(g2, LANES)], axis=1),
    ], axis=0).astype(BF16)


def _split_rwkv_cols(a, width):
    o = 3 * width
    return jnp.concatenate([
        a[:, :o],
        _pad_cols(a[:, o:o + DECAY_LORA], LANES),
        _pad_cols(a[:, o + DECAY_LORA:o + DECAY_LORA + AAA_LORA], LANES),
        _pad_cols(a[:, o + DECAY_LORA + AAA_LORA:o + DECAY_LORA + AAA_LORA + GATE_LORA], LANES),
    ], axis=1)


def kernel(x, norm1_g, w_in, mu_shift, w0, w2, a0, a2, g2, k_k, k_a, r_k, lnx_g, lnx_b, qn_g, kn_g,
           lam_q1, lam_k1, lam_q2, lam_k2, subln_g, rel_bias, w_out, norm2_g, router_w, router_b,
           exp_w1, exp_b1, exp_w2, exp_b2):
    B, S, D = x.shape
    T = B * S
    depth = norm1_g.shape[0]
    width = w0.shape[1]
    n_rwkv_cols = 3 * width + DECAY_LORA + AAA_LORA + GATE_LORA
    n_rwkv_pad = 3 * width + 3 * LANES
    n_groups = width // HEAD_DIM
    H = width // LANES
    relb_t = rel_bias.T
    row = lambda a: a.reshape(1, -1)

    for layer in range(depth):
        lam_init = 0.8 - 0.6 * math.exp(-0.3 * layer)
        w_l = w_in[layer]
        w_all = jnp.concatenate([_split_rwkv_cols(w_l[:, :n_rwkv_cols], width), w_l[:, n_rwkv_cols:]],
                                axis=1).astype(BF16)
        mu = _split_rwkv_cols(row(mu_shift[layer]), width)
        qg = row(jnp.tile(qn_g[layer], n_groups))
        kg = row(jnp.tile(kn_g[layer], n_groups))

        pr, q, k, vt = _inproj(x.reshape(T, D), row(norm1_g[layer]), w_all, qg, kg, n_rwkv_pad, width, S)

        y_a = _rwkv(pr.reshape(B, S, n_rwkv_pad), mu, row(w0[layer]),
                    _lora_weight(w2[layer], a2[layer], g2[layer], width), row(a0[layer]),
                    row(k_k[layer]), row(k_a[layer]), row(r_k[layer]), row(lnx_g[layer]),
                    row(lnx_b[layer]), width)

        y_b = _diff_attn(q.reshape(B, S, width), k.reshape(B, S, width), vt, relb_t,
                         row(lam_q1[layer]), row(lam_k1[layer]), row(lam_q2[layer]), row(lam_k2[layer]),
                         subln_g[layer].reshape(-1, 1), lam_init)

        wo = w_out[layer].astype(BF16)
        rw = _pad_cols(router_w[layer], LANES)
        rb = jnp.concatenate([router_b[layer], jnp.full((LANES - N_EXPERTS,), NEG_BIG, F32)]).reshape(1, -1)
        x1, h2, gd, totals = _outproj(x.reshape(T, D), y_a.reshape(T, width), y_b.reshape(T, width),
                                      wo[:width], wo[width:], row(norm2_g[layer]), rw, rb)

        lp, gate, tab, cnt = _route(gd, totals, MOE_ROWS)
        tab = tab.reshape(-1, 8, CHUNK_LANES)
        tabs = (tab[:, 2, :4].reshape(-1),
                tab[:, 0, :BIG_LIST].reshape(-1), tab[:, 1, :SMALL_LIST].reshape(-1))
        run_end = cnt[1, :N_EXPERTS] + cnt[0, :N_EXPERTS]
        pad_end = cnt[1, :N_EXPERTS] + _padded_rows(cnt[0, :N_EXPERTS], MOE_ROWS)
        run_pad_max = (T // MOVE_ROWS) * N_EXPERTS * (RUN_CHUNK - 1)
        n_blocks = -(-(T * TOP_K + run_pad_max + N_EXPERTS * (MOE_ROWS - 1)) // MOE_ROWS)
        gap_start = jnp.concatenate([run_end, pad_end[-1:]])
        gap_end = jnp.concatenate([pad_end, jnp.full((1,), n_blocks * MOE_ROWS, F32)])
        tails = (gap_start.astype(jnp.int32), ((gap_end - gap_start) / RUN_CHUNK).astype(jnp.int32))
        blk_start = (jnp.arange(n_blocks) * MOE_ROWS).astype(F32)
        blk_expert = jnp.minimum(jnp.sum(pad_end[None, :] <= blk_start[:, None], axis=1), N_EXPERTS - 1)
        n_active = (pad_end[N_EXPERTS - 1] / MOE_ROWS).astype(jnp.int32).reshape(1)

        xs = _scatter_rows(tabs, tails, lp, h2, n_blocks * MOE_ROWS)
        blk_valid = jnp.clip(run_end[blk_expert] - blk_start, 0, MOE_ROWS)
        ys = _moe(blk_expert.astype(jnp.int32), n_active, blk_valid.astype(jnp.int32), xs,
                  exp_w1[layer], exp_b1[layer][:, None, :], exp_w2[layer], exp_b2[layer][:, None, :])
        x = _combine(tabs, lp, gate, x1, ys).reshape(B, S, D)
    return x
```

```python
import functools
import math

import numpy as np
import jax
import jax.numpy as jnp
from jax import lax
from jax.experimental import pallas as pl
from jax.experimental.pallas import tpu as pltpu

F32 = jnp.float32
BF16 = jnp.bfloat16
HP = lax.Precision.HIGHEST

HEAD_DIM = 64
DECAY_LORA = 32
AAA_LORA = 32
GATE_LORA = 96
NUM_BUCKETS = 32
MAX_DISTANCE = 128
N_EXPERTS = 32
TOP_K = 4
SWIGLU_LIMIT = 7.0
SWIGLU_ALPHA = 1.702
NORM_EPS = 1e-5
LNX_EPS = 64e-5

LANES = 128
VMEM_LIMIT_BYTES = 56 * 1024 * 1024

NEG_BIG = -1e30
LOG2E = math.log2(math.e)
ACC_L = LANES
ACC_ROWS = LANES + 8

PROJ_ROWS = 1024
RWKV_CHUNK = 64
RWKV_ROWS = 128
RWKV_SEQS = 4
ATT_TILE = 256
FAR_GROUP = 4
MOE_ROWS = 512
MOVE_ROWS = 256
OUT_ROWS = 256
RUN_CHUNK = 8
LOCAL_ROWS = MOVE_ROWS * 4 + 32 * RUN_CHUNK
LOCAL_CHUNKS = LOCAL_ROWS // RUN_CHUNK
CHUNK_LANES = 256
ISSUE_UNROLL = 4
BIG_COPY = 4
BIG_LIST = 64
SMALL_LIST = 128

NN = (((1,), (0,)), ((), ()))
NT = (((1,), (1,)), ((), ()))
TN = (((0,), (0,)), ((), ()))


def _dot(a, b, dims=NN, prec=None):
    return lax.dot_general(a, b, dims, precision=prec, preferred_element_type=F32)


def _cparams(sem):
    return pltpu.CompilerParams(dimension_semantics=sem, vmem_limit_bytes=VMEM_LIMIT_BYTES)


def _sigmoid(x):
    return 1.0 / (1.0 + jnp.exp(-x))


def _lane_lo(shape):
    lane = lax.broadcasted_iota(jnp.int32, shape, len(shape) - 1)
    return (lane % LANES) < HEAD_DIM


def _head_rms_normalize(x, gain, scale):
    outs = []
    for t in range(x.shape[1] // LANES):
        xt = x[:, t * LANES:(t + 1) * LANES]
        lo = _lane_lo(xt.shape)
        x2 = xt * xt
        s0 = jnp.sum(jnp.where(lo, x2, 0.0), axis=-1, keepdims=True)
        s1 = jnp.sum(jnp.where(lo, 0.0, x2), axis=-1, keepdims=True)
        inv = lax.rsqrt(jnp.where(lo, s0, s1) * (1.0 / HEAD_DIM) + NORM_EPS)
        outs.append(xt * inv)
    return jnp.concatenate(outs, axis=-1) * gain * scale


def _inproj_kernel(x_ref, g_ref, w_ref, qg_ref, kg_ref, pr_ref, q_ref, k_ref, vt_ref, *, n_rwkv, width):
    x = x_ref[...]
    ms = jnp.mean(x * x, axis=-1, keepdims=True)
    h = (x * lax.rsqrt(ms + NORM_EPS) * g_ref[...]).astype(BF16)
    step = 384
    for c0 in range(0, n_rwkv, step):
        pr_ref[:, c0:c0 + step] = _dot(h, w_ref[:, c0:c0 + step])
    q = _dot(h, w_ref[:, n_rwkv:n_rwkv + width])
    q_ref[...] = _head_rms_normalize(q, qg_ref[...], HEAD_DIM ** -0.5 * LOG2E).astype(BF16)
    k = _dot(h, w_ref[:, n_rwkv + width:n_rwkv + 2 * width])
    k_ref[...] = _head_rms_normalize(k, kg_ref[...], 1.0).astype(BF16)
    v = _dot(h, w_ref[:, n_rwkv + 2 * width:n_rwkv + 3 * width])
    for hd in range(width // LANES):
        vt_ref[hd, 0:LANES, :] = v[:, hd * LANES:(hd + 1) * LANES].T.astype(BF16)
        vt_ref[hd, LANES:ACC_ROWS, :] = jnp.ones((ACC_ROWS - LANES, v.shape[0]), BF16)


def _inproj(x2d, g, w_all, qg, kg, n_rwkv, width, seq_len):
    T, D = x2d.shape
    n_all = w_all.shape[1]
    tm = PROJ_ROWS
    H = width // LANES
    per_seq = seq_len // tm
    row = lambda i: (i, 0)
    fix = lambda i: (0, 0)
    return pl.pallas_call(
        functools.partial(_inproj_kernel, n_rwkv=n_rwkv, width=width),
        grid=(T // tm,),
        in_specs=[
            pl.BlockSpec((tm, D), row),
            pl.BlockSpec((1, D), fix),
            pl.BlockSpec((D, n_all), fix),
            pl.BlockSpec((1, width), fix),
            pl.BlockSpec((1, width), fix),
        ],
        out_specs=[
            pl.BlockSpec((tm, n_rwkv), row),
            pl.BlockSpec((tm, width), row),
            pl.BlockSpec((tm, width), row),
            pl.BlockSpec((None, H, ACC_ROWS, tm), lambda i: (i // per_seq, 0, 0, i % per_seq)),
        ],
        out_shape=[
            jax.ShapeDtypeStruct((T, n_rwkv), F32),
            jax.ShapeDtypeStruct((T, width), BF16),
            jax.ShapeDtypeStruct((T, width), BF16),
            jax.ShapeDtypeStruct((T // seq_len, H, ACC_ROWS, seq_len), BF16),
        ],
        compiler_params=_cparams(("arbitrary",)),
        name="inproj",
    )(x2d, g, w_all, qg, kg)


def _split2(x):
    hi = x.astype(BF16)
    return hi, (x - hi.astype(F32)).astype(BF16)


def _mm(a, b, dims=NN, mode="bf16"):
    if mode == "bf16":
        return _dot(a.astype(BF16), b.astype(BF16), dims)
    ah, al = _split2(a)
    bh, bl = _split2(b)
    lhs = jnp.concatenate([ah, ah, al], axis=dims[0][0][0])
    rhs = jnp.concatenate([bh, bl, bh], axis=dims[0][1][0])
    return _dot(lhs, rhs, dims)


def _group_sum(x, ones2):
    hi, lo = _split2(x)
    return _dot(jnp.concatenate([hi, lo], axis=1), ones2)


P_INV = "x3"
P_SUBST = "bf16"
P_STATE = "bf16"
INV_BLOCK = 16
QUAD = 256


def _rwkv_kernel(pr_ref, mu_ref, w0_ref, wl_ref, a0_ref, kk_ref, ka_ref, rk_ref, lg_ref, lb_ref,
                 tri3_ref, ones2_ref, o_ref, state_sc, prev_sc, *, width):
    C = RWKV_CHUNK
    n_seq, seq_rows, _ = pr_ref.shape
    rows = n_seq * seq_rows
    n_quads = width // QUAD
    n_chunks = seq_rows // C
    heads_q = QUAD // HEAD_DIM

    @pl.when(pl.program_id(1) == 0)
    def _():
        state_sc[...] = jnp.zeros_like(state_sc)
        prev_sc[...] = jnp.zeros_like(prev_sc)

    x = jnp.concatenate([pr_ref[s] for s in range(n_seq)], axis=0)
    rid = lax.broadcasted_iota(jnp.int32, x.shape, 0)
    prev = pltpu.roll(x, 1, axis=0)
    for s in range(n_seq):
        prev = jnp.where(rid == s * seq_rows, prev_sc[s:s + 1, :], prev)
        prev_sc[s:s + 1, :] = x[(s + 1) * seq_rows - 1:(s + 1) * seq_rows, :]
    xs = x + (prev - x) * mu_ref[...]

    r = xs[:, 0:width]
    k = xs[:, width:2 * width]
    v = xs[:, 2 * width:3 * width]
    o = 3 * width
    lora_in = jnp.concatenate([jnp.tanh(xs[:, o:o + LANES]), xs[:, o + LANES:o + 2 * LANES],
                               _sigmoid(xs[:, o + 2 * LANES:o + 3 * LANES])], axis=1).astype(BF16)
    lora = _dot(lora_in, wl_ref[...])
    z = -(w0_ref[...] + lora[:, 0:width])
    softplus = jnp.maximum(z, 0.0) + jnp.log(1.0 + jnp.exp(-jnp.abs(z)))
    log_decay = -jnp.exp(-softplus - 0.5)
    a_lr = _sigmoid(a0_ref[...] + lora[:, width:2 * width])
    gate = lora[:, 2 * width:3 * width]

    ones2 = ones2_ref[...]
    kk = k * kk_ref[...]
    kk = kk * lax.rsqrt(jnp.maximum(_group_sum(kk * kk, ones2), 1e-24))
    k = k * (1.0 + (a_lr - 1.0) * ka_ref[...])
    a_vec = -kk
    b_vec = kk * a_lr

    p1 = log_decay.astype(BF16)
    rem = log_decay - p1.astype(F32)
    p2 = rem.astype(BF16)
    p3 = (rem - p2.astype(F32)).astype(BF16)
    cum = _dot(tri3_ref[...], jnp.concatenate([p1, p2, p3], axis=0))

    t_i = lax.broadcasted_iota(jnp.int32, (C, 2 * QUAD), 0)
    s_i = lax.broadcasted_iota(jnp.int32, (C, 2 * QUAD), 1) % C
    strict2 = t_i > s_i
    incl2 = t_i >= s_i
    lane_head = lax.broadcasted_iota(jnp.int32, (C, QUAD), 1) // HEAD_DIM
    row_blk = lax.broadcasted_iota(jnp.int32, (C, QUAD), 0) // INV_BLOCK
    same_blk = row_blk == ((lax.broadcasted_iota(jnp.int32, (C, QUAD), 1) % C) // INV_BLOCK)
    r2 = lax.broadcasted_iota(jnp.int32, (QUAD, QUAD), 0)
    c2 = lax.broadcasted_iota(jnp.int32, (QUAD, QUAD), 1)
    same_head = (r2 // HEAD_DIM) == (c2 // HEAD_DIM)

    def by_head(m):
        zero = jnp.zeros_like(m)
        return jnp.concatenate([jnp.where(lane_head == h, m, zero) for h in range(heads_q)], axis=0)

    def bd_rhs(m, mode):
        if mode == "bf16":
            return by_head(m.astype(BF16))
        hi, lo = _split2(m)
        hi_bd = by_head(hi)
        return jnp.concatenate([hi_bd, by_head(lo), hi_bd], axis=0)

    def mm_bd(a, rhs, mode):
        if mode == "bf16":
            return _dot(a.astype(BF16), rhs)
        ah, al = _split2(a)
        return _dot(jnp.concatenate([ah, ah, al], axis=1), rhs)

    class Work:
        pass

    def make_works(s):
        ws = []
        for c in range(n_chunks):
            for q in range(n_quads):
                w = Work()
                ls = slice(q * QUAD, (q + 1) * QUAD)
                rs = slice(s * seq_rows + c * C, s * seq_rows + (c + 1) * C)
                ld = log_decay[rs, ls]
                cm = cum[rs, ls]
                cl = cm[C - 1:C, :]
                e_in = jnp.exp(cm)
                e_neg = jnp.exp(-cm)
                e_hat = jnp.exp(cl - cm)
                w.q = q
                w.dec = jnp.exp(cl)
                w.vv = v[rs, ls]
                w.ar_t = jnp.concatenate([a_vec[rs, ls] * jnp.exp(cm - ld), r[rs, ls] * e_in], axis=0)
                w.bk_t = jnp.concatenate([bd_rhs(b_vec[rs, ls] * e_neg, "bf16"),
                                          bd_rhs(k[rs, ls] * e_neg, "bf16")], axis=0)
                w.bk_h = jnp.concatenate([b_vec[rs, ls] * e_hat, k[rs, ls] * e_hat], axis=0)
                ws.append(w)
        return ws

    def pre_stages(ws):
        def gram():
            for w in ws:
                g = _dot(w.ar_t.astype(BF16), w.bk_t, NT)
                low = jnp.where(strict2, g[0:C, :], 0.0)
                l_cat = low[:, 0:QUAD]
                rbk = jnp.where(incl2, g[C:2 * C, :], 0.0)
                w.rb_cat = rbk[:, 0:QUAD]
                w.kv_lhs = jnp.concatenate([low[:, QUAD:], rbk[:, QUAD:]], axis=0)
                w.l_off = jnp.where(same_blk, 0.0, l_cat)
                w.d_inv = jnp.where(same_blk, l_cat, 0.0)

        def square0():
            for w in ws:
                w.l_pow = mm_bd(w.d_inv, bd_rhs(w.d_inv, P_INV), P_INV)

        def square(last):
            def run():
                for w in ws:
                    if last:
                        w.d_inv = w.d_inv + w.l_pow + mm_bd(w.d_inv, bd_rhs(w.l_pow, P_INV), P_INV)
                    else:
                        both = mm_bd(jnp.concatenate([w.d_inv, w.l_pow], axis=0), bd_rhs(w.l_pow, P_INV), P_INV)
                        w.d_inv = w.d_inv + w.l_pow + both[0:C]
                        w.l_pow = both[C:2 * C]
            return run

        def neumann1():
            for w in ws:
                w.m1 = w.l_off + mm_bd(w.d_inv, bd_rhs(w.l_off, P_SUBST), P_SUBST)
                w.m1_bd = bd_rhs(w.m1, P_SUBST)

        def neumann2():
            for w in ws:
                w.m2 = mm_bd(w.m1, w.m1_bd, P_SUBST)

        def neumann3():
            for w in ws:
                w.nn = w.m1 + w.m2 + mm_bd(w.m2, w.m1_bd, P_SUBST)

        def finish():
            for w in ws:
                w.t_inv = w.nn + w.d_inv + mm_bd(w.nn, bd_rhs(w.d_inv, P_SUBST), P_SUBST)

        def values():
            for w in ws:
                w.kv = mm_bd(w.kv_lhs, bd_rhs(w.vv, P_STATE), P_STATE)

        n_sq = int(math.log2(INV_BLOCK))
        return ([gram, square0] + [square(it + 1 == n_sq) for it in range(1, n_sq)]
                + [neumann1, neumann2, neumann3, finish, values])

    def chain_levels(s, ws, y_parts):
        levels = []
        for c in range(n_chunks):
            wc = ws[c * n_quads:(c + 1) * n_quads]

            def base(wc=wc):
                for w in wc:
                    st = state_sc[s * n_quads + w.q]
                    w.base = _mm(w.ar_t, st, NT, mode=P_STATE) + w.kv

            def solve(wc=wc):
                for w in wc:
                    u0 = w.base[0:C]
                    w.u = u0 + mm_bd(w.t_inv, bd_rhs(u0, P_STATE), P_STATE)

            def update(wc=wc):
                for w in wc:
                    upd = _mm(jnp.concatenate([w.u, w.vv], axis=0), w.bk_h, TN, mode=P_STATE)
                    sq = s * n_quads + w.q
                    state_sc[sq] = state_sc[sq] * w.dec + jnp.where(same_head, upd, 0.0)
                for w in wc:
                    y_parts[w.q].append(w.base[C:2 * C] + mm_bd(w.rb_cat, bd_rhs(w.u, P_STATE), P_STATE))

            levels += [base, solve, update]
        return levels

    def post_stages(s, y_parts):
        rs = slice(s * seq_rows, (s + 1) * seq_rows)
        box = Work()

        def sums():
            box.y = jnp.concatenate([jnp.concatenate(y_parts[q], axis=0) for q in range(n_quads)], axis=-1)
            box.sums = _group_sum(jnp.concatenate([box.y, r[rs] * k[rs] * rk_ref[...]], axis=0), ones2)

        def variance():
            box.yc = box.y - box.sums[0:seq_rows] * (1.0 / HEAD_DIM)
            box.var = _group_sum(box.yc * box.yc, ones2) * (1.0 / HEAD_DIM)

        def write():
            yn = box.yc * lax.rsqrt(box.var + LNX_EPS) * lg_ref[...] + lb_ref[...]
            o_ref[s] = ((yn + box.sums[seq_rows:2 * seq_rows] * v[rs]) * gate[rs]).astype(o_ref.dtype)

        return [sums, variance, write]

    works = [make_works(s) for s in range(n_seq)]
    pres = [pre_stages(ws) for ws in works]
    y_all = [[[] for _ in range(n_quads)] for _ in range(n_seq)]
    for group in (pres, [chain_levels(s, works[s], y_all[s]) for s in range(n_seq)],
                  [post_stages(s, y_all[s]) for s in range(n_seq)]):
        for stages in zip(*group):
            for stage in stages:
                stage()


def _rwkv(pr, mu, w0, w_lora, a0, k_k, k_a, r_k, lnx_g, lnx_b, width):
    B, S, n_rwkv = pr.shape
    rows = RWKV_ROWS
    n_seq = RWKV_SEQS
    C = RWKV_CHUNK
    rr = np.arange(n_seq * rows)[:, None]
    cc = np.arange(n_seq * rows)[None, :]
    tri = ((rr // C == cc // C) & (cc <= rr)).astype(np.float32)
    tri3 = jnp.asarray(np.concatenate([tri, tri, tri], axis=1), BF16)
    gg = np.arange(width)
    ones = (gg[:, None] // HEAD_DIM == gg[None, :] // HEAD_DIM).astype(np.float32)
    ones2 = jnp.asarray(np.concatenate([ones, ones], axis=0), BF16)
    fix = lambda shape: pl.BlockSpec(shape, lambda b, i: (0, 0))
    vec = lambda n: fix((1, n))
    return pl.pallas_call(
        functools.partial(_rwkv_kernel, width=width),
        grid=(B // n_seq, S // rows),
        in_specs=[
            pl.BlockSpec((n_seq, rows, n_rwkv), lambda b, i: (b, i, 0)),
            vec(n_rwkv), vec(width), fix(w_lora.shape), vec(width),
            vec(width), vec(width), vec(width), vec(width), vec(width),
            fix(tri3.shape), fix(ones2.shape),
        ],
        out_specs=pl.BlockSpec((n_seq, rows, width), lambda b, i: (b, i, 0)),
        out_shape=jax.ShapeDtypeStruct((B, S, width), BF16),
        scratch_shapes=[
            pltpu.VMEM((n_seq * (width // QUAD), QUAD, QUAD), F32),
            pltpu.VMEM((n_seq, n_rwkv), F32),
        ],
        compiler_params=_cparams(("arbitrary", "arbitrary")),
        name="rwkv7",
    )(pr, mu, w0, w_lora, a0, k_k, k_a, r_k, lnx_g, lnx_b, tri3, ones2)


def _t5_bucket_np(dist):
    n = np.maximum(dist, 0)
    max_exact = NUM_BUCKETS // 2
    nf = np.maximum(n, 1).astype(np.float32)
    large = max_exact + (np.log(nf / max_exact) / math.log(MAX_DISTANCE / max_exact)
                         * (NUM_BUCKETS - max_exact)).astype(np.int32)
    large = np.minimum(large, NUM_BUCKETS - 1)
    return np.where(n < max_exact, n, large).astype(np.int32)


def _near_bucket_tiles(tile):
    kpos = np.arange(tile)[:, None]
    qpos = np.arange(tile)[None, :]
    out = []
    for delta in (0, 1):
        dist = delta * tile + qpos - kpos
        out.append(np.where(dist >= 0, _t5_bucket_np(dist), -1))
    return np.stack(out).astype(np.int32)


def _attn_kernel(relb_ref, bucket_ref, lq1_ref, lk1_ref, lq2_ref, lk2_ref, sg_ref,
                 q_ref, k_ref, vt_ref, o_ref, bias_sc, q2_sc, m_sc, acc_sc, st_sc, *, lam_init):
    tq = q_ref.shape[0]
    n_heads = vt_ref.shape[0]
    heads = range(n_heads)
    qi = pl.program_id(1)

    @pl.when((pl.program_id(0) == 0) & (qi == 0))
    def _():
        for h in heads:
            for d in range(2):
                bk = bucket_ref[d]
                tile = jnp.full(bk.shape, NEG_BIG, F32)
                for j in range(NUM_BUCKETS):
                    tile = jnp.where(bk == j, relb_ref[h, j] * LOG2E, tile)
                r0 = (1 - d) * tq
                bias_sc[h, r0:r0 + tq, 0:tq] = tile
                bias_sc[h, r0:r0 + tq, tq:2 * tq] = tile

    lo = _lane_lo((tq, LANES))
    for h in heads:
        q = q_ref[:, h * LANES:(h + 1) * LANES]
        zero = jnp.zeros_like(q)
        q2_sc[h, 0:tq, :] = jnp.where(lo, q, zero)
        q2_sc[h, tq:2 * tq, :] = jnp.where(lo, zero, q)
    m_sc[...] = jnp.full(m_sc.shape, NEG_BIG, F32)
    acc_sc[...] = jnp.zeros_like(acc_sc)

    def step(k0, tk, band):
        for h in heads:
            st_sc[h, 0:tk, :] = _dot(k_ref[pl.ds(k0, tk), h * LANES:(h + 1) * LANES], q2_sc[h], NT)
        pvs, alphas = [], []
        for h in heads:
            m_old = m_sc[h]
            if band is None:
                st = st_sc[h, 0:tk, :]
                far = relb_ref[h, NUM_BUCKETS - 1] * LOG2E
                m_new = jnp.maximum(m_old, jnp.max(st, axis=0, keepdims=True) + far)
                p = jnp.exp2(st - (m_new - far))
            else:
                st = st_sc[h, 0:tk, :] + bias_sc[h, band:band + tk, :]
                m_new = jnp.maximum(m_old, jnp.max(st, axis=0, keepdims=True))
                p = jnp.exp2(st - m_new)
            m_sc[h] = m_new
            pvs.append(_dot(vt_ref[h, :, pl.ds(k0, tk)], p.astype(BF16)))
            alphas.append(jnp.exp2(m_old - m_new))
        for h in heads:
            acc_sc[h] = alphas[h] * acc_sc[h] + pvs[h]

    n_far = jnp.maximum(qi - 1, 0)
    group = FAR_GROUP * tq

    def far_body(i, carry):
        step(pl.multiple_of(i * group, group), group, None)
        return carry

    lax.fori_loop(0, n_far // FAR_GROUP, far_body, 0)
    pos = (n_far // FAR_GROUP) * FAR_GROUP
    left = n_far - pos
    size = FAR_GROUP // 2
    while size >= 1:
        take = (left // size) % 2 == 1

        @pl.when(take)
        def _(pos=pos, size=size):
            step(pl.multiple_of(pos * tq, tq), size * tq, None)

        pos = pos + jnp.where(take, size, 0)
        size //= 2

    @pl.when(qi >= 1)
    def _():
        step(pl.multiple_of((qi - 1) * tq, tq), 2 * tq, 0)

    @pl.when(qi == 0)
    def _():
        step(0, tq, tq)

    lam = (jnp.exp(jnp.sum(lq1_ref[...] * lk1_ref[...], axis=-1, keepdims=True))
           - jnp.exp(jnp.sum(lq2_ref[...] * lk2_ref[...], axis=-1, keepdims=True)) + lam_init)
    for h in heads:
        acc = acc_sc[h]
        l = acc[ACC_L:ACC_L + 1, :]
        o1 = acc[0:LANES, 0:tq] / l[:, 0:tq]
        o2 = acc[0:LANES, tq:2 * tq] / l[:, tq:2 * tq]
        ot = o1 - lam * o2
        ms = jnp.mean(ot * ot, axis=0, keepdims=True)
        y = ot * lax.rsqrt(ms + NORM_EPS) * sg_ref[...] * (1.0 - lam_init)
        o_ref[:, h * LANES:(h + 1) * LANES] = y.T.astype(o_ref.dtype)


def _diff_attn(q, k, vt, relb_t, lq1, lk1, lq2, lk2, sg_col, lam_init):
    B, S, width = q.shape
    H = width // LANES
    t = ATT_TILE
    buckets = jnp.asarray(_near_bucket_tiles(t))
    vec = pl.BlockSpec((1, HEAD_DIM), lambda b, i: (0, 0))
    return pl.pallas_call(
        functools.partial(_attn_kernel, lam_init=lam_init),
        grid=(B, S // t),
        in_specs=[
            pl.BlockSpec(memory_space=pltpu.SMEM),
            pl.BlockSpec((2, t, t), lambda b, i: (0, 0, 0)),
            vec, vec, vec, vec,
            pl.BlockSpec((LANES, 1), lambda b, i: (0, 0)),
            pl.BlockSpec((None, t, width), lambda b, i: (b, i, 0)),
            pl.BlockSpec((None, S, width), lambda b, i: (b, 0, 0)),
            pl.BlockSpec((None, H, ACC_ROWS, S), lambda b, i: (b, 0, 0, 0)),
        ],
        out_specs=pl.BlockSpec((None, t, width), lambda b, i: (b, i, 0)),
        out_shape=jax.ShapeDtypeStruct((B, S, width), BF16),
        scratch_shapes=[
            pltpu.VMEM((H, 2 * t, 2 * t), F32),
            pltpu.VMEM((H, 2 * t, LANES), BF16),
            pltpu.VMEM((H, 1, 2 * t), F32),
            pltpu.VMEM((H, ACC_ROWS, 2 * t), F32),
            pltpu.VMEM((H, FAR_GROUP * t, 2 * t), F32),
        ],
        compiler_params=_cparams(("arbitrary", "arbitrary")),
        name="diff_attn",
    )(relb_t, buckets, lq1, lk1, lq2, lk2, sg_col, q, k, vt)


def _run_rows(count):
    return jnp.ceil(count * (1.0 / RUN_CHUNK)) * RUN_CHUNK


def _outproj_kernel(x_ref, ya_ref, yb_ref, wa_ref, wb_ref, g_ref, rw_ref, rb_ref,
                    x1_ref, h2_ref, gd_ref, cnt_ref):
    x1 = x_ref[...] + _dot(ya_ref[...], wa_ref[...]) + _dot(yb_ref[...], wb_ref[...])
    x1_ref[...] = x1
    ms = jnp.mean(x1 * x1, axis=-1, keepdims=True)
    h2 = x1 * lax.rsqrt(ms + NORM_EPS) * g_ref[...]
    h2_ref[...] = h2.astype(h2_ref.dtype)
    logits = _mm(h2, rw_ref[...], mode="x3") + rb_ref[...]
    work = logits.T
    expert = lax.broadcasted_iota(jnp.int32, work.shape, 0).astype(F32)
    picks = []
    for _ in range(TOP_K):
        m = jnp.max(work, axis=0, keepdims=True)
        idx = jnp.min(jnp.where(work == m, expert, float(LANES)), axis=0, keepdims=True)
        hit = expert == idx
        picks.append((m, hit))
        work = jnp.where(hit, NEG_BIG, work)
    m0 = picks[0][0]
    es = [jnp.exp(m - m0) for m, _ in picks]
    denom = es[0] + es[1] + es[2] + es[3]
    gd_t = jnp.zeros_like(work)
    for e, (_, hit) in zip(es, picks):
        gd_t = jnp.where(hit, e / denom, gd_t)
    gd = gd_t.T
    gd_ref[...] = gd

    @pl.when(pl.program_id(0) == 0)
    def _():
        cnt_ref[...] = jnp.zeros_like(cnt_ref)

    picked = jnp.where(gd > 0.0, 1.0, 0.0)
    for t0 in range(0, gd.shape[0], MOVE_ROWS):
        cnt_ref[0:1, :] = cnt_ref[0:1, :] + _run_rows(jnp.sum(picked[t0:t0 + MOVE_ROWS], axis=0, keepdims=True))


def _outproj(x2d, ya, yb, wa, wb, g2, rw, rb):
    T, D = x2d.shape
    half = ya.shape[1]
    tm = OUT_ROWS
    row = lambda i: (i, 0)
    fix = lambda i: (0, 0)
    return pl.pallas_call(
        _outproj_kernel,
        grid=(T // tm,),
        in_specs=[
            pl.BlockSpec((tm, D), row),
            pl.BlockSpec((tm, half), row),
            pl.BlockSpec((tm, half), row),
            pl.BlockSpec((half, D), fix),
            pl.BlockSpec((half, D), fix),
            pl.BlockSpec((1, D), fix),
            pl.BlockSpec((D, LANES), fix),
            pl.BlockSpec((1, LANES), fix),
        ],
        out_specs=[
            pl.BlockSpec((tm, D), row),
            pl.BlockSpec((tm, D), row),
            pl.BlockSpec((tm, LANES), row),
            pl.BlockSpec((8, LANES), fix),
        ],
        out_shape=[
            jax.ShapeDtypeStruct((T, D), F32),
            jax.ShapeDtypeStruct((T, D), BF16),
            jax.ShapeDtypeStruct((T, LANES), F32),
            jax.ShapeDtypeStruct((8, LANES), F32),
        ],
        compiler_params=_cparams(("arbitrary",)),
        name="outproj_router",
    )(x2d, ya, yb, wa, wb, g2, rw, rb)


def _padded_rows(cnt, rb):
    return jnp.ceil(cnt * (1.0 / rb)) * rb


def _route_kernel(gd_ref, tot_ref, lp_ref, gate_ref, tab_ref, cnt_ref, carry_sc, pstart_sc, *, rb):
    i = pl.program_id(0)
    tb = gd_ref.shape[0]
    gd = gd_ref[...]
    sel = gd > 0.0
    self = jnp.where(sel, 1.0, 0.0)
    run_pad = _run_rows(jnp.sum(self, axis=0, keepdims=True))
    r2 = lax.broadcasted_iota(jnp.int32, (LANES, LANES), 0)
    c2 = lax.broadcasted_iota(jnp.int32, (LANES, LANES), 1)

    @pl.when(i == 0)
    def _():
        cnt = tot_ref[0:1, :]
        upper = jnp.where(r2 < c2, 1.0, 0.0)
        pstart = _dot(jnp.broadcast_to(_padded_rows(cnt, rb), (8, LANES)), upper, prec=HP)[0:1, :]
        pstart_sc[...] = pstart
        cnt_ref[0:1, :] = cnt
        cnt_ref[1:2, :] = pstart
        cnt_ref[2:8, :] = jnp.zeros((6, LANES), F32)
        carry_sc[...] = jnp.zeros_like(carry_sc)

    rr = lax.broadcasted_iota(jnp.int32, (tb, tb), 0)
    cc = lax.broadcasted_iota(jnp.int32, (tb, tb), 1)
    lower = jnp.where(cc < rr, 1.0, 0.0).astype(BF16)
    sel_b = self.astype(BF16)
    prefix = _dot(lower, sel_b)
    upper_b = jnp.where(r2 < c2, 1.0, 0.0).astype(BF16)
    off = _dot(jnp.broadcast_to(run_pad, (8, LANES)).astype(BF16), upper_b)[0:1, :]
    lp = off + prefix
    upper_incl = jnp.where(r2 <= c2, 1.0, 0.0).astype(BF16)
    slot = _dot(sel_b, upper_incl)
    lane = lax.broadcasted_iota(jnp.int32, (tb, LANES), 1)
    p_out = jnp.zeros((tb, LANES), F32)
    g_out = jnp.zeros((tb, LANES), F32)
    for kk in range(TOP_K):
        mk = sel & (slot == float(kk + 1))
        p_k = jnp.sum(jnp.where(mk, lp, 0.0), axis=-1, keepdims=True)
        g_k = jnp.sum(jnp.where(mk, gd, 0.0), axis=-1, keepdims=True)
        p_out = jnp.where(lane == kk, p_k, p_out)
        g_out = jnp.where(lane == kk, g_k, g_out)
    lp_ref[...] = p_out
    gate_ref[...] = g_out
    def as_column(v):
        return jnp.sum(jnp.where(r2 == c2, jnp.broadcast_to(v, (LANES, LANES)), 0.0), axis=1, keepdims=True)

    first = as_column(off * (1.0 / RUN_CHUNK))
    count = as_column(run_pad * (1.0 / RUN_CHUNK))
    start = as_column(pstart_sc[...] + carry_sc[...])
    chunk = lax.broadcasted_iota(jnp.int32, (LANES, CHUNK_LANES), 1).astype(F32)
    mine = (chunk >= first) & (chunk < first + count)
    dst = jnp.sum(jnp.where(mine, start + (chunk - first) * RUN_CHUNK, 0.0), axis=0, keepdims=True)
    pos_in_run = jnp.sum(jnp.where(mine, chunk - first, 0.0), axis=0, keepdims=True)
    run_len = jnp.sum(jnp.where(mine, count, 0.0), axis=0, keepdims=True)
    used = run_len > 0.0
    n_chunks = jnp.sum(count, axis=0, keepdims=True)
    big_part = jnp.floor(run_len * (1.0 / BIG_COPY)) * BIG_COPY
    is_big = used & (pos_in_run < big_part) & (pos_in_run == jnp.floor(pos_in_run * (1.0 / BIG_COPY)) * BIG_COPY)
    is_small = used & (pos_in_run >= big_part)
    code = dst * (256.0 / RUN_CHUNK) + chunk[0:1, :]
    c_hi = jnp.floor(code * (1.0 / 65536.0))
    c_mid = jnp.floor(code * (1.0 / 256.0)) - c_hi * 256.0
    c_lo = code - jnp.floor(code * (1.0 / 256.0)) * 256.0
    pieces = jnp.concatenate([c_hi, c_mid, c_lo, jnp.zeros((5, CHUNK_LANES), F32)], axis=0).astype(BF16)
    rq = lax.broadcasted_iota(jnp.int32, (CHUNK_LANES, CHUNK_LANES), 0)
    cq = lax.broadcasted_iota(jnp.int32, (CHUNK_LANES, CHUNK_LANES), 1)
    before = jnp.where(rq < cq, 1.0, 0.0).astype(BF16)
    slot_id = rq.astype(F32)

    def compact(mask):
        flag = jnp.where(mask, 1.0, 0.0)
        rank = _dot(jnp.broadcast_to(flag, (8, CHUNK_LANES)).astype(BF16), before)[0:1, :]
        pick = jnp.where(mask & (rank == slot_id), 1.0, 0.0).astype(BF16)
        got = _dot(pieces, pick, NT)
        return got[0:1, :] * 65536.0 + got[1:2, :] * 256.0 + got[2:3, :], jnp.sum(flag, axis=1, keepdims=True)

    big_list, n_big = compact(is_big)
    small_list, n_small = compact(is_small)
    lane_c = lax.broadcasted_iota(jnp.int32, (1, CHUNK_LANES), 1)
    counts = jnp.where(lane_c == 0, n_chunks, jnp.where(lane_c == 1, n_big, jnp.where(lane_c == 2, n_small, 0.0)))
    tab_ref[0:1, :] = big_list.astype(jnp.int32)
    tab_ref[1:2, :] = small_list.astype(jnp.int32)
    tab_ref[2:3, :] = counts.astype(jnp.int32)
    tab_ref[3:8, :] = jnp.zeros((5, CHUNK_LANES), jnp.int32)
    carry_sc[...] = carry_sc[...] + run_pad


def _route(gd, totals, rb):
    T = gd.shape[0]
    tb = MOVE_ROWS
    blk = lambda i: (i, 0)
    fix = lambda i: (0, 0)
    return pl.pallas_call(
        functools.partial(_route_kernel, rb=rb),
        grid=(T // tb,),
        in_specs=[pl.BlockSpec((tb, LANES), blk), pl.BlockSpec((8, LANES), fix)],
        out_specs=[
            pl.BlockSpec((tb, LANES), blk),
            pl.BlockSpec((tb, LANES), blk),
            pl.BlockSpec((8, CHUNK_LANES), blk),
            pl.BlockSpec((8, LANES), fix),
        ],
        out_shape=[
            jax.ShapeDtypeStruct((T, LANES), F32),
            jax.ShapeDtypeStruct((T, LANES), F32),
            jax.ShapeDtypeStruct((T // tb * 8, CHUNK_LANES), jnp.int32),
            jax.ShapeDtypeStruct((8, LANES), F32),
        ],
        scratch_shapes=[pltpu.VMEM((1, LANES), F32), pltpu.VMEM((1, LANES), F32)],
        compiler_params=_cparams(("arbitrary",)),
        name="route_positions",
    )(gd, totals)


def _rows_at(src_ref, src_row, dst_ref, dst_row, n_rows, sem):
    return pltpu.make_async_copy(src_ref.at[pl.ds(src_row, n_rows), :], dst_ref.at[pl.ds(dst_row, n_rows), :], sem)


def _start_runs(b, cnt_ref, big_ref, small_ref, copy_rows):
    def issue_list(tab_ref, stride, n_rows, count):
        def one(i):
            code = tab_ref[b * stride + i]
            local = (code & 255) * RUN_CHUNK
            glob = lax.shift_right_logical(code, 8) * RUN_CHUNK
            copy_rows(pl.multiple_of(local, RUN_CHUNK), pl.multiple_of(glob, RUN_CHUNK), n_rows).start()

        def body(g, carry):
            for u in range(ISSUE_UNROLL):
                one(g * ISSUE_UNROLL + u)
            return carry

        groups = lax.shift_right_logical(count, int(math.log2(ISSUE_UNROLL)))
        lax.fori_loop(0, groups, body, 0)
        done = groups * ISSUE_UNROLL
        p = ISSUE_UNROLL // 2
        while p >= 1:
            @pl.when(((count - done) & p) != 0)
            def _(p=p, first=done + ((count - done) & ~(2 * p - 1))):
                for u in range(p):
                    one(first + u)
            p //= 2

    issue_list(big_ref, BIG_LIST, BIG_COPY * RUN_CHUNK, cnt_ref[b * 4 + 1])
    issue_list(small_ref, SMALL_LIST, RUN_CHUNK, cnt_ref[b * 4 + 2])


def _wait_runs(b, nc_ref, wait_rows):
    total = nc_ref[b * 4]
    p = 1
    while p <= LOCAL_CHUNKS:
        @pl.when((total & p) != 0)
        def _(p=p):
            wait_rows(p * RUN_CHUNK).wait()
        p *= 2


def _rows_copy(src_ref, dst_ref, n_rows, sem):
    return pltpu.make_async_copy(src_ref.at[pl.ds(0, n_rows), :], dst_ref.at[pl.ds(0, n_rows), :], sem)


def _local_onehot(lp, lb):
    lpt = lp.T
    j = lax.broadcasted_iota(jnp.int32, (lb, lp.shape[0]), 0).astype(F32)
    hit = lpt[0:1, :] == j
    for kk in range(1, TOP_K):
        hit = hit | (lpt[kk:kk + 1, :] == j)
    return hit


def _scatter_kernel(nc_ref, big_ref, small_ref, t0_ref, tc_ref, lp_ref, h_ref, xs_ref, buf, zero_buf, sems):
    b = pl.program_id(0)
    slot = b % 2
    pick = jnp.where(_local_onehot(lp_ref[...], buf.shape[1]), 1.0, 0.0).astype(BF16)
    buf[slot] = _dot(pick, h_ref[...])
    _start_runs(b, nc_ref, big_ref, small_ref,
                lambda s, d, n: _rows_at(buf.at[slot], s, xs_ref, d, n, sems.at[slot]))

    @pl.when(b >= 1)
    def _():
        _wait_runs(b - 1, nc_ref, lambda n: _rows_copy(buf.at[1 - slot], xs_ref, n, sems.at[1 - slot]))

    @pl.when(b == pl.num_programs(0) - 1)
    def _():
        _wait_runs(b, nc_ref, lambda n: _rows_copy(buf.at[slot], xs_ref, n, sems.at[slot]))
        zero_buf[...] = jnp.zeros_like(zero_buf)
        total = 0
        for e in range(N_EXPERTS + 1):
            t0 = t0_ref[e]

            def issue(c, carry, t0=t0):
                _rows_at(zero_buf, 0, xs_ref, pl.multiple_of(t0 + c * RUN_CHUNK, RUN_CHUNK), RUN_CHUNK,
                         sems.at[0]).start()
                return carry

            lax.fori_loop(0, tc_ref[e], issue, 0)
            total = total + tc_ref[e]

        def drain(c, carry):
            _rows_at(zero_buf, 0, xs_ref, 0, RUN_CHUNK, sems.at[0]).wait()
            return carry

        lax.fori_loop(0, total, drain, 0)


def _scatter_rows(tabs, tails, lp, h2, n_rows):
    T, D = h2.shape
    tb = MOVE_ROWS
    return pl.pallas_call(
        _scatter_kernel,
        grid_spec=pltpu.PrefetchScalarGridSpec(
            num_scalar_prefetch=5,
            grid=(T // tb,),
            in_specs=[
                pl.BlockSpec((tb, LANES), lambda i, *_: (i, 0)),
                pl.BlockSpec((tb, D), lambda i, *_: (i, 0)),
            ],
            out_specs=pl.BlockSpec(memory_space=pl.ANY),
            scratch_shapes=[pltpu.VMEM((2, LOCAL_ROWS, D), F32), pltpu.VMEM((RUN_CHUNK, D), F32),
                            pltpu.SemaphoreType.DMA((2,))],
        ),
        out_shape=jax.ShapeDtypeStruct((n_rows, D), F32),
        compiler_params=_cparams(("arbitrary",)),
        name="scatter_rows",
    )(*tabs, *tails, lp, h2)


def _moe_kernel(be_ref, na_ref, slot_ref, nxt_ref, valid_ref, xs_ref, w1_hbm, b1_ref, w2_hbm, b2_ref, ys_ref,
                w1f, w2f, w1b, w2b, sems):
    j = pl.program_id(0)
    e = be_ref[j]
    slot = slot_ref[j]

    def fetch(expert, sl):
        return (pltpu.make_async_copy(w1_hbm.at[expert], w1f.at[sl], sems.at[0, sl]),
                pltpu.make_async_copy(w2_hbm.at[expert], w2f.at[sl], sems.at[1, sl]))

    @pl.when(j == 0)
    def _():
        for cp in fetch(e, slot):
            cp.start()

    @pl.when((j == 0) | (e != be_ref[jnp.maximum(j - 1, 0)]))
    def _():
        for cp in fetch(e, slot):
            cp.wait()
        w1b[...] = w1f[slot].astype(BF16)
        w2b[...] = w2f[slot].astype(BF16)

        @pl.when(nxt_ref[j] >= 0)
        def _():
            for cp in fetch(nxt_ref[j], 1 - slot):
                cp.start()

    def expert_mlp(rows):
        xb = xs_ref[0:rows, :].astype(BF16)
        hid = _dot(xb, w1b[...]) + b1_ref[...]
        half = hid.shape[1] // 2
        x_glu = jnp.minimum(hid[:, :half], SWIGLU_LIMIT)
        x_lin = jnp.clip(hid[:, half:], -SWIGLU_LIMIT, SWIGLU_LIMIT)
        act = x_glu * _sigmoid(SWIGLU_ALPHA * x_glu) * (x_lin + 1.0)
        ys_ref[0:rows, :] = _dot(act.astype(BF16), w2b[...]) + b2_ref[...]

    rb = xs_ref.shape[0]
    valid = valid_ref[j]

    @pl.when(valid > rb // 2)
    def _():
        expert_mlp(rb)

    @pl.when((valid > 0) & (valid <= rb // 2))
    def _():
        expert_mlp(rb // 2)
        ys_ref[rb // 2:rb, :] = jnp.zeros((rb - rb // 2, ys_ref.shape[1]), ys_ref.dtype)

    @pl.when(valid <= 0)
    def _():
        ys_ref[...] = jnp.zeros_like(ys_ref)


def _expert_runs(blk_expert):
    n = blk_expert.shape[0]
    idx = jnp.arange(n)
    starts = jnp.concatenate([jnp.ones((1,), bool), blk_expert[1:] != blk_expert[:-1]])
    slot = (jnp.cumsum(starts) - 1) % 2
    next_start = lax.cummin(jnp.where(starts, idx, n)[::-1])[::-1]
    after = jnp.concatenate([next_start[1:], jnp.full((1,), n)])
    nxt = jnp.where(after < n, blk_expert[jnp.minimum(after, n - 1)], -1)
    return slot.astype(jnp.int32), nxt.astype(jnp.int32)


def _moe(blk_expert, n_active, valid, xs, w1, b1, w2, b2):
    R, D = xs.shape
    E, _, D2 = w1.shape
    rb = MOE_ROWS
    nb = R // rb
    slot, nxt = _expert_runs(blk_expert)
    blk = lambda j, be, na, *_: (jnp.minimum(j, na[0] - 1), 0)
    return pl.pallas_call(
        _moe_kernel,
        grid_spec=pltpu.PrefetchScalarGridSpec(
            num_scalar_prefetch=5,
            grid=(nb,),
            in_specs=[
                pl.BlockSpec((rb, D), blk),
                pl.BlockSpec(memory_space=pl.ANY),
                pl.BlockSpec((None, 1, D2), lambda j, be, *_: (be[j], 0, 0)),
                pl.BlockSpec(memory_space=pl.ANY),
                pl.BlockSpec((None, 1, D), lambda j, be, *_: (be[j], 0, 0)),
            ],
            out_specs=pl.BlockSpec((rb, D), lambda j, *_: (j, 0)),
            scratch_shapes=[pltpu.VMEM((2, D, D2), F32), pltpu.VMEM((2, D2 // 2, D), F32),
                            pltpu.VMEM((D, D2), BF16), pltpu.VMEM((D2 // 2, D), BF16),
                            pltpu.SemaphoreType.DMA((2, 2))],
        ),
        out_shape=jax.ShapeDtypeStruct((R, D), F32),
        compiler_params=_cparams(("arbitrary",)),
        name="moe_experts",
    )(blk_expert, n_active, slot, nxt, valid, xs, w1, b1, w2, b2)


def _combine_kernel(nc_ref, big_ref, small_ref, lp_ref, gate_ref, x1_ref, ys_ref, o_ref, buf, sems):
    b = pl.program_id(0)
    slot = b % 2

    def fetch(blk, sl):
        _start_runs(blk, nc_ref, big_ref, small_ref,
                    lambda s, d, n: _rows_at(ys_ref, d, buf.at[sl], s, n, sems.at[sl]))

    @pl.when(b == 0)
    def _():
        buf[...] = jnp.zeros_like(buf)
        fetch(b, slot)

    @pl.when(b + 1 < pl.num_programs(0))
    def _():
        fetch(b + 1, 1 - slot)

    _wait_runs(b, nc_ref, lambda n: _rows_copy(ys_ref, buf.at[slot], n, sems.at[slot]))
    lp = lp_ref[...]
    _, lb, d = buf.shape
    tb = lp.shape[0]
    lpt = lp.T
    gt = gate_ref[...].T
    j_row = lax.broadcasted_iota(jnp.int32, (lb, tb), 0).astype(F32)
    row_gate = jnp.zeros((lb, tb), F32)
    for kk in range(TOP_K):
        row_gate = jnp.where(lpt[kk:kk + 1, :] == j_row, gt[kk:kk + 1, :], row_gate)
    g_hi, g_lo = _split2(row_gate)
    ones = jnp.ones((tb, LANES), BF16)
    g_row = _dot(g_hi, ones) + _dot(g_lo, ones)
    rows = (buf[slot] * jnp.concatenate([g_row] * (d // LANES), axis=1)).astype(BF16)
    j_col = lax.broadcasted_iota(jnp.int32, (tb, lb), 1).astype(F32)
    hit = lp[:, 0:1] == j_col
    for kk in range(1, TOP_K):
        hit = hit | (lp[:, kk:kk + 1] == j_col)
    o_ref[...] = x1_ref[...] + _dot(jnp.where(hit, 1.0, 0.0).astype(BF16), rows)


def _combine(tabs, lp, gate, x1, ys):
    T, D = x1.shape
    tb = MOVE_ROWS
    return pl.pallas_call(
        _combine_kernel,
        grid_spec=pltpu.PrefetchScalarGridSpec(
            num_scalar_prefetch=3,
            grid=(T // tb,),
            in_specs=[
                pl.BlockSpec((tb, LANES), lambda i, *_: (i, 0)),
                pl.BlockSpec((tb, LANES), lambda i, *_: (i, 0)),
                pl.BlockSpec((tb, D), lambda i, *_: (i, 0)),
                pl.BlockSpec(memory_space=pl.ANY),
            ],
            out_specs=pl.BlockSpec((tb, D), lambda i, *_: (i, 0)),
            scratch_shapes=[pltpu.VMEM((2, LOCAL_ROWS, D), F32), pltpu.SemaphoreType.DMA((2,))],
        ),
        out_shape=jax.ShapeDtypeStruct((T, D), F32),
        compiler_params=_cparams(("arbitrary",)),
        name="combine_rows",
    )(*tabs, lp, gate, x1, ys)


def _pad_cols(a, n):
    return jnp.pad(a, ((0, 0), (0, n - a.shape[1])))


def _pad_rows(a, n):
    return jnp.pad(a, ((0, n - a.shape[0]), (0, 0)))


def _lora_weight(w2, a2, g2, width):
    z = jnp.zeros((LANES, width), F32)
    return jnp.concatenate([
        jnp.concatenate([_pad_rows(w2, LANES), z, z], axis=1),
        jnp.concatenate([z, _pad_rows(a2, LANES), z], axis=1),
        jnp.concatenate([z, z, _pad_rows(g2, LANES)], axis=1),
    ], axis=0).astype(BF16)


def _split_rwkv_cols(a, width):
    o = 3 * width
    return jnp.concatenate([
        a[:, :o],
        _pad_cols(a[:, o:o + DECAY_LORA], LANES),
        _pad_cols(a[:, o + DECAY_LORA:o + DECAY_LORA + AAA_LORA], LANES),
        _pad_cols(a[:, o + DECAY_LORA + AAA_LORA:o + DECAY_LORA + AAA_LORA + GATE_LORA], LANES),
    ], axis=1)


def kernel(x, norm1_g, w_in, mu_shift, w0, w2, a0, a2, g2, k_k, k_a, r_k, lnx_g, lnx_b, qn_g, kn_g,
           lam_q1, lam_k1, lam_q2, lam_k2, subln_g, rel_bias, w_out, norm2_g, router_w, router_b,
           exp_w1, exp_b1, exp_w2, exp_b2):
    B, S, D = x.shape
    T = B * S
    depth = norm1_g.shape[0]
    width = w0.shape[1]
    n_rwkv_cols = 3 * width + DECAY_LORA + AAA_LORA + GATE_LORA
    n_rwkv_pad = 3 * width + 3 * LANES
    n_groups = width // HEAD_DIM
    H = width // LANES
    relb_t = rel_bias.T
    row = lambda a: a.reshape(1, -1)

    for layer in range(depth):
        lam_init = 0.8 - 0.6 * math.exp(-0.3 * layer)
        w_l = w_in[layer]
        w_all = jnp.concatenate([_split_rwkv_cols(w_l[:, :n_rwkv_cols], width), w_l[:, n_rwkv_cols:]],
                                axis=1).astype(BF16)
        mu = _split_rwkv_cols(row(mu_shift[layer]), width)
        qg = row(jnp.tile(qn_g[layer], n_groups))
        kg = row(jnp.tile(kn_g[layer], n_groups))

        pr, q, k, vt = _inproj(x.reshape(T, D), row(norm1_g[layer]), w_all, qg, kg, n_rwkv_pad, width, S)

        y_a = _rwkv(pr.reshape(B, S, n_rwkv_pad), mu, row(w0[layer]),
                    _lora_weight(w2[layer], a2[layer], g2[layer], width), row(a0[layer]),
                    row(k_k[layer]), row(k_a[layer]), row(r_k[layer]), row(lnx_g[layer]),
                    row(lnx_b[layer]), width)

        y_b = _diff_attn(q.reshape(B, S, width), k.reshape(B, S, width), vt, relb_t,
                         row(lam_q1[layer]), row(lam_k1[layer]), row(lam_q2[layer]), row(lam_k2[layer]),
                         subln_g[layer].reshape(-1, 1), lam_init)

        wo = w_out[layer].astype(BF16)
        rw = _pad_cols(router_w[layer], LANES)
        rb = jnp.concatenate([router_b[layer], jnp.full((LANES - N_EXPERTS,), NEG_BIG, F32)]).reshape(1, -1)
        x1, h2, gd, totals = _outproj(x.reshape(T, D), y_a.reshape(T, width), y_b.reshape(T, width),
                                      wo[:width], wo[width:], row(norm2_g[layer]), rw, rb)

        lp, gate, tab, cnt = _route(gd, totals, MOE_ROWS)
        tab = tab.reshape(-1, 8, CHUNK_LANES)
        tabs = (tab[:, 2, :4].reshape(-1),
                tab[:, 0, :BIG_LIST].reshape(-1), tab[:, 1, :SMALL_LIST].reshape(-1))
        run_end = cnt[1, :N_EXPERTS] + cnt[0, :N_EXPERTS]
        pad_end = cnt[1, :N_EXPERTS] + _padded_rows(cnt[0, :N_EXPERTS], MOE_ROWS)
        run_pad_max = (T // MOVE_ROWS) * N_EXPERTS * (RUN_CHUNK - 1)
        n_blocks = -(-(T * TOP_K + run_pad_max + N_EXPERTS * (MOE_ROWS - 1)) // MOE_ROWS)
        gap_start = jnp.concatenate([run_end, pad_end[-1:]])
        gap_end = jnp.concatenate([pad_end, jnp.full((1,), n_blocks * MOE_ROWS, F32)])
        tails = (gap_start.astype(jnp.int32), ((gap_end - gap_start) / RUN_CHUNK).astype(jnp.int32))
        blk_start = (jnp.arange(n_blocks) * MOE_ROWS).astype(F32)
        blk_expert = jnp.minimum(jnp.sum(pad_end[None, :] <= blk_start[:, None], axis=1), N_EXPERTS - 1)
        n_active = (pad_end[N_EXPERTS - 1] / MOE_ROWS).astype(jnp.int32).reshape(1)

        xs = _scatter_rows(tabs, tails, lp, h2, n_blocks * MOE_ROWS)
        blk_valid = jnp.clip(run_end[blk_expert] - blk_start, 0, MOE_ROWS)
        ys = _moe(blk_expert.astype(jnp.int32), n_active, blk_valid.astype(jnp.int32), xs,
                  exp_w1[layer], exp_b1[layer][:, None, :], exp_w2[layer], exp_b2[layer][:, None, :])
        x = _combine(tabs, lp, gate, x1, ys).reshape(B, S, D)
    return x
```
